```python
import math
import jax
import jax.numpy as jnp
from jax import lax
import numpy as np

D_MODEL = 1024
BATCH = 4
SEQ = 8192
DEPTH = 1

D_MIX = D_MODEL
ML_HEADS = 4
ML_DH = D_MIX // (2 * ML_HEADS)
ML_W = ML_HEADS * ML_DH
ML_CHUNK = 128
CONV_K = 4
DA_HEADS = 4
DA_DH = D_MIX // (4 * DA_HEADS)
DA_W = DA_HEADS * 2 * DA_DH
Q_BLOCK = 128
N_EXPERTS = 256
TOP_K = 8
N_GROUPS = 8
TOPK_GROUPS = 4
D_EXPERT = D_MODEL // 4
D_SHARED = D_EXPERT
ROUTED_SCALE = 2.5
MOE_BLOCK = 128
EPS = 1e-6
COL_SPLITS = (2 * ML_W, 3 * ML_W, 4 * ML_W, 4 * ML_W + ML_HEADS, 4 * ML_W + 2 * ML_HEADS,
              4 * ML_W + 2 * ML_HEADS + DA_W, 4 * ML_W + 2 * ML_HEADS + 2 * DA_W)
D_IN = 4 * ML_W + 2 * ML_HEADS + 3 * DA_W

kernel_name = 'hybrid_mlstm_diffattn_moe_layer'


def rms_norm(x, eps=EPS):
    xf = x.astype(jnp.float32)
    return (xf * lax.rsqrt(jnp.mean(xf * xf, axis=-1, keepdims=True) + eps)).astype(x.dtype)


def lambda_init(layer):
    return 0.8 - 0.6 * math.exp(-0.3 * layer)


def alibi_slopes(n):
    return 2.0 ** (-8.0 * jnp.arange(1, n + 1, dtype=jnp.float32) / n)


def causal_depthwise_conv(u, w, b):
    S = u.shape[1]
    up = jnp.pad(u, ((0, 0), (CONV_K - 1, 0), (0, 0)))
    out = b
    for j in range(CONV_K):
        out = out + w[j] * up[:, j:j + S, :]
    return out


def mlstm_chunkwise(q, k, v, i_pre, f_pre):
    B, S, H, DH = q.shape
    L = ML_CHUNK
    NC = S // L

    def to_chunks(t):
        t = t.astype(jnp.float32).reshape((B, NC, L, H) + t.shape[3:])
        return jnp.moveaxis(t, (1, 3), (0, 2))

    qc = to_chunks(q)
    kc = to_chunks(k) * (DH ** -0.5)
    vc = to_chunks(v)
    ic = to_chunks(i_pre)
    lfc = jax.nn.log_sigmoid(to_chunks(f_pre))
    causal = jnp.tril(jnp.ones((L, L), dtype=bool))

    def step(carry, xs):
        C, n, m = carry
        qb, kb, vb, ib, lfb = xs
        b = jnp.cumsum(lfb, axis=-1)
        d_intra = jnp.where(causal, b[..., :, None] - b[..., None, :] + ib[..., None, :], -jnp.inf)
        d_inter = b + m[..., None]
        m_t = jnp.maximum(d_inter, jnp.max(d_intra, axis=-1))
        w_intra = jnp.exp(d_intra - m_t[..., None])
        w_inter = jnp.exp(d_inter - m_t)
        s = jnp.einsum('bhtd,bhsd->bhts', qb, kb) * w_intra
        num = (w_inter[..., None] * jnp.einsum('bhtd,bhde->bhte', qb, C)
               + jnp.einsum('bhts,bhse->bhte', s, vb))
        den = w_inter * jnp.einsum('bhtd,bhd->bht', qb, n) + jnp.sum(s, axis=-1)
        h = num / jnp.maximum(jnp.abs(den), jnp.exp(-m_t))[..., None]
        b_last = b[..., -1]
        d_state = b_last[..., None] - b + ib
        m_new = jnp.maximum(b_last + m, jnp.max(d_state, axis=-1))
        carry_scale = jnp.exp(b_last + m - m_new)
        wk = jnp.exp(d_state - m_new[..., None])
        C_new = carry_scale[..., None, None] * C + jnp.einsum('bhs,bhsd,bhse->bhde', wk, kb, vb)
        n_new = carry_scale[..., None] * n + jnp.einsum('bhs,bhsd->bhd', wk, kb)
        return (C_new, n_new, m_new), h

    init = (jnp.zeros((B, H, DH, DH), jnp.float32),
            jnp.zeros((B, H, DH), jnp.float32),
            jnp.zeros((B, H), jnp.float32))
    _, h = lax.scan(step, init, (qc, kc, vc, ic, lfc))
    return jnp.moveaxis(h, (0, 2), (1, 3)).reshape(B, S, H, DH)


def diff_attention(q, k, v, lam, slopes):
    B, S, H, _, d = q.shape
    nq = S // Q_BLOCK
    qh = jnp.transpose(q, (0, 2, 3, 1, 4))
    kh = jnp.transpose(k, (0, 2, 3, 1, 4))
    vh = jnp.transpose(v, (0, 2, 1, 3))
    q_blocks = jnp.moveaxis(qh.reshape(B, H, 2, nq, Q_BLOCK, d), 3, 0)
    kpos = jnp.arange(S, dtype=jnp.int32)
    scale = d ** -0.5

    def one_block(args):
        qb, qi = args
        qpos = qi * Q_BLOCK + jnp.arange(Q_BLOCK, dtype=jnp.int32)
        dist = (qpos[:, None] - kpos[None, :]).astype(jnp.float32)
        s = jnp.einsum('bhcqd,bhckd->bhcqk', qb, kh).astype(jnp.float32) * scale
        s = jnp.where(dist >= 0, s - slopes[None, :, None, None, None] * dist, -jnp.inf)
        p = jax.nn.softmax(s, axis=-1)
        a = p[:, :, 0] - lam * p[:, :, 1]
        return jnp.einsum('bhqk,bhke->bhqe', a.astype(vh.dtype), vh)

    o = lax.map(one_block, (q_blocks, jnp.arange(nq, dtype=jnp.int32)))
    return jnp.transpose(o, (1, 0, 3, 2, 4)).reshape(B, S, H, 2 * d)


def swiglu(h, w1, w3, w2):
    return (jax.nn.silu(h @ w1) * (h @ w3)) @ w2


def routed_moe(h, w_router, router_bias, w1, w3, w2, ws1, ws3, ws2):
    T, D = h.shape
    s = jax.nn.sigmoid((h @ w_router).astype(jnp.float32))
    sel = s + router_bias
    g = sel.reshape(T, N_GROUPS, N_EXPERTS // N_GROUPS)
    group_score = jnp.sum(lax.top_k(g, 2)[0], axis=-1)
    _, top_groups = lax.top_k(group_score, TOPK_GROUPS)
    group_mask = jnp.sum(jax.nn.one_hot(top_groups, N_GROUPS, dtype=jnp.float32), axis=1) > 0
    expert_mask = jnp.repeat(group_mask, N_EXPERTS // N_GROUPS, axis=-1)
    _, idx = lax.top_k(jnp.where(expert_mask, sel, -jnp.inf), TOP_K)
    gate = jnp.take_along_axis(s, idx, axis=-1)
    gate = gate / jnp.sum(gate, axis=-1, keepdims=True) * ROUTED_SCALE

    TK = T * TOP_K
    flat_e = idx.reshape(-1)
    flat_tok = jnp.repeat(jnp.arange(T, dtype=jnp.int32), TOP_K)
    flat_w = gate.reshape(-1).astype(h.dtype)
    order = jnp.argsort(flat_e)
    sorted_e = flat_e[order]
    counts = jnp.bincount(flat_e, length=N_EXPERTS)
    padded = (counts + MOE_BLOCK - 1) // MOE_BLOCK * MOE_BLOCK
    pad_end = jnp.cumsum(padded)
    pad_start = pad_end - padded
    start = jnp.cumsum(counts) - counts
    dest = pad_start[sorted_e] + jnp.arange(TK, dtype=jnp.int32) - start[sorted_e]
    n_pad = -(-TK // MOE_BLOCK) * MOE_BLOCK + N_EXPERTS * MOE_BLOCK
    n_blocks = n_pad // MOE_BLOCK
    slot_tok = jnp.full((n_pad,), T, jnp.int32).at[dest].set(flat_tok[order])
    slot_w = jnp.zeros((n_pad,), h.dtype).at[dest].set(flat_w[order])
    block_e = jnp.minimum(
        jnp.searchsorted(pad_end, jnp.arange(n_blocks, dtype=jnp.int32) * MOE_BLOCK, side='right'),
        N_EXPERTS - 1)
    h_pad = jnp.concatenate([h, jnp.zeros((1, D), h.dtype)], axis=0)

    def expert_block(args):
        tok, wt, e = args
        xb = h_pad[tok]
        return swiglu(xb, w1[e], w3[e], w2[e]) * wt[:, None]

    out = lax.map(expert_block, (slot_tok.reshape(n_blocks, MOE_BLOCK),
                                 slot_w.reshape(n_blocks, MOE_BLOCK), block_e))
    routed = jnp.zeros((T + 1, D), h.dtype).at[slot_tok].add(out.reshape(n_pad, D))[:T]
    return routed + swiglu(h, ws1, ws3, ws2)


def setup_inputs(seed: int = 0) -> dict:
    key = jax.random.key(seed)
    ks = jax.random.split(key, 25)
    f32 = jnp.float32
    L = DEPTH

    def nrm(k, shape, scale):
        return jax.random.normal(k, shape, f32) * scale

    f_bias = jnp.linspace(3.0, 6.0, ML_HEADS, dtype=f32)
    gate_b = jnp.concatenate([nrm(ks[7], (L, ML_HEADS), 0.1),
                              f_bias[None, :] + nrm(ks[8], (L, ML_HEADS), 0.1)], axis=-1)
    return {
        'x': nrm(ks[0], (BATCH, SEQ, D_MODEL), 1.0),
        'c': nrm(ks[1], (BATCH, D_MODEL), 1.0),
        'w_ada': nrm(ks[2], (L, D_MODEL, 6 * D_MODEL), 0.5 * D_MODEL ** -0.5),
        'b_ada': nrm(ks[3], (L, 6 * D_MODEL), 0.02),
        'w_in': nrm(ks[4], (L, D_MODEL, D_IN), D_MODEL ** -0.5),
        'conv_w': nrm(ks[5], (L, CONV_K, 2 * ML_W), CONV_K ** -0.5),
        'conv_b': nrm(ks[6], (L, 2 * ML_W), 0.02),
        'gate_b': gate_b,
        'ml_norm_g': 1.0 + nrm(ks[9], (L, ML_DH), 0.02),
        'da_q_norm_g': 1.0 + nrm(ks[10], (L, DA_DH), 0.02),
        'da_k_norm_g': 1.0 + nrm(ks[11], (L, DA_DH), 0.02),
        'lambda_q1': nrm(ks[12], (L, DA_DH), 0.1),
        'lambda_k1': nrm(ks[13], (L, DA_DH), 0.1),
        'lambda_q2': nrm(ks[14], (L, DA_DH), 0.1),
        'lambda_k2': nrm(ks[15], (L, DA_DH), 0.1),
        'da_norm_g': 1.0 + nrm(ks[16], (L, 2 * DA_DH), 0.02),
        'w_out': nrm(ks[17], (L, D_MIX, D_MODEL), D_MIX ** -0.5),
        'w_router': nrm(ks[18], (L, D_MODEL, N_EXPERTS), D_MODEL ** -0.5),
        'router_bias': nrm(ks[19], (L, N_EXPERTS), 0.01),
        'w1': nrm(ks[20], (L, N_EXPERTS, D_MODEL, D_EXPERT), D_MODEL ** -0.5),
        'w3': nrm(ks[21], (L, N_EXPERTS, D_MODEL, D_EXPERT), D_MODEL ** -0.5),
        'w2': nrm(ks[22], (L, N_EXPERTS, D_EXPERT, D_MODEL), D_EXPERT ** -0.5),
        'ws1': nrm(ks[23], (L, D_MODEL, D_SHARED), D_MODEL ** -0.5),
        'ws3': nrm(ks[24], (L, D_MODEL, D_SHARED), D_MODEL ** -0.5),
        'ws2': nrm(jax.random.fold_in(ks[24], 1), (L, D_SHARED, D_MODEL), D_SHARED ** -0.5),
    }


def reference(x, c, w_ada, b_ada, w_in, conv_w, conv_b, gate_b, ml_norm_g, da_q_norm_g, da_k_norm_g,
              lambda_q1, lambda_k1, lambda_q2, lambda_k2, da_norm_g, w_out, w_router, router_bias,
              w1, w3, w2, ws1, ws3, ws2):
    B, S, D = x.shape
    slopes = alibi_slopes(DA_HEADS)
    for l in range(DEPTH):
        lam_init = lambda_init(l)
        mod = (jax.nn.silu(c) @ w_ada[l] + b_ada[l]).reshape(B, 6, 1, D)
        shift_a, scale_a, gate_a = mod[:, 0], mod[:, 1], mod[:, 2]
        shift_f, scale_f, gate_f = mod[:, 3], mod[:, 4], mod[:, 5]

        h = rms_norm(x) * (1.0 + scale_a) + shift_a
        proj = h @ w_in[l]
        qk_raw, mv, mo, mi, mf, dq, dk, dv = jnp.split(proj, COL_SPLITS, axis=-1)

        qk = jax.nn.silu(causal_depthwise_conv(qk_raw, conv_w[l], conv_b[l]))
        mq, mk = jnp.split(qk, 2, axis=-1)
        hm = mlstm_chunkwise(mq.reshape(B, S, ML_HEADS, ML_DH),
                             mk.reshape(B, S, ML_HEADS, ML_DH),
                             mv.reshape(B, S, ML_HEADS, ML_DH),
                             mi + gate_b[l, :ML_HEADS],
                             mf + gate_b[l, ML_HEADS:])
        hm = rms_norm(hm) * ml_norm_g[l] * jax.nn.sigmoid(mo.reshape(B, S, ML_HEADS, ML_DH).astype(jnp.float32))

        dq = rms_norm(dq.reshape(B, S, DA_HEADS, 2, DA_DH)) * da_q_norm_g[l]
        dk = rms_norm(dk.reshape(B, S, DA_HEADS, 2, DA_DH)) * da_k_norm_g[l]
        lam = (jnp.exp(jnp.sum(lambda_q1[l].astype(jnp.float32) * lambda_k1[l].astype(jnp.float32)))
               - jnp.exp(jnp.sum(lambda_q2[l].astype(jnp.float32) * lambda_k2[l].astype(jnp.float32)))
               + lam_init)
        hd = diff_attention(dq, dk, dv.reshape(B, S, DA_HEADS, 2 * DA_DH), lam, slopes)
        hd = rms_norm(hd) * da_norm_g[l] * (1.0 - lam_init)

        mix = jnp.concatenate([hm.reshape(B, S, ML_W).astype(x.dtype),
                               hd.reshape(B, S, DA_W).astype(x.dtype)], axis=-1) @ w_out[l]
        x = x + gate_a * mix

        h2 = rms_norm(x) * (1.0 + scale_f) + shift_f
        y = routed_moe(h2.reshape(B * S, D), w_router[l], router_bias[l], w1[l], w3[l], w2[l],
                       ws1[l], ws3[l], ws2[l]).reshape(B, S, D)
        x = x + gate_f * y
    return x
```

```python
import functools
import math

import jax
import jax.numpy as jnp
from jax import lax
from jax.experimental import pallas as pl
from jax.experimental.pallas import tpu as pltpu

D_MODEL = 1024
ML_HEADS = 4
ML_DH = 128
ML_W = ML_HEADS * ML_DH
ML_CHUNK = 128
CONV_K = 4
DA_HEADS = 4
DA_DH = 64
DA_W = DA_HEADS * 2 * DA_DH
N_EXPERTS = 256
TOP_K = 8
N_GROUPS = 8
GROUP_SIZE = N_EXPERTS // N_GROUPS
TOPK_GROUPS = 4
D_EXPERT = 256
ROUTED_SCALE = 2.5
EPS = 1e-6
N_GATES = 2 * ML_HEADS

LANES = 128
SUBLANES = 8
VMEM_LIMIT_BYTES = 48 * 1024 * 1024

TM_INPROJ = 512
ATT_BLOCK = 512
TM_ROUTE = 256
TM_MOE = 256
MOE_BLOCK = 256
HALF = D_MODEL // 2

NEG_BIG = -1e30
LOG2E = 1.4426950408889634

f32 = jnp.float32
bf16 = jnp.bfloat16
HIGHEST = lax.Precision.HIGHEST


def _cparams(n_axes):
    return pltpu.CompilerParams(dimension_semantics=("arbitrary",) * n_axes,
                                vmem_limit_bytes=VMEM_LIMIT_BYTES)


def _dot(a, b):
    return jnp.dot(a, b, preferred_element_type=f32)


def _dot_nt(a, b):
    return lax.dot_general(a, b, (((1,), (1,)), ((), ())), preferred_element_type=f32)


def _sigmoid(x):
    return 1.0 / (1.0 + jnp.exp(-x))


def _silu(x):
    return x * _sigmoid(x)


def _log_sigmoid(x):
    return jnp.minimum(x, 0.0) - jnp.log(1.0 + jnp.exp(-jnp.abs(x)))


def _pack_bf16_pair(lo, hi):
    lo_bits = lax.bitcast_convert_type(lo.astype(bf16).astype(f32), jnp.uint32)
    hi_bits = lax.bitcast_convert_type(hi.astype(bf16).astype(f32), jnp.uint32)
    return (hi_bits & jnp.uint32(0xFFFF0000)) | (lo_bits >> 16)


def _unpack_bf16_pair(w):
    lo = lax.bitcast_convert_type(w << 16, f32)
    hi = lax.bitcast_convert_type(w & jnp.uint32(0xFFFF0000), f32)
    return lo, hi


def _adaln_kernel(c_ref, w_ref, b_ref, o_ref):
    c = c_ref[...]
    o_ref[...] = jnp.dot(_silu(c), w_ref[...], precision=HIGHEST, preferred_element_type=f32) + b_ref[...]


def _adaln(c_pad, w_ada, b_ada):
    rows, d = c_pad.shape
    n = w_ada.shape[1]
    tn = 1536
    return pl.pallas_call(
        _adaln_kernel,
        out_shape=jax.ShapeDtypeStruct((rows, n), f32),
        grid=(n // tn,),
        in_specs=[pl.BlockSpec((rows, d), lambda j: (0, 0)),
                  pl.BlockSpec((d, tn), lambda j: (0, j)),
                  pl.BlockSpec((1, tn), lambda j: (0, j))],
        out_specs=pl.BlockSpec((rows, tn), lambda j: (0, j)),
        compiler_params=_cparams(1),
        name="adaln",
    )(c_pad, w_ada, b_ada)


def _inproj_kernel(x_ref, mod_ref, w_ref, wg_ref, wgt_ref, gbr_ref, gbc_ref, qg_ref, kg_ref, bd_ref,
                   pm_ref, q_ref, k_ref, v_ref, gcol_ref, grow_ref):
    x = x_ref[...]
    h = x * lax.rsqrt(jnp.mean(x * x, axis=-1, keepdims=True) + EPS)
    h = h * (1.0 + mod_ref[0, 1:2, :]) + mod_ref[0, 0:1, :]
    hb = h.astype(bf16)
    cw = ML_W
    for n in range(4):
        pm_ref[:, n * cw:(n + 1) * cw] = _dot(hb, w_ref[:, n * cw:(n + 1) * cw]).astype(bf16)
    for n, g_ref, o_ref in ((4, qg_ref, q_ref), (5, kg_ref, k_ref)):
        y = _dot(hb, w_ref[:, n * cw:(n + 1) * cw])
        ss = _dot((y * y).astype(bf16), bd_ref[...])
        o_ref[...] = (y * lax.rsqrt(ss * (1.0 / DA_DH) + EPS) * g_ref[...]).astype(bf16)
    v_ref[...] = _dot(hb, w_ref[:, 6 * cw:7 * cw]).astype(bf16)
    gcol_ref[...] = _dot(hb, wg_ref[...]) + gbr_ref[...]
    grow_ref[...] = _dot_nt(wgt_ref[...], hb) + gbc_ref[...]


def _inproj(x2, mod, w_main, wg, wgt, gb_row, gb_col, qgain, kgain, bd, seq):
    t, d = x2.shape
    tm = TM_INPROJ
    tiles_per_seq = seq // tm
    n_main = w_main.shape[1]
    const = lambda shape: pl.BlockSpec(shape, lambda i: (0,) * len(shape))
    return pl.pallas_call(
        _inproj_kernel,
        out_shape=(jax.ShapeDtypeStruct((t, 4 * ML_W), bf16),
                   jax.ShapeDtypeStruct((t, DA_W), bf16),
                   jax.ShapeDtypeStruct((t, DA_W), bf16),
                   jax.ShapeDtypeStruct((t, DA_W), bf16),
                   jax.ShapeDtypeStruct((t, LANES), f32),
                   jax.ShapeDtypeStruct((N_GATES, t), f32)),
        grid=(t // tm,),
        in_specs=[pl.BlockSpec((tm, d), lambda i: (i, 0)),
                  pl.BlockSpec((1, 6, d), lambda i: (i // tiles_per_seq, 0, 0)),
                  const((d, n_main)), const((d, LANES)), const((N_GATES, d)),
                  const((1, LANES)), const((N_GATES, 1)),
                  const((1, DA_W)), const((1, DA_W)), const((DA_W, DA_W))],
        out_specs=(pl.BlockSpec((tm, 4 * ML_W), lambda i: (i, 0)),
                   pl.BlockSpec((tm, DA_W), lambda i: (i, 0)),
                   pl.BlockSpec((tm, DA_W), lambda i: (i, 0)),
                   pl.BlockSpec((tm, DA_W), lambda i: (i, 0)),
                   pl.BlockSpec((tm, LANES), lambda i: (i, 0)),
                   pl.BlockSpec((N_GATES, tm), lambda i: (0, i))),
        compiler_params=_cparams(1),
        name="inproj",
    )(x2, mod, w_main, wg, wgt, gb_row, gb_col, qgain, kgain, bd)


def _mlstm_kernel(qk_ref, v_ref, o_ref, gcol_ref, grow_ref, cw_ref, cb_ref, ng_ref,
                  out_ref, state_ref, m_ref, prev_ref, ext_ref):
    L = ML_CHUNK
    dh = ML_DH
    c = pl.program_id(1)

    @pl.when(c == 0)
    def _():
        state_ref[...] = jnp.zeros_like(state_ref)
        m_ref[...] = jnp.zeros_like(m_ref)
        prev_ref[...] = jnp.zeros_like(prev_ref)

    cur = qk_ref[...].astype(f32)
    ext_ref[0:SUBLANES, :] = prev_ref[...]
    ext_ref[SUBLANES:SUBLANES + L, :] = cur
    prev_ref[...] = cur[L - SUBLANES:L, :]
    acc = cb_ref[...] + cw_ref[CONV_K - 1:CONV_K, :] * cur
    for j in range(CONV_K - 1):
        off = SUBLANES - (CONV_K - 1) + j
        acc = acc + cw_ref[j:j + 1, :] * ext_ref[off:off + L, :]
    qk = _silu(acc)

    row_i = lax.broadcasted_iota(jnp.int32, (L, L), 0)
    col_i = lax.broadcasted_iota(jnp.int32, (L, L), 1)
    causal = row_i >= col_i
    tril = causal.astype(f32)
    triu = (row_i <= col_i).astype(f32)
    gcol = gcol_ref[...]
    grow = grow_ref[...]
    bcol_all = jnp.dot(tril, _log_sigmoid(gcol), precision=HIGHEST, preferred_element_type=f32)
    brow_all = jnp.dot(_log_sigmoid(grow), triu, precision=HIGHEST, preferred_element_type=f32)

    lane = lax.broadcasted_iota(jnp.int32, (L, dh), 1)
    ones_col = (lane == 0).astype(f32)
    g = ng_ref[...]

    for hd in range(ML_HEADS):
        q = qk[:, hd * dh:(hd + 1) * dh]
        k = qk[:, ML_W + hd * dh:ML_W + (hd + 1) * dh] * (dh ** -0.5)
        v = v_ref[:, hd * dh:(hd + 1) * dh].astype(f32)
        v_aug = jnp.concatenate([v, ones_col], axis=1)
        qb = q.astype(bf16)
        kb = k.astype(bf16)

        bcol = bcol_all[:, ML_HEADS + hd:ML_HEADS + hd + 1]
        icol = gcol[:, hd:hd + 1]
        brow = brow_all[ML_HEADS + hd:ML_HEADS + hd + 1, :]
        irow = grow[hd:hd + 1, :]
        m_prev = m_ref[hd:hd + 1, 0:1]

        d_intra = jnp.where(causal, bcol - brow + irow, NEG_BIG)
        d_inter = bcol + m_prev
        m_t = jnp.maximum(d_inter, jnp.max(d_intra, axis=1, keepdims=True))
        w_intra = jnp.exp(d_intra - m_t)
        w_inter = jnp.exp(d_inter - m_t)

        s = _dot_nt(qb, kb) * w_intra
        state = state_ref[hd]
        q_state = _dot(qb, state.astype(bf16))
        s_v = _dot(s.astype(bf16), v_aug.astype(bf16))
        num = w_inter * q_state[:, 0:dh] + s_v[:, 0:dh]
        den = w_inter * q_state[:, dh:dh + 1] + s_v[:, dh:dh + 1]
        hval = num / jnp.maximum(jnp.abs(den), jnp.exp(-m_t))

        b_last = bcol[L - 1:L, :]
        d_state = b_last - bcol + icol
        m_new = jnp.maximum(b_last + m_prev, jnp.max(d_state, axis=0, keepdims=True))
        carry_scale = jnp.exp(b_last + m_prev - m_new)
        wk = jnp.exp(d_state - m_new)
        upd = _dot(k.T.astype(bf16), (v_aug * wk).astype(bf16))
        state_ref[hd] = carry_scale * state + upd
        m_ref[hd:hd + 1, :] = jnp.broadcast_to(m_new, (1, LANES))

        hn = hval * lax.rsqrt(jnp.mean(hval * hval, axis=-1, keepdims=True) + EPS) * g
        og = _sigmoid(o_ref[:, hd * dh:(hd + 1) * dh].astype(f32))
        out_ref[:, hd * dh:(hd + 1) * dh] = (hn * og).astype(bf16)


def _mlstm(pm, gcol, grow, conv_w, conv_b, ml_norm_g, batch, seq):
    t = pm.shape[0]
    L = ML_CHUNK
    nc = seq // L
    row = lambda b, c: b * nc + c
    return pl.pallas_call(
        _mlstm_kernel,
        out_shape=jax.ShapeDtypeStruct((t, ML_W), bf16),
        grid=(batch, nc),
        in_specs=[pl.BlockSpec((L, 2 * ML_W), lambda b, c: (row(b, c), 0)),
                  pl.BlockSpec((L, ML_W), lambda b, c: (row(b, c), 2)),
                  pl.BlockSpec((L, ML_W), lambda b, c: (row(b, c), 3)),
                  pl.BlockSpec((L, LANES), lambda b, c: (row(b, c), 0)),
                  pl.BlockSpec((N_GATES, L), lambda b, c: (0, row(b, c))),
                  pl.BlockSpec((CONV_K, 2 * ML_W), lambda b, c: (0, 0)),
                  pl.BlockSpec((1, 2 * ML_W), lambda b, c: (0, 0)),
                  pl.BlockSpec((1, ML_DH), lambda b, c: (0, 0))],
        out_specs=pl.BlockSpec((L, ML_W), lambda b, c: (row(b, c), 0)),
        scratch_shapes=[pltpu.VMEM((ML_HEADS, ML_DH, 2 * ML_DH), f32),
                        pltpu.VMEM((SUBLANES, LANES), f32),
                        pltpu.VMEM((SUBLANES, 2 * ML_W), f32),
                        pltpu.VMEM((SUBLANES + ML_CHUNK, 2 * ML_W), f32)],
        compiler_params=_cparams(2),
        name="mlstm",
    )(pm, pm, pm, gcol, grow, conv_w, conv_b, ml_norm_g)


def _attn_kernel(qt_ref, kt_ref, slope_ref, q_ref, k_ref, v_ref, lam_ref, g_ref, o_ref,
                 m_ref, l_ref, acc_ref, *, lam_init):
    blk = ATT_BLOCK
    hd = pl.program_id(1)
    step = pl.program_id(2)
    qi = qt_ref[step]
    ki = kt_ref[step]

    @pl.when(ki == 0)
    def _():
        m_ref[...] = jnp.full_like(m_ref, NEG_BIG)
        l_ref[...] = jnp.zeros_like(l_ref)
        acc_ref[...] = jnp.zeros_like(acc_ref)

    q = q_ref[...]
    k = k_ref[...]
    v = v_ref[...]
    lane = lax.broadcasted_iota(jnp.int32, q.shape, 1)
    zero = jnp.zeros_like(q)
    qmaps = (jnp.where(lane < DA_DH, q, zero), jnp.where(lane >= DA_DH, q, zero))

    kpos = ki * blk + lax.broadcasted_iota(jnp.int32, (1, blk), 1)
    bias = slope_ref[hd] * kpos.astype(f32)
    qpos = qi * blk + lax.broadcasted_iota(jnp.int32, (blk, 1), 0)
    keep = kpos <= qpos

    for c in range(2):
        s = _dot_nt(qmaps[c], k) + bias
        s = jnp.where(keep, s, NEG_BIG)
        m_old = m_ref[c]
        m_new = jnp.maximum(m_old, jnp.max(s, axis=1, keepdims=True))
        alpha = jnp.exp2(m_old - m_new)
        p = jnp.exp2(s - m_new)
        l_ref[c] = alpha * l_ref[c] + jnp.sum(p, axis=1, keepdims=True)
        acc_ref[c] = alpha * acc_ref[c] + _dot(p.astype(bf16), v)
        m_ref[c] = m_new

    @pl.when(ki == qi)
    def _():
        lam = lam_ref[...]
        o = acc_ref[0] / l_ref[0] - lam * (acc_ref[1] / l_ref[1])
        o = o * lax.rsqrt(jnp.mean(o * o, axis=-1, keepdims=True) + EPS)
        o_ref[...] = (o * g_ref[...] * (1.0 - lam_init)).astype(bf16)


def _attention(qh, kh, vh, lam, da_norm_g, slopes_l2, batch, seq, lam_init):
    t = qh.shape[0]
    blk = ATT_BLOCK
    nb = seq // blk
    pairs = [(i, j) for i in range(nb) for j in range(i + 1)]
    qt = jnp.asarray([p[0] for p in pairs], jnp.int32)
    kt = jnp.asarray([p[1] for p in pairs], jnp.int32)
    w = 2 * DA_DH
    grid_spec = pltpu.PrefetchScalarGridSpec(
        num_scalar_prefetch=3,
        grid=(batch, DA_HEADS, len(pairs)),
        in_specs=[pl.BlockSpec((blk, w), lambda b, h, s, qt, kt, sl: (b * nb + qt[s], h)),
                  pl.BlockSpec((blk, w), lambda b, h, s, qt, kt, sl: (b * nb + kt[s], h)),
                  pl.BlockSpec((blk, w), lambda b, h, s, qt, kt, sl: (b * nb + kt[s], h)),
                  pl.BlockSpec((1, 1), lambda b, h, s, qt, kt, sl: (0, 0)),
                  pl.BlockSpec((1, w), lambda b, h, s, qt, kt, sl: (0, 0))],
        out_specs=pl.BlockSpec((blk, w), lambda b, h, s, qt, kt, sl: (b * nb + qt[s], h)),
        scratch_shapes=[pltpu.VMEM((2, blk, 1), f32),
                        pltpu.VMEM((2, blk, 1), f32),
                        pltpu.VMEM((2, blk, w), f32)],
    )
    return pl.pallas_call(
        functools.partial(_attn_kernel, lam_init=lam_init),
        out_shape=jax.ShapeDtypeStruct((t, DA_W), bf16),
        grid_spec=grid_spec,
        compiler_params=_cparams(3),
        name="attn",
    )(qt, kt, slopes_l2, qh, kh, vh, lam, da_norm_g)


def _first_index_of_max(x, iota_f, size):
    m = jnp.max(x, axis=0, keepdims=True)
    idx = jnp.min(jnp.where(x == m, iota_f, float(size)), axis=0, keepdims=True)
    return m, idx


def _route_kernel(hm_ref, hd_ref, x_ref, mod_ref, wo_ref, wr_ref, rb_ref,
                  x1_ref, h2p_ref, idx_ref, gate_ref, rank_ref, cnt_ref, base_ref):
    tm = TM_ROUTE
    i = pl.program_id(0)

    @pl.when(i == 0)
    def _():
        base_ref[...] = jnp.zeros_like(base_ref)

    mix = _dot(hm_ref[...], wo_ref[0:ML_W, :]) + _dot(hd_ref[...], wo_ref[ML_W:ML_W + DA_W, :])
    x1 = x_ref[...] + mod_ref[0, 2:3, :] * mix
    x1_ref[...] = x1
    h2 = x1 * lax.rsqrt(jnp.mean(x1 * x1, axis=-1, keepdims=True) + EPS)
    h2 = h2 * (1.0 + mod_ref[0, 4:5, :]) + mod_ref[0, 3:4, :]
    h2p_ref[...] = _pack_bf16_pair(h2[:, 0:HALF], h2[:, HALF:D_MODEL])

    sc = _sigmoid(_dot_nt(wr_ref[...], h2.astype(bf16)))
    sel = sc + rb_ref[...]

    gi = lax.broadcasted_iota(jnp.int32, (GROUP_SIZE, tm), 0).astype(f32)
    gscores = []
    for g in range(N_GROUPS):
        blk = sel[g * GROUP_SIZE:(g + 1) * GROUP_SIZE, :]
        m1, i1 = _first_index_of_max(blk, gi, GROUP_SIZE)
        m2 = jnp.max(jnp.where(gi == i1, NEG_BIG, blk), axis=0, keepdims=True)
        gscores.append(m1 + m2)
    gs = jnp.concatenate(gscores, axis=0)

    ngi = lax.broadcasted_iota(jnp.int32, (N_GROUPS, tm), 0).astype(f32)
    gkeep = jnp.zeros((N_GROUPS, tm), f32)
    for _ in range(TOPK_GROUPS):
        _, gidx = _first_index_of_max(gs, ngi, N_GROUPS)
        hit = ngi == gidx
        gkeep = jnp.where(hit, 1.0, gkeep)
        gs = jnp.where(hit, NEG_BIG, gs)
    masked = jnp.concatenate(
        [jnp.where(gkeep[g:g + 1, :] > 0.0, sel[g * GROUP_SIZE:(g + 1) * GROUP_SIZE, :], NEG_BIG)
         for g in range(N_GROUPS)], axis=0)

    ei = lax.broadcasted_iota(jnp.int32, (N_EXPERTS, tm), 0).astype(f32)
    idxs, gates = [], []
    chosen = jnp.zeros((N_EXPERTS, tm), f32)
    for _ in range(TOP_K):
        _, eidx = _first_index_of_max(masked, ei, N_EXPERTS)
        hit = ei == eidx
        idxs.append(eidx)
        gates.append(jnp.sum(jnp.where(hit, sc, 0.0), axis=0, keepdims=True))
        chosen = jnp.where(hit, 1.0, chosen)
        masked = jnp.where(hit, NEG_BIG, masked)
    gate = jnp.concatenate(gates, axis=0)
    gate = gate / jnp.sum(gate, axis=0, keepdims=True) * ROUTED_SCALE
    gate_ref[...] = gate
    idx_ref[...] = jnp.concatenate(idxs, axis=0).astype(jnp.int32)

    tr = lax.broadcasted_iota(jnp.int32, (tm, tm), 0)
    tc = lax.broadcasted_iota(jnp.int32, (tm, tm), 1)
    before = (tr < tc).astype(bf16)
    seen = base_ref[...] + _dot(chosen.astype(bf16), before)
    ranks = [jnp.sum(jnp.where(ei == eidx, seen, 0.0), axis=0, keepdims=True) for eidx in idxs]
    rank_ref[...] = jnp.concatenate(ranks, axis=0).astype(jnp.int32)
    total = base_ref[...] + jnp.sum(chosen, axis=1, keepdims=True)
    base_ref[...] = total
    cnt_ref[...] = total


def _route(hm, hd, x2, mod, w_out, w_router_t, rbias_col, seq):
    t, d = x2.shape
    tm = TM_ROUTE
    tiles_per_seq = seq // tm
    const = lambda shape: pl.BlockSpec(shape, lambda i: (0,) * len(shape))
    return pl.pallas_call(
        _route_kernel,
        out_shape=(jax.ShapeDtypeStruct((t, d), f32),
                   jax.ShapeDtypeStruct((t, HALF), jnp.uint32),
                   jax.ShapeDtypeStruct((TOP_K, t), jnp.int32),
                   jax.ShapeDtypeStruct((TOP_K, t), f32),
                   jax.ShapeDtypeStruct((TOP_K, t), jnp.int32),
                   jax.ShapeDtypeStruct((N_EXPERTS, 1), f32)),
        grid=(t // tm,),
        in_specs=[pl.BlockSpec((tm, ML_W), lambda i: (i, 0)),
                  pl.BlockSpec((tm, DA_W), lambda i: (i, 0)),
                  pl.BlockSpec((tm, d), lambda i: (i, 0)),
                  pl.BlockSpec((1, 6, d), lambda i: (i // tiles_per_seq, 0, 0)),
                  const((d, d)), const((N_EXPERTS, d)), const((N_EXPERTS, 1))],
        out_specs=(pl.BlockSpec((tm, d), lambda i: (i, 0)),
                   pl.BlockSpec((tm, HALF), lambda i: (i, 0)),
                   pl.BlockSpec((TOP_K, tm), lambda i: (0, i)),
                   pl.BlockSpec((TOP_K, tm), lambda i: (0, i)),
                   pl.BlockSpec((TOP_K, tm), lambda i: (0, i)),
                   const((N_EXPERTS, 1))),
        scratch_shapes=[pltpu.VMEM((N_EXPERTS, 1), f32)],
        compiler_params=_cparams(1),
        name="route",
    )(hm, hd, x2, mod, w_out, w_router_t, rbias_col)


def _row_copy(src_ref, src_row, dst_ref, dst_row, sem):
    return pltpu.make_async_copy(src_ref.at[pl.ds(src_row, 1)], dst_ref.at[pl.ds(dst_row, 1)], sem)


def _dispatch_kernel(dest_ref, h_ref, xs_in_ref, xs_ref, sem):
    del xs_in_ref
    tm = TM_MOE

    def issue(tok, carry):
        for k in range(TOP_K):
            _row_copy(h_ref, tok, xs_ref, dest_ref[k, tok], sem).start()
        return carry

    lax.fori_loop(0, tm, issue, 0)

    def drain(tok, carry):
        for k in range(TOP_K):
            _row_copy(h_ref, tok, xs_ref, dest_ref[k, tok], sem).wait()
        return carry

    lax.fori_loop(0, tm, drain, 0)


def _dispatch(dest, h2p, xs_zero):
    t = h2p.shape[0]
    tm = TM_MOE
    return pl.pallas_call(
        _dispatch_kernel,
        out_shape=jax.ShapeDtypeStruct(xs_zero.shape, xs_zero.dtype),
        grid=(t // tm,),
        in_specs=[pl.BlockSpec((TOP_K, tm), lambda i: (0, i), memory_space=pltpu.SMEM),
                  pl.BlockSpec((tm, HALF), lambda i: (i, 0)),
                  pl.BlockSpec(memory_space=pl.ANY)],
        out_specs=pl.BlockSpec(memory_space=pl.ANY),
        scratch_shapes=[pltpu.SemaphoreType.DMA(())],
        input_output_aliases={2: 0},
        compiler_params=_cparams(1),
        name="dispatch",
    )(dest, h2p, xs_zero)


def _expert_kernel(be_ref, nu_ref, xs_ref, w1_ref, w3_ref, w2_ref, ys_ref, w1b, w3b, w2b):
    i = pl.program_id(0)
    active = i < nu_ref[0]

    @pl.when(active)
    def _():
        changed = jnp.logical_or(i == 0, be_ref[i] != be_ref[jnp.maximum(i - 1, 0)])

        @pl.when(changed)
        def _():
            w1b[...] = w1_ref[0].astype(bf16)
            w3b[...] = w3_ref[0].astype(bf16)
            w2b[...] = w2_ref[0].astype(bf16)

        lo, hi = _unpack_bf16_pair(xs_ref[...])
        lo = lo.astype(bf16)
        hi = hi.astype(bf16)
        h1 = _dot(lo, w1b[0:HALF, :]) + _dot(hi, w1b[HALF:D_MODEL, :])
        h3 = _dot(lo, w3b[0:HALF, :]) + _dot(hi, w3b[HALF:D_MODEL, :])
        y = _dot((_silu(h1) * h3).astype(bf16), w2b[...])
        ys_ref[...] = _pack_bf16_pair(y[:, 0:HALF], y[:, HALF:D_MODEL])

    @pl.when(jnp.logical_not(active))
    def _():
        ys_ref[...] = jnp.zeros_like(ys_ref)


def _experts(block_expert, n_used, xs, w1, w3, w2):
    n_pad = xs.shape[0]
    bm = MOE_BLOCK
    last = lambda i, nu: jnp.minimum(i, nu[0] - 1)
    grid_spec = pltpu.PrefetchScalarGridSpec(
        num_scalar_prefetch=2,
        grid=(n_pad // bm,),
        in_specs=[pl.BlockSpec((bm, HALF), lambda i, be, nu: (last(i, nu), 0)),
                  pl.BlockSpec((1, D_MODEL, D_EXPERT), lambda i, be, nu: (be[last(i, nu)], 0, 0)),
                  pl.BlockSpec((1, D_MODEL, D_EXPERT), lambda i, be, nu: (be[last(i, nu)], 0, 0)),
                  pl.BlockSpec((1, D_EXPERT, D_MODEL), lambda i, be, nu: (be[last(i, nu)], 0, 0))],
        out_specs=pl.BlockSpec((bm, HALF), lambda i, be, nu: (i, 0)),
        scratch_shapes=[pltpu.VMEM((D_MODEL, D_EXPERT), bf16),
                        pltpu.VMEM((D_MODEL, D_EXPERT), bf16),
                        pltpu.VMEM((D_EXPERT, D_MODEL), bf16)],
    )
    return pl.pallas_call(
        _expert_kernel,
        out_shape=jax.ShapeDtypeStruct((n_pad, HALF), jnp.uint32),
        grid_spec=grid_spec,
        compiler_params=_cparams(1),
        name="experts",
    )(block_expert, n_used, xs, w1, w3, w2)


def _combine_kernel(dest_ref, gate_ref, h_ref, x1_ref, mod_ref, ws1_ref, ws3_ref, ws2_ref, ys_ref,
                    out_ref, buf_ref, sem):
    tm = TM_MOE

    def issue(tok, carry):
        for k in range(TOP_K):
            _row_copy(ys_ref, dest_ref[k, tok], buf_ref.at[k], tok, sem).start()
        return carry

    lax.fori_loop(0, tm, issue, 0)

    lo, hi = _unpack_bf16_pair(h_ref[...])
    lo = lo.astype(bf16)
    hi = hi.astype(bf16)
    s1 = _dot(lo, ws1_ref[0:HALF, :]) + _dot(hi, ws1_ref[HALF:D_MODEL, :])
    s3 = _dot(lo, ws3_ref[0:HALF, :]) + _dot(hi, ws3_ref[HALF:D_MODEL, :])
    y = _dot((_silu(s1) * s3).astype(bf16), ws2_ref[...])

    def drain(tok, carry):
        for k in range(TOP_K):
            _row_copy(ys_ref, dest_ref[k, tok], buf_ref.at[k], tok, sem).wait()
        return carry

    lax.fori_loop(0, tm, drain, 0)

    gate = gate_ref[...]
    acc_lo = y[:, 0:HALF]
    acc_hi = y[:, HALF:D_MODEL]
    for k in range(TOP_K):
        rlo, rhi = _unpack_bf16_pair(buf_ref[k])
        gk = gate[:, k:k + 1]
        acc_lo = acc_lo + gk * rlo
        acc_hi = acc_hi + gk * rhi
    gate_f = mod_ref[0, 5:6, :]
    out_ref[:, 0:HALF] = x1_ref[:, 0:HALF] + gate_f[:, 0:HALF] * acc_lo
    out_ref[:, HALF:D_MODEL] = x1_ref[:, HALF:D_MODEL] + gate_f[:, HALF:D_MODEL] * acc_hi


def _combine(dest, gate_col, h2p, x1, mod, ws1, ws3, ws2, ys, seq):
    t, d = x1.shape
    tm = TM_MOE
    tiles_per_seq = seq // tm
    const = lambda shape: pl.BlockSpec(shape, lambda i: (0,) * len(shape))
    return pl.pallas_call(
        _combine_kernel,
        out_shape=jax.ShapeDtypeStruct((t, d), f32),
        grid=(t // tm,),
        in_specs=[pl.BlockSpec((TOP_K, tm), lambda i: (0, i), memory_space=pltpu.SMEM),
                  pl.BlockSpec((tm, TOP_K), lambda i: (i, 0)),
                  pl.BlockSpec((tm, HALF), lambda i: (i, 0)),
                  pl.BlockSpec((tm, d), lambda i: (i, 0)),
                  pl.BlockSpec((1, 6, d), lambda i: (i // tiles_per_seq, 0, 0)),
                  const((d, D_EXPERT)), const((d, D_EXPERT)), const((D_EXPERT, d)),
                  pl.BlockSpec(memory_space=pl.ANY)],
        out_specs=pl.BlockSpec((tm, d), lambda i: (i, 0)),
        scratch_shapes=[pltpu.VMEM((TOP_K, tm, HALF), jnp.uint32),
                        pltpu.SemaphoreType.DMA(())],
        compiler_params=_cparams(1),
        name="combine",
    )(dest, gate_col, h2p, x1, mod, ws1, ws3, ws2, ys)


def _lambda_init(layer):
    return 0.8 - 0.6 * math.exp(-0.3 * layer)


def _layer(x, c, w_ada, b_ada, w_in, conv_w, conv_b, gate_b, ml_norm_g, da_q_norm_g, da_k_norm_g,
           lambda_q1, lambda_k1, lambda_q2, lambda_k2, da_norm_g, w_out, w_router, router_bias,
           w1, w3, w2, ws1, ws3, ws2, layer):
    batch, seq, d = x.shape
    t = batch * seq
    lam_init = _lambda_init(layer)
    x2 = x.reshape(t, d)

    c_pad = jnp.pad(c, ((0, -batch % SUBLANES), (0, 0)))
    mod = _adaln(c_pad, w_ada, b_ada.reshape(1, -1))[:batch].reshape(batch, 6, d)

    g0 = 4 * ML_W
    w_main = jnp.concatenate([w_in[:, :g0], w_in[:, g0 + N_GATES:]], axis=1).astype(bf16)
    w_gate = w_in[:, g0:g0 + N_GATES]
    wg = jnp.pad(w_gate, ((0, 0), (0, LANES - N_GATES))).astype(bf16)
    wgt = w_gate.T.astype(bf16)
    gb_row = jnp.pad(gate_b, (0, LANES - N_GATES)).reshape(1, LANES)
    gb_col = gate_b.reshape(N_GATES, 1)
    reps = DA_W // DA_DH
    qgain = (jnp.tile(da_q_norm_g, reps) * (DA_DH ** -0.5 * LOG2E)).reshape(1, DA_W)
    kgain = jnp.tile(da_k_norm_g, reps).reshape(1, DA_W)
    seg = jnp.arange(DA_W) // DA_DH
    bd = (seg[:, None] == seg[None, :]).astype(bf16)

    pm, qh, kh, vh, gcol, grow = _inproj(x2, mod, w_main, wg, wgt, gb_row, gb_col, qgain, kgain, bd, seq)

    hm = _mlstm(pm, gcol, grow, conv_w, conv_b.reshape(1, -1), ml_norm_g.reshape(1, -1), batch, seq)

    lam = (jnp.exp(jnp.sum(lambda_q1 * lambda_k1)) - jnp.exp(jnp.sum(lambda_q2 * lambda_k2))
           + lam_init).reshape(1, 1).astype(f32)
    slopes_l2 = (2.0 ** (-8.0 * jnp.arange(1, DA_HEADS + 1, dtype=f32) / DA_HEADS)) * LOG2E
    hd = _attention(qh, kh, vh, lam, da_norm_g.reshape(1, -1), slopes_l2, batch, seq, lam_init)

    x1, h2p, idx, gate, rank, counts = _route(hm, hd, x2, mod, w_out.astype(bf16),
                                              w_router.T.astype(bf16), router_bias.reshape(-1, 1), seq)

    bm = MOE_BLOCK
    n_blocks = (t * TOP_K) // bm + N_EXPERTS
    counts_i = counts.reshape(-1).astype(jnp.int32)
    padded = (counts_i + bm - 1) // bm * bm
    pad_end = jnp.cumsum(padded)
    pad_start = pad_end - padded
    dest = pad_start[idx] + rank
    block_expert = jnp.minimum(
        jnp.searchsorted(pad_end, jnp.arange(n_blocks, dtype=jnp.int32) * bm, side='right'),
        N_EXPERTS - 1).astype(jnp.int32)
    n_used = (pad_end[-1:] // bm).astype(jnp.int32)

    xs = _dispatch(dest, h2p, jnp.zeros((n_blocks * bm, HALF), jnp.uint32))
    ys = _experts(block_expert, n_used, xs, w1, w3, w2)
    out = _combine(dest, gate.T, h2p, x1, mod, ws1.astype(bf16), ws3.astype(bf16), ws2.astype(bf16), ys, seq)
    return out.reshape(batch, seq, d)


def kernel(x, c, w_ada, b_ada, w_in, conv_w, conv_b, gate_b, ml_norm_g, da_q_norm_g, da_k_norm_g,
           lambda_q1, lambda_k1, lambda_q2, lambda_k2, da_norm_g, w_out, w_router, router_bias,
           w1, w3, w2, ws1, ws3, ws2):
    depth = w_ada.shape[0]
    for l in range(depth):
        x = _layer(x, c, w_ada[l], b_ada[l], w_in[l], conv_w[l], conv_b[l], gate_b[l], ml_norm_g[l],
                   da_q_norm_g[l], da_k_norm_g[l], lambda_q1[l], lambda_k1[l], lambda_q2[l], lambda_k2[l],
                   da_norm_g[l], w_out[l], w_router[l], router_bias[l], w1[l], w3[l], w2[l],
                   ws1[l], ws3[l], ws2[l], l)
    return x
```

```python
import functools
import math

import jax
import jax.numpy as jnp
from jax import lax
from jax.experimental import pallas as pl
from jax.experimental.pallas import tpu as pltpu

D_MODEL = 1024
ML_HEADS = 4
ML_DH = 128
ML_W = ML_HEADS * ML_DH
ML_CHUNK = 128
CONV_K = 4
DA_HEADS = 4
DA_DH = 64
DA_W = DA_HEADS * 2 * DA_DH
N_EXPERTS = 256
TOP_K = 8
N_GROUPS = 8
GROUP_SIZE = N_EXPERTS // N_GROUPS
TOPK_GROUPS = 4
D_EXPERT = 256
ROUTED_SCALE = 2.5
EPS = 1e-6
N_GATES = 2 * ML_HEADS

LANES = 128
SUBLANES = 8
VMEM_LIMIT_BYTES = 48 * 1024 * 1024

TM_INPROJ = 512
ATT_BLOCK = 1024
TM_ROUTE = 256
TM_SLOTS = 512
TM_MOE = 256
MOE_BLOCK = 256
HALF = D_MODEL // 2

NEG_BIG = -1e30
LOG2E = 1.4426950408889634

f32 = jnp.float32
bf16 = jnp.bfloat16
HIGHEST = lax.Precision.HIGHEST


def _cparams(n_axes):
    return pltpu.CompilerParams(dimension_semantics=("arbitrary",) * n_axes,
                                vmem_limit_bytes=VMEM_LIMIT_BYTES)


def _dot(a, b):
    return jnp.dot(a, b, preferred_element_type=f32)


def _dot_nt(a, b):
    return lax.dot_general(a, b, (((1,), (1,)), ((), ())), preferred_element_type=f32)


def _sigmoid(x):
    return 1.0 / (1.0 + jnp.exp(-x))


def _silu(x):
    return x * _sigmoid(x)


def _log_sigmoid(x):
    return jnp.minimum(x, 0.0) - jnp.log(1.0 + jnp.exp(-jnp.abs(x)))


def _pack_bf16_pair(lo, hi):
    lo_bits = lax.bitcast_convert_type(lo.astype(bf16).astype(f32), jnp.uint32)
    hi_bits = lax.bitcast_convert_type(hi.astype(bf16).astype(f32), jnp.uint32)
    return (hi_bits & jnp.uint32(0xFFFF0000)) | (lo_bits >> 16)


def _unpack_bf16_pair(w):
    lo = lax.bitcast_convert_type(w << 16, f32)
    hi = lax.bitcast_convert_type(w & jnp.uint32(0xFFFF0000), f32)
    return lo, hi


def _adaln_kernel(c_ref, w_ref, b_ref, o_ref):
    c = c_ref[...]
    o_ref[...] = jnp.dot(_silu(c), w_ref[...], precision=HIGHEST, preferred_element_type=f32) + b_ref[...]


def _adaln(c_pad, w_ada, b_ada):
    rows, d = c_pad.shape
    n = w_ada.shape[1]
    tn = 1536
    return pl.pallas_call(
        _adaln_kernel,
        out_shape=jax.ShapeDtypeStruct((rows, n), f32),
        grid=(n // tn,),
        in_specs=[pl.BlockSpec((rows, d), lambda j: (0, 0)),
                  pl.BlockSpec((d, tn), lambda j: (0, j)),
                  pl.BlockSpec((1, tn), lambda j: (0, j))],
        out_specs=pl.BlockSpec((rows, tn), lambda j: (0, j)),
        compiler_params=_cparams(1),
        name="adaln",
    )(c_pad, w_ada, b_ada)


def _alibi_slope_log2(head):
    return 2.0 ** (-8.0 * (head + 1) / DA_HEADS) * LOG2E


def _inproj_kernel(x_ref, mod_ref, w_ref, wg_ref, wgt_ref, gbr_ref, gbc_ref, qg_ref, kg_ref,
                   pm_ref, q_ref, k_ref, v_ref, gcol_ref, grow_ref, *, seq):
    tm = TM_INPROJ
    x = x_ref[...]
    h = x * lax.rsqrt(jnp.mean(x * x, axis=-1, keepdims=True) + EPS)
    h = h * (1.0 + mod_ref[0, 1:2, :]) + mod_ref[0, 0:1, :]
    hb = h.astype(bf16)
    cw = ML_W
    for n in range(4):
        pm_ref[:, n * cw:(n + 1) * cw] = _dot(hb, w_ref[:, n * cw:(n + 1) * cw]).astype(bf16)

    lane = lax.broadcasted_iota(jnp.int32, (tm, LANES), 1)
    q_aug = jnp.where(jnp.logical_and(lane >= DA_DH, lane < DA_DH + 3), 1.0, 0.0)
    pos = (pl.program_id(0) * tm) % seq + lax.broadcasted_iota(jnp.int32, (tm, 1), 0)
    pos = pos.astype(f32)
    q0 = 4 * cw
    k0 = q0 + 2 * DA_W
    yq = _dot(hb, w_ref[:, q0:q0 + 2 * DA_W])
    yk = _dot(hb, w_ref[:, k0:k0 + 2 * DA_W])
    for grp in range(2 * DA_HEADS):
        sl = slice(grp * LANES, (grp + 1) * LANES)
        y = yq[:, sl]
        ss = jnp.sum(y * y, axis=-1, keepdims=True)
        q_ref[:, sl] = (y * lax.rsqrt(ss * (1.0 / DA_DH) + EPS) * qg_ref[:, sl] + q_aug).astype(bf16)

        y = yk[:, sl]
        ss = jnp.sum(y * y, axis=-1, keepdims=True)
        r = pos * _alibi_slope_log2(grp // 2)
        r_hi = r.astype(bf16).astype(f32)
        r_mid = (r - r_hi).astype(bf16).astype(f32)
        r_lo = r - r_hi - r_mid
        k_aug = jnp.where(lane == DA_DH, r_hi,
                          jnp.where(lane == DA_DH + 1, r_mid, jnp.where(lane == DA_DH + 2, r_lo, 0.0)))
        k_ref[:, sl] = (y * lax.rsqrt(ss * (1.0 / DA_DH) + EPS) * kg_ref[:, sl] + k_aug).astype(bf16)

    v0 = k0 + 2 * DA_W
    v_ref[...] = _dot(hb, w_ref[:, v0:v0 + DA_W]).astype(bf16)
    gcol_ref[...] = _dot(hb, wg_ref[...]) + gbr_ref[...]
    grow_ref[...] = _dot_nt(wgt_ref[...], hb) + gbc_ref[...]


def _inproj(x2, mod, w_main, wg, wgt, gb_row, gb_col, qgain, kgain, seq):
    t, d = x2.shape
    tm = TM_INPROJ
    tiles_per_seq = seq // tm
    n_main = w_main.shape[1]
    const = lambda shape: pl.BlockSpec(shape, lambda i: (0,) * len(shape))
    return pl.pallas_call(
        functools.partial(_inproj_kernel, seq=seq),
        out_shape=(jax.ShapeDtypeStruct((t, 4 * ML_W), bf16),
                   jax.ShapeDtypeStruct((t, 2 * DA_W), bf16),
                   jax.ShapeDtypeStruct((t, 2 * DA_W), bf16),
                   jax.ShapeDtypeStruct((t, DA_W), bf16),
                   jax.ShapeDtypeStruct((t, LANES), f32),
                   jax.ShapeDtypeStruct((N_GATES, t), f32)),
        grid=(t // tm,),
        in_specs=[pl.BlockSpec((tm, d), lambda i: (i, 0)),
                  pl.BlockSpec((1, 6, d), lambda i: (i // tiles_per_seq, 0, 0)),
                  const((d, n_main)), const((d, LANES)), const((N_GATES, d)),
                  const((1, LANES)), const((N_GATES, 1)),
                  const((1, 2 * DA_W)), const((1, 2 * DA_W))],
        out_specs=(pl.BlockSpec((tm, 4 * ML_W), lambda i: (i, 0)),
                   pl.BlockSpec((tm, 2 * DA_W), lambda i: (i, 0)),
                   pl.BlockSpec((tm, 2 * DA_W), lambda i: (i, 0)),
                   pl.BlockSpec((tm, DA_W), lambda i: (i, 0)),
                   pl.BlockSpec((tm, LANES), lambda i: (i, 0)),
                   pl.BlockSpec((N_GATES, tm), lambda i: (0, i))),
        compiler_params=_cparams(1),
        name="inproj",
    )(x2, mod, w_main, wg, wgt, gb_row, gb_col, qgain, kgain)


def _mlstm_kernel(qk_ref, v_ref, o_ref, gcol_ref, grow_ref, cw_ref, cb_ref, ng_ref,
                  out_ref, state_ref, m_ref, prev_ref, ext_ref):
    L = ML_CHUNK
    dh = ML_DH
    c = pl.program_id(1)

    @pl.when(c == 0)
    def _():
        state_ref[...] = jnp.zeros_like(state_ref)
        m_ref[...] = jnp.zeros_like(m_ref)
        prev_ref[...] = jnp.zeros_like(prev_ref)

    cur = qk_ref[...].astype(f32)
    ext_ref[0:SUBLANES, :] = prev_ref[...]
    ext_ref[SUBLANES:SUBLANES + L, :] = cur
    prev_ref[...] = cur[L - SUBLANES:L, :]
    acc = cb_ref[...] + cw_ref[CONV_K - 1:CONV_K, :] * cur
    for j in range(CONV_K - 1):
        off = SUBLANES - (CONV_K - 1) + j
        acc = acc + cw_ref[j:j + 1, :] * ext_ref[off:off + L, :]
    qk = _silu(acc)

    row_i = lax.broadcasted_iota(jnp.int32, (L, L), 0)
    col_i = lax.broadcasted_iota(jnp.int32, (L, L), 1)
    causal = row_i >= col_i
    tril = causal.astype(f32)
    triu = (row_i <= col_i).astype(f32)
    gcol = gcol_ref[...]
    grow = grow_ref[...]
    bcol_all = jnp.dot(tril, _log_sigmoid(gcol), precision=HIGHEST, preferred_element_type=f32)
    brow_all = jnp.dot(_log_sigmoid(grow), triu, precision=HIGHEST, preferred_element_type=f32)

    lane = lax.broadcasted_iota(jnp.int32, (L, dh), 1)
    ones_col = (lane == 0).astype(f32)
    g = ng_ref[...]

    for hd in range(ML_HEADS):
        q = qk[:, hd * dh:(hd + 1) * dh]
        k = qk[:, ML_W + hd * dh:ML_W + (hd + 1) * dh] * (dh ** -0.5)
        v = v_ref[:, hd * dh:(hd + 1) * dh].astype(f32)
        v_aug = jnp.concatenate([v, ones_col], axis=1)
        qb = q.astype(bf16)
        kb = k.astype(bf16)

        bcol = bcol_all[:, ML_HEADS + hd:ML_HEADS + hd + 1]
        icol = gcol[:, hd:hd + 1]
        brow = brow_all[ML_HEADS + hd:ML_HEADS + hd + 1, :]
        irow = grow[hd:hd + 1, :]
        m_prev = m_ref[hd:hd + 1, 0:1]

        d_intra = jnp.where(causal, bcol - brow + irow, NEG_BIG)
        d_inter = bcol + m_prev
        m_t = jnp.maximum(d_inter, jnp.max(d_intra, axis=1, keepdims=True))
        w_intra = jnp.exp(d_intra - m_t)
        w_inter = jnp.exp(d_inter - m_t)

        s = _dot_nt(qb, kb) * w_intra
        state = state_ref[hd]
        q_state = _dot(qb, state.astype(bf16))
        s_v = _dot(s.astype(bf16), v_aug.astype(bf16))
        num = w_inter * q_state[:, 0:dh] + s_v[:, 0:dh]
        den = w_inter * q_state[:, dh:dh + 1] + s_v[:, dh:dh + 1]
        hval = num / jnp.maximum(jnp.abs(den), jnp.exp(-m_t))

        b_last = bcol[L - 1:L, :]
        d_state = b_last - bcol + icol
        m_new = jnp.maximum(b_last + m_prev, jnp.max(d_state, axis=0, keepdims=True))
        carry_scale = jnp.exp(b_last + m_prev - m_new)
        wk = jnp.exp(d_state - m_new)
        upd = _dot(k.T.astype(bf16), (v_aug * wk).astype(bf16))
        state_ref[hd] = carry_scale * state + upd
        m_ref[hd:hd + 1, :] = jnp.broadcast_to(m_new, (1, LANES))

        hn = hval * lax.rsqrt(jnp.mean(hval * hval, axis=-1, keepdims=True) + EPS) * g
        og = _sigmoid(o_ref[:, hd * dh:(hd + 1) * dh].astype(f32))
        out_ref[:, hd * dh:(hd + 1) * dh] = (hn * og).astype(bf16)


def _mlstm(pm, gcol, grow, conv_w, conv_b, ml_norm_g, batch, seq):
    t = pm.shape[0]
    L = ML_CHUNK
    nc = seq // L
    row = lambda b, c: b * nc + c
    return pl.pallas_call(
        _mlstm_kernel,
        out_shape=jax.ShapeDtypeStruct((t, ML_W), bf16),
        grid=(batch, nc),
        in_specs=[pl.BlockSpec((L, 2 * ML_W), lambda b, c: (row(b, c), 0)),
                  pl.BlockSpec((L, ML_W), lambda b, c: (row(b, c), 2)),
                  pl.BlockSpec((L, ML_W), lambda b, c: (row(b, c), 3)),
                  pl.BlockSpec((L, LANES), lambda b, c: (row(b, c), 0)),
                  pl.BlockSpec((N_GATES, L), lambda b, c: (0, row(b, c))),
                  pl.BlockSpec((CONV_K, 2 * ML_W), lambda b, c: (0, 0)),
                  pl.BlockSpec((1, 2 * ML_W), lambda b, c: (0, 0)),
                  pl.BlockSpec((1, ML_DH), lambda b, c: (0, 0))],
        out_specs=pl.BlockSpec((L, ML_W), lambda b, c: (row(b, c), 0)),
        scratch_shapes=[pltpu.VMEM((ML_HEADS, ML_DH, 2 * ML_DH), f32),
                        pltpu.VMEM((SUBLANES, LANES), f32),
                        pltpu.VMEM((SUBLANES, 2 * ML_W), f32),
                        pltpu.VMEM((SUBLANES + ML_CHUNK, 2 * ML_W), f32)],
        compiler_params=_cparams(2),
        name="mlstm",
    )(pm, pm, pm, gcol, grow, conv_w, conv_b, ml_norm_g)


def _attn_block(q_ref, k_ref, v_ref, m_ref, l_ref, acc_ref, diagonal):
    blk = ATT_BLOCK
    nch = blk // LANES
    v = v_ref[...]
    if diagonal:
        keep = (lax.broadcasted_iota(jnp.int32, (blk, blk), 1)
                <= lax.broadcasted_iota(jnp.int32, (blk, blk), 0))
    for c in range(2):
        sl = slice(c * LANES, (c + 1) * LANES)
        s = _dot_nt(q_ref[:, sl], k_ref[:, sl])
        if diagonal:
            s = jnp.where(keep, s, NEG_BIG)
        chunks = [s[:, j * LANES:(j + 1) * LANES] for j in range(nch)]
        mc = chunks[0]
        for ch in chunks[1:]:
            mc = jnp.maximum(mc, ch)
        m_old = m_ref[c]
        m_new = jnp.maximum(m_old, jnp.max(mc, axis=1, keepdims=True))
        alpha = jnp.exp2(m_old - m_new)
        ps = [jnp.exp2(ch - m_new) for ch in chunks]
        lsum = ps[0]
        for pj in ps[1:]:
            lsum = lsum + pj
        p = jnp.concatenate([pj.astype(bf16) for pj in ps], axis=1)
        l_ref[c] = alpha * l_ref[c] + lsum
        acc_ref[c] = alpha * acc_ref[c] + _dot(p, v)
        m_ref[c] = m_new


def _attn_kernel(qt_ref, kt_ref, q_ref, k_ref, v_ref, lam_ref, g_ref, o_ref,
                 m_ref, l_ref, acc_ref, *, lam_init):
    step = pl.program_id(2)
    qi = qt_ref[step]
    ki = kt_ref[step]

    @pl.when(ki == 0)
    def _():
        m_ref[...] = jnp.full_like(m_ref, NEG_BIG)
        l_ref[...] = jnp.zeros_like(l_ref)
        acc_ref[...] = jnp.zeros_like(acc_ref)

    @pl.when(ki < qi)
    def _():
        _attn_block(q_ref, k_ref, v_ref, m_ref, l_ref, acc_ref, diagonal=False)

    @pl.when(ki == qi)
    def _():
        _attn_block(q_ref, k_ref, v_ref, m_ref, l_ref, acc_ref, diagonal=True)
        lam = lam_ref[...]
        l0 = jnp.sum(l_ref[0], axis=1, keepdims=True)
        l1 = jnp.sum(l_ref[1], axis=1, keepdims=True)
        o = acc_ref[0] / l0 - lam * (acc_ref[1] / l1)
        o = o * lax.rsqrt(jnp.mean(o * o, axis=-1, keepdims=True) + EPS)
        o_ref[...] = (o * g_ref[...] * (1.0 - lam_init)).astype(bf16)


def _attention(qh, kh, vh, lam, da_norm_g, batch, seq, lam_init):
    t = qh.shape[0]
    blk = ATT_BLOCK
    nb = seq // blk
    pairs = [(i, j) for i in range(nb) for j in range(i + 1)]
    qt = jnp.asarray([p[0] for p in pairs], jnp.int32)
    kt = jnp.asarray([p[1] for p in pairs], jnp.int32)
    w = 2 * DA_DH
    grid_spec = pltpu.PrefetchScalarGridSpec(
        num_scalar_prefetch=2,
        grid=(batch, DA_HEADS, len(pairs)),
        in_specs=[pl.BlockSpec((blk, 2 * LANES), lambda b, h, s, qt, kt: (b * nb + qt[s], h)),
                  pl.BlockSpec((blk, 2 * LANES), lambda b, h, s, qt, kt: (b * nb + kt[s], h)),
                  pl.BlockSpec((blk, w), lambda b, h, s, qt, kt: (b * nb + kt[s], h)),
                  pl.BlockSpec((1, 1), lambda b, h, s, qt, kt: (0, 0)),
                  pl.BlockSpec((1, w), lambda b, h, s, qt, kt: (0, 0))],
        out_specs=pl.BlockSpec((blk, w), lambda b, h, s, qt, kt: (b * nb + qt[s], h)),
        scratch_shapes=[pltpu.VMEM((2, blk, LANES), f32),
                        pltpu.VMEM((2, blk, LANES), f32),
                        pltpu.VMEM((2, blk, w), f32)],
    )
    return pl.pallas_call(
        functools.partial(_attn_kernel, lam_init=lam_init),
        out_shape=jax.ShapeDtypeStruct((t, DA_W), bf16),
        grid_spec=grid_spec,
        compiler_params=_cparams(3),
        name="attn",
    )(qt, kt, qh, kh, vh, lam, da_norm_g)


def _first_index_of_max(x, iota_f, size):
    m = jnp.max(x, axis=0, keepdims=True)
    idx = jnp.min(jnp.where(x == m, iota_f, float(size)), axis=0, keepdims=True)
    return m, idx


def _route_kernel(hm_ref, hd_ref, x_ref, mod_ref, wo_ref, wr_ref, rb_ref,
                  x1_ref, h2p_ref, idx_ref, gate_ref, rank_ref, cnt_ref, base_ref):
    tm = TM_ROUTE
    i = pl.program_id(0)

    @pl.when(i == 0)
    def _():
        base_ref[...] = jnp.zeros_like(base_ref)

    mix = _dot(hm_ref[...], wo_ref[0:ML_W, :]) + _dot(hd_ref[...], wo_ref[ML_W:ML_W + DA_W, :])
    x1 = x_ref[...] + mod_ref[0, 2:3, :] * mix
    x1_ref[...] = x1
    h2 = x1 * lax.rsqrt(jnp.mean(x1 * x1, axis=-1, keepdims=True) + EPS)
    h2 = h2 * (1.0 + mod_ref[0, 4:5, :]) + mod_ref[0, 3:4, :]
    h2p_ref[...] = _pack_bf16_pair(h2[:, 0:HALF], h2[:, HALF:D_MODEL])

    sc = _sigmoid(_dot_nt(wr_ref[...], h2.astype(bf16)))
    sel = sc + rb_ref[...]

    gi = lax.broadcasted_iota(jnp.int32, (GROUP_SIZE, tm), 0).astype(f32)
    gscores = []
    for g in range(N_GROUPS):
        blk = sel[g * GROUP_SIZE:(g + 1) * GROUP_SIZE, :]
        m1, i1 = _first_index_of_max(blk, gi, GROUP_SIZE)
        m2 = jnp.max(jnp.where(gi == i1, NEG_BIG, blk), axis=0, keepdims=True)
        gscores.append(m1 + m2)
    gs = jnp.concatenate(gscores, axis=0)

    ngi = lax.broadcasted_iota(jnp.int32, (N_GROUPS, tm), 0).astype(f32)
    gkeep = jnp.zeros((N_GROUPS, tm), f32)
    for _ in range(TOPK_GROUPS):
        _, gidx = _first_index_of_max(gs, ngi, N_GROUPS)
        hit = ngi == gidx
        gkeep = jnp.where(hit, 1.0, gkeep)
        gs = jnp.where(hit, NEG_BIG, gs)
    masked = jnp.concatenate(
        [jnp.where(gkeep[g:g + 1, :] > 0.0, sel[g * GROUP_SIZE:(g + 1) * GROUP_SIZE, :], NEG_BIG)
         for g in range(N_GROUPS)], axis=0)

    ei = lax.broadcasted_iota(jnp.int32, (N_EXPERTS, tm), 0).astype(f32)
    idxs, gates = [], []
    chosen = jnp.zeros((N_EXPERTS, tm), f32)
    for _ in range(TOP_K):
        _, eidx = _first_index_of_max(masked, ei, N_EXPERTS)
        hit = ei == eidx
        idxs.append(eidx)
        gates.append(jnp.sum(jnp.where(hit, sc, 0.0), axis=0, keepdims=True))
        chosen = jnp.where(hit, 1.0, chosen)
        masked = jnp.where(hit, NEG_BIG, masked)
    gate = jnp.concatenate(gates, axis=0)
    gate = gate / jnp.sum(gate, axis=0, keepdims=True) * ROUTED_SCALE
    gate_ref[...] = gate
    idx_ref[...] = jnp.concatenate(idxs, axis=0).astype(jnp.int32)

    tr = lax.broadcasted_iota(jnp.int32, (tm, tm), 0)
    tc = lax.broadcasted_iota(jnp.int32, (tm, tm), 1)
    before = (tr < tc).astype(bf16)
    seen = base_ref[...] + _dot(chosen.astype(bf16), before)
    ranks = [jnp.sum(jnp.where(ei == eidx, seen, 0.0), axis=0, keepdims=True) for eidx in idxs]
    rank_ref[...] = jnp.concatenate(ranks, axis=0).astype(jnp.int32)
    total = base_ref[...] + jnp.sum(chosen, axis=1, keepdims=True)
    base_ref[...] = total
    cnt_ref[...] = total


def _route(hm, hd, x2, mod, w_out, w_router_t, rbias_col, seq):
    t, d = x2.shape
    tm = TM_ROUTE
    tiles_per_seq = seq // tm
    const = lambda shape: pl.BlockSpec(shape, lambda i: (0,) * len(shape))
    return pl.pallas_call(
        _route_kernel,
        out_shape=(jax.ShapeDtypeStruct((t, d), f32),
                   jax.ShapeDtypeStruct((t, HALF), jnp.uint32),
                   jax.ShapeDtypeStruct((TOP_K, t), jnp.int32),
                   jax.ShapeDtypeStruct((TOP_K, t), f32),
                   jax.ShapeDtypeStruct((TOP_K, t), jnp.int32),
                   jax.ShapeDtypeStruct((N_EXPERTS, 1), f32)),
        grid=(t // tm,),
        in_specs=[pl.BlockSpec((tm, ML_W), lambda i: (i, 0)),
                  pl.BlockSpec((tm, DA_W), lambda i: (i, 0)),
                  pl.BlockSpec((tm, d), lambda i: (i, 0)),
                  pl.BlockSpec((1, 6, d), lambda i: (i // tiles_per_seq, 0, 0)),
                  const((d, d)), const((N_EXPERTS, d)), const((N_EXPERTS, 1))],
        out_specs=(pl.BlockSpec((tm, d), lambda i: (i, 0)),
                   pl.BlockSpec((tm, HALF), lambda i: (i, 0)),
                   pl.BlockSpec((TOP_K, tm), lambda i: (0, i)),
                   pl.BlockSpec((TOP_K, tm), lambda i: (0, i)),
                   pl.BlockSpec((TOP_K, tm), lambda i: (0, i)),
                   const((N_EXPERTS, 1))),
        scratch_shapes=[pltpu.VMEM((N_EXPERTS, 1), f32)],
        compiler_params=_cparams(1),
        name="route",
    )(hm, hd, x2, mod, w_out, w_router_t, rbias_col)


SLOT_RADIX = 256
SLOT_DIGITS = 3


def _slots_kernel(idx_ref, rank_ref, dig_ref, dest_ref):
    tm = TM_SLOTS
    ei = lax.broadcasted_iota(jnp.int32, (N_EXPERTS, tm), 0)
    rows = []
    for k in range(TOP_K):
        onehot = jnp.where(ei == idx_ref[k:k + 1, :], 1.0, 0.0).astype(bf16)
        dg = _dot(dig_ref[...], onehot)
        start = dg[0:1, :]
        for j in range(1, SLOT_DIGITS):
            start = start + dg[j:j + 1, :] * float(SLOT_RADIX ** j)
        rows.append(start.astype(jnp.int32) + rank_ref[k:k + 1, :])
    dest_ref[...] = jnp.concatenate(rows, axis=0)


def _slots(idx, rank, digits):
    t = idx.shape[1]
    tm = TM_SLOTS
    return pl.pallas_call(
        _slots_kernel,
        out_shape=jax.ShapeDtypeStruct((TOP_K, t), jnp.int32),
        grid=(t // tm,),
        in_specs=[pl.BlockSpec((TOP_K, tm), lambda i: (0, i)),
                  pl.BlockSpec((TOP_K, tm), lambda i: (0, i)),
                  pl.BlockSpec((SUBLANES, N_EXPERTS), lambda i: (0, 0))],
        out_specs=pl.BlockSpec((TOP_K, tm), lambda i: (0, i)),
        compiler_params=_cparams(1),
        name="slots",
    )(idx, rank, digits)


def _row_copy(src_ref, src_row, dst_ref, dst_row, sem):
    return pltpu.make_async_copy(src_ref.at[pl.ds(src_row, 1)], dst_ref.at[pl.ds(dst_row, 1)], sem)


def _dispatch_kernel(dest_ref, h_ref, xs_in_ref, xs_ref, sem):
    del xs_in_ref
    tm = TM_MOE

    def issue(tok, carry):
        for k in range(TOP_K):
            _row_copy(h_ref, tok, xs_ref, dest_ref[k, tok], sem).start(priority=k % 2)
        return carry

    lax.fori_loop(0, tm, issue, 0)

    def drain(tok, carry):
        for k in range(TOP_K):
            _row_copy(h_ref, tok, xs_ref, dest_ref[k, tok], sem).wait()
        return carry

    lax.fori_loop(0, tm, drain, 0)


def _dispatch(dest, h2p, xs_zero):
    t = h2p.shape[0]
    tm = TM_MOE
    return pl.pallas_call(
        _dispatch_kernel,
        out_shape=jax.ShapeDtypeStruct(xs_zero.shape, xs_zero.dtype),
        grid=(t // tm,),
        in_specs=[pl.BlockSpec((TOP_K, tm), lambda i: (0, i), memory_space=pltpu.SMEM),
                  pl.BlockSpec((tm, HALF), lambda i: (i, 0)),
                  pl.BlockSpec(memory_space=pl.ANY)],
        out_specs=pl.BlockSpec(memory_space=pl.ANY),
        scratch_shapes=[pltpu.SemaphoreType.DMA(())],
        input_output_aliases={2: 0},
        compiler_params=_cparams(1),
        name="dispatch",
    )(dest, h2p, xs_zero)


def _expert_kernel(be_ref, nu_ref, xs_ref, w1_ref, w3_ref, w2_ref, ys_ref, w1b, w3b, w2b):
    i = pl.program_id(0)
    active = i < nu_ref[0]

    @pl.when(active)
    def _():
        changed = jnp.logical_or(i == 0, be_ref[i] != be_ref[jnp.maximum(i - 1, 0)])

        @pl.when(changed)
        def _():
            w1b[...] = w1_ref[0].astype(bf16)
            w3b[...] = w3_ref[0].astype(bf16)
            w2b[...] = w2_ref[0].astype(bf16)

        lo, hi = _unpack_bf16_pair(xs_ref[...])
        lo = lo.astype(bf16)
        hi = hi.astype(bf16)
        h1 = _dot(lo, w1b[0:HALF, :]) + _dot(hi, w1b[HALF:D_MODEL, :])
        h3 = _dot(lo, w3b[0:HALF, :]) + _dot(hi, w3b[HALF:D_MODEL, :])
        y = _dot((_silu(h1) * h3).astype(bf16), w2b[...])
        ys_ref[...] = _pack_bf16_pair(y[:, 0:HALF], y[:, HALF:D_MODEL])

    @pl.when(jnp.logical_not(active))
    def _():
        ys_ref[...] = jnp.zeros_like(ys_ref)


def _experts(block_expert, n_used, xs, w1, w3, w2):
    n_pad = xs.shape[0]
    bm = MOE_BLOCK
    last = lambda i, nu: jnp.minimum(i, nu[0] - 1)
    grid_spec = pltpu.PrefetchScalarGridSpec(
        num_scalar_prefetch=2,
        grid=(n_pad // bm,),
        in_specs=[pl.BlockSpec((bm, HALF), lambda i, be, nu: (last(i, nu), 0)),
                  pl.BlockSpec((1, D_MODEL, D_EXPERT), lambda i, be, nu: (be[last(i, nu)], 0, 0)),
                  pl.BlockSpec((1, D_MODEL, D_EXPERT), lambda i, be, nu: (be[last(i, nu)], 0, 0)),
                  pl.BlockSpec((1, D_EXPERT, D_MODEL), lambda i, be, nu: (be[last(i, nu)], 0, 0))],
        out_specs=pl.BlockSpec((bm, HALF), lambda i, be, nu: (i, 0)),
        scratch_shapes=[pltpu.VMEM((D_MODEL, D_EXPERT), bf16),
                        pltpu.VMEM((D_MODEL, D_EXPERT), bf16),
                        pltpu.VMEM((D_EXPERT, D_MODEL), bf16)],
    )
    return pl.pallas_call(
        _expert_kernel,
        out_shape=jax.ShapeDtypeStruct((n_pad, HALF), jnp.uint32),
        grid_spec=grid_spec,
        compiler_params=_cparams(1),
        name="experts",
    )(block_expert, n_used, xs, w1, w3, w2)


def _combine_kernel(dest_ref, gate_ref, h_ref, x1_ref, mod_ref, ws1_ref, ws3_ref, ws2_ref, ys_ref,
                    out_ref, buf_ref, sem):
    tm = TM_MOE

    def issue(tok, carry):
        for k in range(TOP_K):
            _row_copy(ys_ref, dest_ref[k, tok], buf_ref.at[k], tok, sem).start(priority=k % 2)
        return carry

    lax.fori_loop(0, tm, issue, 0)

    lo, hi = _unpack_bf16_pair(h_ref[...])
    lo = lo.astype(bf16)
    hi = hi.astype(bf16)
    s1 = _dot(lo, ws1_ref[0:HALF, :]) + _dot(hi, ws1_ref[HALF:D_MODEL, :])
    s3 = _dot(lo, ws3_ref[0:HALF, :]) + _dot(hi, ws3_ref[HALF:D_MODEL, :])
    y = _dot((_silu(s1) * s3).astype(bf16), ws2_ref[...])

    def drain(tok, carry):
        for k in range(TOP_K):
            _row_copy(ys_ref, dest_ref[k, tok], buf_ref.at[k], tok, sem).wait()
        return carry

    lax.fori_loop(0, tm, drain, 0)

    gate = gate_ref[...]
    acc_lo = y[:, 0:HALF]
    acc_hi = y[:, HALF:D_MODEL]
    for k in range(TOP_K):
        rlo, rhi = _unpack_bf16_pair(buf_ref[k])
        gk = gate[:, k:k + 1]
        acc_lo = acc_lo + gk * rlo
        acc_hi = acc_hi + gk * rhi
    gate_f = mod_ref[0, 5:6, :]
    out_ref[:, 0:HALF] = x1_ref[:, 0:HALF] + gate_f[:, 0:HALF] * acc_lo
    out_ref[:, HALF:D_MODEL] = x1_ref[:, HALF:D_MODEL] + gate_f[:, HALF:D_MODEL] * acc_hi


def _combine(dest, gate_col, h2p, x1, mod, ws1, ws3, ws2, ys, seq):
    t, d = x1.shape
    tm = TM_MOE
    tiles_per_seq = seq // tm
    const = lambda shape: pl.BlockSpec(shape, lambda i: (0,) * len(shape))
    return pl.pallas_call(
        _combine_kernel,
        out_shape=jax.ShapeDtypeStruct((t, d), f32),
        grid=(t // tm,),
        in_specs=[pl.BlockSpec((TOP_K, tm), lambda i: (0, i), memory_space=pltpu.SMEM),
                  pl.BlockSpec((tm, TOP_K), lambda i: (i, 0)),
                  pl.BlockSpec((tm, HALF), lambda i: (i, 0)),
                  pl.BlockSpec((tm, d), lambda i: (i, 0)),
                  pl.BlockSpec((1, 6, d), lambda i: (i // tiles_per_seq, 0, 0)),
                  const((d, D_EXPERT)), const((d, D_EXPERT)), const((D_EXPERT, d)),
                  pl.BlockSpec(memory_space=pl.ANY)],
        out_specs=pl.BlockSpec((tm, d), lambda i: (i, 0)),
        scratch_shapes=[pltpu.VMEM((TOP_K, tm, HALF), jnp.uint32),
                        pltpu.SemaphoreType.DMA(())],
        compiler_params=_cparams(1),
        name="combine",
    )(dest, gate_col, h2p, x1, mod, ws1, ws3, ws2, ys)


def _lambda_init(layer):
    return 0.8 - 0.6 * math.exp(-0.3 * layer)


def _layer(x, c, w_ada, b_ada, w_in, conv_w, conv_b, gate_b, ml_norm_g, da_q_norm_g, da_k_norm_g,
           lambda_q1, lambda_k1, lambda_q2, lambda_k2, da_norm_g, w_out, w_router, router_bias,
           w1, w3, w2, ws1, ws3, ws2, layer):
    batch, seq, d = x.shape
    t = batch * seq
    lam_init = _lambda_init(layer)
    x2 = x.reshape(t, d)

    c_pad = jnp.pad(c, ((0, -batch % SUBLANES), (0, 0)))
    mod = _adaln(c_pad, w_ada, b_ada.reshape(1, -1))[:batch].reshape(batch, 6, d)

    g0 = 4 * ML_W
    q0 = g0 + N_GATES
    n_grp = 2 * DA_HEADS

    def lane_groups(cols):
        r = cols.shape[0]
        return jnp.pad(cols.reshape(r, n_grp, DA_DH), ((0, 0), (0, 0), (0, LANES - DA_DH))).reshape(r, -1)

    w_main = jnp.concatenate([w_in[:, :g0],
                              lane_groups(w_in[:, q0:q0 + DA_W]),
                              lane_groups(w_in[:, q0 + DA_W:q0 + 2 * DA_W]),
                              w_in[:, q0 + 2 * DA_W:]], axis=1).astype(bf16)
    w_gate = w_in[:, g0:g0 + N_GATES]
    wg = jnp.pad(w_gate, ((0, 0), (0, LANES - N_GATES))).astype(bf16)
    wgt = w_gate.T.astype(bf16)
    gb_row = jnp.pad(gate_b, (0, LANES - N_GATES)).reshape(1, LANES)
    gb_col = gate_b.reshape(N_GATES, 1)
    qgain = lane_groups((jnp.tile(da_q_norm_g, n_grp) * (DA_DH ** -0.5 * LOG2E)).reshape(1, DA_W))
    kgain = lane_groups(jnp.tile(da_k_norm_g, n_grp).reshape(1, DA_W))

    pm, qh, kh, vh, gcol, grow = _inproj(x2, mod, w_main, wg, wgt, gb_row, gb_col, qgain, kgain, seq)

    hm = _mlstm(pm, gcol, grow, conv_w, conv_b.reshape(1, -1), ml_norm_g.reshape(1, -1), batch, seq)

    lam = (jnp.exp(jnp.sum(lambda_q1 * lambda_k1)) - jnp.exp(jnp.sum(lambda_q2 * lambda_k2))
           + lam_init).reshape(1, 1).astype(f32)
    hd = _attention(qh, kh, vh, lam, da_norm_g.reshape(1, -1), batch, seq, lam_init)

    x1, h2p, idx, gate, rank, counts = _route(hm, hd, x2, mod, w_out.astype(bf16),
                                              w_router.T.astype(bf16), router_bias.reshape(-1, 1), seq)

    bm = MOE_BLOCK
    n_blocks = (t * TOP_K) // bm + N_EXPERTS
    counts_i = counts.reshape(-1).astype(jnp.int32)
    padded = (counts_i + bm - 1) // bm * bm
    pad_end = jnp.cumsum(padded)
    pad_start = pad_end - padded
    digits = jnp.stack([(pad_start // SLOT_RADIX ** j) % SLOT_RADIX for j in range(SLOT_DIGITS)])
    digits = jnp.pad(digits, ((0, SUBLANES - SLOT_DIGITS), (0, 0))).astype(bf16)
    dest = _slots(idx, rank, digits)
    block_start = jnp.arange(n_blocks, dtype=jnp.int32) * bm
    block_expert = jnp.minimum(jnp.sum(pad_end[None, :] <= block_start[:, None], axis=1),
                               N_EXPERTS - 1).astype(jnp.int32)
    n_used = (pad_end[-1:] // bm).astype(jnp.int32)

    xs = _dispatch(dest, h2p, jnp.zeros((n_blocks * bm, HALF), jnp.uint32))
    ys = _experts(block_expert, n_used, xs, w1, w3, w2)
    out = _combine(dest, gate.T, h2p, x1, mod, ws1.astype(bf16), ws3.astype(bf16), ws2.astype(bf16), ys, seq)
    return out.reshape(batch, seq, d)


def kernel(x, c, w_ada, b_ada, w_in, conv_w, conv_b, gate_b, ml_norm_g, da_q_norm_g, da_k_norm_g,
           lambda_q1, lambda_k1, lambda_q2, lambda_k2, da_norm_g, w_out, w_router, router_bias,
           w1, w3, w2, ws1, ws3, ws2):
    depth = w_ada.shape[0]
    for l in range(depth):
        x = _layer(x, c, w_ada[l], b_ada[l], w_in[l], conv_w[l], conv_b[l], gate_b[l], ml_norm_g[l],
                   da_q_norm_g[l], da_k_norm_g[l], lambda_q1[l], lambda_k1[l], lambda_q2[l], lambda_k2[l],
                   da_norm_g[l], w_out[l], w_router[l], router_bias[l], w1[l], w3[l], w2[l],
                   ws1[l], ws3[l], ws2[l], l)
    return x
```

```python
import functools
import math

import jax
import jax.numpy as jnp
from jax import lax
from jax.experimental import pallas as pl
from jax.experimental.pallas import tpu as pltpu

D_MODEL = 1024
ML_HEADS = 4
ML_DH = 128
ML_W = ML_HEADS * ML_DH
ML_CHUNK = 128
CONV_K = 4
DA_HEADS = 4
DA_DH = 64
DA_W = DA_HEADS * 2 * DA_DH
N_EXPERTS = 256
TOP_K = 8
N_GROUPS = 8
GROUP_SIZE = N_EXPERTS // N_GROUPS
TOPK_GROUPS = 4
D_EXPERT = 256
ROUTED_SCALE = 2.5
EPS = 1e-6
N_GATES = 2 * ML_HEADS

LANES = 128
SUBLANES = 8
VMEM_LIMIT_BYTES = 48 * 1024 * 1024

TM_INPROJ = 512
ATT_BLOCK = 1024
TM_ROUTE = 256
TM_SLOTS = 512
TM_MOE = 256
MOE_BLOCK = 256
HALF = D_MODEL // 2

NEG_BIG = -1e30
LOG2E = 1.4426950408889634

f32 = jnp.float32
bf16 = jnp.bfloat16
HIGHEST = lax.Precision.HIGHEST


def _cparams(n_axes):
    return pltpu.CompilerParams(dimension_semantics=("arbitrary",) * n_axes,
                                vmem_limit_bytes=VMEM_LIMIT_BYTES)


def _dot(a, b):
    return jnp.dot(a, b, preferred_element_type=f32)


def _dot_nt(a, b):
    return lax.dot_general(a, b, (((1,), (1,)), ((), ())), preferred_element_type=f32)


def _sigmoid(x):
    return 1.0 / (1.0 + jnp.exp(-x))


def _silu(x):
    return x * _sigmoid(x)


def _log_sigmoid(x):
    return jnp.minimum(x, 0.0) - jnp.log(1.0 + jnp.exp(-jnp.abs(x)))


def _pack_bf16_pair(lo, hi):
    lo_bits = lax.bitcast_convert_type(lo.astype(bf16).astype(f32), jnp.uint32)
    hi_bits = lax.bitcast_convert_type(hi.astype(bf16).astype(f32), jnp.uint32)
    return (hi_bits & jnp.uint32(0xFFFF0000)) | (lo_bits >> 16)


def _unpack_bf16_pair(w):
    lo = lax.bitcast_convert_type(w << 16, f32)
    hi = lax.bitcast_convert_type(w & jnp.uint32(0xFFFF0000), f32)
    return lo, hi


def _adaln_kernel(c_ref, w_ref, b_ref, o_ref):
    c = c_ref[...]
    o_ref[...] = jnp.dot(_silu(c), w_ref[...], precision=HIGHEST, preferred_element_type=f32) + b_ref[...]


def _adaln(c_pad, w_ada, b_ada):
    rows, d = c_pad.shape
    n = w_ada.shape[1]
    tn = 1536
    return pl.pallas_call(
        _adaln_kernel,
        out_shape=jax.ShapeDtypeStruct((rows, n), f32),
        grid=(n // tn,),
        in_specs=[pl.BlockSpec((rows, d), lambda j: (0, 0)),
                  pl.BlockSpec((d, tn), lambda j: (0, j)),
                  pl.BlockSpec((1, tn), lambda j: (0, j))],
        out_specs=pl.BlockSpec((rows, tn), lambda j: (0, j)),
        compiler_params=_cparams(1),
        name="adaln",
    )(c_pad, w_ada, b_ada)


def _alibi_slope_log2(head):
    return 2.0 ** (-8.0 * (head + 1) / DA_HEADS) * LOG2E


def _inproj_kernel(x_ref, mod_ref, w_ref, wg_ref, wgt_ref, gbr_ref, gbc_ref, qg_ref, kg_ref,
                   pm_ref, q_ref, k_ref, v_ref, gcol_ref, grow_ref, *, seq):
    tm = TM_INPROJ
    x = x_ref[...]
    h = x * lax.rsqrt(jnp.mean(x * x, axis=-1, keepdims=True) + EPS)
    h = h * (1.0 + mod_ref[0, 1:2, :]) + mod_ref[0, 0:1, :]
    hb = h.astype(bf16)
    cw = ML_W
    for n in range(4):
        pm_ref[:, n * cw:(n + 1) * cw] = _dot(hb, w_ref[:, n * cw:(n + 1) * cw]).astype(bf16)

    lane = lax.broadcasted_iota(jnp.int32, (tm, LANES), 1)
    q_aug = jnp.where(jnp.logical_and(lane >= DA_DH, lane < DA_DH + 3), 1.0, 0.0)
    pos = (pl.program_id(0) * tm) % seq + lax.broadcasted_iota(jnp.int32, (tm, 1), 0)
    pos = pos.astype(f32)
    q0 = 4 * cw
    k0 = q0 + 2 * DA_W
    yq = _dot(hb, w_ref[:, q0:q0 + 2 * DA_W])
    yk = _dot(hb, w_ref[:, k0:k0 + 2 * DA_W])
    for grp in range(2 * DA_HEADS):
        sl = slice(grp * LANES, (grp + 1) * LANES)
        y = yq[:, sl]
        ss = jnp.sum(y * y, axis=-1, keepdims=True)
        q_ref[:, sl] = (y * lax.rsqrt(ss * (1.0 / DA_DH) + EPS) * qg_ref[:, sl] + q_aug).astype(bf16)

        y = yk[:, sl]
        ss = jnp.sum(y * y, axis=-1, keepdims=True)
        r = pos * _alibi_slope_log2(grp // 2)
        r_hi = r.astype(bf16).astype(f32)
        r_mid = (r - r_hi).astype(bf16).astype(f32)
        r_lo = r - r_hi - r_mid
        k_aug = jnp.where(lane == DA_DH, r_hi,
                          jnp.where(lane == DA_DH + 1, r_mid, jnp.where(lane == DA_DH + 2, r_lo, 0.0)))
        k_ref[:, sl] = (y * lax.rsqrt(ss * (1.0 / DA_DH) + EPS) * kg_ref[:, sl] + k_aug).astype(bf16)

    v0 = k0 + 2 * DA_W
    v_ref[...] = _dot(hb, w_ref[:, v0:v0 + DA_W]).astype(bf16)
    gcol_ref[...] = _dot(hb, wg_ref[...]) + gbr_ref[...]
    grow_ref[...] = _dot_nt(wgt_ref[...], hb) + gbc_ref[...]


def _inproj(x2, mod, w_main, wg, wgt, gb_row, gb_col, qgain, kgain, seq):
    t, d = x2.shape
    tm = TM_INPROJ
    tiles_per_seq = seq // tm
    n_main = w_main.shape[1]
    const = lambda shape: pl.BlockSpec(shape, lambda i: (0,) * len(shape))
    return pl.pallas_call(
        functools.partial(_inproj_kernel, seq=seq),
        out_shape=(jax.ShapeDtypeStruct((t, 4 * ML_W), bf16),
                   jax.ShapeDtypeStruct((t, 2 * DA_W), bf16),
                   jax.ShapeDtypeStruct((t, 2 * DA_W), bf16),
                   jax.ShapeDtypeStruct((t, DA_W), bf16),
                   jax.ShapeDtypeStruct((t, LANES), f32),
                   jax.ShapeDtypeStruct((N_GATES, t), f32)),
        grid=(t // tm,),
        in_specs=[pl.BlockSpec((tm, d), lambda i: (i, 0)),
                  pl.BlockSpec((1, 6, d), lambda i: (i // tiles_per_seq, 0, 0)),
                  const((d, n_main)), const((d, LANES)), const((N_GATES, d)),
                  const((1, LANES)), const((N_GATES, 1)),
                  const((1, 2 * DA_W)), const((1, 2 * DA_W))],
        out_specs=(pl.BlockSpec((tm, 4 * ML_W), lambda i: (i, 0)),
                   pl.BlockSpec((tm, 2 * DA_W), lambda i: (i, 0)),
                   pl.BlockSpec((tm, 2 * DA_W), lambda i: (i, 0)),
                   pl.BlockSpec((tm, DA_W), lambda i: (i, 0)),
                   pl.BlockSpec((tm, LANES), lambda i: (i, 0)),
                   pl.BlockSpec((N_GATES, tm), lambda i: (0, i))),
        compiler_params=_cparams(1),
        name="inproj",
    )(x2, mod, w_main, wg, wgt, gb_row, gb_col, qgain, kgain)


def _mlstm_kernel(qk_ref, v_ref, o_ref, gcol_ref, grow_ref, cw_ref, cb_ref, ng_ref,
                  out_ref, state_ref, m_ref, prev_ref, ext_ref):
    L = ML_CHUNK
    dh = ML_DH
    c = pl.program_id(1)

    @pl.when(c == 0)
    def _():
        state_ref[...] = jnp.zeros_like(state_ref)
        m_ref[...] = jnp.zeros_like(m_ref)
        prev_ref[...] = jnp.zeros_like(prev_ref)

    cur = qk_ref[...].astype(f32)
    ext_ref[0:SUBLANES, :] = prev_ref[...]
    ext_ref[SUBLANES:SUBLANES + L, :] = cur
    prev_ref[...] = cur[L - SUBLANES:L, :]
    acc = cb_ref[...] + cw_ref[CONV_K - 1:CONV_K, :] * cur
    for j in range(CONV_K - 1):
        off = SUBLANES - (CONV_K - 1) + j
        acc = acc + cw_ref[j:j + 1, :] * ext_ref[off:off + L, :]
    qk = _silu(acc)

    row_i = lax.broadcasted_iota(jnp.int32, (L, L), 0)
    col_i = lax.broadcasted_iota(jnp.int32, (L, L), 1)
    causal = row_i >= col_i
    tril = causal.astype(f32)
    triu = (row_i <= col_i).astype(f32)
    gcol = gcol_ref[...]
    grow = grow_ref[...]
    bcol_all = jnp.dot(tril, _log_sigmoid(gcol), precision=HIGHEST, preferred_element_type=f32)
    brow_all = jnp.dot(_log_sigmoid(grow), triu, precision=HIGHEST, preferred_element_type=f32)

    lane = lax.broadcasted_iota(jnp.int32, (L, dh), 1)
    ones_col = (lane == 0).astype(f32)
    g = ng_ref[...]

    for hd in range(ML_HEADS):
        q = qk[:, hd * dh:(hd + 1) * dh]
        k = qk[:, ML_W + hd * dh:ML_W + (hd + 1) * dh] * (dh ** -0.5)
        v = v_ref[:, hd * dh:(hd + 1) * dh].astype(f32)
        v_aug = jnp.concatenate([v, ones_col], axis=1)
        qb = q.astype(bf16)
        kb = k.astype(bf16)

        bcol = bcol_all[:, ML_HEADS + hd:ML_HEADS + hd + 1]
        icol = gcol[:, hd:hd + 1]
        brow = brow_all[ML_HEADS + hd:ML_HEADS + hd + 1, :]
        irow = grow[hd:hd + 1, :]
        m_prev = m_ref[hd:hd + 1, 0:1]

        d_intra = jnp.where(causal, bcol - brow + irow, NEG_BIG)
        d_inter = bcol + m_prev
        m_t = jnp.maximum(d_inter, jnp.max(d_intra, axis=1, keepdims=True))
        w_intra = jnp.exp(d_intra - m_t)
        w_inter = jnp.exp(d_inter - m_t)

        s = _dot_nt(qb, kb) * w_intra
        state = state_ref[hd]
        q_state = _dot(qb, state.astype(bf16))
        s_v = _dot(s.astype(bf16), v_aug.astype(bf16))
        num = w_inter * q_state[:, 0:dh] + s_v[:, 0:dh]
        den = w_inter * q_state[:, dh:dh + 1] + s_v[:, dh:dh + 1]
        hval = num / jnp.maximum(jnp.abs(den), jnp.exp(-m_t))

        b_last = bcol[L - 1:L, :]
        d_state = b_last - bcol + icol
        m_new = jnp.maximum(b_last + m_prev, jnp.max(d_state, axis=0, keepdims=True))
        carry_scale = jnp.exp(b_last + m_prev - m_new)
        wk = jnp.exp(d_state - m_new)
        upd = _dot(k.T.astype(bf16), (v_aug * wk).astype(bf16))
        state_ref[hd] = carry_scale * state + upd
        m_ref[hd:hd + 1, :] = jnp.broadcast_to(m_new, (1, LANES))

        hn = hval * lax.rsqrt(jnp.mean(hval * hval, axis=-1, keepdims=True) + EPS) * g
        og = _sigmoid(o_ref[:, hd * dh:(hd + 1) * dh].astype(f32))
        out_ref[:, hd * dh:(hd + 1) * dh] = (hn * og).astype(bf16)


def _mlstm(pm, gcol, grow, conv_w, conv_b, ml_norm_g, batch, seq):
    t = pm.shape[0]
    L = ML_CHUNK
    nc = seq // L
    row = lambda b, c: b * nc + c
    return pl.pallas_call(
        _mlstm_kernel,
        out_shape=jax.ShapeDtypeStruct((t, ML_W), bf16),
        grid=(batch, nc),
        in_specs=[pl.BlockSpec((L, 2 * ML_W), lambda b, c: (row(b, c), 0)),
                  pl.BlockSpec((L, ML_W), lambda b, c: (row(b, c), 2)),
                  pl.BlockSpec((L, ML_W), lambda b, c: (row(b, c), 3)),
                  pl.BlockSpec((L, LANES), lambda b, c: (row(b, c), 0)),
                  pl.BlockSpec((N_GATES, L), lambda b, c: (0, row(b, c))),
                  pl.BlockSpec((CONV_K, 2 * ML_W), lambda b, c: (0, 0)),
                  pl.BlockSpec((1, 2 * ML_W), lambda b, c: (0, 0)),
                  pl.BlockSpec((1, ML_DH), lambda b, c: (0, 0))],
        out_specs=pl.BlockSpec((L, ML_W), lambda b, c: (row(b, c), 0)),
        scratch_shapes=[pltpu.VMEM((ML_HEADS, ML_DH, 2 * ML_DH), f32),
                        pltpu.VMEM((SUBLANES, LANES), f32),
                        pltpu.VMEM((SUBLANES, 2 * ML_W), f32),
                        pltpu.VMEM((SUBLANES + ML_CHUNK, 2 * ML_W), f32)],
        compiler_params=_cparams(2),
        name="mlstm",
    )(pm, pm, pm, gcol, grow, conv_w, conv_b, ml_norm_g)


def _attn_block(q_ref, k_ref, v_ref, m_ref, l_ref, acc_ref, diagonal):
    blk = ATT_BLOCK
    nch = blk // LANES
    v = v_ref[...]
    if diagonal:
        keep = (lax.broadcasted_iota(jnp.int32, (blk, blk), 1)
                <= lax.broadcasted_iota(jnp.int32, (blk, blk), 0))
    for c in range(2):
        sl = slice(c * LANES, (c + 1) * LANES)
        s = _dot_nt(q_ref[:, sl], k_ref[:, sl])
        if diagonal:
            s = jnp.where(keep, s, NEG_BIG)
        chunks = [s[:, j * LANES:(j + 1) * LANES] for j in range(nch)]
        mc = chunks[0]
        for ch in chunks[1:]:
            mc = jnp.maximum(mc, ch)
        m_old = m_ref[c]
        m_new = jnp.maximum(m_old, jnp.max(mc, axis=1, keepdims=True))
        alpha = jnp.exp2(m_old - m_new)
        ps = [jnp.exp2(ch - m_new) for ch in chunks]
        lsum = ps[0]
        for pj in ps[1:]:
            lsum = lsum + pj
        p = jnp.concatenate([pj.astype(bf16) for pj in ps], axis=1)
        l_ref[c] = alpha * l_ref[c] + lsum
        acc_ref[c] = alpha * acc_ref[c] + _dot(p, v)
        m_ref[c] = m_new


def _attn_kernel(qt_ref, kt_ref, q_ref, k_ref, v_ref, lam_ref, g_ref, o_ref,
                 m_ref, l_ref, acc_ref, *, lam_init):
    step = pl.program_id(2)
    qi = qt_ref[step]
    ki = kt_ref[step]

    @pl.when(ki == 0)
    def _():
        m_ref[...] = jnp.full_like(m_ref, NEG_BIG)
        l_ref[...] = jnp.zeros_like(l_ref)
        acc_ref[...] = jnp.zeros_like(acc_ref)

    @pl.when(ki < qi)
    def _():
        _attn_block(q_ref, k_ref, v_ref, m_ref, l_ref, acc_ref, diagonal=False)

    @pl.when(ki == qi)
    def _():
        _attn_block(q_ref, k_ref, v_ref, m_ref, l_ref, acc_ref, diagonal=True)
        lam = lam_ref[...]
        l0 = jnp.sum(l_ref[0], axis=1, keepdims=True)
        l1 = jnp.sum(l_ref[1], axis=1, keepdims=True)
        o = acc_ref[0] / l0 - lam * (acc_ref[1] / l1)
        o = o * lax.rsqrt(jnp.mean(o * o, axis=-1, keepdims=True) + EPS)
        o_ref[...] = (o * g_ref[...] * (1.0 - lam_init)).astype(bf16)


def _attention(qh, kh, vh, lam, da_norm_g, batch, seq, lam_init):
    t = qh.shape[0]
    blk = ATT_BLOCK
    nb = seq // blk
    pairs = [(i, j) for i in range(nb) for j in range(i + 1)]
    qt = jnp.asarray([p[0] for p in pairs], jnp.int32)
    kt = jnp.asarray([p[1] for p in pairs], jnp.int32)
    w = 2 * DA_DH
    grid_spec = pltpu.PrefetchScalarGridSpec(
        num_scalar_prefetch=2,
        grid=(batch, DA_HEADS, len(pairs)),
        in_specs=[pl.BlockSpec((blk, 2 * LANES), lambda b, h, s, qt, kt: (b * nb + qt[s], h)),
                  pl.BlockSpec((blk, 2 * LANES), lambda b, h, s, qt, kt: (b * nb + kt[s], h)),
                  pl.BlockSpec((blk, w), lambda b, h, s, qt, kt: (b * nb + kt[s], h)),
                  pl.BlockSpec((1, 1), lambda b, h, s, qt, kt: (0, 0)),
                  pl.BlockSpec((1, w), lambda b, h, s, qt, kt: (0, 0))],
        out_specs=pl.BlockSpec((blk, w), lambda b, h, s, qt, kt: (b * nb + qt[s], h)),
        scratch_shapes=[pltpu.VMEM((2, blk, LANES), f32),
                        pltpu.VMEM((2, blk, LANES), f32),
                        pltpu.VMEM((2, blk, w), f32)],
    )
    return pl.pallas_call(
        functools.partial(_attn_kernel, lam_init=lam_init),
        out_shape=jax.ShapeDtypeStruct((t, DA_W), bf16),
        grid_spec=grid_spec,
        compiler_params=_cparams(3),
        name="attn",
    )(qt, kt, qh, kh, vh, lam, da_norm_g)


def _first_index_of_max(x, iota_f, size):
    m = jnp.max(x, axis=0, keepdims=True)
    idx = jnp.min(jnp.where(x == m, iota_f, float(size)), axis=0, keepdims=True)
    return m, idx


def _route_kernel(hm_ref, hd_ref, x_ref, mod_ref, wo_ref, wr_ref, rb_ref,
                  x1_ref, h2p_ref, idx_ref, gate_ref, rank_ref, cnt_ref, base_ref):
    tm = TM_ROUTE
    i = pl.program_id(0)

    @pl.when(i == 0)
    def _():
        base_ref[...] = jnp.zeros_like(base_ref)

    mix = _dot(hm_ref[...], wo_ref[0:ML_W, :]) + _dot(hd_ref[...], wo_ref[ML_W:ML_W + DA_W, :])
    x1 = x_ref[...] + mod_ref[0, 2:3, :] * mix
    x1_ref[...] = x1
    h2 = x1 * lax.rsqrt(jnp.mean(x1 * x1, axis=-1, keepdims=True) + EPS)
    h2 = h2 * (1.0 + mod_ref[0, 4:5, :]) + mod_ref[0, 3:4, :]
    h2p_ref[...] = _pack_bf16_pair(h2[:, 0:HALF], h2[:, HALF:D_MODEL])

    sc = _sigmoid(_dot_nt(wr_ref[...], h2.astype(bf16)))
    sel = sc + rb_ref[...]

    gi = lax.broadcasted_iota(jnp.int32, (GROUP_SIZE, tm), 0).astype(f32)
    gscores = []
    for g in range(N_GROUPS):
        blk = sel[g * GROUP_SIZE:(g + 1) * GROUP_SIZE, :]
        m1, i1 = _first_index_of_max(blk, gi, GROUP_SIZE)
        m2 = jnp.max(jnp.where(gi == i1, NEG_BIG, blk), axis=0, keepdims=True)
        gscores.append(m1 + m2)
    gs = jnp.concatenate(gscores, axis=0)

    ngi = lax.broadcasted_iota(jnp.int32, (N_GROUPS, tm), 0).astype(f32)
    gkeep = jnp.zeros((N_GROUPS, tm), f32)
    for _ in range(TOPK_GROUPS):
        _, gidx = _first_index_of_max(gs, ngi, N_GROUPS)
        hit = ngi == gidx
        gkeep = jnp.where(hit, 1.0, gkeep)
        gs = jnp.where(hit, NEG_BIG, gs)
    masked = jnp.concatenate(
        [jnp.where(gkeep[g:g + 1, :] > 0.0, sel[g * GROUP_SIZE:(g + 1) * GROUP_SIZE, :], NEG_BIG)
         for g in range(N_GROUPS)], axis=0)

    ei = lax.broadcasted_iota(jnp.int32, (N_EXPERTS, tm), 0).astype(f32)
    idxs, gates = [], []
    chosen = jnp.zeros((N_EXPERTS, tm), f32)
    for _ in range(TOP_K):
        _, eidx = _first_index_of_max(masked, ei, N_EXPERTS)
        hit = ei == eidx
        idxs.append(eidx)
        gates.append(jnp.sum(jnp.where(hit, sc, 0.0), axis=0, keepdims=True))
        chosen = jnp.where(hit, 1.0, chosen)
        masked = jnp.where(hit, NEG_BIG, masked)
    gate = jnp.concatenate(gates, axis=0)
    gate = gate / jnp.sum(gate, axis=0, keepdims=True) * ROUTED_SCALE
    gate_ref[...] = gate
    idx_ref[...] = jnp.concatenate(idxs, axis=0).astype(jnp.int32)

    tr = lax.broadcasted_iota(jnp.int32, (tm, tm), 0)
    tc = lax.broadcasted_iota(jnp.int32, (tm, tm), 1)
    before = (tr < tc).astype(bf16)
    seen = base_ref[...] + _dot(chosen.astype(bf16), before)
    ranks = [jnp.sum(jnp.where(ei == eidx, seen, 0.0), axis=0, keepdims=True) for eidx in idxs]
    rank_ref[...] = jnp.concatenate(ranks, axis=0).astype(jnp.int32)
    total = base_ref[...] + jnp.sum(chosen, axis=1, keepdims=True)
    base_ref[...] = total
    cnt_ref[...] = total


def _route(hm, hd, x2, mod, w_out, w_router_t, rbias_col, seq):
    t, d = x2.shape
    tm = TM_ROUTE
    tiles_per_seq = seq // tm
    const = lambda shape: pl.BlockSpec(shape, lambda i: (0,) * len(shape))
    return pl.pallas_call(
        _route_kernel,
        out_shape=(jax.ShapeDtypeStruct((t, d), f32),
                   jax.ShapeDtypeStruct((t, HALF), jnp.uint32),
                   jax.ShapeDtypeStruct((TOP_K, t), jnp.int32),
                   jax.ShapeDtypeStruct((TOP_K, t), f32),
                   jax.ShapeDtypeStruct((TOP_K, t), jnp.int32),
                   jax.ShapeDtypeStruct((N_EXPERTS, 1), f32)),
        grid=(t // tm,),
        in_specs=[pl.BlockSpec((tm, ML_W), lambda i: (i, 0)),
                  pl.BlockSpec((tm, DA_W), lambda i: (i, 0)),
                  pl.BlockSpec((tm, d), lambda i: (i, 0)),
                  pl.BlockSpec((1, 6, d), lambda i: (i // tiles_per_seq, 0, 0)),
                  const((d, d)), const((N_EXPERTS, d)), const((N_EXPERTS, 1))],
        out_specs=(pl.BlockSpec((tm, d), lambda i: (i, 0)),
                   pl.BlockSpec((tm, HALF), lambda i: (i, 0)),
                   pl.BlockSpec((TOP_K, tm), lambda i: (0, i)),
                   pl.BlockSpec((TOP_K, tm), lambda i: (0, i)),
                   pl.BlockSpec((TOP_K, tm), lambda i: (0, i)),
                   const((N_EXPERTS, 1))),
        scratch_shapes=[pltpu.VMEM((N_EXPERTS, 1), f32)],
        compiler_params=_cparams(1),
        name="route",
    )(hm, hd, x2, mod, w_out, w_router_t, rbias_col)


SLOT_RADIX = 256
SLOT_DIGITS = 3


def _slots_kernel(idx_ref, rank_ref, dig_ref, dest_ref):
    tm = TM_SLOTS
    ei = lax.broadcasted_iota(jnp.int32, (N_EXPERTS, tm), 0)
    rows = []
    for k in range(TOP_K):
        onehot = jnp.where(ei == idx_ref[k:k + 1, :], 1.0, 0.0).astype(bf16)
        dg = _dot(dig_ref[...], onehot)
        start = dg[0:1, :]
        for j in range(1, SLOT_DIGITS):
            start = start + dg[j:j + 1, :] * float(SLOT_RADIX ** j)
        rows.append(start.astype(jnp.int32) + rank_ref[k:k + 1, :])
    dest_ref[...] = jnp.concatenate(rows, axis=0)


def _slots(idx, rank, digits):
    t = idx.shape[1]
    tm = TM_SLOTS
    return pl.pallas_call(
        _slots_kernel,
        out_shape=jax.ShapeDtypeStruct((TOP_K, t), jnp.int32),
        grid=(t // tm,),
        in_specs=[pl.BlockSpec((TOP_K, tm), lambda i: (0, i)),
                  pl.BlockSpec((TOP_K, tm), lambda i: (0, i)),
                  pl.BlockSpec((SUBLANES, N_EXPERTS), lambda i: (0, 0))],
        out_specs=pl.BlockSpec((TOP_K, tm), lambda i: (0, i)),
        compiler_params=_cparams(1),
        name="slots",
    )(idx, rank, digits)


def _row_copy(src_ref, src_row, dst_ref, dst_row, sem):
    return pltpu.make_async_copy(src_ref.at[pl.ds(src_row, 1)], dst_ref.at[pl.ds(dst_row, 1)], sem)


def _dispatch_kernel(dest_ref, h_ref, xs_ref, sem):
    tm = TM_MOE

    def issue(tok, carry):
        for k in range(TOP_K):
            _row_copy(h_ref, tok, xs_ref, dest_ref[k, tok], sem).start(priority=k % 2)
        return carry

    lax.fori_loop(0, tm, issue, 0)

    def drain(tok, carry):
        for k in range(TOP_K):
            _row_copy(h_ref, tok, xs_ref, dest_ref[k, tok], sem).wait()
        return carry

    lax.fori_loop(0, tm, drain, 0)


def _dispatch(dest, h2p, n_slots):
    t = h2p.shape[0]
    tm = TM_MOE
    return pl.pallas_call(
        _dispatch_kernel,
        out_shape=jax.ShapeDtypeStruct((n_slots, HALF), jnp.uint32),
        grid=(t // tm,),
        in_specs=[pl.BlockSpec((TOP_K, tm), lambda i: (0, i), memory_space=pltpu.SMEM),
                  pl.BlockSpec((tm, HALF), lambda i: (i, 0))],
        out_specs=pl.BlockSpec(memory_space=pl.ANY),
        scratch_shapes=[pltpu.SemaphoreType.DMA(())],
        compiler_params=_cparams(1),
        name="dispatch",
    )(dest, h2p)


def _expert_kernel(start_ref, cnt_ref, xs_ref, w1_ref, w3_ref, w2_ref, ys_ref,
                   xbuf, ybuf, w1b, w3b, w2b, in_sem, out_sem):
    bm = MOE_BLOCK
    e = pl.program_id(0)
    base = start_ref[e]
    cnt = cnt_ref[e]
    nb = lax.shift_right_logical(cnt + (bm - 1), int(math.log2(bm)))

    def rows(j):
        return pl.ds(pl.multiple_of(base + j * bm, bm), bm)

    def in_copy(j, slot):
        return pltpu.make_async_copy(xs_ref.at[rows(j)], xbuf.at[slot], in_sem.at[slot])

    def out_copy(j, slot):
        return pltpu.make_async_copy(ybuf.at[slot], ys_ref.at[rows(j)], out_sem.at[slot])

    @pl.when(nb > 0)
    def _():
        in_copy(0, 0).start()
        w1b[...] = w1_ref[0].astype(bf16)
        w3b[...] = w3_ref[0].astype(bf16)
        w2b[...] = w2_ref[0].astype(bf16)

    def block(j, slot):
        @pl.when(j < nb)
        def _():
            @pl.when(j + 1 < nb)
            def _():
                in_copy(j + 1, 1 - slot).start()

            in_copy(j, slot).wait()

            @pl.when(j >= 2)
            def _():
                out_copy(j - 2, slot).wait()

            row = j * bm + lax.broadcasted_iota(jnp.int32, (bm, HALF), 0)
            lo, hi = _unpack_bf16_pair(jnp.where(row < cnt, xbuf[slot], jnp.uint32(0)))
            lo = lo.astype(bf16)
            hi = hi.astype(bf16)
            h1 = _dot(lo, w1b[0:HALF, :]) + _dot(hi, w1b[HALF:D_MODEL, :])
            h3 = _dot(lo, w3b[0:HALF, :]) + _dot(hi, w3b[HALF:D_MODEL, :])
            y = _dot((_silu(h1) * h3).astype(bf16), w2b[...])
            ybuf[slot] = _pack_bf16_pair(y[:, 0:HALF], y[:, HALF:D_MODEL])
            out_copy(j, slot).start()

    def pair(i, carry):
        block(2 * i, 0)
        block(2 * i + 1, 1)
        return carry

    lax.fori_loop(0, lax.shift_right_logical(nb + 1, 1), pair, 0)

    for back in (2, 1):
        j = nb - back

        @pl.when(j >= 0)
        def _():
            out_copy(j, jnp.bitwise_and(j, 1)).wait()


def _experts(start, counts, xs, w1, w3, w2):
    n_pad = xs.shape[0]
    bm = MOE_BLOCK
    grid_spec = pltpu.PrefetchScalarGridSpec(
        num_scalar_prefetch=2,
        grid=(N_EXPERTS,),
        in_specs=[pl.BlockSpec(memory_space=pl.ANY),
                  pl.BlockSpec((1, D_MODEL, D_EXPERT), lambda e, st, ct: (e, 0, 0)),
                  pl.BlockSpec((1, D_MODEL, D_EXPERT), lambda e, st, ct: (e, 0, 0)),
                  pl.BlockSpec((1, D_EXPERT, D_MODEL), lambda e, st, ct: (e, 0, 0))],
        out_specs=pl.BlockSpec(memory_space=pl.ANY),
        scratch_shapes=[pltpu.VMEM((2, bm, HALF), jnp.uint32),
                        pltpu.VMEM((2, bm, HALF), jnp.uint32),
                        pltpu.VMEM((D_MODEL, D_EXPERT), bf16),
                        pltpu.VMEM((D_MODEL, D_EXPERT), bf16),
                        pltpu.VMEM((D_EXPERT, D_MODEL), bf16),
                        pltpu.SemaphoreType.DMA((2,)),
                        pltpu.SemaphoreType.DMA((2,))],
    )
    return pl.pallas_call(
        _expert_kernel,
        out_shape=jax.ShapeDtypeStruct((n_pad, HALF), jnp.uint32),
        grid_spec=grid_spec,
        compiler_params=_cparams(1),
        name="experts",
    )(start, counts, xs, w1, w3, w2)


def _combine_kernel(dest_ref, dnext_ref, gate_ref, h_ref, x1_ref, mod_ref, ws1_ref, ws3_ref, ws2_ref,
                    ys_ref, out_ref, buf_ref, shared_ref, sem):
    tm = TM_MOE
    i = pl.program_id(0)
    slot = jnp.bitwise_and(i, 1)
    nslot = 1 - slot

    def gather(dref, tok, k, s):
        return _row_copy(ys_ref, dref[k, tok], buf_ref.at[s, k], tok, sem.at[s])

    @pl.when(i == 0)
    def _():
        def issue_first(tok, carry):
            for k in range(TOP_K):
                gather(dest_ref, tok, k, 0).start()
            return carry

        lax.fori_loop(0, tm, issue_first, 0)

    lo, hi = _unpack_bf16_pair(h_ref[...])
    lo = lo.astype(bf16)
    hi = hi.astype(bf16)
    s1 = _dot(lo, ws1_ref[0:HALF, :]) + _dot(hi, ws1_ref[HALF:D_MODEL, :])
    s3 = _dot(lo, ws3_ref[0:HALF, :]) + _dot(hi, ws3_ref[HALF:D_MODEL, :])
    shared_ref[...] = _dot((_silu(s1) * s3).astype(bf16), ws2_ref[...])

    def drain(tok, carry):
        for k in range(TOP_K):
            gather(dest_ref, tok, k, slot).wait()
        return carry

    lax.fori_loop(0, tm, drain, 0)

    gate_f = mod_ref[0, 5:6, :]

    def group(g, carry):
        r0 = pl.multiple_of(g * SUBLANES, SUBLANES)
        for r in range(SUBLANES):
            for k in range(TOP_K):
                gather(dnext_ref, r0 + r, k, nslot).start()
        rows = pl.ds(r0, SUBLANES)
        gates = gate_ref[rows, :]
        acc_lo = shared_ref[rows, 0:HALF]
        acc_hi = shared_ref[rows, HALF:D_MODEL]
        for k in range(TOP_K):
            rlo, rhi = _unpack_bf16_pair(buf_ref[slot, k, rows, :])
            gk = gates[:, k:k + 1]
            acc_lo = acc_lo + gk * rlo
            acc_hi = acc_hi + gk * rhi
        out_ref[rows, 0:HALF] = x1_ref[rows, 0:HALF] + gate_f[:, 0:HALF] * acc_lo
        out_ref[rows, HALF:D_MODEL] = x1_ref[rows, HALF:D_MODEL] + gate_f[:, HALF:D_MODEL] * acc_hi
        return carry

    lax.fori_loop(0, tm // SUBLANES, group, 0)

    @pl.when(i == pl.num_programs(0) - 1)
    def _():
        def drain_last(tok, carry):
            for k in range(TOP_K):
                gather(dnext_ref, tok, k, nslot).wait()
            return carry

        lax.fori_loop(0, tm, drain_last, 0)


def _combine(dest, gate_col, h2p, x1, mod, ws1, ws3, ws2, ys, seq):
    t, d = x1.shape
    tm = TM_MOE
    n_tiles = t // tm
    tiles_per_seq = seq // tm
    const = lambda shape: pl.BlockSpec(shape, lambda i: (0,) * len(shape))
    return pl.pallas_call(
        _combine_kernel,
        out_shape=jax.ShapeDtypeStruct((t, d), f32),
        grid=(n_tiles,),
        in_specs=[pl.BlockSpec((TOP_K, tm), lambda i: (0, i), memory_space=pltpu.SMEM),
                  pl.BlockSpec((TOP_K, tm), lambda i: (0, jnp.minimum(i + 1, n_tiles - 1)),
                               memory_space=pltpu.SMEM),
                  pl.BlockSpec((tm, TOP_K), lambda i: (i, 0)),
                  pl.BlockSpec((tm, HALF), lambda i: (i, 0)),
                  pl.BlockSpec((tm, d), lambda i: (i, 0)),
                  pl.BlockSpec((1, 6, d), lambda i: (i // tiles_per_seq, 0, 0)),
                  const((d, D_EXPERT)), const((d, D_EXPERT)), const((D_EXPERT, d)),
                  pl.BlockSpec(memory_space=pl.ANY)],
        out_specs=pl.BlockSpec((tm, d), lambda i: (i, 0)),
        scratch_shapes=[pltpu.VMEM((2, TOP_K, tm, HALF), jnp.uint32),
                        pltpu.VMEM((tm, d), f32),
                        pltpu.SemaphoreType.DMA((2,))],
        compiler_params=_cparams(1),
        name="combine",
    )(dest, dest, gate_col, h2p, x1, mod, ws1, ws3, ws2, ys)


def _lambda_init(layer):
    return 0.8 - 0.6 * math.exp(-0.3 * layer)


def _layer(x, c, w_ada, b_ada, w_in, conv_w, conv_b, gate_b, ml_norm_g, da_q_norm_g, da_k_norm_g,
           lambda_q1, lambda_k1, lambda_q2, lambda_k2, da_norm_g, w_out, w_router, router_bias,
           w1, w3, w2, ws1, ws3, ws2, layer):
    batch, seq, d = x.shape
    t = batch * seq
    lam_init = _lambda_init(layer)
    x2 = x.reshape(t, d)

    c_pad = jnp.pad(c, ((0, -batch % SUBLANES), (0, 0)))
    mod = _adaln(c_pad, w_ada, b_ada.reshape(1, -1))[:batch].reshape(batch, 6, d)

    g0 = 4 * ML_W
    q0 = g0 + N_GATES
    n_grp = 2 * DA_HEADS

    def lane_groups(cols):
        r = cols.shape[0]
        return jnp.pad(cols.reshape(r, n_grp, DA_DH), ((0, 0), (0, 0), (0, LANES - DA_DH))).reshape(r, -1)

    w_main = jnp.concatenate([w_in[:, :g0],
                              lane_groups(w_in[:, q0:q0 + DA_W]),
                              lane_groups(w_in[:, q0 + DA_W:q0 + 2 * DA_W]),
                              w_in[:, q0 + 2 * DA_W:]], axis=1).astype(bf16)
    w_gate = w_in[:, g0:g0 + N_GATES]
    wg = jnp.pad(w_gate, ((0, 0), (0, LANES - N_GATES))).astype(bf16)
    wgt = w_gate.T.astype(bf16)
    gb_row = jnp.pad(gate_b, (0, LANES - N_GATES)).reshape(1, LANES)
    gb_col = gate_b.reshape(N_GATES, 1)
    qgain = lane_groups((jnp.tile(da_q_norm_g, n_grp) * (DA_DH ** -0.5 * LOG2E)).reshape(1, DA_W))
    kgain = lane_groups(jnp.tile(da_k_norm_g, n_grp).reshape(1, DA_W))

    pm, qh, kh, vh, gcol, grow = _inproj(x2, mod, w_main, wg, wgt, gb_row, gb_col, qgain, kgain, seq)

    hm = _mlstm(pm, gcol, grow, conv_w, conv_b.reshape(1, -1), ml_norm_g.reshape(1, -1), batch, seq)

    lam = (jnp.exp(jnp.sum(lambda_q1 * lambda_k1)) - jnp.exp(jnp.sum(lambda_q2 * lambda_k2))
           + lam_init).reshape(1, 1).astype(f32)
    hd = _attention(qh, kh, vh, lam, da_norm_g.reshape(1, -1), batch, seq, lam_init)

    x1, h2p, idx, gate, rank, counts = _route(hm, hd, x2, mod, w_out.astype(bf16),
                                              w_router.T.astype(bf16), router_bias.reshape(-1, 1), seq)

    bm = MOE_BLOCK
    n_blocks = (t * TOP_K) // bm + N_EXPERTS
    counts_i = counts.reshape(-1).astype(jnp.int32)
    padded = (counts_i + bm - 1) // bm * bm
    pad_end = jnp.cumsum(padded)
    pad_start = pad_end - padded
    digits = jnp.stack([(pad_start // SLOT_RADIX ** j) % SLOT_RADIX for j in range(SLOT_DIGITS)])
    digits = jnp.pad(digits, ((0, SUBLANES - SLOT_DIGITS), (0, 0))).astype(bf16)
    dest = _slots(idx, rank, digits)

    xs = _dispatch(dest, h2p, n_blocks * bm)
    ys = _experts(pad_start.astype(jnp.int32), counts_i, xs, w1, w3, w2)
    out = _combine(dest, gate.T, h2p, x1, mod, ws1.astype(bf16), ws3.astype(bf16), ws2.astype(bf16), ys, seq)
    return out.reshape(batch, seq, d)


def kernel(x, c, w_ada, b_ada, w_in, conv_w, conv_b, gate_b, ml_norm_g, da_q_norm_g, da_k_norm_g,
           lambda_q1, lambda_k1, lambda_q2, lambda_k2, da_norm_g, w_out, w_router, router_bias,
           w1, w3, w2, ws1, ws3, ws2):
    depth = w_ada.shape[0]
    for l in range(depth):
        x = _layer(x, c, w_ada[l], b_ada[l], w_in[l], conv_w[l], conv_b[l], gate_b[l], ml_norm_g[l],
                   da_q_norm_g[l], da_k_norm_g[l], lambda_q1[l], lambda_k1[l], lambda_q2[l], lambda_k2[l],
                   da_norm_g[l], w_out[l], w_router[l], router_bias[l], w1[l], w3[l], w2[l],
                   ws1[l], ws3[l], ws2[l], l)
    return x
```

```python
import functools
import math

import jax
import jax.numpy as jnp
from jax import lax
from jax.experimental import pallas as pl
from jax.experimental.pallas import tpu as pltpu

D_MODEL = 1024
ML_HEADS = 4
ML_DH = 128
ML_W = ML_HEADS * ML_DH
ML_CHUNK = 128
CONV_K = 4
DA_HEADS = 4
DA_DH = 64
DA_W = DA_HEADS * 2 * DA_DH
N_EXPERTS = 256
TOP_K = 8
N_GROUPS = 8
GROUP_SIZE = N_EXPERTS // N_GROUPS
TOPK_GROUPS = 4
D_EXPERT = 256
ROUTED_SCALE = 2.5
EPS = 1e-6
N_GATES = 2 * ML_HEADS

LANES = 128
SUBLANES = 8
VMEM_LIMIT_BYTES = 48 * 1024 * 1024

TM_INPROJ = 512
ATT_BLOCK = 1024
TM_ROUTE = 256
TM_SLOTS = 512
TM_MOE = 256
MOE_BLOCK = 256
HALF = D_MODEL // 2

NEG_BIG = -1e30
LOG2E = 1.4426950408889634

f32 = jnp.float32
bf16 = jnp.bfloat16
HIGHEST = lax.Precision.HIGHEST


def _cparams(n_axes):
    return pltpu.CompilerParams(dimension_semantics=("arbitrary",) * n_axes,
                                vmem_limit_bytes=VMEM_LIMIT_BYTES)


def _dot(a, b):
    return jnp.dot(a, b, preferred_element_type=f32)


def _dot_nt(a, b):
    return lax.dot_general(a, b, (((1,), (1,)), ((), ())), preferred_element_type=f32)


def _sigmoid(x):
    return 1.0 / (1.0 + jnp.exp(-x))


def _silu(x):
    return x * _sigmoid(x)


def _log_sigmoid(x):
    return jnp.minimum(x, 0.0) - jnp.log(1.0 + jnp.exp(-jnp.abs(x)))


def _pack_bf16_pair(lo, hi):
    lo_bits = lax.bitcast_convert_type(lo.astype(bf16).astype(f32), jnp.uint32)
    hi_bits = lax.bitcast_convert_type(hi.astype(bf16).astype(f32), jnp.uint32)
    return (hi_bits & jnp.uint32(0xFFFF0000)) | (lo_bits >> 16)


def _unpack_bf16_pair(w):
    lo = lax.bitcast_convert_type(w << 16, f32)
    hi = lax.bitcast_convert_type(w & jnp.uint32(0xFFFF0000), f32)
    return lo, hi


def _adaln_kernel(c_ref, w_ref, b_ref, o_ref):
    c = c_ref[...]
    o_ref[...] = jnp.dot(_silu(c), w_ref[...], precision=HIGHEST, preferred_element_type=f32) + b_ref[...]


def _adaln(c_pad, w_ada, b_ada):
    rows, d = c_pad.shape
    n = w_ada.shape[1]
    tn = 1536
    return pl.pallas_call(
        _adaln_kernel,
        out_shape=jax.ShapeDtypeStruct((rows, n), f32),
        grid=(n // tn,),
        in_specs=[pl.BlockSpec((rows, d), lambda j: (0, 0)),
                  pl.BlockSpec((d, tn), lambda j: (0, j)),
                  pl.BlockSpec((1, tn), lambda j: (0, j))],
        out_specs=pl.BlockSpec((rows, tn), lambda j: (0, j)),
        compiler_params=_cparams(1),
        name="adaln",
    )(c_pad, w_ada, b_ada)


def _alibi_slope_log2(head):
    return 2.0 ** (-8.0 * (head + 1) / DA_HEADS) * LOG2E


def _inproj_kernel(x_ref, mod_ref, w_ref, wg_ref, wgt_ref, gbr_ref, gbc_ref, qg_ref, kg_ref,
                   pm_ref, q_ref, k_ref, v_ref, gcol_ref, grow_ref, *, seq):
    tm = TM_INPROJ
    x = x_ref[...]
    h = x * lax.rsqrt(jnp.mean(x * x, axis=-1, keepdims=True) + EPS)
    h = h * (1.0 + mod_ref[0, 1:2, :]) + mod_ref[0, 0:1, :]
    hb = h.astype(bf16)
    cw = ML_W
    for n in range(4):
        pm_ref[:, n * cw:(n + 1) * cw] = _dot(hb, w_ref[:, n * cw:(n + 1) * cw]).astype(bf16)

    lane = lax.broadcasted_iota(jnp.int32, (tm, LANES), 1)
    q_aug = jnp.where(jnp.logical_and(lane >= DA_DH, lane < DA_DH + 3), 1.0, 0.0)
    pos = (pl.program_id(0) * tm) % seq + lax.broadcasted_iota(jnp.int32, (tm, 1), 0)
    pos = pos.astype(f32)
    q0 = 4 * cw
    k0 = q0 + 2 * DA_W
    yq = _dot(hb, w_ref[:, q0:q0 + 2 * DA_W])
    yk = _dot(hb, w_ref[:, k0:k0 + 2 * DA_W])
    for grp in range(2 * DA_HEADS):
        sl = slice(grp * LANES, (grp + 1) * LANES)
        y = yq[:, sl]
        ss = jnp.sum(y * y, axis=-1, keepdims=True)
        q_ref[:, sl] = (y * lax.rsqrt(ss * (1.0 / DA_DH) + EPS) * qg_ref[:, sl] + q_aug).astype(bf16)

        y = yk[:, sl]
        ss = jnp.sum(y * y, axis=-1, keepdims=True)
        r = pos * _alibi_slope_log2(grp // 2)
        r_hi = r.astype(bf16).astype(f32)
        r_mid = (r - r_hi).astype(bf16).astype(f32)
        r_lo = r - r_hi - r_mid
        k_aug = jnp.where(lane == DA_DH, r_hi,
                          jnp.where(lane == DA_DH + 1, r_mid, jnp.where(lane == DA_DH + 2, r_lo, 0.0)))
        k_ref[:, sl] = (y * lax.rsqrt(ss * (1.0 / DA_DH) + EPS) * kg_ref[:, sl] + k_aug).astype(bf16)

    v0 = k0 + 2 * DA_W
    v_ref[...] = _dot(hb, w_ref[:, v0:v0 + DA_W]).astype(bf16)
    gcol_ref[...] = _dot(hb, wg_ref[...]) + gbr_ref[...]
    grow_ref[...] = _dot_nt(wgt_ref[...], hb) + gbc_ref[...]


def _inproj(x2, mod, w_main, wg, wgt, gb_row, gb_col, qgain, kgain, seq):
    t, d = x2.shape
    tm = TM_INPROJ
    tiles_per_seq = seq // tm
    n_main = w_main.shape[1]
    const = lambda shape: pl.BlockSpec(shape, lambda i: (0,) * len(shape))
    return pl.pallas_call(
        functools.partial(_inproj_kernel, seq=seq),
        out_shape=(jax.ShapeDtypeStruct((t, 4 * ML_W), bf16),
                   jax.ShapeDtypeStruct((t, 2 * DA_W), bf16),
                   jax.ShapeDtypeStruct((t, 2 * DA_W), bf16),
                   jax.ShapeDtypeStruct((t, DA_W), bf16),
                   jax.ShapeDtypeStruct((t, LANES), f32),
                   jax.ShapeDtypeStruct((N_GATES, t), f32)),
        grid=(t // tm,),
        in_specs=[pl.BlockSpec((tm, d), lambda i: (i, 0)),
                  pl.BlockSpec((1, 6, d), lambda i: (i // tiles_per_seq, 0, 0)),
                  const((d, n_main)), const((d, LANES)), const((N_GATES, d)),
                  const((1, LANES)), const((N_GATES, 1)),
                  const((1, 2 * DA_W)), const((1, 2 * DA_W))],
        out_specs=(pl.BlockSpec((tm, 4 * ML_W), lambda i: (i, 0)),
                   pl.BlockSpec((tm, 2 * DA_W), lambda i: (i, 0)),
                   pl.BlockSpec((tm, 2 * DA_W), lambda i: (i, 0)),
                   pl.BlockSpec((tm, DA_W), lambda i: (i, 0)),
                   pl.BlockSpec((tm, LANES), lambda i: (i, 0)),
                   pl.BlockSpec((N_GATES, tm), lambda i: (0, i))),
        compiler_params=_cparams(1),
        name="inproj",
    )(x2, mod, w_main, wg, wgt, gb_row, gb_col, qgain, kgain)


def _mlstm_kernel(qk_ref, v_ref, o_ref, gcol_ref, grow_ref, cw_ref, cb_ref, ng_ref,
                  out_ref, state_ref, m_ref, prev_ref, ext_ref):
    L = ML_CHUNK
    dh = ML_DH
    c = pl.program_id(1)

    @pl.when(c == 0)
    def _():
        state_ref[...] = jnp.zeros_like(state_ref)
        m_ref[...] = jnp.zeros_like(m_ref)
        prev_ref[...] = jnp.zeros_like(prev_ref)

    cur = qk_ref[...].astype(f32)
    ext_ref[0:SUBLANES, :] = prev_ref[...]
    ext_ref[SUBLANES:SUBLANES + L, :] = cur
    prev_ref[...] = cur[L - SUBLANES:L, :]
    acc = cb_ref[...] + cw_ref[CONV_K - 1:CONV_K, :] * cur
    for j in range(CONV_K - 1):
        off = SUBLANES - (CONV_K - 1) + j
        acc = acc + cw_ref[j:j + 1, :] * ext_ref[off:off + L, :]
    qk = _silu(acc)

    row_i = lax.broadcasted_iota(jnp.int32, (L, L), 0)
    col_i = lax.broadcasted_iota(jnp.int32, (L, L), 1)
    causal = row_i >= col_i
    tril = causal.astype(f32)
    triu = (row_i <= col_i).astype(f32)
    gcol = gcol_ref[...]
    grow = grow_ref[...]
    bcol_all = jnp.dot(tril, _log_sigmoid(gcol), precision=HIGHEST, preferred_element_type=f32)
    brow_all = jnp.dot(_log_sigmoid(grow), triu, precision=HIGHEST, preferred_element_type=f32)

    lane = lax.broadcasted_iota(jnp.int32, (L, dh), 1)
    ones_col = (lane == 0).astype(f32)
    g = ng_ref[...]

    for hd in range(ML_HEADS):
        q = qk[:, hd * dh:(hd + 1) * dh]
        k = qk[:, ML_W + hd * dh:ML_W + (hd + 1) * dh] * (dh ** -0.5)
        v = v_ref[:, hd * dh:(hd + 1) * dh].astype(f32)
        v_aug = jnp.concatenate([v, ones_col], axis=1)
        qb = q.astype(bf16)
        kb = k.astype(bf16)

        bcol = bcol_all[:, ML_HEADS + hd:ML_HEADS + hd + 1]
        icol = gcol[:, hd:hd + 1]
        brow = brow_all[ML_HEADS + hd:ML_HEADS + hd + 1, :]
        irow = grow[hd:hd + 1, :]
        m_prev = m_ref[hd:hd + 1, 0:1]

        d_intra = jnp.where(causal, bcol - brow + irow, NEG_BIG)
        d_inter = bcol + m_prev
        m_t = jnp.maximum(d_inter, jnp.max(d_intra, axis=1, keepdims=True))
        w_intra = jnp.exp(d_intra - m_t)
        w_inter = jnp.exp(d_inter - m_t)

        s = _dot_nt(qb, kb) * w_intra
        state = state_ref[hd]
        q_state = _dot(qb, state.astype(bf16))
        s_v = _dot(s.astype(bf16), v_aug.astype(bf16))
        num = w_inter * q_state[:, 0:dh] + s_v[:, 0:dh]
        den = w_inter * q_state[:, dh:dh + 1] + s_v[:, dh:dh + 1]
        hval = num / jnp.maximum(jnp.abs(den), jnp.exp(-m_t))

        b_last = bcol[L - 1:L, :]
        d_state = b_last - bcol + icol
        m_new = jnp.maximum(b_last + m_prev, jnp.max(d_state, axis=0, keepdims=True))
        carry_scale = jnp.exp(b_last + m_prev - m_new)
        wk = jnp.exp(d_state - m_new)
        upd = _dot(k.T.astype(bf16), (v_aug * wk).astype(bf16))
        state_ref[hd] = carry_scale * state + upd
        m_ref[hd:hd + 1, :] = jnp.broadcast_to(m_new, (1, LANES))

        hn = hval * lax.rsqrt(jnp.mean(hval * hval, axis=-1, keepdims=True) + EPS) * g
        og = _sigmoid(o_ref[:, hd * dh:(hd + 1) * dh].astype(f32))
        out_ref[:, hd * dh:(hd + 1) * dh] = (hn * og).astype(bf16)


def _mlstm(pm, gcol, grow, conv_w, conv_b, ml_norm_g, batch, seq):
    t = pm.shape[0]
    L = ML_CHUNK
    nc = seq // L
    row = lambda b, c: b * nc + c
    return pl.pallas_call(
        _mlstm_kernel,
        out_shape=jax.ShapeDtypeStruct((t, ML_W), bf16),
        grid=(batch, nc),
        in_specs=[pl.BlockSpec((L, 2 * ML_W), lambda b, c: (row(b, c), 0)),
                  pl.BlockSpec((L, ML_W), lambda b, c: (row(b, c), 2)),
                  pl.BlockSpec((L, ML_W), lambda b, c: (row(b, c), 3)),
                  pl.BlockSpec((L, LANES), lambda b, c: (row(b, c), 0)),
                  pl.BlockSpec((N_GATES, L), lambda b, c: (0, row(b, c))),
                  pl.BlockSpec((CONV_K, 2 * ML_W), lambda b, c: (0, 0)),
                  pl.BlockSpec((1, 2 * ML_W), lambda b, c: (0, 0)),
                  pl.BlockSpec((1, ML_DH), lambda b, c: (0, 0))],
        out_specs=pl.BlockSpec((L, ML_W), lambda b, c: (row(b, c), 0)),
        scratch_shapes=[pltpu.VMEM((ML_HEADS, ML_DH, 2 * ML_DH), f32),
                        pltpu.VMEM((SUBLANES, LANES), f32),
                        pltpu.VMEM((SUBLANES, 2 * ML_W), f32),
                        pltpu.VMEM((SUBLANES + ML_CHUNK, 2 * ML_W), f32)],
        compiler_params=_cparams(2),
        name="mlstm",
    )(pm, pm, pm, gcol, grow, conv_w, conv_b, ml_norm_g)


def _attn_block(q_ref, k_ref, v_ref, m_ref, l_ref, acc_ref, diagonal):
    blk = ATT_BLOCK
    nch = blk // LANES
    v = v_ref[...]
    if diagonal:
        keep = (lax.broadcasted_iota(jnp.int32, (blk, blk), 1)
                <= lax.broadcasted_iota(jnp.int32, (blk, blk), 0))
    for c in range(2):
        sl = slice(c * LANES, (c + 1) * LANES)
        s = _dot_nt(q_ref[:, sl], k_ref[:, sl])
        if diagonal:
            s = jnp.where(keep, s, NEG_BIG)
        chunks = [s[:, j * LANES:(j + 1) * LANES] for j in range(nch)]
        mc = chunks[0]
        for ch in chunks[1:]:
            mc = jnp.maximum(mc, ch)
        m_old = m_ref[c]
        m_new = jnp.maximum(m_old, jnp.max(mc, axis=1, keepdims=True))
        alpha = jnp.exp2(m_old - m_new)
        ps = [jnp.exp2(ch - m_new) for ch in chunks]
        lsum = ps[0]
        for pj in ps[1:]:
            lsum = lsum + pj
        p = jnp.concatenate([pj.astype(bf16) for pj in ps], axis=1)
        l_ref[c] = alpha * l_ref[c] + lsum
        acc_ref[c] = alpha * acc_ref[c] + _dot(p, v)
        m_ref[c] = m_new


def _attn_kernel(qt_ref, kt_ref, q_ref, k_ref, v_ref, lam_ref, g_ref, o_ref,
                 m_ref, l_ref, acc_ref, *, lam_init):
    step = pl.program_id(2)
    qi = qt_ref[step]
    ki = kt_ref[step]

    @pl.when(ki == 0)
    def _():
        m_ref[...] = jnp.full_like(m_ref, NEG_BIG)
        l_ref[...] = jnp.zeros_like(l_ref)
        acc_ref[...] = jnp.zeros_like(acc_ref)

    @pl.when(ki < qi)
    def _():
        _attn_block(q_ref, k_ref, v_ref, m_ref, l_ref, acc_ref, diagonal=False)

    @pl.when(ki == qi)
    def _():
        _attn_block(q_ref, k_ref, v_ref, m_ref, l_ref, acc_ref, diagonal=True)
        lam = lam_ref[...]
        l0 = jnp.sum(l_ref[0], axis=1, keepdims=True)
        l1 = jnp.sum(l_ref[1], axis=1, keepdims=True)
        o = acc_ref[0] / l0 - lam * (acc_ref[1] / l1)
        o = o * lax.rsqrt(jnp.mean(o * o, axis=-1, keepdims=True) + EPS)
        o_ref[...] = (o * g_ref[...] * (1.0 - lam_init)).astype(bf16)


def _attention(qh, kh, vh, lam, da_norm_g, batch, seq, lam_init):
    t = qh.shape[0]
    blk = ATT_BLOCK
    nb = seq // blk
    pairs = [(i, j) for i in range(nb) for j in range(i + 1)]
    qt = jnp.asarray([p[0] for p in pairs], jnp.int32)
    kt = jnp.asarray([p[1] for p in pairs], jnp.int32)
    w = 2 * DA_DH
    grid_spec = pltpu.PrefetchScalarGridSpec(
        num_scalar_prefetch=2,
        grid=(batch, DA_HEADS, len(pairs)),
        in_specs=[pl.BlockSpec((blk, 2 * LANES), lambda b, h, s, qt, kt: (b * nb + qt[s], h)),
                  pl.BlockSpec((blk, 2 * LANES), lambda b, h, s, qt, kt: (b * nb + kt[s], h)),
                  pl.BlockSpec((blk, w), lambda b, h, s, qt, kt: (b * nb + kt[s], h)),
                  pl.BlockSpec((1, 1), lambda b, h, s, qt, kt: (0, 0)),
                  pl.BlockSpec((1, w), lambda b, h, s, qt, kt: (0, 0))],
        out_specs=pl.BlockSpec((blk, w), lambda b, h, s, qt, kt: (b * nb + qt[s], h)),
        scratch_shapes=[pltpu.VMEM((2, blk, LANES), f32),
                        pltpu.VMEM((2, blk, LANES), f32),
                        pltpu.VMEM((2, blk, w), f32)],
    )
    return pl.pallas_call(
        functools.partial(_attn_kernel, lam_init=lam_init),
        out_shape=jax.ShapeDtypeStruct((t, DA_W), bf16),
        grid_spec=grid_spec,
        compiler_params=_cparams(3),
        name="attn",
    )(qt, kt, qh, kh, vh, lam, da_norm_g)


def _first_index_of_max(x, iota_f, size):
    m = jnp.max(x, axis=0, keepdims=True)
    idx = jnp.min(jnp.where(x == m, iota_f, float(size)), axis=0, keepdims=True)
    return m, idx


def _route_kernel(hm_ref, hd_ref, x_ref, mod_ref, wo_ref, wr_ref, rb_ref,
                  x1_ref, h2p_ref, idx_ref, gate_ref, rank_ref, cnt_ref, base_ref):
    tm = TM_ROUTE
    i = pl.program_id(0)

    @pl.when(i == 0)
    def _():
        base_ref[...] = jnp.zeros_like(base_ref)

    mix = _dot(hm_ref[...], wo_ref[0:ML_W, :]) + _dot(hd_ref[...], wo_ref[ML_W:ML_W + DA_W, :])
    x1 = x_ref[...] + mod_ref[0, 2:3, :] * mix
    x1_ref[...] = x1
    h2 = x1 * lax.rsqrt(jnp.mean(x1 * x1, axis=-1, keepdims=True) + EPS)
    h2 = h2 * (1.0 + mod_ref[0, 4:5, :]) + mod_ref[0, 3:4, :]
    h2p_ref[...] = _pack_bf16_pair(h2[:, 0:HALF], h2[:, HALF:D_MODEL])

    sc = _sigmoid(_dot_nt(wr_ref[...], h2.astype(bf16)))
    sel = sc + rb_ref[...]

    gi = lax.broadcasted_iota(jnp.int32, (GROUP_SIZE, tm), 0).astype(f32)
    gscores = []
    for g in range(N_GROUPS):
        blk = sel[g * GROUP_SIZE:(g + 1) * GROUP_SIZE, :]
        m1, i1 = _first_index_of_max(blk, gi, GROUP_SIZE)
        m2 = jnp.max(jnp.where(gi == i1, NEG_BIG, blk), axis=0, keepdims=True)
        gscores.append(m1 + m2)
    gs = jnp.concatenate(gscores, axis=0)

    ngi = lax.broadcasted_iota(jnp.int32, (N_GROUPS, tm), 0).astype(f32)
    gkeep = jnp.zeros((N_GROUPS, tm), f32)
    for _ in range(TOPK_GROUPS):
        _, gidx = _first_index_of_max(gs, ngi, N_GROUPS)
        hit = ngi == gidx
        gkeep = jnp.where(hit, 1.0, gkeep)
        gs = jnp.where(hit, NEG_BIG, gs)
    masked = jnp.concatenate(
        [jnp.where(gkeep[g:g + 1, :] > 0.0, sel[g * GROUP_SIZE:(g + 1) * GROUP_SIZE, :], NEG_BIG)
         for g in range(N_GROUPS)], axis=0)

    ei = lax.broadcasted_iota(jnp.int32, (N_EXPERTS, tm), 0).astype(f32)
    idxs, gates = [], []
    chosen = jnp.zeros((N_EXPERTS, tm), f32)
    for _ in range(TOP_K):
        _, eidx = _first_index_of_max(masked, ei, N_EXPERTS)
        hit = ei == eidx
        idxs.append(eidx)
        gates.append(jnp.sum(jnp.where(hit, sc, 0.0), axis=0, keepdims=True))
        chosen = jnp.where(hit, 1.0, chosen)
        masked = jnp.where(hit, NEG_BIG, masked)
    gate = jnp.concatenate(gates, axis=0)
    gate = gate / jnp.sum(gate, axis=0, keepdims=True) * ROUTED_SCALE
    gate_ref[...] = gate
    idx_ref[...] = jnp.concatenate(idxs, axis=0).astype(jnp.int32)

    tr = lax.broadcasted_iota(jnp.int32, (tm, tm), 0)
    tc = lax.broadcasted_iota(jnp.int32, (tm, tm), 1)
    before = (tr < tc).astype(bf16)
    seen = base_ref[...] + _dot(chosen.astype(bf16), before)
    ranks = [jnp.sum(jnp.where(ei == eidx, seen, 0.0), axis=0, keepdims=True) for eidx in idxs]
    rank_ref[...] = jnp.concatenate(ranks, axis=0).astype(jnp.int32)
    total = base_ref[...] + jnp.sum(chosen, axis=1, keepdims=True)
    base_ref[...] = total
    cnt_ref[...] = total


def _route(hm, hd, x2, mod, w_out, w_router_t, rbias_col, seq):
    t, d = x2.shape
    tm = TM_ROUTE
    tiles_per_seq = seq // tm
    const = lambda shape: pl.BlockSpec(shape, lambda i: (0,) * len(shape))
    return pl.pallas_call(
        _route_kernel,
        out_shape=(jax.ShapeDtypeStruct((t, d), f32),
                   jax.ShapeDtypeStruct((t, HALF), jnp.uint32),
                   jax.ShapeDtypeStruct((TOP_K, t), jnp.int32),
                   jax.ShapeDtypeStruct((TOP_K, t), f32),
                   jax.ShapeDtypeStruct((TOP_K, t), jnp.int32),
                   jax.ShapeDtypeStruct((N_EXPERTS, 1), f32)),
        grid=(t // tm,),
        in_specs=[pl.BlockSpec((tm, ML_W), lambda i: (i, 0)),
                  pl.BlockSpec((tm, DA_W), lambda i: (i, 0)),
                  pl.BlockSpec((tm, d), lambda i: (i, 0)),
                  pl.BlockSpec((1, 6, d), lambda i: (i // tiles_per_seq, 0, 0)),
                  const((d, d)), const((N_EXPERTS, d)), const((N_EXPERTS, 1))],
        out_specs=(pl.BlockSpec((tm, d), lambda i: (i, 0)),
                   pl.BlockSpec((tm, HALF), lambda i: (i, 0)),
                   pl.BlockSpec((TOP_K, tm), lambda i: (0, i)),
                   pl.BlockSpec((TOP_K, tm), lambda i: (0, i)),
                   pl.BlockSpec((TOP_K, tm), lambda i: (0, i)),
                   const((N_EXPERTS, 1))),
        scratch_shapes=[pltpu.VMEM((N_EXPERTS, 1), f32)],
        compiler_params=_cparams(1),
        name="route",
    )(hm, hd, x2, mod, w_out, w_router_t, rbias_col)


SLOT_RADIX = 256
SLOT_DIGITS = 3


def _slots_kernel(idx_ref, rank_ref, dig_ref, dest_ref):
    tm = TM_SLOTS
    ei = lax.broadcasted_iota(jnp.int32, (N_EXPERTS, tm), 0)
    rows = []
    for k in range(TOP_K):
        onehot = jnp.where(ei == idx_ref[k:k + 1, :], 1.0, 0.0).astype(bf16)
        dg = _dot(dig_ref[...], onehot)
        start = dg[0:1, :]
        for j in range(1, SLOT_DIGITS):
            start = start + dg[j:j + 1, :] * float(SLOT_RADIX ** j)
        rows.append(start.astype(jnp.int32) + rank_ref[k:k + 1, :])
    dest_ref[...] = jnp.concatenate(rows, axis=0)


def _slots(idx, rank, digits):
    t = idx.shape[1]
    tm = TM_SLOTS
    return pl.pallas_call(
        _slots_kernel,
        out_shape=jax.ShapeDtypeStruct((TOP_K, t), jnp.int32),
        grid=(t // tm,),
        in_specs=[pl.BlockSpec((TOP_K, tm), lambda i: (0, i)),
                  pl.BlockSpec((TOP_K, tm), lambda i: (0, i)),
                  pl.BlockSpec((SUBLANES, N_EXPERTS), lambda i: (0, 0))],
        out_specs=pl.BlockSpec((TOP_K, tm), lambda i: (0, i)),
        compiler_params=_cparams(1),
        name="slots",
    )(idx, rank, digits)


def _row_copy(src_ref, src_row, dst_ref, dst_row, sem):
    return pltpu.make_async_copy(src_ref.at[pl.ds(src_row, 1)], dst_ref.at[pl.ds(dst_row, 1)], sem)


def _dispatch_kernel(dest_ref, h_ref, xs_ref, sem):
    tm = TM_MOE

    def issue(tok, carry):
        for k in range(TOP_K):
            _row_copy(h_ref, tok, xs_ref, dest_ref[k, tok], sem).start(priority=k % 2)
        return carry

    lax.fori_loop(0, tm, issue, 0)

    def drain(tok, carry):
        for k in range(TOP_K):
            _row_copy(h_ref, tok, xs_ref, dest_ref[k, tok], sem).wait()
        return carry

    lax.fori_loop(0, tm, drain, 0)


def _dispatch(dest, h2p, n_slots):
    t = h2p.shape[0]
    tm = TM_MOE
    return pl.pallas_call(
        _dispatch_kernel,
        out_shape=jax.ShapeDtypeStruct((n_slots, HALF), jnp.uint32),
        grid=(t // tm,),
        in_specs=[pl.BlockSpec((TOP_K, tm), lambda i: (0, i), memory_space=pltpu.SMEM),
                  pl.BlockSpec((tm, HALF), lambda i: (i, 0))],
        out_specs=pl.BlockSpec(memory_space=pl.ANY),
        scratch_shapes=[pltpu.SemaphoreType.DMA(())],
        compiler_params=_cparams(1),
        name="dispatch",
    )(dest, h2p)


EXPERT_RING = 4


def _expert_kernel(start_ref, cnt_ref, xs_ref, w1_ref, w3_ref, w2_ref, ys_ref,
                   xbuf, ybuf, w1b, w3b, w2b, in_sem, out_sem, done_ref):
    bm = MOE_BLOCK
    ring = EXPERT_RING
    ahead = ring - 1
    e = pl.program_id(0)
    last = pl.num_programs(0) - 1

    def n_blocks(ex):
        return lax.shift_right_logical(cnt_ref[ex] + (bm - 1), int(math.log2(bm)))

    def slot_of(g):
        return jnp.bitwise_and(g, ring - 1)

    def in_copy(ex_base, j, g):
        s = slot_of(g)
        src = xs_ref.at[pl.ds(pl.multiple_of(ex_base + j * bm, bm), bm)]
        return pltpu.make_async_copy(src, xbuf.at[s], in_sem.at[s])

    def out_copy(ex_base, j, g):
        s = slot_of(g)
        dst = ys_ref.at[pl.ds(pl.multiple_of(ex_base + j * bm, bm), bm)]
        return pltpu.make_async_copy(ybuf.at[s], dst, out_sem.at[s])

    def request_first_blocks(ex, g_first):
        nbx = n_blocks(ex)
        for j in range(ahead):
            @pl.when(j < nbx)
            def _():
                in_copy(start_ref[ex], j, g_first + j).start()

    @pl.when(e == 0)
    def _():
        done_ref[0] = 0
        request_first_blocks(0, 0)

    base = start_ref[e]
    cnt = cnt_ref[e]
    nb = n_blocks(e)
    g0 = done_ref[0]

    @pl.when(nb > 0)
    def _():
        w1b[...] = w1_ref[0].astype(bf16)
        w3b[...] = w3_ref[0].astype(bf16)
        w2b[...] = w2_ref[0].astype(bf16)

    def block(j, carry):
        g = g0 + j
        s = slot_of(g)

        @pl.when(j + ahead < nb)
        def _():
            in_copy(base, j + ahead, g + ahead).start()

        in_copy(base, j, g).wait()

        @pl.when(g >= ring)
        def _():
            out_copy(base, 0, g).wait()

        row = j * bm + lax.broadcasted_iota(jnp.int32, (bm, HALF), 0)
        lo, hi = _unpack_bf16_pair(jnp.where(row < cnt, xbuf[s], jnp.uint32(0)))
        lo = lo.astype(bf16)
        hi = hi.astype(bf16)
        h1 = _dot(lo, w1b[0:HALF, :]) + _dot(hi, w1b[HALF:D_MODEL, :])
        h3 = _dot(lo, w3b[0:HALF, :]) + _dot(hi, w3b[HALF:D_MODEL, :])
        y = _dot((_silu(h1) * h3).astype(bf16), w2b[...])
        ybuf[s] = _pack_bf16_pair(y[:, 0:HALF], y[:, HALF:D_MODEL])
        out_copy(base, j, g).start()
        return carry

    lax.fori_loop(0, nb, block, 0)
    g_end = g0 + nb
    done_ref[0] = g_end

    @pl.when(e < last)
    def _():
        request_first_blocks(e + 1, g_end)

    @pl.when(e == last)
    def _():
        for back in range(1, ring + 1):
            @pl.when(g_end - back >= 0)
            def _():
                out_copy(base, 0, g_end - back).wait()


def _experts(start, counts, xs, w1, w3, w2):
    n_pad = xs.shape[0]
    bm = MOE_BLOCK
    grid_spec = pltpu.PrefetchScalarGridSpec(
        num_scalar_prefetch=2,
        grid=(N_EXPERTS,),
        in_specs=[pl.BlockSpec(memory_space=pl.ANY),
                  pl.BlockSpec((1, D_MODEL, D_EXPERT), lambda e, st, ct: (e, 0, 0)),
                  pl.BlockSpec((1, D_MODEL, D_EXPERT), lambda e, st, ct: (e, 0, 0)),
                  pl.BlockSpec((1, D_EXPERT, D_MODEL), lambda e, st, ct: (e, 0, 0))],
        out_specs=pl.BlockSpec(memory_space=pl.ANY),
        scratch_shapes=[pltpu.VMEM((EXPERT_RING, bm, HALF), jnp.uint32),
                        pltpu.VMEM((EXPERT_RING, bm, HALF), jnp.uint32),
                        pltpu.VMEM((D_MODEL, D_EXPERT), bf16),
                        pltpu.VMEM((D_MODEL, D_EXPERT), bf16),
                        pltpu.VMEM((D_EXPERT, D_MODEL), bf16),
                        pltpu.SemaphoreType.DMA((EXPERT_RING,)),
                        pltpu.SemaphoreType.DMA((EXPERT_RING,)),
                        pltpu.SMEM((1,), jnp.int32)],
    )
    return pl.pallas_call(
        _expert_kernel,
        out_shape=jax.ShapeDtypeStruct((n_pad, HALF), jnp.uint32),
        grid_spec=grid_spec,
        compiler_params=_cparams(1),
        name="experts",
    )(start, counts, xs, w1, w3, w2)


def _combine_kernel(dest_ref, dnext_ref, gate_ref, h_ref, x1_ref, mod_ref, ws1_ref, ws3_ref, ws2_ref,
                    ys_ref, out_ref, buf_ref, shared_ref, sem):
    tm = TM_MOE
    i = pl.program_id(0)
    slot = jnp.bitwise_and(i, 1)
    nslot = 1 - slot

    def gather(dref, tok, k, s):
        return _row_copy(ys_ref, dref[k, tok], buf_ref.at[s, k], tok, sem.at[s])

    @pl.when(i == 0)
    def _():
        def issue_first(tok, carry):
            for k in range(TOP_K):
                gather(dest_ref, tok, k, 0).start()
            return carry

        lax.fori_loop(0, tm, issue_first, 0)

    lo, hi = _unpack_bf16_pair(h_ref[...])
    lo = lo.astype(bf16)
    hi = hi.astype(bf16)
    s1 = _dot(lo, ws1_ref[0:HALF, :]) + _dot(hi, ws1_ref[HALF:D_MODEL, :])
    s3 = _dot(lo, ws3_ref[0:HALF, :]) + _dot(hi, ws3_ref[HALF:D_MODEL, :])
    shared_ref[...] = _dot((_silu(s1) * s3).astype(bf16), ws2_ref[...])

    def drain(tok, carry):
        for k in range(TOP_K):
            gather(dest_ref, tok, k, slot).wait()
        return carry

    lax.fori_loop(0, tm, drain, 0)

    gate_f = mod_ref[0, 5:6, :]

    def group(g, carry):
        r0 = pl.multiple_of(g * SUBLANES, SUBLANES)
        for r in range(SUBLANES):
            for k in range(TOP_K):
                gather(dnext_ref, r0 + r, k, nslot).start()
        rows = pl.ds(r0, SUBLANES)
        gates = gate_ref[rows, :]
        acc_lo = shared_ref[rows, 0:HALF]
        acc_hi = shared_ref[rows, HALF:D_MODEL]
        for k in range(TOP_K):
            rlo, rhi = _unpack_bf16_pair(buf_ref[slot, k, rows, :])
            gk = gates[:, k:k + 1]
            acc_lo = acc_lo + gk * rlo
            acc_hi = acc_hi + gk * rhi
        out_ref[rows, 0:HALF] = x1_ref[rows, 0:HALF] + gate_f[:, 0:HALF] * acc_lo
        out_ref[rows, HALF:D_MODEL] = x1_ref[rows, HALF:D_MODEL] + gate_f[:, HALF:D_MODEL] * acc_hi
        return carry

    lax.fori_loop(0, tm // SUBLANES, group, 0)

    @pl.when(i == pl.num_programs(0) - 1)
    def _():
        def drain_last(tok, carry):
            for k in range(TOP_K):
                gather(dnext_ref, tok, k, nslot).wait()
            return carry

        lax.fori_loop(0, tm, drain_last, 0)


def _combine(dest, gate_col, h2p, x1, mod, ws1, ws3, ws2, ys, seq):
    t, d = x1.shape
    tm = TM_MOE
    n_tiles = t // tm
    tiles_per_seq = seq // tm
    const = lambda shape: pl.BlockSpec(shape, lambda i: (0,) * len(shape))
    return pl.pallas_call(
        _combine_kernel,
        out_shape=jax.ShapeDtypeStruct((t, d), f32),
        grid=(n_tiles,),
        in_specs=[pl.BlockSpec((TOP_K, tm), lambda i: (0, i), memory_space=pltpu.SMEM),
                  pl.BlockSpec((TOP_K, tm), lambda i: (0, jnp.minimum(i + 1, n_tiles - 1)),
                               memory_space=pltpu.SMEM),
                  pl.BlockSpec((tm, TOP_K), lambda i: (i, 0)),
                  pl.BlockSpec((tm, HALF), lambda i: (i, 0)),
                  pl.BlockSpec((tm, d), lambda i: (i, 0)),
                  pl.BlockSpec((1, 6, d), lambda i: (i // tiles_per_seq, 0, 0)),
                  const((d, D_EXPERT)), const((d, D_EXPERT)), const((D_EXPERT, d)),
                  pl.BlockSpec(memory_space=pl.ANY)],
        out_specs=pl.BlockSpec((tm, d), lambda i: (i, 0)),
        scratch_shapes=[pltpu.VMEM((2, TOP_K, tm, HALF), jnp.uint32),
                        pltpu.VMEM((tm, d), f32),
                        pltpu.SemaphoreType.DMA((2,))],
        compiler_params=_cparams(1),
        name="combine",
    )(dest, dest, gate_col, h2p, x1, mod, ws1, ws3, ws2, ys)


def _lambda_init(layer):
    return 0.8 - 0.6 * math.exp(-0.3 * layer)


def _layer(x, c, w_ada, b_ada, w_in, conv_w, conv_b, gate_b, ml_norm_g, da_q_norm_g, da_k_norm_g,
           lambda_q1, lambda_k1, lambda_q2, lambda_k2, da_norm_g, w_out, w_router, router_bias,
           w1, w3, w2, ws1, ws3, ws2, layer):
    batch, seq, d = x.shape
    t = batch * seq
    lam_init = _lambda_init(layer)
    x2 = x.reshape(t, d)

    c_pad = jnp.pad(c, ((0, -batch % SUBLANES), (0, 0)))
    mod = _adaln(c_pad, w_ada, b_ada.reshape(1, -1))[:batch].reshape(batch, 6, d)

    g0 = 4 * ML_W
    q0 = g0 + N_GATES
    n_grp = 2 * DA_HEADS

    def lane_groups(cols):
        r = cols.shape[0]
        return jnp.pad(cols.reshape(r, n_grp, DA_DH), ((0, 0), (0, 0), (0, LANES - DA_DH))).reshape(r, -1)

    w_main = jnp.concatenate([w_in[:, :g0],
                              lane_groups(w_in[:, q0:q0 + DA_W]),
                              lane_groups(w_in[:, q0 + DA_W:q0 + 2 * DA_W]),
                              w_in[:, q0 + 2 * DA_W:]], axis=1).astype(bf16)
    w_gate = w_in[:, g0:g0 + N_GATES]
    wg = jnp.pad(w_gate, ((0, 0), (0, LANES - N_GATES))).astype(bf16)
    wgt = w_gate.T.astype(bf16)
    gb_row = jnp.pad(gate_b, (0, LANES - N_GATES)).reshape(1, LANES)
    gb_col = gate_b.reshape(N_GATES, 1)
    qgain = lane_groups((jnp.tile(da_q_norm_g, n_grp) * (DA_DH ** -0.5 * LOG2E)).reshape(1, DA_W))
    kgain = lane_groups(jnp.tile(da_k_norm_g, n_grp).reshape(1, DA_W))

    pm, qh, kh, vh, gcol, grow = _inproj(x2, mod, w_main, wg, wgt, gb_row, gb_col, qgain, kgain, seq)

    hm = _mlstm(pm, gcol, grow, conv_w, conv_b.reshape(1, -1), ml_norm_g.reshape(1, -1), batch, seq)

    lam = (jnp.exp(jnp.sum(lambda_q1 * lambda_k1)) - jnp.exp(jnp.sum(lambda_q2 * lambda_k2))
           + lam_init).reshape(1, 1).astype(f32)
    hd = _attention(qh, kh, vh, lam, da_norm_g.reshape(1, -1), batch, seq, lam_init)

    x1, h2p, idx, gate, rank, counts = _route(hm, hd, x2, mod, w_out.astype(bf16),
                                              w_router.T.astype(bf16), router_bias.reshape(-1, 1), seq)

    bm = MOE_BLOCK
    n_blocks = (t * TOP_K) // bm + N_EXPERTS
    counts_i = counts.reshape(-1).astype(jnp.int32)
    padded = (counts_i + bm - 1) // bm * bm
    pad_end = jnp.cumsum(padded)
    pad_start = pad_end - padded
    digits = jnp.stack([(pad_start // SLOT_RADIX ** j) % SLOT_RADIX for j in range(SLOT_DIGITS)])
    digits = jnp.pad(digits, ((0, SUBLANES - SLOT_DIGITS), (0, 0))).astype(bf16)
    dest = _slots(idx, rank, digits)

    xs = _dispatch(dest, h2p, n_blocks * bm)
    ys = _experts(pad_start.astype(jnp.int32), counts_i, xs, w1, w3, w2)
    out = _combine(dest, gate.T, h2p, x1, mod, ws1.astype(bf16), ws3.astype(bf16), ws2.astype(bf16), ys, seq)
    return out.reshape(batch, seq, d)


def kernel(x, c, w_ada, b_ada, w_in, conv_w, conv_b, gate_b, ml_norm_g, da_q_norm_g, da_k_norm_g,
           lambda_q1, lambda_k1, lambda_q2, lambda_k2, da_norm_g, w_out, w_router, router_bias,
           w1, w3, w2, ws1, ws3, ws2):
    depth = w_ada.shape[0]
    for l in range(depth):
        x = _layer(x, c, w_ada[l], b_ada[l], w_in[l], conv_w[l], conv_b[l], gate_b[l], ml_norm_g[l],
                   da_q_norm_g[l], da_k_norm_g[l], lambda_q1[l], lambda_k1[l], lambda_q2[l], lambda_k2[l],
                   da_norm_g[l], w_out[l], w_router[l], router_bias[l], w1[l], w3[l], w2[l],
                   ws1[l], ws3[l], ws2[l], l)
    return x
```

```python
import functools
import math

import jax
import jax.numpy as jnp
from jax import lax
from jax.experimental import pallas as pl
from jax.experimental.pallas import tpu as pltpu
from jax.experimental.pallas import tpu_sc as plsc

D_MODEL = 1024
ML_HEADS = 4
ML_DH = 128
ML_W = ML_HEADS * ML_DH
ML_CHUNK = 128
CONV_K = 4
DA_HEADS = 4
DA_DH = 64
DA_W = DA_HEADS * 2 * DA_DH
N_EXPERTS = 256
TOP_K = 8
N_GROUPS = 8
GROUP_SIZE = N_EXPERTS // N_GROUPS
TOPK_GROUPS = 4
D_EXPERT = 256
ROUTED_SCALE = 2.5
EPS = 1e-6
N_GATES = 2 * ML_HEADS

LANES = 128
SUBLANES = 8
VMEM_LIMIT_BYTES = 48 * 1024 * 1024

TM_INPROJ = 512
ATT_BLOCK = 1024
TM_ROUTE = 256
TM_SLOTS = 512
TM_MOE = 256
MOE_BLOCK = 256
HALF = D_MODEL // 2
QUARTER = HALF // 2
SC_WINDOW = 128

NEG_BIG = -1e30
LOG2E = 1.4426950408889634

f32 = jnp.float32
bf16 = jnp.bfloat16
HIGHEST = lax.Precision.HIGHEST


def _cparams(n_axes):
    return pltpu.CompilerParams(dimension_semantics=("arbitrary",) * n_axes,
                                vmem_limit_bytes=VMEM_LIMIT_BYTES)


def _dot(a, b):
    return jnp.dot(a, b, preferred_element_type=f32)


def _dot_nt(a, b):
    return lax.dot_general(a, b, (((1,), (1,)), ((), ())), preferred_element_type=f32)


def _sigmoid(x):
    return 1.0 / (1.0 + jnp.exp(-x))


def _silu(x):
    return x * _sigmoid(x)


def _log_sigmoid(x):
    return jnp.minimum(x, 0.0) - jnp.log(1.0 + jnp.exp(-jnp.abs(x)))


def _pack_bf16_pair(lo, hi):
    lo_bits = lax.bitcast_convert_type(lo.astype(bf16).astype(f32), jnp.uint32)
    hi_bits = lax.bitcast_convert_type(hi.astype(bf16).astype(f32), jnp.uint32)
    return (hi_bits & jnp.uint32(0xFFFF0000)) | (lo_bits >> 16)


def _unpack_bf16_pair(w):
    lo = lax.bitcast_convert_type(w << 16, f32)
    hi = lax.bitcast_convert_type(w & jnp.uint32(0xFFFF0000), f32)
    return lo, hi


def _adaln_kernel(c_ref, w_ref, b_ref, o_ref):
    c = c_ref[...]
    o_ref[...] = jnp.dot(_silu(c), w_ref[...], precision=HIGHEST, preferred_element_type=f32) + b_ref[...]


def _adaln(c_pad, w_ada, b_ada):
    rows, d = c_pad.shape
    n = w_ada.shape[1]
    tn = 1536
    return pl.pallas_call(
        _adaln_kernel,
        out_shape=jax.ShapeDtypeStruct((rows, n), f32),
        grid=(n // tn,),
        in_specs=[pl.BlockSpec((rows, d), lambda j: (0, 0)),
                  pl.BlockSpec((d, tn), lambda j: (0, j)),
                  pl.BlockSpec((1, tn), lambda j: (0, j))],
        out_specs=pl.BlockSpec((rows, tn), lambda j: (0, j)),
        compiler_params=_cparams(1),
        name="adaln",
    )(c_pad, w_ada, b_ada)


def _alibi_slope_log2(head):
    return 2.0 ** (-8.0 * (head + 1) / DA_HEADS) * LOG2E


def _inproj_kernel(x_ref, mod_ref, w_ref, wg_ref, wgt_ref, gbr_ref, gbc_ref, qg_ref, kg_ref,
                   pm_ref, q_ref, k_ref, v_ref, gcol_ref, grow_ref, *, seq):
    tm = TM_INPROJ
    x = x_ref[...]
    h = x * lax.rsqrt(jnp.mean(x * x, axis=-1, keepdims=True) + EPS)
    h = h * (1.0 + mod_ref[0, 1:2, :]) + mod_ref[0, 0:1, :]
    hb = h.astype(bf16)
    cw = ML_W
    for n in range(4):
        pm_ref[:, n * cw:(n + 1) * cw] = _dot(hb, w_ref[:, n * cw:(n + 1) * cw]).astype(bf16)

    lane = lax.broadcasted_iota(jnp.int32, (tm, LANES), 1)
    q_aug = jnp.where(jnp.logical_and(lane >= DA_DH, lane < DA_DH + 3), 1.0, 0.0)
    pos = (pl.program_id(0) * tm) % seq + lax.broadcasted_iota(jnp.int32, (tm, 1), 0)
    pos = pos.astype(f32)
    q0 = 4 * cw
    k0 = q0 + 2 * DA_W
    yq = _dot(hb, w_ref[:, q0:q0 + 2 * DA_W])
    yk = _dot(hb, w_ref[:, k0:k0 + 2 * DA_W])
    for grp in range(2 * DA_HEADS):
        sl = slice(grp * LANES, (grp + 1) * LANES)
        y = yq[:, sl]
        ss = jnp.sum(y * y, axis=-1, keepdims=True)
        q_ref[:, sl] = (y * lax.rsqrt(ss * (1.0 / DA_DH) + EPS) * qg_ref[:, sl] + q_aug).astype(bf16)

        y = yk[:, sl]
        ss = jnp.sum(y * y, axis=-1, keepdims=True)
        r = pos * _alibi_slope_log2(grp // 2)
        r_hi = r.astype(bf16).astype(f32)
        r_mid = (r - r_hi).astype(bf16).astype(f32)
        r_lo = r - r_hi - r_mid
        k_aug = jnp.where(lane == DA_DH, r_hi,
                          jnp.where(lane == DA_DH + 1, r_mid, jnp.where(lane == DA_DH + 2, r_lo, 0.0)))
        k_ref[:, sl] = (y * lax.rsqrt(ss * (1.0 / DA_DH) + EPS) * kg_ref[:, sl] + k_aug).astype(bf16)

    v0 = k0 + 2 * DA_W
    v_ref[...] = _dot(hb, w_ref[:, v0:v0 + DA_W]).astype(bf16)
    gcol_ref[...] = _dot(hb, wg_ref[...]) + gbr_ref[...]
    grow_ref[...] = _dot_nt(wgt_ref[...], hb) + gbc_ref[...]


def _inproj(x2, mod, w_main, wg, wgt, gb_row, gb_col, qgain, kgain, seq):
    t, d = x2.shape
    tm = TM_INPROJ
    tiles_per_seq = seq // tm
    n_main = w_main.shape[1]
    const = lambda shape: pl.BlockSpec(shape, lambda i: (0,) * len(shape))
    return pl.pallas_call(
        functools.partial(_inproj_kernel, seq=seq),
        out_shape=(jax.ShapeDtypeStruct((t, 4 * ML_W), bf16),
                   jax.ShapeDtypeStruct((t, 2 * DA_W), bf16),
                   jax.ShapeDtypeStruct((t, 2 * DA_W), bf16),
                   jax.ShapeDtypeStruct((t, DA_W), bf16),
                   jax.ShapeDtypeStruct((t, LANES), f32),
                   jax.ShapeDtypeStruct((N_GATES, t), f32)),
        grid=(t // tm,),
        in_specs=[pl.BlockSpec((tm, d), lambda i: (i, 0)),
                  pl.BlockSpec((1, 6, d), lambda i: (i // tiles_per_seq, 0, 0)),
                  const((d, n_main)), const((d, LANES)), const((N_GATES, d)),
                  const((1, LANES)), const((N_GATES, 1)),
                  const((1, 2 * DA_W)), const((1, 2 * DA_W))],
        out_specs=(pl.BlockSpec((tm, 4 * ML_W), lambda i: (i, 0)),
                   pl.BlockSpec((tm, 2 * DA_W), lambda i: (i, 0)),
                   pl.BlockSpec((tm, 2 * DA_W), lambda i: (i, 0)),
                   pl.BlockSpec((tm, DA_W), lambda i: (i, 0)),
                   pl.BlockSpec((tm, LANES), lambda i: (i, 0)),
                   pl.BlockSpec((N_GATES, tm), lambda i: (0, i))),
        compiler_params=_cparams(1),
        name="inproj",
    )(x2, mod, w_main, wg, wgt, gb_row, gb_col, qgain, kgain)


def _mlstm_kernel(qk_ref, v_ref, o_ref, gcol_ref, grow_ref, cw_ref, cb_ref, ng_ref,
                  out_ref, state_ref, m_ref, prev_ref, ext_ref):
    L = ML_CHUNK
    dh = ML_DH
    c = pl.program_id(1)

    @pl.when(c == 0)
    def _():
        state_ref[...] = jnp.zeros_like(state_ref)
        m_ref[...] = jnp.zeros_like(m_ref)
        prev_ref[...] = jnp.zeros_like(prev_ref)

    cur = qk_ref[...].astype(f32)
    ext_ref[0:SUBLANES, :] = prev_ref[...]
    ext_ref[SUBLANES:SUBLANES + L, :] = cur
    prev_ref[...] = cur[L - SUBLANES:L, :]
    acc = cb_ref[...] + cw_ref[CONV_K - 1:CONV_K, :] * cur
    for j in range(CONV_K - 1):
        off = SUBLANES - (CONV_K - 1) + j
        acc = acc + cw_ref[j:j + 1, :] * ext_ref[off:off + L, :]
    qk = _silu(acc)

    row_i = lax.broadcasted_iota(jnp.int32, (L, L), 0)
    col_i = lax.broadcasted_iota(jnp.int32, (L, L), 1)
    causal = row_i >= col_i
    tril = causal.astype(f32)
    triu = (row_i <= col_i).astype(f32)
    gcol = gcol_ref[...]
    grow = grow_ref[...]
    bcol_all = jnp.dot(tril, _log_sigmoid(gcol), precision=HIGHEST, preferred_element_type=f32)
    brow_all = jnp.dot(_log_sigmoid(grow), triu, precision=HIGHEST, preferred_element_type=f32)

    lane = lax.broadcasted_iota(jnp.int32, (L, dh), 1)
    ones_col = (lane == 0).astype(f32)
    g = ng_ref[...]

    for hd in range(ML_HEADS):
        q = qk[:, hd * dh:(hd + 1) * dh]
        k = qk[:, ML_W + hd * dh:ML_W + (hd + 1) * dh] * (dh ** -0.5)
        v = v_ref[:, hd * dh:(hd + 1) * dh].astype(f32)
        v_aug = jnp.concatenate([v, ones_col], axis=1)
        qb = q.astype(bf16)
        kb = k.astype(bf16)

        bcol = bcol_all[:, ML_HEADS + hd:ML_HEADS + hd + 1]
        icol = gcol[:, hd:hd + 1]
        brow = brow_all[ML_HEADS + hd:ML_HEADS + hd + 1, :]
        irow = grow[hd:hd + 1, :]
        m_prev = m_ref[hd:hd + 1, 0:1]

        d_intra = jnp.where(causal, bcol - brow + irow, NEG_BIG)
        d_inter = bcol + m_prev
        m_t = jnp.maximum(d_inter, jnp.max(d_intra, axis=1, keepdims=True))
        w_intra = jnp.exp(d_intra - m_t)
        w_inter = jnp.exp(d_inter - m_t)

        s = _dot_nt(qb, kb) * w_intra
        state = state_ref[hd]
        q_state = _dot(qb, state.astype(bf16))
        s_v = _dot(s.astype(bf16), v_aug.astype(bf16))
        num = w_inter * q_state[:, 0:dh] + s_v[:, 0:dh]
        den = w_inter * q_state[:, dh:dh + 1] + s_v[:, dh:dh + 1]
        hval = num / jnp.maximum(jnp.abs(den), jnp.exp(-m_t))

        b_last = bcol[L - 1:L, :]
        d_state = b_last - bcol + icol
        m_new = jnp.maximum(b_last + m_prev, jnp.max(d_state, axis=0, keepdims=True))
        carry_scale = jnp.exp(b_last + m_prev - m_new)
        wk = jnp.exp(d_state - m_new)
        upd = _dot(k.T.astype(bf16), (v_aug * wk).astype(bf16))
        state_ref[hd] = carry_scale * state + upd
        m_ref[hd:hd + 1, :] = jnp.broadcast_to(m_new, (1, LANES))

        hn = hval * lax.rsqrt(jnp.mean(hval * hval, axis=-1, keepdims=True) + EPS) * g
        og = _sigmoid(o_ref[:, hd * dh:(hd + 1) * dh].astype(f32))
        out_ref[:, hd * dh:(hd + 1) * dh] = (hn * og).astype(bf16)


def _mlstm(pm, gcol, grow, conv_w, conv_b, ml_norm_g, batch, seq):
    t = pm.shape[0]
    L = ML_CHUNK
    nc = seq // L
    row = lambda b, c: b * nc + c
    return pl.pallas_call(
        _mlstm_kernel,
        out_shape=jax.ShapeDtypeStruct((t, ML_W), bf16),
        grid=(batch, nc),
        in_specs=[pl.BlockSpec((L, 2 * ML_W), lambda b, c: (row(b, c), 0)),
                  pl.BlockSpec((L, ML_W), lambda b, c: (row(b, c), 2)),
                  pl.BlockSpec((L, ML_W), lambda b, c: (row(b, c), 3)),
                  pl.BlockSpec((L, LANES), lambda b, c: (row(b, c), 0)),
                  pl.BlockSpec((N_GATES, L), lambda b, c: (0, row(b, c))),
                  pl.BlockSpec((CONV_K, 2 * ML_W), lambda b, c: (0, 0)),
                  pl.BlockSpec((1, 2 * ML_W), lambda b, c: (0, 0)),
                  pl.BlockSpec((1, ML_DH), lambda b, c: (0, 0))],
        out_specs=pl.BlockSpec((L, ML_W), lambda b, c: (row(b, c), 0)),
        scratch_shapes=[pltpu.VMEM((ML_HEADS, ML_DH, 2 * ML_DH), f32),
                        pltpu.VMEM((SUBLANES, LANES), f32),
                        pltpu.VMEM((SUBLANES, 2 * ML_W), f32),
                        pltpu.VMEM((SUBLANES + ML_CHUNK, 2 * ML_W), f32)],
        compiler_params=_cparams(2),
        name="mlstm",
    )(pm, pm, pm, gcol, grow, conv_w, conv_b, ml_norm_g)


def _attn_block(q_ref, k_ref, v_ref, m_ref, l_ref, acc_ref, diagonal):
    blk = ATT_BLOCK
    nch = blk // LANES
    v = v_ref[...]
    if diagonal:
        keep = (lax.broadcasted_iota(jnp.int32, (blk, blk), 1)
                <= lax.broadcasted_iota(jnp.int32, (blk, blk), 0))
    for c in range(2):
        sl = slice(c * LANES, (c + 1) * LANES)
        s = _dot_nt(q_ref[:, sl], k_ref[:, sl])
        if diagonal:
            s = jnp.where(keep, s, NEG_BIG)
        chunks = [s[:, j * LANES:(j + 1) * LANES] for j in range(nch)]
        mc = chunks[0]
        for ch in chunks[1:]:
            mc = jnp.maximum(mc, ch)
        m_old = m_ref[c]
        m_new = jnp.maximum(m_old, jnp.max(mc, axis=1, keepdims=True))
        alpha = jnp.exp2(m_old - m_new)
        ps = [jnp.exp2(ch - m_new) for ch in chunks]
        lsum = ps[0]
        for pj in ps[1:]:
            lsum = lsum + pj
        p = jnp.concatenate([pj.astype(bf16) for pj in ps], axis=1)
        l_ref[c] = alpha * l_ref[c] + lsum
        acc_ref[c] = alpha * acc_ref[c] + _dot(p, v)
        m_ref[c] = m_new


def _attn_kernel(qt_ref, kt_ref, q_ref, k_ref, v_ref, lam_ref, g_ref, o_ref,
                 m_ref, l_ref, acc_ref, *, lam_init):
    step = pl.program_id(2)
    qi = qt_ref[step]
    ki = kt_ref[step]

    @pl.when(ki == 0)
    def _():
        m_ref[...] = jnp.full_like(m_ref, NEG_BIG)
        l_ref[...] = jnp.zeros_like(l_ref)
        acc_ref[...] = jnp.zeros_like(acc_ref)

    @pl.when(ki < qi)
    def _():
        _attn_block(q_ref, k_ref, v_ref, m_ref, l_ref, acc_ref, diagonal=False)

    @pl.when(ki == qi)
    def _():
        _attn_block(q_ref, k_ref, v_ref, m_ref, l_ref, acc_ref, diagonal=True)
        lam = lam_ref[...]
        l0 = jnp.sum(l_ref[0], axis=1, keepdims=True)
        l1 = jnp.sum(l_ref[1], axis=1, keepdims=True)
        o = acc_ref[0] / l0 - lam * (acc_ref[1] / l1)
        o = o * lax.rsqrt(jnp.mean(o * o, axis=-1, keepdims=True) + EPS)
        o_ref[...] = (o * g_ref[...] * (1.0 - lam_init)).astype(bf16)


def _attention(qh, kh, vh, lam, da_norm_g, batch, seq, lam_init):
    t = qh.shape[0]
    blk = ATT_BLOCK
    nb = seq // blk
    pairs = [(i, j) for i in range(nb) for j in range(i + 1)]
    qt = jnp.asarray([p[0] for p in pairs], jnp.int32)
    kt = jnp.asarray([p[1] for p in pairs], jnp.int32)
    w = 2 * DA_DH
    grid_spec = pltpu.PrefetchScalarGridSpec(
        num_scalar_prefetch=2,
        grid=(batch, DA_HEADS, len(pairs)),
        in_specs=[pl.BlockSpec((blk, 2 * LANES), lambda b, h, s, qt, kt: (b * nb + qt[s], h)),
                  pl.BlockSpec((blk, 2 * LANES), lambda b, h, s, qt, kt: (b * nb + kt[s], h)),
                  pl.BlockSpec((blk, w), lambda b, h, s, qt, kt: (b * nb + kt[s], h)),
                  pl.BlockSpec((1, 1), lambda b, h, s, qt, kt: (0, 0)),
                  pl.BlockSpec((1, w), lambda b, h, s, qt, kt: (0, 0))],
        out_specs=pl.BlockSpec((blk, w), lambda b, h, s, qt, kt: (b * nb + qt[s], h)),
        scratch_shapes=[pltpu.VMEM((2, blk, LANES), f32),
                        pltpu.VMEM((2, blk, LANES), f32),
                        pltpu.VMEM((2, blk, w), f32)],
    )
    return pl.pallas_call(
        functools.partial(_attn_kernel, lam_init=lam_init),
        out_shape=jax.ShapeDtypeStruct((t, DA_W), bf16),
        grid_spec=grid_spec,
        compiler_params=_cparams(3),
        name="attn",
    )(qt, kt, qh, kh, vh, lam, da_norm_g)


def _first_index_of_max(x, iota_f, size):
    m = jnp.max(x, axis=0, keepdims=True)
    idx = jnp.min(jnp.where(x == m, iota_f, float(size)), axis=0, keepdims=True)
    return m, idx


def _route_kernel(hm_ref, hd_ref, x_ref, mod_ref, wo_ref, wr_ref, rb_ref,
                  x1_ref, h2p_ref, idx_ref, gate_ref, rank_ref, cnt_ref, base_ref):
    tm = TM_ROUTE
    i = pl.program_id(0)

    @pl.when(i == 0)
    def _():
        base_ref[...] = jnp.zeros_like(base_ref)

    mix = _dot(hm_ref[...], wo_ref[0:ML_W, :]) + _dot(hd_ref[...], wo_ref[ML_W:ML_W + DA_W, :])
    x1 = x_ref[...] + mod_ref[0, 2:3, :] * mix
    x1_ref[...] = x1
    h2 = x1 * lax.rsqrt(jnp.mean(x1 * x1, axis=-1, keepdims=True) + EPS)
    h2 = h2 * (1.0 + mod_ref[0, 4:5, :]) + mod_ref[0, 3:4, :]
    h2p_ref[...] = _pack_bf16_pair(h2[:, 0:HALF], h2[:, HALF:D_MODEL])

    sc = _sigmoid(_dot_nt(wr_ref[...], h2.astype(bf16)))
    sel = sc + rb_ref[...]

    gi = lax.broadcasted_iota(jnp.int32, (GROUP_SIZE, tm), 0).astype(f32)
    gscores = []
    for g in range(N_GROUPS):
        blk = sel[g * GROUP_SIZE:(g + 1) * GROUP_SIZE, :]
        m1, i1 = _first_index_of_max(blk, gi, GROUP_SIZE)
        m2 = jnp.max(jnp.where(gi == i1, NEG_BIG, blk), axis=0, keepdims=True)
        gscores.append(m1 + m2)
    gs = jnp.concatenate(gscores, axis=0)

    ngi = lax.broadcasted_iota(jnp.int32, (N_GROUPS, tm), 0).astype(f32)
    gkeep = jnp.zeros((N_GROUPS, tm), f32)
    for _ in range(TOPK_GROUPS):
        _, gidx = _first_index_of_max(gs, ngi, N_GROUPS)
        hit = ngi == gidx
        gkeep = jnp.where(hit, 1.0, gkeep)
        gs = jnp.where(hit, NEG_BIG, gs)
    masked = jnp.concatenate(
        [jnp.where(gkeep[g:g + 1, :] > 0.0, sel[g * GROUP_SIZE:(g + 1) * GROUP_SIZE, :], NEG_BIG)
         for g in range(N_GROUPS)], axis=0)

    ei = lax.broadcasted_iota(jnp.int32, (N_EXPERTS, tm), 0).astype(f32)
    idxs, gates = [], []
    chosen = jnp.zeros((N_EXPERTS, tm), f32)
    for _ in range(TOP_K):
        _, eidx = _first_index_of_max(masked, ei, N_EXPERTS)
        hit = ei == eidx
        idxs.append(eidx)
        gates.append(jnp.sum(jnp.where(hit, sc, 0.0), axis=0, keepdims=True))
        chosen = jnp.where(hit, 1.0, chosen)
        masked = jnp.where(hit, NEG_BIG, masked)
    gate = jnp.concatenate(gates, axis=0)
    gate = gate / jnp.sum(gate, axis=0, keepdims=True) * ROUTED_SCALE
    gate_ref[...] = gate
    idx_ref[...] = jnp.concatenate(idxs, axis=0).astype(jnp.int32)

    tr = lax.broadcasted_iota(jnp.int32, (tm, tm), 0)
    tc = lax.broadcasted_iota(jnp.int32, (tm, tm), 1)
    before = (tr < tc).astype(bf16)
    seen = base_ref[...] + _dot(chosen.astype(bf16), before)
    ranks = [jnp.sum(jnp.where(ei == eidx, seen, 0.0), axis=0, keepdims=True) for eidx in idxs]
    rank_ref[...] = jnp.concatenate(ranks, axis=0).astype(jnp.int32)
    total = base_ref[...] + jnp.sum(chosen, axis=1, keepdims=True)
    base_ref[...] = total
    cnt_ref[...] = total


def _route(hm, hd, x2, mod, w_out, w_router_t, rbias_col, seq):
    t, d = x2.shape
    tm = TM_ROUTE
    tiles_per_seq = seq // tm
    const = lambda shape: pl.BlockSpec(shape, lambda i: (0,) * len(shape))
    return pl.pallas_call(
        _route_kernel,
        out_shape=(jax.ShapeDtypeStruct((t, d), f32),
                   jax.ShapeDtypeStruct((t, HALF), jnp.uint32),
                   jax.ShapeDtypeStruct((TOP_K, t), jnp.int32),
                   jax.ShapeDtypeStruct((TOP_K, t), f32),
                   jax.ShapeDtypeStruct((TOP_K, t), jnp.int32),
                   jax.ShapeDtypeStruct((N_EXPERTS, 1), f32)),
        grid=(t // tm,),
        in_specs=[pl.BlockSpec((tm, ML_W), lambda i: (i, 0)),
                  pl.BlockSpec((tm, DA_W), lambda i: (i, 0)),
                  pl.BlockSpec((tm, d), lambda i: (i, 0)),
                  pl.BlockSpec((1, 6, d), lambda i: (i // tiles_per_seq, 0, 0)),
                  const((d, d)), const((N_EXPERTS, d)), const((N_EXPERTS, 1))],
        out_specs=(pl.BlockSpec((tm, d), lambda i: (i, 0)),
                   pl.BlockSpec((tm, HALF), lambda i: (i, 0)),
                   pl.BlockSpec((TOP_K, tm), lambda i: (0, i)),
                   pl.BlockSpec((TOP_K, tm), lambda i: (0, i)),
                   pl.BlockSpec((TOP_K, tm), lambda i: (0, i)),
                   const((N_EXPERTS, 1))),
        scratch_shapes=[pltpu.VMEM((N_EXPERTS, 1), f32)],
        compiler_params=_cparams(1),
        name="route",
    )(hm, hd, x2, mod, w_out, w_router_t, rbias_col)


SLOT_RADIX = 256
SLOT_DIGITS = 3


def _slots_kernel(idx_ref, rank_ref, dig_ref, dest_ref):
    tm = TM_SLOTS
    ei = lax.broadcasted_iota(jnp.int32, (N_EXPERTS, tm), 0)
    rows = []
    for k in range(TOP_K):
        onehot = jnp.where(ei == idx_ref[k:k + 1, :], 1.0, 0.0).astype(bf16)
        dg = _dot(dig_ref[...], onehot)
        start = dg[0:1, :]
        for j in range(1, SLOT_DIGITS):
            start = start + dg[j:j + 1, :] * float(SLOT_RADIX ** j)
        rows.append(start.astype(jnp.int32) + rank_ref[k:k + 1, :])
    dest_ref[...] = jnp.concatenate(rows, axis=0)


def _slots(idx, rank, digits):
    t = idx.shape[1]
    tm = TM_SLOTS
    return pl.pallas_call(
        _slots_kernel,
        out_shape=jax.ShapeDtypeStruct((TOP_K, t), jnp.int32),
        grid=(t // tm,),
        in_specs=[pl.BlockSpec((TOP_K, tm), lambda i: (0, i)),
                  pl.BlockSpec((TOP_K, tm), lambda i: (0, i)),
                  pl.BlockSpec((SUBLANES, N_EXPERTS), lambda i: (0, 0))],
        out_specs=pl.BlockSpec((TOP_K, tm), lambda i: (0, i)),
        compiler_params=_cparams(1),
        name="slots",
    )(idx, rank, digits)


def _row_copy(src_ref, src_row, dst_ref, dst_row, sem):
    return pltpu.make_async_copy(src_ref.at[pl.ds(src_row, 1)], dst_ref.at[pl.ds(dst_row, 1)], sem)


def _dispatch_kernel(dest_ref, h_ref, xs_ref, sem):
    tm = TM_MOE

    def issue(tok, carry):
        for k in range(TOP_K):
            _row_copy(h_ref, tok, xs_ref, dest_ref[k, tok], sem).start(priority=k % 2)
        return carry

    lax.fori_loop(0, tm, issue, 0)

    def drain(tok, carry):
        for k in range(TOP_K):
            _row_copy(h_ref, tok, xs_ref, dest_ref[k, tok], sem).wait()
        return carry

    lax.fori_loop(0, tm, drain, 0)


def _dispatch(dest, h2p, n_slots):
    t = h2p.shape[0]
    tm = TM_MOE
    return pl.pallas_call(
        _dispatch_kernel,
        out_shape=jax.ShapeDtypeStruct((n_slots, HALF), jnp.uint32),
        grid=(t // tm,),
        in_specs=[pl.BlockSpec((TOP_K, tm), lambda i: (0, i), memory_space=pltpu.SMEM),
                  pl.BlockSpec((tm, HALF), lambda i: (i, 0))],
        out_specs=pl.BlockSpec(memory_space=pl.ANY),
        scratch_shapes=[pltpu.SemaphoreType.DMA(())],
        compiler_params=_cparams(1),
        name="dispatch",
    )(dest, h2p)


EXPERT_RING = 4


def _expert_kernel(start_ref, cnt_ref, xs_ref, w1_ref, w3_ref, w2_ref, ysa_ref, ysb_ref,
                   xbuf, ybuf, w1b, w3b, w2b, in_sem, out_sem, done_ref):
    bm = MOE_BLOCK
    ring = EXPERT_RING
    ahead = ring - 1
    e = pl.program_id(0)
    last = pl.num_programs(0) - 1

    def n_blocks(ex):
        return lax.shift_right_logical(cnt_ref[ex] + (bm - 1), int(math.log2(bm)))

    def slot_of(g):
        return jnp.bitwise_and(g, ring - 1)

    def in_copy(ex_base, j, g):
        s = slot_of(g)
        src = xs_ref.at[pl.ds(pl.multiple_of(ex_base + j * bm, bm), bm)]
        return pltpu.make_async_copy(src, xbuf.at[s], in_sem.at[s])

    def out_copies(ex_base, j, g):
        s = slot_of(g)
        rows = pl.ds(pl.multiple_of(ex_base + j * bm, bm), bm)
        return [pltpu.make_async_copy(ybuf.at[s, half], y_ref.at[rows], out_sem.at[s])
                for half, y_ref in enumerate((ysa_ref, ysb_ref))]

    def request_first_blocks(ex, g_first):
        nbx = n_blocks(ex)
        for j in range(ahead):
            @pl.when(j < nbx)
            def _():
                in_copy(start_ref[ex], j, g_first + j).start()

    @pl.when(e == 0)
    def _():
        done_ref[0] = 0
        request_first_blocks(0, 0)

    base = start_ref[e]
    cnt = cnt_ref[e]
    nb = n_blocks(e)
    g0 = done_ref[0]

    @pl.when(nb > 0)
    def _():
        w1b[...] = w1_ref[0].astype(bf16)
        w3b[...] = w3_ref[0].astype(bf16)
        w2b[...] = w2_ref[0].astype(bf16)

    def block(j, carry):
        g = g0 + j
        s = slot_of(g)

        @pl.when(j + ahead < nb)
        def _():
            in_copy(base, j + ahead, g + ahead).start()

        in_copy(base, j, g).wait()

        @pl.when(g >= ring)
        def _():
            for cp in out_copies(base, 0, g):
                cp.wait()

        row = j * bm + lax.broadcasted_iota(jnp.int32, (bm, HALF), 0)
        lo, hi = _unpack_bf16_pair(jnp.where(row < cnt, xbuf[s], jnp.uint32(0)))
        lo = lo.astype(bf16)
        hi = hi.astype(bf16)
        h1 = _dot(lo, w1b[0:HALF, :]) + _dot(hi, w1b[HALF:D_MODEL, :])
        h3 = _dot(lo, w3b[0:HALF, :]) + _dot(hi, w3b[HALF:D_MODEL, :])
        y = _dot((_silu(h1) * h3).astype(bf16), w2b[...])
        packed = lax.bitcast_convert_type(_pack_bf16_pair(y[:, 0:HALF], y[:, HALF:D_MODEL]), jnp.int32)
        ybuf[s, 0] = packed[:, 0:QUARTER]
        ybuf[s, 1] = packed[:, QUARTER:HALF]
        for cp in out_copies(base, j, g):
            cp.start()
        return carry

    lax.fori_loop(0, nb, block, 0)
    g_end = g0 + nb
    done_ref[0] = g_end

    @pl.when(e < last)
    def _():
        request_first_blocks(e + 1, g_end)

    @pl.when(e == last)
    def _():
        for back in range(1, ring + 1):
            @pl.when(g_end - back >= 0)
            def _():
                for cp in out_copies(base, 0, g_end - back):
                    cp.wait()


def _experts(start, counts, xs, w1, w3, w2):
    n_pad = xs.shape[0]
    bm = MOE_BLOCK
    grid_spec = pltpu.PrefetchScalarGridSpec(
        num_scalar_prefetch=2,
        grid=(N_EXPERTS,),
        in_specs=[pl.BlockSpec(memory_space=pl.ANY),
                  pl.BlockSpec((1, D_MODEL, D_EXPERT), lambda e, st, ct: (e, 0, 0)),
                  pl.BlockSpec((1, D_MODEL, D_EXPERT), lambda e, st, ct: (e, 0, 0)),
                  pl.BlockSpec((1, D_EXPERT, D_MODEL), lambda e, st, ct: (e, 0, 0))],
        out_specs=(pl.BlockSpec(memory_space=pl.ANY), pl.BlockSpec(memory_space=pl.ANY)),
        scratch_shapes=[pltpu.VMEM((EXPERT_RING, bm, HALF), jnp.uint32),
                        pltpu.VMEM((EXPERT_RING, 2, bm, QUARTER), jnp.int32),
                        pltpu.VMEM((D_MODEL, D_EXPERT), bf16),
                        pltpu.VMEM((D_MODEL, D_EXPERT), bf16),
                        pltpu.VMEM((D_EXPERT, D_MODEL), bf16),
                        pltpu.SemaphoreType.DMA((EXPERT_RING,)),
                        pltpu.SemaphoreType.DMA((EXPERT_RING,)),
                        pltpu.SMEM((1,), jnp.int32)],
    )
    return pl.pallas_call(
        _expert_kernel,
        out_shape=(jax.ShapeDtypeStruct((n_pad, QUARTER), jnp.int32),
                   jax.ShapeDtypeStruct((n_pad, QUARTER), jnp.int32)),
        grid_spec=grid_spec,
        compiler_params=_cparams(1),
        name="experts",
    )(start, counts, xs, w1, w3, w2)


def _sc_gather_table(table, idx2):
    n = idx2.shape[1]
    width = table.shape[1]
    mesh = plsc.VectorSubcoreMesh(core_axis_name="core", subcore_axis_name="subcore")

    @pl.kernel(out_type=jax.ShapeDtypeStruct((n, width), table.dtype), mesh=mesh, scratch_types=[],
               name="sc_gather")
    def gather_kernel(tab_hbm, idx_hbm, out_hbm):
        def body(idx_vmem, out_vmem):
            pltpu.sync_copy(tab_hbm.at[idx_vmem.at[0]], out_vmem)

        pltpu.emit_pipeline(
            body,
            grid=(n // SC_WINDOW,),
            in_specs=[pl.BlockSpec((1, SC_WINDOW), lambda i: (0, i))],
            out_specs=[pl.BlockSpec((SC_WINDOW, width), lambda i: (i, 0))],
            core_axis_name=("core", "subcore"),
            dimension_semantics=(pltpu.PARALLEL,),
        )(idx_hbm, out_hbm)

    return gather_kernel(table, idx2)


def _sc_gather_rows(table_a, table_b, idx):
    idx2 = idx.reshape(1, -1)
    return _sc_gather_table(table_a, idx2), _sc_gather_table(table_b, idx2)


def _combine_kernel(gate_ref, h_ref, x1_ref, mod_ref, ws1_ref, ws3_ref, ws2_ref, ga_ref, gb_ref, out_ref):
    lo, hi = _unpack_bf16_pair(h_ref[...])
    lo = lo.astype(bf16)
    hi = hi.astype(bf16)
    s1 = _dot(lo, ws1_ref[0:HALF, :]) + _dot(hi, ws1_ref[HALF:D_MODEL, :])
    s3 = _dot(lo, ws3_ref[0:HALF, :]) + _dot(hi, ws3_ref[HALF:D_MODEL, :])
    y = _dot((_silu(s1) * s3).astype(bf16), ws2_ref[...])

    gate = gate_ref[...]
    gate_f = mod_ref[0, 5:6, :]
    for part, g_ref in enumerate((ga_ref, gb_ref)):
        c_lo = slice(part * QUARTER, (part + 1) * QUARTER)
        c_hi = slice(HALF + part * QUARTER, HALF + (part + 1) * QUARTER)
        acc_lo = y[:, c_lo]
        acc_hi = y[:, c_hi]
        for k in range(TOP_K):
            rlo, rhi = _unpack_bf16_pair(lax.bitcast_convert_type(g_ref[k], jnp.uint32))
            gk = gate[:, k:k + 1]
            acc_lo = acc_lo + gk * rlo
            acc_hi = acc_hi + gk * rhi
        out_ref[:, c_lo] = x1_ref[:, c_lo] + gate_f[:, c_lo] * acc_lo
        out_ref[:, c_hi] = x1_ref[:, c_hi] + gate_f[:, c_hi] * acc_hi


def _combine(gate_col, h2p, x1, mod, ws1, ws3, ws2, ga, gb, seq):
    t, d = x1.shape
    tm = TM_MOE
    tiles_per_seq = seq // tm
    const = lambda shape: pl.BlockSpec(shape, lambda i: (0,) * len(shape))
    return pl.pallas_call(
        _combine_kernel,
        out_shape=jax.ShapeDtypeStruct((t, d), f32),
        grid=(t // tm,),
        in_specs=[pl.BlockSpec((tm, TOP_K), lambda i: (i, 0)),
                  pl.BlockSpec((tm, HALF), lambda i: (i, 0)),
                  pl.BlockSpec((tm, d), lambda i: (i, 0)),
                  pl.BlockSpec((1, 6, d), lambda i: (i // tiles_per_seq, 0, 0)),
                  const((d, D_EXPERT)), const((d, D_EXPERT)), const((D_EXPERT, d)),
                  pl.BlockSpec((TOP_K, tm, QUARTER), lambda i: (0, i, 0)),
                  pl.BlockSpec((TOP_K, tm, QUARTER), lambda i: (0, i, 0))],
        out_specs=pl.BlockSpec((tm, d), lambda i: (i, 0)),
        compiler_params=_cparams(1),
        name="combine",
    )(gate_col, h2p, x1, mod, ws1, ws3, ws2, ga, gb)


def _lambda_init(layer):
    return 0.8 - 0.6 * math.exp(-0.3 * layer)


def _layer(x, c, w_ada, b_ada, w_in, conv_w, conv_b, gate_b, ml_norm_g, da_q_norm_g, da_k_norm_g,
           lambda_q1, lambda_k1, lambda_q2, lambda_k2, da_norm_g, w_out, w_router, router_bias,
           w1, w3, w2, ws1, ws3, ws2, layer):
    batch, seq, d = x.shape
    t = batch * seq
    lam_init = _lambda_init(layer)
    x2 = x.reshape(t, d)

    c_pad = jnp.pad(c, ((0, -batch % SUBLANES), (0, 0)))
    mod = _adaln(c_pad, w_ada, b_ada.reshape(1, -1))[:batch].reshape(batch, 6, d)

    g0 = 4 * ML_W
    q0 = g0 + N_GATES
    n_grp = 2 * DA_HEADS

    def lane_groups(cols):
        r = cols.shape[0]
        return jnp.pad(cols.reshape(r, n_grp, DA_DH), ((0, 0), (0, 0), (0, LANES - DA_DH))).reshape(r, -1)

    w_main = jnp.concatenate([w_in[:, :g0],
                              lane_groups(w_in[:, q0:q0 + DA_W]),
                              lane_groups(w_in[:, q0 + DA_W:q0 + 2 * DA_W]),
                              w_in[:, q0 + 2 * DA_W:]], axis=1).astype(bf16)
    w_gate = w_in[:, g0:g0 + N_GATES]
    wg = jnp.pad(w_gate, ((0, 0), (0, LANES - N_GATES))).astype(bf16)
    wgt = w_gate.T.astype(bf16)
    gb_row = jnp.pad(gate_b, (0, LANES - N_GATES)).reshape(1, LANES)
    gb_col = gate_b.reshape(N_GATES, 1)
    qgain = lane_groups((jnp.tile(da_q_norm_g, n_grp) * (DA_DH ** -0.5 * LOG2E)).reshape(1, DA_W))
    kgain = lane_groups(jnp.tile(da_k_norm_g, n_grp).reshape(1, DA_W))

    pm, qh, kh, vh, gcol, grow = _inproj(x2, mod, w_main, wg, wgt, gb_row, gb_col, qgain, kgain, seq)

    hm = _mlstm(pm, gcol, grow, conv_w, conv_b.reshape(1, -1), ml_norm_g.reshape(1, -1), batch, seq)

    lam = (jnp.exp(jnp.sum(lambda_q1 * lambda_k1)) - jnp.exp(jnp.sum(lambda_q2 * lambda_k2))
           + lam_init).reshape(1, 1).astype(f32)
    hd = _attention(qh, kh, vh, lam, da_norm_g.reshape(1, -1), batch, seq, lam_init)

    x1, h2p, idx, gate, rank, counts = _route(hm, hd, x2, mod, w_out.astype(bf16),
                                              w_router.T.astype(bf16), router_bias.reshape(-1, 1), seq)

    bm = MOE_BLOCK
    n_blocks = (t * TOP_K) // bm + N_EXPERTS
    counts_i = counts.reshape(-1).astype(jnp.int32)
    padded = (counts_i + bm - 1) // bm * bm
    pad_end = jnp.cumsum(padded)
    pad_start = pad_end - padded
    digits = jnp.stack([(pad_start // SLOT_RADIX ** j) % SLOT_RADIX for j in range(SLOT_DIGITS)])
    digits = jnp.pad(digits, ((0, SUBLANES - SLOT_DIGITS), (0, 0))).astype(bf16)
    dest = _slots(idx, rank, digits)

    xs = _dispatch(dest, h2p, n_blocks * bm)
    ys_a, ys_b = _experts(pad_start.astype(jnp.int32), counts_i, xs, w1, w3, w2)
    ga, gb = _sc_gather_rows(ys_a, ys_b, dest.reshape(-1))
    out = _combine(gate.T, h2p, x1, mod, ws1.astype(bf16), ws3.astype(bf16), ws2.astype(bf16),
                   ga.reshape(TOP_K, t, QUARTER), gb.reshape(TOP_K, t, QUARTER), seq)
    return out.reshape(batch, seq, d)


def kernel(x, c, w_ada, b_ada, w_in, conv_w, conv_b, gate_b, ml_norm_g, da_q_norm_g, da_k_norm_g,
           lambda_q1, lambda_k1, lambda_q2, lambda_k2, da_norm_g, w_out, w_router, router_bias,
           w1, w3, w2, ws1, ws3, ws2):
    depth = w_ada.shape[0]
    for l in range(depth):
        x = _layer(x, c, w_ada[l], b_ada[l], w_in[l], conv_w[l], conv_b[l], gate_b[l], ml_norm_g[l],
                   da_q_norm_g[l], da_k_norm_g[l], lambda_q1[l], lambda_k1[l], lambda_q2[l], lambda_k2[l],
                   da_norm_g[l], w_out[l], w_router[l], router_bias[l], w1[l], w3[l], w2[l],
                   ws1[l], ws3[l], ws2[l], l)
    return x
```

```python
import functools
import math

import jax
import jax.numpy as jnp
from jax import lax
from jax.experimental import pallas as pl
from jax.experimental.pallas import tpu as pltpu
from jax.experimental.pallas import tpu_sc as plsc

D_MODEL = 1024
ML_HEADS = 4
ML_DH = 128
ML_W = ML_HEADS * ML_DH
ML_CHUNK = 128
CONV_K = 4
DA_HEADS = 4
DA_DH = 64
DA_W = DA_HEADS * 2 * DA_DH
N_EXPERTS = 256
TOP_K = 8
N_GROUPS = 8
GROUP_SIZE = N_EXPERTS // N_GROUPS
TOPK_GROUPS = 4
D_EXPERT = 256
ROUTED_SCALE = 2.5
EPS = 1e-6
N_GATES = 2 * ML_HEADS

LANES = 128
SUBLANES = 8
VMEM_LIMIT_BYTES = 48 * 1024 * 1024

TM_INPROJ = 512
ATT_BLOCK = 1024
TM_ROUTE = 256
TM_SLOTS = 512
TM_MOE = 256
MOE_BLOCK = 256
HALF = D_MODEL // 2
QUARTER = HALF // 2
SC_WINDOW = 128

NEG_BIG = -1e30
LOG2E = 1.4426950408889634

f32 = jnp.float32
bf16 = jnp.bfloat16
HIGHEST = lax.Precision.HIGHEST


def _cparams(n_axes):
    return pltpu.CompilerParams(dimension_semantics=("arbitrary",) * n_axes,
                                vmem_limit_bytes=VMEM_LIMIT_BYTES)


def _dot(a, b):
    return jnp.dot(a, b, preferred_element_type=f32)


def _dot_nt(a, b):
    return lax.dot_general(a, b, (((1,), (1,)), ((), ())), preferred_element_type=f32)


def _sigmoid(x):
    return 1.0 / (1.0 + jnp.exp(-x))


def _silu(x):
    return x * _sigmoid(x)


def _log_sigmoid(x):
    return jnp.minimum(x, 0.0) - jnp.log(1.0 + jnp.exp(-jnp.abs(x)))


def _pack_bf16_pair(lo, hi):
    lo_bits = lax.bitcast_convert_type(lo.astype(bf16).astype(f32), jnp.uint32)
    hi_bits = lax.bitcast_convert_type(hi.astype(bf16).astype(f32), jnp.uint32)
    return (hi_bits & jnp.uint32(0xFFFF0000)) | (lo_bits >> 16)


def _unpack_bf16_pair(w):
    lo = lax.bitcast_convert_type(w << 16, f32)
    hi = lax.bitcast_convert_type(w & jnp.uint32(0xFFFF0000), f32)
    return lo, hi


def _adaln_kernel(c_ref, w_ref, b_ref, o_ref):
    c = c_ref[...]
    o_ref[...] = jnp.dot(_silu(c), w_ref[...], precision=HIGHEST, preferred_element_type=f32) + b_ref[...]


def _adaln(c_pad, w_ada, b_ada):
    rows, d = c_pad.shape
    n = w_ada.shape[1]
    tn = 1536
    return pl.pallas_call(
        _adaln_kernel,
        out_shape=jax.ShapeDtypeStruct((rows, n), f32),
        grid=(n // tn,),
        in_specs=[pl.BlockSpec((rows, d), lambda j: (0, 0)),
                  pl.BlockSpec((d, tn), lambda j: (0, j)),
                  pl.BlockSpec((1, tn), lambda j: (0, j))],
        out_specs=pl.BlockSpec((rows, tn), lambda j: (0, j)),
        compiler_params=_cparams(1),
        name="adaln",
    )(c_pad, w_ada, b_ada)


def _alibi_slope_log2(head):
    return 2.0 ** (-8.0 * (head + 1) / DA_HEADS) * LOG2E


def _inproj_kernel(x_ref, mod_ref, w_ref, wg_ref, wgt_ref, gbr_ref, gbc_ref, qg_ref, kg_ref,
                   pm_ref, q_ref, k_ref, v_ref, gcol_ref, grow_ref, *, seq):
    tm = TM_INPROJ
    x = x_ref[...]
    h = x * lax.rsqrt(jnp.mean(x * x, axis=-1, keepdims=True) + EPS)
    h = h * (1.0 + mod_ref[0, 1:2, :]) + mod_ref[0, 0:1, :]
    hb = h.astype(bf16)
    cw = ML_W
    for n in range(4):
        pm_ref[:, n * cw:(n + 1) * cw] = _dot(hb, w_ref[:, n * cw:(n + 1) * cw]).astype(bf16)

    lane = lax.broadcasted_iota(jnp.int32, (tm, LANES), 1)
    q_aug = jnp.where(jnp.logical_and(lane >= DA_DH, lane < DA_DH + 3), 1.0, 0.0)
    pos = (pl.program_id(0) * tm) % seq + lax.broadcasted_iota(jnp.int32, (tm, 1), 0)
    pos = pos.astype(f32)
    q0 = 4 * cw
    k0 = q0 + 2 * DA_W
    yq = _dot(hb, w_ref[:, q0:q0 + 2 * DA_W])
    yk = _dot(hb, w_ref[:, k0:k0 + 2 * DA_W])
    for grp in range(2 * DA_HEADS):
        sl = slice(grp * LANES, (grp + 1) * LANES)
        y = yq[:, sl]
        ss = jnp.sum(y * y, axis=-1, keepdims=True)
        q_ref[:, sl] = (y * lax.rsqrt(ss * (1.0 / DA_DH) + EPS) * qg_ref[:, sl] + q_aug).astype(bf16)

        y = yk[:, sl]
        ss = jnp.sum(y * y, axis=-1, keepdims=True)
        r = pos * _alibi_slope_log2(grp // 2)
        r_hi = r.astype(bf16).astype(f32)
        r_mid = (r - r_hi).astype(bf16).astype(f32)
        r_lo = r - r_hi - r_mid
        k_aug = jnp.where(lane == DA_DH, r_hi,
                          jnp.where(lane == DA_DH + 1, r_mid, jnp.where(lane == DA_DH + 2, r_lo, 0.0)))
        k_ref[:, sl] = (y * lax.rsqrt(ss * (1.0 / DA_DH) + EPS) * kg_ref[:, sl] + k_aug).astype(bf16)

    v0 = k0 + 2 * DA_W
    v_ref[...] = _dot(hb, w_ref[:, v0:v0 + DA_W]).astype(bf16)
    gcol_ref[...] = _dot(hb, wg_ref[...]) + gbr_ref[...]
    grow_ref[...] = _dot_nt(wgt_ref[...], hb) + gbc_ref[...]


def _inproj(x2, mod, w_main, wg, wgt, gb_row, gb_col, qgain, kgain, seq):
    t, d = x2.shape
    tm = TM_INPROJ
    tiles_per_seq = seq // tm
    n_main = w_main.shape[1]
    const = lambda shape: pl.BlockSpec(shape, lambda i: (0,) * len(shape))
    return pl.pallas_call(
        functools.partial(_inproj_kernel, seq=seq),
        out_shape=(jax.ShapeDtypeStruct((t, 4 * ML_W), bf16),
                   jax.ShapeDtypeStruct((t, 2 * DA_W), bf16),
                   jax.ShapeDtypeStruct((t, 2 * DA_W), bf16),
                   jax.ShapeDtypeStruct((t, DA_W), bf16),
                   jax.ShapeDtypeStruct((t, LANES), f32),
                   jax.ShapeDtypeStruct((N_GATES, t), f32)),
        grid=(t // tm,),
        in_specs=[pl.BlockSpec((tm, d), lambda i: (i, 0)),
                  pl.BlockSpec((1, 6, d), lambda i: (i // tiles_per_seq, 0, 0)),
                  const((d, n_main)), const((d, LANES)), const((N_GATES, d)),
                  const((1, LANES)), const((N_GATES, 1)),
                  const((1, 2 * DA_W)), const((1, 2 * DA_W))],
        out_specs=(pl.BlockSpec((tm, 4 * ML_W), lambda i: (i, 0)),
                   pl.BlockSpec((tm, 2 * DA_W), lambda i: (i, 0)),
                   pl.BlockSpec((tm, 2 * DA_W), lambda i: (i, 0)),
                   pl.BlockSpec((tm, DA_W), lambda i: (i, 0)),
                   pl.BlockSpec((tm, LANES), lambda i: (i, 0)),
                   pl.BlockSpec((N_GATES, tm), lambda i: (0, i))),
        compiler_params=_cparams(1),
        name="inproj",
    )(x2, mod, w_main, wg, wgt, gb_row, gb_col, qgain, kgain)


def _mlstm_kernel(qk_ref, v_ref, o_ref, gcol_ref, grow_ref, cw_ref, cb_ref, ng_ref,
                  out_ref, state_ref, m_ref, prev_ref, ext_ref):
    L = ML_CHUNK
    dh = ML_DH
    c = pl.program_id(1)

    @pl.when(c == 0)
    def _():
        state_ref[...] = jnp.zeros_like(state_ref)
        m_ref[...] = jnp.zeros_like(m_ref)
        prev_ref[...] = jnp.zeros_like(prev_ref)

    cur = qk_ref[...].astype(f32)
    ext_ref[0:SUBLANES, :] = prev_ref[...]
    ext_ref[SUBLANES:SUBLANES + L, :] = cur
    prev_ref[...] = cur[L - SUBLANES:L, :]
    acc = cb_ref[...] + cw_ref[CONV_K - 1:CONV_K, :] * cur
    for j in range(CONV_K - 1):
        off = SUBLANES - (CONV_K - 1) + j
        acc = acc + cw_ref[j:j + 1, :] * ext_ref[off:off + L, :]
    qk = _silu(acc)

    row_i = lax.broadcasted_iota(jnp.int32, (L, L), 0)
    col_i = lax.broadcasted_iota(jnp.int32, (L, L), 1)
    causal = row_i >= col_i
    tril = causal.astype(f32)
    triu = (row_i <= col_i).astype(f32)
    gcol = gcol_ref[...]
    grow = grow_ref[...]
    bcol_all = jnp.dot(tril, _log_sigmoid(gcol), precision=HIGHEST, preferred_element_type=f32)
    brow_all = jnp.dot(_log_sigmoid(grow), triu, precision=HIGHEST, preferred_element_type=f32)

    lane = lax.broadcasted_iota(jnp.int32, (L, dh), 1)
    ones_col = (lane == 0).astype(f32)
    g = ng_ref[...]

    for hd in range(ML_HEADS):
        q = qk[:, hd * dh:(hd + 1) * dh]
        k = qk[:, ML_W + hd * dh:ML_W + (hd + 1) * dh] * (dh ** -0.5)
        v = v_ref[:, hd * dh:(hd + 1) * dh].astype(f32)
        v_aug = jnp.concatenate([v, ones_col], axis=1)
        qb = q.astype(bf16)
        kb = k.astype(bf16)

        bcol = bcol_all[:, ML_HEADS + hd:ML_HEADS + hd + 1]
        icol = gcol[:, hd:hd + 1]
        brow = brow_all[ML_HEADS + hd:ML_HEADS + hd + 1, :]
        irow = grow[hd:hd + 1, :]
        m_prev = m_ref[hd:hd + 1, 0:1]

        d_intra = jnp.where(causal, bcol - brow + irow, NEG_BIG)
        d_inter = bcol + m_prev
        m_t = jnp.maximum(d_inter, jnp.max(d_intra, axis=1, keepdims=True))
        w_intra = jnp.exp(d_intra - m_t)
        w_inter = jnp.exp(d_inter - m_t)

        s = _dot_nt(qb, kb) * w_intra
        state = state_ref[hd]
        q_state = _dot(qb, state.astype(bf16))
        s_v = _dot(s.astype(bf16), v_aug.astype(bf16))
        num = w_inter * q_state[:, 0:dh] + s_v[:, 0:dh]
        den = w_inter * q_state[:, dh:dh + 1] + s_v[:, dh:dh + 1]
        hval = num / jnp.maximum(jnp.abs(den), jnp.exp(-m_t))

        b_last = bcol[L - 1:L, :]
        d_state = b_last - bcol + icol
        m_new = jnp.maximum(b_last + m_prev, jnp.max(d_state, axis=0, keepdims=True))
        carry_scale = jnp.exp(b_last + m_prev - m_new)
        wk = jnp.exp(d_state - m_new)
        upd = _dot(k.T.astype(bf16), (v_aug * wk).astype(bf16))
        state_ref[hd] = carry_scale * state + upd
        m_ref[hd:hd + 1, :] = jnp.broadcast_to(m_new, (1, LANES))

        hn = hval * lax.rsqrt(jnp.mean(hval * hval, axis=-1, keepdims=True) + EPS) * g
        og = _sigmoid(o_ref[:, hd * dh:(hd + 1) * dh].astype(f32))
        out_ref[:, hd * dh:(hd + 1) * dh] = (hn * og).astype(bf16)


def _mlstm(pm, gcol, grow, conv_w, conv_b, ml_norm_g, batch, seq):
    t = pm.shape[0]
    L = ML_CHUNK
    nc = seq // L
    row = lambda b, c: b * nc + c
    return pl.pallas_call(
        _mlstm_kernel,
        out_shape=jax.ShapeDtypeStruct((t, ML_W), bf16),
        grid=(batch, nc),
        in_specs=[pl.BlockSpec((L, 2 * ML_W), lambda b, c: (row(b, c), 0)),
                  pl.BlockSpec((L, ML_W), lambda b, c: (row(b, c), 2)),
                  pl.BlockSpec((L, ML_W), lambda b, c: (row(b, c), 3)),
                  pl.BlockSpec((L, LANES), lambda b, c: (row(b, c), 0)),
                  pl.BlockSpec((N_GATES, L), lambda b, c: (0, row(b, c))),
                  pl.BlockSpec((CONV_K, 2 * ML_W), lambda b, c: (0, 0)),
                  pl.BlockSpec((1, 2 * ML_W), lambda b, c: (0, 0)),
                  pl.BlockSpec((1, ML_DH), lambda b, c: (0, 0))],
        out_specs=pl.BlockSpec((L, ML_W), lambda b, c: (row(b, c), 0)),
        scratch_shapes=[pltpu.VMEM((ML_HEADS, ML_DH, 2 * ML_DH), f32),
                        pltpu.VMEM((SUBLANES, LANES), f32),
                        pltpu.VMEM((SUBLANES, 2 * ML_W), f32),
                        pltpu.VMEM((SUBLANES + ML_CHUNK, 2 * ML_W), f32)],
        compiler_params=_cparams(2),
        name="mlstm",
    )(pm, pm, pm, gcol, grow, conv_w, conv_b, ml_norm_g)


def _attn_block(q_ref, k_ref, v_ref, m_ref, l_ref, acc_ref, diagonal):
    blk = ATT_BLOCK
    nch = blk // LANES
    v = v_ref[...]
    if diagonal:
        keep = (lax.broadcasted_iota(jnp.int32, (blk, blk), 1)
                <= lax.broadcasted_iota(jnp.int32, (blk, blk), 0))
    for c in range(2):
        sl = slice(c * LANES, (c + 1) * LANES)
        s = _dot_nt(q_ref[:, sl], k_ref[:, sl])
        if diagonal:
            s = jnp.where(keep, s, NEG_BIG)
        chunks = [s[:, j * LANES:(j + 1) * LANES] for j in range(nch)]
        mc = chunks[0]
        for ch in chunks[1:]:
            mc = jnp.maximum(mc, ch)
        m_old = m_ref[c]
        m_new = jnp.maximum(m_old, jnp.max(mc, axis=1, keepdims=True))
        alpha = jnp.exp2(m_old - m_new)
        ps = [jnp.exp2(ch - m_new) for ch in chunks]
        lsum = ps[0]
        for pj in ps[1:]:
            lsum = lsum + pj
        p = jnp.concatenate([pj.astype(bf16) for pj in ps], axis=1)
        l_ref[c] = alpha * l_ref[c] + lsum
        acc_ref[c] = alpha * acc_ref[c] + _dot(p, v)
        m_ref[c] = m_new


def _attn_kernel(qt_ref, kt_ref, q_ref, k_ref, v_ref, lam_ref, g_ref, o_ref,
                 m_ref, l_ref, acc_ref, *, lam_init):
    step = pl.program_id(2)
    qi = qt_ref[step]
    ki = kt_ref[step]

    @pl.when(ki == 0)
    def _():
        m_ref[...] = jnp.full_like(m_ref, NEG_BIG)
        l_ref[...] = jnp.zeros_like(l_ref)
        acc_ref[...] = jnp.zeros_like(acc_ref)

    @pl.when(ki < qi)
    def _():
        _attn_block(q_ref, k_ref, v_ref, m_ref, l_ref, acc_ref, diagonal=False)

    @pl.when(ki == qi)
    def _():
        _attn_block(q_ref, k_ref, v_ref, m_ref, l_ref, acc_ref, diagonal=True)
        lam = lam_ref[...]
        l0 = jnp.sum(l_ref[0], axis=1, keepdims=True)
        l1 = jnp.sum(l_ref[1], axis=1, keepdims=True)
        o = acc_ref[0] / l0 - lam * (acc_ref[1] / l1)
        o = o * lax.rsqrt(jnp.mean(o * o, axis=-1, keepdims=True) + EPS)
        o_ref[...] = (o * g_ref[...] * (1.0 - lam_init)).astype(bf16)


def _attention(qh, kh, vh, lam, da_norm_g, batch, seq, lam_init):
    t = qh.shape[0]
    blk = ATT_BLOCK
    nb = seq // blk
    pairs = [(i, j) for i in range(nb) for j in range(i + 1)]
    qt = jnp.asarray([p[0] for p in pairs], jnp.int32)
    kt = jnp.asarray([p[1] for p in pairs], jnp.int32)
    w = 2 * DA_DH
    grid_spec = pltpu.PrefetchScalarGridSpec(
        num_scalar_prefetch=2,
        grid=(batch, DA_HEADS, len(pairs)),
        in_specs=[pl.BlockSpec((blk, 2 * LANES), lambda b, h, s, qt, kt: (b * nb + qt[s], h)),
                  pl.BlockSpec((blk, 2 * LANES), lambda b, h, s, qt, kt: (b * nb + kt[s], h)),
                  pl.BlockSpec((blk, w), lambda b, h, s, qt, kt: (b * nb + kt[s], h)),
                  pl.BlockSpec((1, 1), lambda b, h, s, qt, kt: (0, 0)),
                  pl.BlockSpec((1, w), lambda b, h, s, qt, kt: (0, 0))],
        out_specs=pl.BlockSpec((blk, w), lambda b, h, s, qt, kt: (b * nb + qt[s], h)),
        scratch_shapes=[pltpu.VMEM((2, blk, LANES), f32),
                        pltpu.VMEM((2, blk, LANES), f32),
                        pltpu.VMEM((2, blk, w), f32)],
    )
    return pl.pallas_call(
        functools.partial(_attn_kernel, lam_init=lam_init),
        out_shape=jax.ShapeDtypeStruct((t, DA_W), bf16),
        grid_spec=grid_spec,
        compiler_params=_cparams(3),
        name="attn",
    )(qt, kt, qh, kh, vh, lam, da_norm_g)


def _first_index_of_max(x, iota_f, size):
    m = jnp.max(x, axis=0, keepdims=True)
    idx = jnp.min(jnp.where(x == m, iota_f, float(size)), axis=0, keepdims=True)
    return m, idx


def _route_kernel(hm_ref, hd_ref, x_ref, mod_ref, wo_ref, wr_ref, rb_ref,
                  x1_ref, ha_ref, hb_ref, idx_ref, gate_ref, rank_ref, cnt_ref, base_ref):
    tm = TM_ROUTE
    i = pl.program_id(0)

    @pl.when(i == 0)
    def _():
        base_ref[...] = jnp.zeros_like(base_ref)

    mix = _dot(hm_ref[...], wo_ref[0:ML_W, :]) + _dot(hd_ref[...], wo_ref[ML_W:ML_W + DA_W, :])
    x1 = x_ref[...] + mod_ref[0, 2:3, :] * mix
    x1_ref[...] = x1
    h2 = x1 * lax.rsqrt(jnp.mean(x1 * x1, axis=-1, keepdims=True) + EPS)
    h2 = h2 * (1.0 + mod_ref[0, 4:5, :]) + mod_ref[0, 3:4, :]
    packed = lax.bitcast_convert_type(_pack_bf16_pair(h2[:, 0:HALF], h2[:, HALF:D_MODEL]), jnp.int32)
    ha_ref[...] = packed[:, 0:QUARTER]
    hb_ref[...] = packed[:, QUARTER:HALF]

    sc = _sigmoid(_dot_nt(wr_ref[...], h2.astype(bf16)))
    sel = sc + rb_ref[...]

    gi = lax.broadcasted_iota(jnp.int32, (GROUP_SIZE, tm), 0).astype(f32)
    gscores = []
    for g in range(N_GROUPS):
        blk = sel[g * GROUP_SIZE:(g + 1) * GROUP_SIZE, :]
        m1, i1 = _first_index_of_max(blk, gi, GROUP_SIZE)
        m2 = jnp.max(jnp.where(gi == i1, NEG_BIG, blk), axis=0, keepdims=True)
        gscores.append(m1 + m2)
    gs = jnp.concatenate(gscores, axis=0)

    ngi = lax.broadcasted_iota(jnp.int32, (N_GROUPS, tm), 0).astype(f32)
    gkeep = jnp.zeros((N_GROUPS, tm), f32)
    for _ in range(TOPK_GROUPS):
        _, gidx = _first_index_of_max(gs, ngi, N_GROUPS)
        hit = ngi == gidx
        gkeep = jnp.where(hit, 1.0, gkeep)
        gs = jnp.where(hit, NEG_BIG, gs)
    masked = jnp.concatenate(
        [jnp.where(gkeep[g:g + 1, :] > 0.0, sel[g * GROUP_SIZE:(g + 1) * GROUP_SIZE, :], NEG_BIG)
         for g in range(N_GROUPS)], axis=0)

    ei = lax.broadcasted_iota(jnp.int32, (N_EXPERTS, tm), 0).astype(f32)
    idxs, gates = [], []
    chosen = jnp.zeros((N_EXPERTS, tm), f32)
    for _ in range(TOP_K):
        _, eidx = _first_index_of_max(masked, ei, N_EXPERTS)
        hit = ei == eidx
        idxs.append(eidx)
        gates.append(jnp.sum(jnp.where(hit, sc, 0.0), axis=0, keepdims=True))
        chosen = jnp.where(hit, 1.0, chosen)
        masked = jnp.where(hit, NEG_BIG, masked)
    gate = jnp.concatenate(gates, axis=0)
    gate = gate / jnp.sum(gate, axis=0, keepdims=True) * ROUTED_SCALE
    gate_ref[...] = gate
    idx_ref[...] = jnp.concatenate(idxs, axis=0).astype(jnp.int32)

    tr = lax.broadcasted_iota(jnp.int32, (tm, tm), 0)
    tc = lax.broadcasted_iota(jnp.int32, (tm, tm), 1)
    before = (tr < tc).astype(bf16)
    seen = base_ref[...] + _dot(chosen.astype(bf16), before)
    ranks = [jnp.sum(jnp.where(ei == eidx, seen, 0.0), axis=0, keepdims=True) for eidx in idxs]
    rank_ref[...] = jnp.concatenate(ranks, axis=0).astype(jnp.int32)
    total = base_ref[...] + jnp.sum(chosen, axis=1, keepdims=True)
    base_ref[...] = total
    cnt_ref[...] = total


def _route(hm, hd, x2, mod, w_out, w_router_t, rbias_col, seq):
    t, d = x2.shape
    tm = TM_ROUTE
    tiles_per_seq = seq // tm
    const = lambda shape: pl.BlockSpec(shape, lambda i: (0,) * len(shape))
    return pl.pallas_call(
        _route_kernel,
        out_shape=(jax.ShapeDtypeStruct((t, d), f32),
                   jax.ShapeDtypeStruct((t, QUARTER), jnp.int32),
                   jax.ShapeDtypeStruct((t, QUARTER), jnp.int32),
                   jax.ShapeDtypeStruct((TOP_K, t), jnp.int32),
                   jax.ShapeDtypeStruct((TOP_K, t), f32),
                   jax.ShapeDtypeStruct((TOP_K, t), jnp.int32),
                   jax.ShapeDtypeStruct((N_EXPERTS, 1), f32)),
        grid=(t // tm,),
        in_specs=[pl.BlockSpec((tm, ML_W), lambda i: (i, 0)),
                  pl.BlockSpec((tm, DA_W), lambda i: (i, 0)),
                  pl.BlockSpec((tm, d), lambda i: (i, 0)),
                  pl.BlockSpec((1, 6, d), lambda i: (i // tiles_per_seq, 0, 0)),
                  const((d, d)), const((N_EXPERTS, d)), const((N_EXPERTS, 1))],
        out_specs=(pl.BlockSpec((tm, d), lambda i: (i, 0)),
                   pl.BlockSpec((tm, QUARTER), lambda i: (i, 0)),
                   pl.BlockSpec((tm, QUARTER), lambda i: (i, 0)),
                   pl.BlockSpec((TOP_K, tm), lambda i: (0, i)),
                   pl.BlockSpec((TOP_K, tm), lambda i: (0, i)),
                   pl.BlockSpec((TOP_K, tm), lambda i: (0, i)),
                   const((N_EXPERTS, 1))),
        scratch_shapes=[pltpu.VMEM((N_EXPERTS, 1), f32)],
        compiler_params=_cparams(1),
        name="route",
    )(hm, hd, x2, mod, w_out, w_router_t, rbias_col)


SLOT_RADIX = 256
SLOT_DIGITS = 3


def _slots_kernel(idx_ref, rank_ref, dig_ref, dest_ref):
    tm = TM_SLOTS
    ei = lax.broadcasted_iota(jnp.int32, (N_EXPERTS, tm), 0)
    rows = []
    for k in range(TOP_K):
        onehot = jnp.where(ei == idx_ref[k:k + 1, :], 1.0, 0.0).astype(bf16)
        dg = _dot(dig_ref[...], onehot)
        start = dg[0:1, :]
        for j in range(1, SLOT_DIGITS):
            start = start + dg[j:j + 1, :] * float(SLOT_RADIX ** j)
        rows.append(start.astype(jnp.int32) + rank_ref[k:k + 1, :])
    dest_ref[...] = jnp.concatenate(rows, axis=0)


def _slots(idx, rank, digits):
    t = idx.shape[1]
    tm = TM_SLOTS
    return pl.pallas_call(
        _slots_kernel,
        out_shape=jax.ShapeDtypeStruct((TOP_K, t), jnp.int32),
        grid=(t // tm,),
        in_specs=[pl.BlockSpec((TOP_K, tm), lambda i: (0, i)),
                  pl.BlockSpec((TOP_K, tm), lambda i: (0, i)),
                  pl.BlockSpec((SUBLANES, N_EXPERTS), lambda i: (0, 0))],
        out_specs=pl.BlockSpec((TOP_K, tm), lambda i: (0, i)),
        compiler_params=_cparams(1),
        name="slots",
    )(idx, rank, digits)


def _sc_scatter_table(rows, dest, n_slots):
    n, width = rows.shape
    mesh = plsc.VectorSubcoreMesh(core_axis_name="core", subcore_axis_name="subcore")

    @pl.kernel(out_type=jax.ShapeDtypeStruct((n_slots, width), rows.dtype), mesh=mesh, scratch_types=[],
               name="sc_scatter")
    def scatter_kernel(rows_hbm, dest_hbm, out_hbm):
        def body(rows_vmem, dest_vmem):
            for k in range(TOP_K):
                pltpu.sync_copy(rows_vmem, out_hbm.at[dest_vmem.at[k]])

        pltpu.emit_pipeline(
            body,
            grid=(n // SC_WINDOW,),
            in_specs=[pl.BlockSpec((SC_WINDOW, width), lambda i: (i, 0)),
                      pl.BlockSpec((TOP_K, SC_WINDOW), lambda i: (0, i))],
            out_specs=[],
            core_axis_name=("core", "subcore"),
            dimension_semantics=(pltpu.PARALLEL,),
        )(rows_hbm, dest_hbm)

    return scatter_kernel(rows, dest)


EXPERT_RING = 4


def _expert_kernel(start_ref, cnt_ref, xsa_ref, xsb_ref, w1_ref, w3_ref, w2_ref, ysa_ref, ysb_ref,
                   xbuf, ybuf, w1b, w3b, w2b, in_sem, out_sem, done_ref):
    bm = MOE_BLOCK
    ring = EXPERT_RING
    ahead = ring - 1
    e = pl.program_id(0)
    last = pl.num_programs(0) - 1

    def n_blocks(ex):
        return lax.shift_right_logical(cnt_ref[ex] + (bm - 1), int(math.log2(bm)))

    def slot_of(g):
        return jnp.bitwise_and(g, ring - 1)

    def in_copies(ex_base, j, g):
        s = slot_of(g)
        rows = pl.ds(pl.multiple_of(ex_base + j * bm, bm), bm)
        return [pltpu.make_async_copy(x_ref.at[rows], xbuf.at[s, half], in_sem.at[s])
                for half, x_ref in enumerate((xsa_ref, xsb_ref))]

    def out_copies(ex_base, j, g):
        s = slot_of(g)
        rows = pl.ds(pl.multiple_of(ex_base + j * bm, bm), bm)
        return [pltpu.make_async_copy(ybuf.at[s, half], y_ref.at[rows], out_sem.at[s])
                for half, y_ref in enumerate((ysa_ref, ysb_ref))]

    def request_first_blocks(ex, g_first):
        nbx = n_blocks(ex)
        for j in range(ahead):
            @pl.when(j < nbx)
            def _():
                for cp in in_copies(start_ref[ex], j, g_first + j):
                    cp.start()

    @pl.when(e == 0)
    def _():
        done_ref[0] = 0
        request_first_blocks(0, 0)

    base = start_ref[e]
    cnt = cnt_ref[e]
    nb = n_blocks(e)
    g0 = done_ref[0]

    @pl.when(nb > 0)
    def _():
        w1b[...] = w1_ref[0].astype(bf16)
        w3b[...] = w3_ref[0].astype(bf16)
        w2b[...] = w2_ref[0].astype(bf16)

    def block(j, carry):
        g = g0 + j
        s = slot_of(g)

        @pl.when(j + ahead < nb)
        def _():
            for cp in in_copies(base, j + ahead, g + ahead):
                cp.start()

        for cp in in_copies(base, j, g):
            cp.wait()

        @pl.when(g >= ring)
        def _():
            for cp in out_copies(base, 0, g):
                cp.wait()

        row = j * bm + lax.broadcasted_iota(jnp.int32, (bm, HALF), 0)
        words = jnp.concatenate([xbuf[s, 0], xbuf[s, 1]], axis=1)
        lo, hi = _unpack_bf16_pair(lax.bitcast_convert_type(jnp.where(row < cnt, words, 0), jnp.uint32))
        lo = lo.astype(bf16)
        hi = hi.astype(bf16)
        h1 = _dot(lo, w1b[0:HALF, :]) + _dot(hi, w1b[HALF:D_MODEL, :])
        h3 = _dot(lo, w3b[0:HALF, :]) + _dot(hi, w3b[HALF:D_MODEL, :])
        y = _dot((_silu(h1) * h3).astype(bf16), w2b[...])
        packed = lax.bitcast_convert_type(_pack_bf16_pair(y[:, 0:HALF], y[:, HALF:D_MODEL]), jnp.int32)
        ybuf[s, 0] = packed[:, 0:QUARTER]
        ybuf[s, 1] = packed[:, QUARTER:HALF]
        for cp in out_copies(base, j, g):
            cp.start()
        return carry

    lax.fori_loop(0, nb, block, 0)
    g_end = g0 + nb
    done_ref[0] = g_end

    @pl.when(e < last)
    def _():
        request_first_blocks(e + 1, g_end)

    @pl.when(e == last)
    def _():
        for back in range(1, ring + 1):
            @pl.when(g_end - back >= 0)
            def _():
                for cp in out_copies(base, 0, g_end - back):
                    cp.wait()


def _experts(start, counts, xs_a, xs_b, w1, w3, w2):
    n_pad = xs_a.shape[0]
    bm = MOE_BLOCK
    grid_spec = pltpu.PrefetchScalarGridSpec(
        num_scalar_prefetch=2,
        grid=(N_EXPERTS,),
        in_specs=[pl.BlockSpec(memory_space=pl.ANY),
                  pl.BlockSpec(memory_space=pl.ANY),
                  pl.BlockSpec((1, D_MODEL, D_EXPERT), lambda e, st, ct: (e, 0, 0)),
                  pl.BlockSpec((1, D_MODEL, D_EXPERT), lambda e, st, ct: (e, 0, 0)),
                  pl.BlockSpec((1, D_EXPERT, D_MODEL), lambda e, st, ct: (e, 0, 0))],
        out_specs=(pl.BlockSpec(memory_space=pl.ANY), pl.BlockSpec(memory_space=pl.ANY)),
        scratch_shapes=[pltpu.VMEM((EXPERT_RING, 2, bm, QUARTER), jnp.int32),
                        pltpu.VMEM((EXPERT_RING, 2, bm, QUARTER), jnp.int32),
                        pltpu.VMEM((D_MODEL, D_EXPERT), bf16),
                        pltpu.VMEM((D_MODEL, D_EXPERT), bf16),
                        pltpu.VMEM((D_EXPERT, D_MODEL), bf16),
                        pltpu.SemaphoreType.DMA((EXPERT_RING,)),
                        pltpu.SemaphoreType.DMA((EXPERT_RING,)),
                        pltpu.SMEM((1,), jnp.int32)],
    )
    return pl.pallas_call(
        _expert_kernel,
        out_shape=(jax.ShapeDtypeStruct((n_pad, QUARTER), jnp.int32),
                   jax.ShapeDtypeStruct((n_pad, QUARTER), jnp.int32)),
        grid_spec=grid_spec,
        compiler_params=_cparams(1),
        name="experts",
    )(start, counts, xs_a, xs_b, w1, w3, w2)


def _sc_gather_table(table, idx2):
    n = idx2.shape[1]
    width = table.shape[1]
    mesh = plsc.VectorSubcoreMesh(core_axis_name="core", subcore_axis_name="subcore")

    @pl.kernel(out_type=jax.ShapeDtypeStruct((n, width), table.dtype), mesh=mesh, scratch_types=[],
               name="sc_gather")
    def gather_kernel(tab_hbm, idx_hbm, out_hbm):
        def body(idx_vmem, out_vmem):
            pltpu.sync_copy(tab_hbm.at[idx_vmem.at[0]], out_vmem)

        pltpu.emit_pipeline(
            body,
            grid=(n // SC_WINDOW,),
            in_specs=[pl.BlockSpec((1, SC_WINDOW), lambda i: (0, i))],
            out_specs=[pl.BlockSpec((SC_WINDOW, width), lambda i: (i, 0))],
            core_axis_name=("core", "subcore"),
            dimension_semantics=(pltpu.PARALLEL,),
        )(idx_hbm, out_hbm)

    return gather_kernel(table, idx2)


def _sc_gather_rows(table_a, table_b, idx):
    idx2 = idx.reshape(1, -1)
    return _sc_gather_table(table_a, idx2), _sc_gather_table(table_b, idx2)


def _combine_kernel(gate_ref, ha_ref, hb_ref, x1_ref, mod_ref, ws1_ref, ws3_ref, ws2_ref, ga_ref, gb_ref,
                    out_ref):
    words = jnp.concatenate([ha_ref[...], hb_ref[...]], axis=1)
    lo, hi = _unpack_bf16_pair(lax.bitcast_convert_type(words, jnp.uint32))
    lo = lo.astype(bf16)
    hi = hi.astype(bf16)
    s1 = _dot(lo, ws1_ref[0:HALF, :]) + _dot(hi, ws1_ref[HALF:D_MODEL, :])
    s3 = _dot(lo, ws3_ref[0:HALF, :]) + _dot(hi, ws3_ref[HALF:D_MODEL, :])
    y = _dot((_silu(s1) * s3).astype(bf16), ws2_ref[...])

    gate = gate_ref[...]
    gate_f = mod_ref[0, 5:6, :]
    for part, g_ref in enumerate((ga_ref, gb_ref)):
        c_lo = slice(part * QUARTER, (part + 1) * QUARTER)
        c_hi = slice(HALF + part * QUARTER, HALF + (part + 1) * QUARTER)
        acc_lo = y[:, c_lo]
        acc_hi = y[:, c_hi]
        for k in range(TOP_K):
            rlo, rhi = _unpack_bf16_pair(lax.bitcast_convert_type(g_ref[k], jnp.uint32))
            gk = gate[:, k:k + 1]
            acc_lo = acc_lo + gk * rlo
            acc_hi = acc_hi + gk * rhi
        out_ref[:, c_lo] = x1_ref[:, c_lo] + gate_f[:, c_lo] * acc_lo
        out_ref[:, c_hi] = x1_ref[:, c_hi] + gate_f[:, c_hi] * acc_hi


def _combine(gate_col, h2a, h2b, x1, mod, ws1, ws3, ws2, ga, gb, seq):
    t, d = x1.shape
    tm = TM_MOE
    tiles_per_seq = seq // tm
    const = lambda shape: pl.BlockSpec(shape, lambda i: (0,) * len(shape))
    return pl.pallas_call(
        _combine_kernel,
        out_shape=jax.ShapeDtypeStruct((t, d), f32),
        grid=(t // tm,),
        in_specs=[pl.BlockSpec((tm, TOP_K), lambda i: (i, 0)),
                  pl.BlockSpec((tm, QUARTER), lambda i: (i, 0)),
                  pl.BlockSpec((tm, QUARTER), lambda i: (i, 0)),
                  pl.BlockSpec((tm, d), lambda i: (i, 0)),
                  pl.BlockSpec((1, 6, d), lambda i: (i // tiles_per_seq, 0, 0)),
                  const((d, D_EXPERT)), const((d, D_EXPERT)), const((D_EXPERT, d)),
                  pl.BlockSpec((TOP_K, tm, QUARTER), lambda i: (0, i, 0)),
                  pl.BlockSpec((TOP_K, tm, QUARTER), lambda i: (0, i, 0))],
        out_specs=pl.BlockSpec((tm, d), lambda i: (i, 0)),
        compiler_params=_cparams(1),
        name="combine",
    )(gate_col, h2a, h2b, x1, mod, ws1, ws3, ws2, ga, gb)


def _lambda_init(layer):
    return 0.8 - 0.6 * math.exp(-0.3 * layer)


def _layer(x, c, w_ada, b_ada, w_in, conv_w, conv_b, gate_b, ml_norm_g, da_q_norm_g, da_k_norm_g,
           lambda_q1, lambda_k1, lambda_q2, lambda_k2, da_norm_g, w_out, w_router, router_bias,
           w1, w3, w2, ws1, ws3, ws2, layer):
    batch, seq, d = x.shape
    t = batch * seq
    lam_init = _lambda_init(layer)
    x2 = x.reshape(t, d)

    c_pad = jnp.pad(c, ((0, -batch % SUBLANES), (0, 0)))
    mod = _adaln(c_pad, w_ada, b_ada.reshape(1, -1))[:batch].reshape(batch, 6, d)

    g0 = 4 * ML_W
    q0 = g0 + N_GATES
    n_grp = 2 * DA_HEADS

    def lane_groups(cols):
        r = cols.shape[0]
        return jnp.pad(cols.reshape(r, n_grp, DA_DH), ((0, 0), (0, 0), (0, LANES - DA_DH))).reshape(r, -1)

    w_main = jnp.concatenate([w_in[:, :g0],
                              lane_groups(w_in[:, q0:q0 + DA_W]),
                              lane_groups(w_in[:, q0 + DA_W:q0 + 2 * DA_W]),
                              w_in[:, q0 + 2 * DA_W:]], axis=1).astype(bf16)
    w_gate = w_in[:, g0:g0 + N_GATES]
    wg = jnp.pad(w_gate, ((0, 0), (0, LANES - N_GATES))).astype(bf16)
    wgt = w_gate.T.astype(bf16)
    gb_row = jnp.pad(gate_b, (0, LANES - N_GATES)).reshape(1, LANES)
    gb_col = gate_b.reshape(N_GATES, 1)
    qgain = lane_groups((jnp.tile(da_q_norm_g, n_grp) * (DA_DH ** -0.5 * LOG2E)).reshape(1, DA_W))
    kgain = lane_groups(jnp.tile(da_k_norm_g, n_grp).reshape(1, DA_W))

    pm, qh, kh, vh, gcol, grow = _inproj(x2, mod, w_main, wg, wgt, gb_row, gb_col, qgain, kgain, seq)

    hm = _mlstm(pm, gcol, grow, conv_w, conv_b.reshape(1, -1), ml_norm_g.reshape(1, -1), batch, seq)

    lam = (jnp.exp(jnp.sum(lambda_q1 * lambda_k1)) - jnp.exp(jnp.sum(lambda_q2 * lambda_k2))
           + lam_init).reshape(1, 1).astype(f32)
    hd = _attention(qh, kh, vh, lam, da_norm_g.reshape(1, -1), batch, seq, lam_init)

    x1, h2a, h2b, idx, gate, rank, counts = _route(hm, hd, x2, mod, w_out.astype(bf16),
                                                   w_router.T.astype(bf16), router_bias.reshape(-1, 1), seq)

    bm = MOE_BLOCK
    n_blocks = (t * TOP_K) // bm + N_EXPERTS
    counts_i = counts.reshape(-1).astype(jnp.int32)
    padded = (counts_i + bm - 1) // bm * bm
    pad_end = jnp.cumsum(padded)
    pad_start = pad_end - padded
    digits = jnp.stack([(pad_start // SLOT_RADIX ** j) % SLOT_RADIX for j in range(SLOT_DIGITS)])
    digits = jnp.pad(digits, ((0, SUBLANES - SLOT_DIGITS), (0, 0))).astype(bf16)
    dest = _slots(idx, rank, digits)

    xs_a = _sc_scatter_table(h2a, dest, n_blocks * bm)
    xs_b = _sc_scatter_table(h2b, dest, n_blocks * bm)
    ys_a, ys_b = _experts(pad_start.astype(jnp.int32), counts_i, xs_a, xs_b, w1, w3, w2)
    ga, gb = _sc_gather_rows(ys_a, ys_b, dest.reshape(-1))
    out = _combine(gate.T, h2a, h2b, x1, mod, ws1.astype(bf16), ws3.astype(bf16), ws2.astype(bf16),
                   ga.reshape(TOP_K, t, QUARTER), gb.reshape(TOP_K, t, QUARTER), seq)
    return out.reshape(batch, seq, d)


def kernel(x, c, w_ada, b_ada, w_in, conv_w, conv_b, gate_b, ml_norm_g, da_q_norm_g, da_k_norm_g,
           lambda_q1, lambda_k1, lambda_q2, lambda_k2, da_norm_g, w_out, w_router, router_bias,
           w1, w3, w2, ws1, ws3, ws2):
    depth = w_ada.shape[0]
    for l in range(depth):
        x = _layer(x, c, w_ada[l], b_ada[l], w_in[l], conv_w[l], conv_b[l], gate_b[l], ml_norm_g[l],
                   da_q_norm_g[l], da_k_norm_g[l], lambda_q1[l], lambda_k1[l], lambda_q2[l], lambda_k2[l],
                   da_norm_g[l], w_out[l], w_router[l], router_bias[l], w1[l], w3[l], w2[l],
                   ws1[l], ws3[l], ws2[l], l)
    return x
```

```python
import functools
import math

import jax
import jax.numpy as jnp
from jax import lax
from jax.experimental import pallas as pl
from jax.experimental.pallas import tpu as pltpu
from jax.experimental.pallas import tpu_sc as plsc

D_MODEL = 1024
ML_HEADS = 4
ML_DH = 128
ML_W = ML_HEADS * ML_DH
ML_CHUNK = 128
CONV_K = 4
DA_HEADS = 4
DA_DH = 64
DA_W = DA_HEADS * 2 * DA_DH
N_EXPERTS = 256
TOP_K = 8
N_GROUPS = 8
GROUP_SIZE = N_EXPERTS // N_GROUPS
TOPK_GROUPS = 4
D_EXPERT = 256
ROUTED_SCALE = 2.5
EPS = 1e-6
N_GATES = 2 * ML_HEADS

LANES = 128
SUBLANES = 8
VMEM_LIMIT_BYTES = 48 * 1024 * 1024

TM_INPROJ = 512
ML_STEP_CHUNKS = 4
ATT_BLOCK = 1024
TM_ROUTE = 256
TM_SLOTS = 512
TM_MOE = 512
MOE_BLOCK = 256
HALF = D_MODEL // 2
QUARTER = HALF // 2
SC_WINDOW = 128

NEG_BIG = -1e30
LOG2E = 1.4426950408889634

f32 = jnp.float32
bf16 = jnp.bfloat16
HIGHEST = lax.Precision.HIGHEST


def _cparams(n_axes):
    return pltpu.CompilerParams(dimension_semantics=("arbitrary",) * n_axes,
                                vmem_limit_bytes=VMEM_LIMIT_BYTES)


def _dot(a, b):
    return jnp.dot(a, b, preferred_element_type=f32)


def _dot_nt(a, b):
    return lax.dot_general(a, b, (((1,), (1,)), ((), ())), preferred_element_type=f32)


def _sigmoid(x):
    return 1.0 / (1.0 + jnp.exp(-x))


def _silu(x):
    return x * _sigmoid(x)


def _log_sigmoid(x):
    return jnp.minimum(x, 0.0) - jnp.log(1.0 + jnp.exp(-jnp.abs(x)))


def _pack_bf16_pair(lo, hi):
    lo_bits = lax.bitcast_convert_type(lo.astype(bf16).astype(f32), jnp.uint32)
    hi_bits = lax.bitcast_convert_type(hi.astype(bf16).astype(f32), jnp.uint32)
    return (hi_bits & jnp.uint32(0xFFFF0000)) | (lo_bits >> 16)


def _unpack_bf16_pair(w):
    lo = lax.bitcast_convert_type(w << 16, f32)
    hi = lax.bitcast_convert_type(w & jnp.uint32(0xFFFF0000), f32)
    return lo, hi


def _adaln_kernel(c_ref, w_ref, b_ref, o_ref):
    c = c_ref[...]
    o_ref[...] = jnp.dot(_silu(c), w_ref[...], precision=HIGHEST, preferred_element_type=f32) + b_ref[...]


def _adaln(c_pad, w_ada, b_ada):
    rows, d = c_pad.shape
    n = w_ada.shape[1]
    tn = 1536
    return pl.pallas_call(
        _adaln_kernel,
        out_shape=jax.ShapeDtypeStruct((rows, n), f32),
        grid=(n // tn,),
        in_specs=[pl.BlockSpec((rows, d), lambda j: (0, 0)),
                  pl.BlockSpec((d, tn), lambda j: (0, j)),
                  pl.BlockSpec((1, tn), lambda j: (0, j))],
        out_specs=pl.BlockSpec((rows, tn), lambda j: (0, j)),
        compiler_params=_cparams(1),
        name="adaln",
    )(c_pad, w_ada, b_ada)


def _alibi_slope_log2(head):
    return 2.0 ** (-8.0 * (head + 1) / DA_HEADS) * LOG2E


def _inproj_kernel(x_ref, mod_ref, w_ref, wg_ref, wgt_ref, gbr_ref, gbc_ref, qg_ref, kg_ref, cw_ref, cb_ref,
                   pm_ref, q_ref, k_ref, v_ref, gcol_ref, grow_ref, ext_ref, *, seq):
    tm = TM_INPROJ
    x = x_ref[...]
    h = x * lax.rsqrt(jnp.mean(x * x, axis=-1, keepdims=True) + EPS)
    h = h * (1.0 + mod_ref[0, 1:2, :]) + mod_ref[0, 0:1, :]
    hb = h.astype(bf16)
    cw = ML_W

    @pl.when((pl.program_id(0) * tm) % seq == 0)
    def _():
        ext_ref[0:SUBLANES, :] = jnp.zeros((SUBLANES, 2 * cw), f32)

    cur = _dot(hb, w_ref[:, 0:2 * cw])
    ext_ref[SUBLANES:SUBLANES + tm, :] = cur
    acc = cb_ref[...] + cw_ref[CONV_K - 1:CONV_K, :] * cur
    for j in range(CONV_K - 1):
        off = SUBLANES - (CONV_K - 1) + j
        acc = acc + cw_ref[j:j + 1, :] * ext_ref[off:off + tm, :]
    ext_ref[0:SUBLANES, :] = cur[tm - SUBLANES:tm, :]
    qk = _silu(acc)
    pm_ref[:, 0:cw] = qk[:, 0:cw].astype(bf16)
    pm_ref[:, cw:2 * cw] = (qk[:, cw:2 * cw] * (ML_DH ** -0.5)).astype(bf16)
    for n in range(2, 4):
        pm_ref[:, n * cw:(n + 1) * cw] = _dot(hb, w_ref[:, n * cw:(n + 1) * cw]).astype(bf16)

    lane = lax.broadcasted_iota(jnp.int32, (tm, LANES), 1)
    q_aug = jnp.where(jnp.logical_and(lane >= DA_DH, lane < DA_DH + 3), 1.0, 0.0)
    pos = (pl.program_id(0) * tm) % seq + lax.broadcasted_iota(jnp.int32, (tm, 1), 0)
    pos = pos.astype(f32)
    q0 = 4 * cw
    k0 = q0 + 2 * DA_W
    yq = _dot(hb, w_ref[:, q0:q0 + 2 * DA_W])
    yk = _dot(hb, w_ref[:, k0:k0 + 2 * DA_W])
    for grp in range(2 * DA_HEADS):
        sl = slice(grp * LANES, (grp + 1) * LANES)
        y = yq[:, sl]
        ss = jnp.sum(y * y, axis=-1, keepdims=True)
        q_ref[:, sl] = (y * lax.rsqrt(ss * (1.0 / DA_DH) + EPS) * qg_ref[:, sl] + q_aug).astype(bf16)

        y = yk[:, sl]
        ss = jnp.sum(y * y, axis=-1, keepdims=True)
        r = pos * _alibi_slope_log2(grp // 2)
        r_hi = r.astype(bf16).astype(f32)
        r_mid = (r - r_hi).astype(bf16).astype(f32)
        r_lo = r - r_hi - r_mid
        k_aug = jnp.where(lane == DA_DH, r_hi,
                          jnp.where(lane == DA_DH + 1, r_mid, jnp.where(lane == DA_DH + 2, r_lo, 0.0)))
        k_ref[:, sl] = (y * lax.rsqrt(ss * (1.0 / DA_DH) + EPS) * kg_ref[:, sl] + k_aug).astype(bf16)

    v0 = k0 + 2 * DA_W
    v_ref[...] = _dot(hb, w_ref[:, v0:v0 + DA_W]).astype(bf16)
    gcol_ref[...] = _dot(hb, wg_ref[...]) + gbr_ref[...]
    grow_ref[...] = _dot_nt(wgt_ref[...], hb) + gbc_ref[...]


def _inproj(x2, mod, w_main, wg, wgt, gb_row, gb_col, qgain, kgain, conv_w, conv_b, seq):
    t, d = x2.shape
    tm = TM_INPROJ
    tiles_per_seq = seq // tm
    n_main = w_main.shape[1]
    const = lambda shape: pl.BlockSpec(shape, lambda i: (0,) * len(shape))
    return pl.pallas_call(
        functools.partial(_inproj_kernel, seq=seq),
        out_shape=(jax.ShapeDtypeStruct((t, 4 * ML_W), bf16),
                   jax.ShapeDtypeStruct((t, 2 * DA_W), bf16),
                   jax.ShapeDtypeStruct((t, 2 * DA_W), bf16),
                   jax.ShapeDtypeStruct((t, DA_W), bf16),
                   jax.ShapeDtypeStruct((t, LANES), f32),
                   jax.ShapeDtypeStruct((N_GATES, t), f32)),
        grid=(t // tm,),
        in_specs=[pl.BlockSpec((tm, d), lambda i: (i, 0)),
                  pl.BlockSpec((1, 6, d), lambda i: (i // tiles_per_seq, 0, 0)),
                  const((d, n_main)), const((d, LANES)), const((N_GATES, d)),
                  const((1, LANES)), const((N_GATES, 1)),
                  const((1, 2 * DA_W)), const((1, 2 * DA_W)),
                  const((CONV_K, 2 * ML_W)), const((1, 2 * ML_W))],
        out_specs=(pl.BlockSpec((tm, 4 * ML_W), lambda i: (i, 0)),
                   pl.BlockSpec((tm, 2 * DA_W), lambda i: (i, 0)),
                   pl.BlockSpec((tm, 2 * DA_W), lambda i: (i, 0)),
                   pl.BlockSpec((tm, DA_W), lambda i: (i, 0)),
                   pl.BlockSpec((tm, LANES), lambda i: (i, 0)),
                   pl.BlockSpec((N_GATES, tm), lambda i: (0, i))),
        scratch_shapes=[pltpu.VMEM((SUBLANES + tm, 2 * ML_W), f32)],
        compiler_params=_cparams(1),
        name="inproj",
    )(x2, mod, w_main, wg, wgt, gb_row, gb_col, qgain, kgain, conv_w, conv_b)


def _mlstm_kernel(q_ref, k_ref, v_ref, o_ref, gcol_ref, grow_ref, ng_ref, out_ref, state_ref, m_ref):
    L = ML_CHUNK
    dh = ML_DH

    @pl.when(pl.program_id(1) == 0)
    def _():
        state_ref[...] = jnp.zeros_like(state_ref)
        m_ref[...] = jnp.zeros_like(m_ref)

    row_i = lax.broadcasted_iota(jnp.int32, (L, L), 0)
    col_i = lax.broadcasted_iota(jnp.int32, (L, L), 1)
    causal = row_i >= col_i
    tril = causal.astype(f32)
    triu = (row_i <= col_i).astype(f32)
    lane = lax.broadcasted_iota(jnp.int32, (L, dh), 1)
    ones_col = (lane == 0).astype(f32)
    g = ng_ref[...]

    for cc in range(ML_STEP_CHUNKS):
        rows = slice(cc * L, (cc + 1) * L)
        gcol = gcol_ref[rows, :]
        grow = grow_ref[:, rows]
        bcol_all = jnp.dot(tril, _log_sigmoid(gcol), precision=HIGHEST, preferred_element_type=f32)
        brow_all = jnp.dot(_log_sigmoid(grow), triu, precision=HIGHEST, preferred_element_type=f32)

        for hd in range(ML_HEADS):
            cols = slice(hd * dh, (hd + 1) * dh)
            qb = q_ref[rows, cols]
            kb = k_ref[rows, cols]
            v = v_ref[rows, cols].astype(f32)
            v_aug = jnp.concatenate([v, ones_col], axis=1)

            bcol = bcol_all[:, ML_HEADS + hd:ML_HEADS + hd + 1]
            icol = gcol[:, hd:hd + 1]
            brow = brow_all[ML_HEADS + hd:ML_HEADS + hd + 1, :]
            irow = grow[hd:hd + 1, :]
            m_prev = m_ref[hd:hd + 1, 0:1]

            d_intra = jnp.where(causal, bcol - brow + irow, NEG_BIG)
            d_inter = bcol + m_prev
            m_t = jnp.maximum(d_inter, jnp.max(d_intra, axis=1, keepdims=True))
            w_intra = jnp.exp(d_intra - m_t)
            w_inter = jnp.exp(d_inter - m_t)

            s = _dot_nt(qb, kb) * w_intra
            state = state_ref[hd]
            q_state = _dot(qb, state.astype(bf16))
            s_v = _dot(s.astype(bf16), v_aug.astype(bf16))
            num = w_inter * q_state[:, 0:dh] + s_v[:, 0:dh]
            den = w_inter * q_state[:, dh:dh + 1] + s_v[:, dh:dh + 1]
            hval = num / jnp.maximum(jnp.abs(den), jnp.exp(-m_t))

            b_last = bcol[L - 1:L, :]
            d_state = b_last - bcol + icol
            m_new = jnp.maximum(b_last + m_prev, jnp.max(d_state, axis=0, keepdims=True))
            carry_scale = jnp.exp(b_last + m_prev - m_new)
            wk = jnp.exp(d_state - m_new)
            k_t = kb.astype(f32).T.astype(bf16)
            upd = _dot(k_t, (v_aug * wk).astype(bf16))
            state_ref[hd] = carry_scale * state + upd
            m_ref[hd:hd + 1, :] = jnp.broadcast_to(m_new, (1, LANES))

            hn = hval * lax.rsqrt(jnp.mean(hval * hval, axis=-1, keepdims=True) + EPS) * g
            og = _sigmoid(o_ref[rows, cols].astype(f32))
            out_ref[rows, cols] = (hn * og).astype(bf16)


def _mlstm(pm, gcol, grow, ml_norm_g, batch, seq):
    t = pm.shape[0]
    rows = ML_STEP_CHUNKS * ML_CHUNK
    ns = seq // rows
    row = lambda b, c: b * ns + c
    return pl.pallas_call(
        _mlstm_kernel,
        out_shape=jax.ShapeDtypeStruct((t, ML_W), bf16),
        grid=(batch, ns),
        in_specs=[pl.BlockSpec((rows, ML_W), lambda b, c: (row(b, c), 0)),
                  pl.BlockSpec((rows, ML_W), lambda b, c: (row(b, c), 1)),
                  pl.BlockSpec((rows, ML_W), lambda b, c: (row(b, c), 2)),
                  pl.BlockSpec((rows, ML_W), lambda b, c: (row(b, c), 3)),
                  pl.BlockSpec((rows, LANES), lambda b, c: (row(b, c), 0)),
                  pl.BlockSpec((N_GATES, rows), lambda b, c: (0, row(b, c))),
                  pl.BlockSpec((1, ML_DH), lambda b, c: (0, 0))],
        out_specs=pl.BlockSpec((rows, ML_W), lambda b, c: (row(b, c), 0)),
        scratch_shapes=[pltpu.VMEM((ML_HEADS, ML_DH, 2 * ML_DH), f32),
                        pltpu.VMEM((SUBLANES, LANES), f32)],
        compiler_params=_cparams(2),
        name="mlstm",
    )(pm, pm, pm, pm, gcol, grow, ml_norm_g)


def _attn_block(q_ref, k_ref, v_ref, m_ref, l_ref, acc_ref, row0, nrows, nkeys, diagonal):
    rows = slice(row0, row0 + nrows)
    nch = nkeys // LANES
    v = v_ref[0:nkeys, :]
    if diagonal:
        keep = (lax.broadcasted_iota(jnp.int32, (nrows, nkeys), 1)
                <= row0 + lax.broadcasted_iota(jnp.int32, (nrows, nkeys), 0))
    for c in range(2):
        sl = slice(c * LANES, (c + 1) * LANES)
        s = _dot_nt(q_ref[rows, sl], k_ref[0:nkeys, sl])
        if diagonal:
            s = jnp.where(keep, s, NEG_BIG)
        chunks = [s[:, j * LANES:(j + 1) * LANES] for j in range(nch)]
        mc = chunks[0]
        for ch in chunks[1:]:
            mc = jnp.maximum(mc, ch)
        m_old = m_ref[c, rows, :]
        m_new = jnp.maximum(m_old, jnp.max(mc, axis=1, keepdims=True))
        alpha = jnp.exp2(m_old - m_new)
        ps = [jnp.exp2(ch - m_new) for ch in chunks]
        lsum = ps[0]
        for pj in ps[1:]:
            lsum = lsum + pj
        p = jnp.concatenate([pj.astype(bf16) for pj in ps], axis=1)
        l_ref[c, rows, :] = alpha * l_ref[c, rows, :] + lsum
        acc_ref[c, rows, :] = alpha * acc_ref[c, rows, :] + _dot(p, v)
        m_ref[c, rows, :] = m_new


def _attn_kernel(qt_ref, kt_ref, q_ref, k_ref, v_ref, lam_ref, g_ref, o_ref,
                 m_ref, l_ref, acc_ref, *, lam_init):
    step = pl.program_id(2)
    qi = qt_ref[step]
    ki = kt_ref[step]

    @pl.when(ki == 0)
    def _():
        m_ref[...] = jnp.full_like(m_ref, NEG_BIG)
        l_ref[...] = jnp.zeros_like(l_ref)
        acc_ref[...] = jnp.zeros_like(acc_ref)

    blk = ATT_BLOCK
    half = blk // 2

    @pl.when(ki < qi)
    def _():
        _attn_block(q_ref, k_ref, v_ref, m_ref, l_ref, acc_ref, 0, blk, blk, diagonal=False)

    @pl.when(ki == qi)
    def _():
        _attn_block(q_ref, k_ref, v_ref, m_ref, l_ref, acc_ref, 0, half, half, diagonal=True)
        _attn_block(q_ref, k_ref, v_ref, m_ref, l_ref, acc_ref, half, half, blk, diagonal=True)
        lam = lam_ref[...]
        l0 = jnp.sum(l_ref[0], axis=1, keepdims=True)
        l1 = jnp.sum(l_ref[1], axis=1, keepdims=True)
        o = acc_ref[0] / l0 - lam * (acc_ref[1] / l1)
        o = o * lax.rsqrt(jnp.mean(o * o, axis=-1, keepdims=True) + EPS)
        o_ref[...] = (o * g_ref[...] * (1.0 - lam_init)).astype(bf16)


def _attention(qh, kh, vh, lam, da_norm_g, batch, seq, lam_init):
    t = qh.shape[0]
    blk = ATT_BLOCK
    nb = seq // blk
    pairs = [(i, j) for i in range(nb) for j in range(i + 1)]
    qt = jnp.asarray([p[0] for p in pairs], jnp.int32)
    kt = jnp.asarray([p[1] for p in pairs], jnp.int32)
    w = 2 * DA_DH
    grid_spec = pltpu.PrefetchScalarGridSpec(
        num_scalar_prefetch=2,
        grid=(batch, DA_HEADS, len(pairs)),
        in_specs=[pl.BlockSpec((blk, 2 * LANES), lambda b, h, s, qt, kt: (b * nb + qt[s], h)),
                  pl.BlockSpec((blk, 2 * LANES), lambda b, h, s, qt, kt: (b * nb + kt[s], h)),
                  pl.BlockSpec((blk, w), lambda b, h, s, qt, kt: (b * nb + kt[s], h)),
                  pl.BlockSpec((1, 1), lambda b, h, s, qt, kt: (0, 0)),
                  pl.BlockSpec((1, w), lambda b, h, s, qt, kt: (0, 0))],
        out_specs=pl.BlockSpec((blk, w), lambda b, h, s, qt, kt: (b * nb + qt[s], h)),
        scratch_shapes=[pltpu.VMEM((2, blk, LANES), f32),
                        pltpu.VMEM((2, blk, LANES), f32),
                        pltpu.VMEM((2, blk, w), f32)],
    )
    return pl.pallas_call(
        functools.partial(_attn_kernel, lam_init=lam_init),
        out_shape=jax.ShapeDtypeStruct((t, DA_W), bf16),
        grid_spec=grid_spec,
        compiler_params=_cparams(3),
        name="attn",
    )(qt, kt, qh, kh, vh, lam, da_norm_g)


def _first_index_of_max(x, iota_f, size):
    m = jnp.max(x, axis=0, keepdims=True)
    idx = jnp.min(jnp.where(x == m, iota_f, float(size)), axis=0, keepdims=True)
    return m, idx


def _route_kernel(hm_ref, hd_ref, x_ref, mod_ref, wo_ref, wr_ref, rb_ref,
                  x1_ref, ha_ref, hb_ref, idx_ref, gate_ref, rank_ref, cnt_ref, base_ref):
    tm = TM_ROUTE
    i = pl.program_id(0)

    @pl.when(i == 0)
    def _():
        base_ref[...] = jnp.zeros_like(base_ref)

    mix = _dot(hm_ref[...], wo_ref[0:ML_W, :]) + _dot(hd_ref[...], wo_ref[ML_W:ML_W + DA_W, :])
    x1 = x_ref[...] + mod_ref[0, 2:3, :] * mix
    x1_ref[...] = x1
    h2 = x1 * lax.rsqrt(jnp.mean(x1 * x1, axis=-1, keepdims=True) + EPS)
    h2 = h2 * (1.0 + mod_ref[0, 4:5, :]) + mod_ref[0, 3:4, :]
    packed = lax.bitcast_convert_type(_pack_bf16_pair(h2[:, 0:HALF], h2[:, HALF:D_MODEL]), jnp.int32)
    ha_ref[...] = packed[:, 0:QUARTER]
    hb_ref[...] = packed[:, QUARTER:HALF]

    sc = _sigmoid(_dot_nt(wr_ref[...], h2.astype(bf16)))
    sel = sc + rb_ref[...]

    gi = lax.broadcasted_iota(jnp.int32, (GROUP_SIZE, tm), 0).astype(f32)
    gscores = []
    for g in range(N_GROUPS):
        blk = sel[g * GROUP_SIZE:(g + 1) * GROUP_SIZE, :]
        m1, i1 = _first_index_of_max(blk, gi, GROUP_SIZE)
        m2 = jnp.max(jnp.where(gi == i1, NEG_BIG, blk), axis=0, keepdims=True)
        gscores.append(m1 + m2)
    gs = jnp.concatenate(gscores, axis=0)

    ngi = lax.broadcasted_iota(jnp.int32, (N_GROUPS, tm), 0).astype(f32)
    gkeep = jnp.zeros((N_GROUPS, tm), f32)
    for _ in range(TOPK_GROUPS):
        _, gidx = _first_index_of_max(gs, ngi, N_GROUPS)
        hit = ngi == gidx
        gkeep = jnp.where(hit, 1.0, gkeep)
        gs = jnp.where(hit, NEG_BIG, gs)
    masked = jnp.concatenate(
        [jnp.where(gkeep[g:g + 1, :] > 0.0, sel[g * GROUP_SIZE:(g + 1) * GROUP_SIZE, :], NEG_BIG)
         for g in range(N_GROUPS)], axis=0)

    ei = lax.broadcasted_iota(jnp.int32, (N_EXPERTS, tm), 0).astype(f32)
    idxs, gates = [], []
    chosen = jnp.zeros((N_EXPERTS, tm), f32)
    for _ in range(TOP_K):
        _, eidx = _first_index_of_max(masked, ei, N_EXPERTS)
        hit = ei == eidx
        idxs.append(eidx)
        gates.append(jnp.sum(jnp.where(hit, sc, 0.0), axis=0, keepdims=True))
        chosen = jnp.where(hit, 1.0, chosen)
        masked = jnp.where(hit, NEG_BIG, masked)
    gate = jnp.concatenate(gates, axis=0)
    gate = gate / jnp.sum(gate, axis=0, keepdims=True) * ROUTED_SCALE
    gate_ref[...] = gate
    idx_ref[...] = jnp.concatenate(idxs, axis=0).astype(jnp.int32)

    tr = lax.broadcasted_iota(jnp.int32, (tm, tm), 0)
    tc = lax.broadcasted_iota(jnp.int32, (tm, tm), 1)
    before = (tr < tc).astype(bf16)
    seen = base_ref[...] + _dot(chosen.astype(bf16), before)
    ranks = [jnp.sum(jnp.where(ei == eidx, seen, 0.0), axis=0, keepdims=True) for eidx in idxs]
    rank_ref[...] = jnp.concatenate(ranks, axis=0).astype(jnp.int32)
    total = base_ref[...] + jnp.sum(chosen, axis=1, keepdims=True)
    base_ref[...] = total
    cnt_ref[...] = total


def _route(hm, hd, x2, mod, w_out, w_router_t, rbias_col, seq):
    t, d = x2.shape
    tm = TM_ROUTE
    tiles_per_seq = seq // tm
    const = lambda shape: pl.BlockSpec(shape, lambda i: (0,) * len(shape))
    return pl.pallas_call(
        _route_kernel,
        out_shape=(jax.ShapeDtypeStruct((t, d), f32),
                   jax.ShapeDtypeStruct((t, QUARTER), jnp.int32),
                   jax.ShapeDtypeStruct((t, QUARTER), jnp.int32),
                   jax.ShapeDtypeStruct((TOP_K, t), jnp.int32),
                   jax.ShapeDtypeStruct((TOP_K, t), f32),
                   jax.ShapeDtypeStruct((TOP_K, t), jnp.int32),
                   jax.ShapeDtypeStruct((N_EXPERTS, 1), f32)),
        grid=(t // tm,),
        in_specs=[pl.BlockSpec((tm, ML_W), lambda i: (i, 0)),
                  pl.BlockSpec((tm, DA_W), lambda i: (i, 0)),
                  pl.BlockSpec((tm, d), lambda i: (i, 0)),
                  pl.BlockSpec((1, 6, d), lambda i: (i // tiles_per_seq, 0, 0)),
                  const((d, d)), const((N_EXPERTS, d)), const((N_EXPERTS, 1))],
        out_specs=(pl.BlockSpec((tm, d), lambda i: (i, 0)),
                   pl.BlockSpec((tm, QUARTER), lambda i: (i, 0)),
                   pl.BlockSpec((tm, QUARTER), lambda i: (i, 0)),
                   pl.BlockSpec((TOP_K, tm), lambda i: (0, i)),
                   pl.BlockSpec((TOP_K, tm), lambda i: (0, i)),
                   pl.BlockSpec((TOP_K, tm), lambda i: (0, i)),
                   const((N_EXPERTS, 1))),
        scratch_shapes=[pltpu.VMEM((N_EXPERTS, 1), f32)],
        compiler_params=_cparams(1),
        name="route",
    )(hm, hd, x2, mod, w_out, w_router_t, rbias_col)


SLOT_RADIX = 256
SLOT_DIGITS = 3


def _slots_kernel(idx_ref, rank_ref, dig_ref, dest_ref):
    tm = TM_SLOTS
    ei = lax.broadcasted_iota(jnp.int32, (N_EXPERTS, tm), 0)
    rows = []
    for k in range(TOP_K):
        onehot = jnp.where(ei == idx_ref[k:k + 1, :], 1.0, 0.0).astype(bf16)
        dg = _dot(dig_ref[...], onehot)
        start = dg[0:1, :]
        for j in range(1, SLOT_DIGITS):
            start = start + dg[j:j + 1, :] * float(SLOT_RADIX ** j)
        rows.append(start.astype(jnp.int32) + rank_ref[k:k + 1, :])
    dest_ref[...] = jnp.concatenate(rows, axis=0)


def _slots(idx, rank, digits):
    t = idx.shape[1]
    tm = TM_SLOTS
    return pl.pallas_call(
        _slots_kernel,
        out_shape=jax.ShapeDtypeStruct((TOP_K, t), jnp.int32),
        grid=(t // tm,),
        in_specs=[pl.BlockSpec((TOP_K, tm), lambda i: (0, i)),
                  pl.BlockSpec((TOP_K, tm), lambda i: (0, i)),
                  pl.BlockSpec((SUBLANES, N_EXPERTS), lambda i: (0, 0))],
        out_specs=pl.BlockSpec((TOP_K, tm), lambda i: (0, i)),
        compiler_params=_cparams(1),
        name="slots",
    )(idx, rank, digits)


def _sc_scatter_table(rows, dest, n_slots):
    n, width = rows.shape
    mesh = plsc.VectorSubcoreMesh(core_axis_name="core", subcore_axis_name="subcore")

    @pl.kernel(out_type=jax.ShapeDtypeStruct((n_slots, width), rows.dtype), mesh=mesh, scratch_types=[],
               name="sc_scatter")
    def scatter_kernel(rows_hbm, dest_hbm, out_hbm):
        def body(rows_vmem, dest_vmem):
            for k in range(TOP_K):
                pltpu.sync_copy(rows_vmem, out_hbm.at[dest_vmem.at[k]])

        pltpu.emit_pipeline(
            body,
            grid=(n // SC_WINDOW,),
            in_specs=[pl.BlockSpec((SC_WINDOW, width), lambda i: (i, 0)),
                      pl.BlockSpec((TOP_K, SC_WINDOW), lambda i: (0, i))],
            out_specs=[],
            core_axis_name=("core", "subcore"),
            dimension_semantics=(pltpu.PARALLEL,),
        )(rows_hbm, dest_hbm)

    return scatter_kernel(rows, dest)


EXPERT_GROUP = 4
EXPERT_RING = 16
EXPERT_AHEAD = EXPERT_RING - EXPERT_GROUP


def _expert_kernel(start_ref, cnt_ref, total_ref, xsa_ref, xsb_ref, w1_ref, w3_ref, w2_ref,
                   ysa_ref, ysb_ref, xbuf, ybuf, w1b, w3b, w2b, in_sem, out_sem):
    bm = MOE_BLOCK
    ring = EXPERT_RING
    ahead = EXPERT_AHEAD
    shift = int(math.log2(bm))
    e = pl.program_id(0)
    total = total_ref[0]
    cnt = cnt_ref[e]
    nb = lax.shift_right_logical(cnt + (bm - 1), shift)
    g0 = lax.shift_right_logical(start_ref[e], shift)

    def slot_of(g):
        return jnp.bitwise_and(g, ring - 1)

    def in_copies(g):
        s = slot_of(g)
        rows = pl.ds(pl.multiple_of(g * bm, bm), bm)
        return [pltpu.make_async_copy(x_ref.at[rows], xbuf.at[s, half], in_sem.at[s])
                for half, x_ref in enumerate((xsa_ref, xsb_ref))]

    def out_copies(g):
        s = slot_of(g)
        rows = pl.ds(pl.multiple_of(g * bm, bm), bm)
        return [pltpu.make_async_copy(ybuf.at[s, half], y_ref.at[rows], out_sem.at[s])
                for half, y_ref in enumerate((ysa_ref, ysb_ref))]

    def request(g):
        @pl.when(g < total)
        def _():
            for cp in in_copies(g):
                cp.start()

    def acquire(g):
        for cp in in_copies(g):
            cp.wait()

        @pl.when(g >= ring)
        def _():
            for cp in out_copies(g - ring):
                cp.wait()

    def compute(g):
        s = slot_of(g)
        row = (g - g0) * bm + lax.broadcasted_iota(jnp.int32, (bm, HALF), 0)
        words = jnp.concatenate([xbuf[s, 0], xbuf[s, 1]], axis=1)
        lo, hi = _unpack_bf16_pair(lax.bitcast_convert_type(jnp.where(row < cnt, words, 0), jnp.uint32))
        lo = lo.astype(bf16)
        hi = hi.astype(bf16)
        h1 = _dot(lo, w1b[0:HALF, :]) + _dot(hi, w1b[HALF:D_MODEL, :])
        h3 = _dot(lo, w3b[0:HALF, :]) + _dot(hi, w3b[HALF:D_MODEL, :])
        y = _dot((_silu(h1) * h3).astype(bf16), w2b[...])
        packed = lax.bitcast_convert_type(_pack_bf16_pair(y[:, 0:HALF], y[:, HALF:D_MODEL]), jnp.int32)
        ybuf[s, 0] = packed[:, 0:QUARTER]
        ybuf[s, 1] = packed[:, QUARTER:HALF]

    def release(g):
        for cp in out_copies(g):
            cp.start()

    @pl.when(e == 0)
    def _():
        for g in range(ahead):
            request(g)

    @pl.when(nb > 0)
    def _():
        w1b[...] = w1_ref[0].astype(bf16)
        w3b[...] = w3_ref[0].astype(bf16)
        w2b[...] = w2_ref[0].astype(bf16)

    def run(g, n):
        for d in range(n):
            request(g + d + ahead)
        for d in range(n):
            acquire(g + d)
        for d in range(n):
            compute(g + d)
        for d in range(n):
            release(g + d)

    def group(i, carry):
        run(g0 + EXPERT_GROUP * i, EXPERT_GROUP)
        return carry

    lax.fori_loop(0, lax.shift_right_logical(nb, int(math.log2(EXPERT_GROUP))), group, 0)

    n = EXPERT_GROUP // 2
    while n >= 1:
        @pl.when(jnp.bitwise_and(nb, n) != 0)
        def _(n=n):
            done = jnp.bitwise_and(nb, -2 * n)
            run(g0 + done, n)
        n //= 2

    @pl.when(e == pl.num_programs(0) - 1)
    def _():
        for back in range(1, ring + 1):
            @pl.when(total - back >= 0)
            def _():
                for cp in out_copies(total - back):
                    cp.wait()


def _experts(start, counts, total, xs_a, xs_b, w1, w3, w2):
    n_pad = xs_a.shape[0]
    bm = MOE_BLOCK
    grid_spec = pltpu.PrefetchScalarGridSpec(
        num_scalar_prefetch=3,
        grid=(N_EXPERTS,),
        in_specs=[pl.BlockSpec(memory_space=pl.ANY),
                  pl.BlockSpec(memory_space=pl.ANY),
                  pl.BlockSpec((1, D_MODEL, D_EXPERT), lambda e, st, ct, tt: (e, 0, 0)),
                  pl.BlockSpec((1, D_MODEL, D_EXPERT), lambda e, st, ct, tt: (e, 0, 0)),
                  pl.BlockSpec((1, D_EXPERT, D_MODEL), lambda e, st, ct, tt: (e, 0, 0))],
        out_specs=(pl.BlockSpec(memory_space=pl.ANY), pl.BlockSpec(memory_space=pl.ANY)),
        scratch_shapes=[pltpu.VMEM((EXPERT_RING, 2, bm, QUARTER), jnp.int32),
                        pltpu.VMEM((EXPERT_RING, 2, bm, QUARTER), jnp.int32),
                        pltpu.VMEM((D_MODEL, D_EXPERT), bf16),
                        pltpu.VMEM((D_MODEL, D_EXPERT), bf16),
                        pltpu.VMEM((D_EXPERT, D_MODEL), bf16),
                        pltpu.SemaphoreType.DMA((EXPERT_RING,)),
                        pltpu.SemaphoreType.DMA((EXPERT_RING,))],
    )
    return pl.pallas_call(
        _expert_kernel,
        out_shape=(jax.ShapeDtypeStruct((n_pad, QUARTER), jnp.int32),
                   jax.ShapeDtypeStruct((n_pad, QUARTER), jnp.int32)),
        grid_spec=grid_spec,
        compiler_params=_cparams(1),
        name="experts",
    )(start, counts, total, xs_a, xs_b, w1, w3, w2)


def _sc_gather_table(table, idx2):
    n = idx2.shape[1]
    width = table.shape[1]
    mesh = plsc.VectorSubcoreMesh(core_axis_name="core", subcore_axis_name="subcore")

    @pl.kernel(out_type=jax.ShapeDtypeStruct((n, width), table.dtype), mesh=mesh, scratch_types=[],
               name="sc_gather")
    def gather_kernel(tab_hbm, idx_hbm, out_hbm):
        def body(idx_vmem, out_vmem):
            pltpu.sync_copy(tab_hbm.at[idx_vmem.at[0]], out_vmem)

        pltpu.emit_pipeline(
            body,
            grid=(n // SC_WINDOW,),
            in_specs=[pl.BlockSpec((1, SC_WINDOW), lambda i: (0, i))],
            out_specs=[pl.BlockSpec((SC_WINDOW, width), lambda i: (i, 0))],
            core_axis_name=("core", "subcore"),
            dimension_semantics=(pltpu.PARALLEL,),
        )(idx_hbm, out_hbm)

    return gather_kernel(table, idx2)


def _sc_gather_rows(table_a, table_b, idx):
    idx2 = idx.reshape(1, -1)
    return _sc_gather_table(table_a, idx2), _sc_gather_table(table_b, idx2)


def _combine_kernel(gate_ref, ha_ref, hb_ref, x1_ref, mod_ref, ws1_ref, ws3_ref, ws2_ref, ga_ref, gb_ref,
                    out_ref):
    words = jnp.concatenate([ha_ref[...], hb_ref[...]], axis=1)
    lo, hi = _unpack_bf16_pair(lax.bitcast_convert_type(words, jnp.uint32))
    lo = lo.astype(bf16)
    hi = hi.astype(bf16)
    s1 = _dot(lo, ws1_ref[0:HALF, :]) + _dot(hi, ws1_ref[HALF:D_MODEL, :])
    s3 = _dot(lo, ws3_ref[0:HALF, :]) + _dot(hi, ws3_ref[HALF:D_MODEL, :])
    y = _dot((_silu(s1) * s3).astype(bf16), ws2_ref[...])

    gate = gate_ref[...]
    gate_f = mod_ref[0, 5:6, :]
    for part, g_ref in enumerate((ga_ref, gb_ref)):
        c_lo = slice(part * QUARTER, (part + 1) * QUARTER)
        c_hi = slice(HALF + part * QUARTER, HALF + (part + 1) * QUARTER)
        acc_lo = y[:, c_lo]
        acc_hi = y[:, c_hi]
        for k in range(TOP_K):
            rlo, rhi = _unpack_bf16_pair(lax.bitcast_convert_type(g_ref[k], jnp.uint32))
            gk = gate[:, k:k + 1]
            acc_lo = acc_lo + gk * rlo
            acc_hi = acc_hi + gk * rhi
        out_ref[:, c_lo] = x1_ref[:, c_lo] + gate_f[:, c_lo] * acc_lo
        out_ref[:, c_hi] = x1_ref[:, c_hi] + gate_f[:, c_hi] * acc_hi


def _combine(gate_col, h2a, h2b, x1, mod, ws1, ws3, ws2, ga, gb, seq):
    t, d = x1.shape
    tm = TM_MOE
    tiles_per_seq = seq // tm
    const = lambda shape: pl.BlockSpec(shape, lambda i: (0,) * len(shape))
    return pl.pallas_call(
        _combine_kernel,
        out_shape=jax.ShapeDtypeStruct((t, d), f32),
        grid=(t // tm,),
        in_specs=[pl.BlockSpec((tm, TOP_K), lambda i: (i, 0)),
                  pl.BlockSpec((tm, QUARTER), lambda i: (i, 0)),
                  pl.BlockSpec((tm, QUARTER), lambda i: (i, 0)),
                  pl.BlockSpec((tm, d), lambda i: (i, 0)),
                  pl.BlockSpec((1, 6, d), lambda i: (i // tiles_per_seq, 0, 0)),
                  const((d, D_EXPERT)), const((d, D_EXPERT)), const((D_EXPERT, d)),
                  pl.BlockSpec((TOP_K, tm, QUARTER), lambda i: (0, i, 0)),
                  pl.BlockSpec((TOP_K, tm, QUARTER), lambda i: (0, i, 0))],
        out_specs=pl.BlockSpec((tm, d), lambda i: (i, 0)),
        compiler_params=_cparams(1),
        name="combine",
    )(gate_col, h2a, h2b, x1, mod, ws1, ws3, ws2, ga, gb)


def _lambda_init(layer):
    return 0.8 - 0.6 * math.exp(-0.3 * layer)


def _layer(x, c, w_ada, b_ada, w_in, conv_w, conv_b, gate_b, ml_norm_g, da_q_norm_g, da_k_norm_g,
           lambda_q1, lambda_k1, lambda_q2, lambda_k2, da_norm_g, w_out, w_router, router_bias,
           w1, w3, w2, ws1, ws3, ws2, layer):
    batch, seq, d = x.shape
    t = batch * seq
    lam_init = _lambda_init(layer)
    x2 = x.reshape(t, d)

    c_pad = jnp.pad(c, ((0, -batch % SUBLANES), (0, 0)))
    mod = _adaln(c_pad, w_ada, b_ada.reshape(1, -1))[:batch].reshape(batch, 6, d)

    g0 = 4 * ML_W
    q0 = g0 + N_GATES
    n_grp = 2 * DA_HEADS

    def lane_groups(cols):
        r = cols.shape[0]
        return jnp.pad(cols.reshape(r, n_grp, DA_DH), ((0, 0), (0, 0), (0, LANES - DA_DH))).reshape(r, -1)

    w_main = jnp.concatenate([w_in[:, :g0],
                              lane_groups(w_in[:, q0:q0 + DA_W]),
                              lane_groups(w_in[:, q0 + DA_W:q0 + 2 * DA_W]),
                              w_in[:, q0 + 2 * DA_W:]], axis=1).astype(bf16)
    w_gate = w_in[:, g0:g0 + N_GATES]
    wg = jnp.pad(w_gate, ((0, 0), (0, LANES - N_GATES))).astype(bf16)
    wgt = w_gate.T.astype(bf16)
    gb_row = jnp.pad(gate_b, (0, LANES - N_GATES)).reshape(1, LANES)
    gb_col = gate_b.reshape(N_GATES, 1)
    qgain = lane_groups((jnp.tile(da_q_norm_g, n_grp) * (DA_DH ** -0.5 * LOG2E)).reshape(1, DA_W))
    kgain = lane_groups(jnp.tile(da_k_norm_g, n_grp).reshape(1, DA_W))

    pm, qh, kh, vh, gcol, grow = _inproj(x2, mod, w_main, wg, wgt, gb_row, gb_col, qgain, kgain,
                                         conv_w, conv_b.reshape(1, -1), seq)

    hm = _mlstm(pm, gcol, grow, ml_norm_g.reshape(1, -1), batch, seq)

    lam = (jnp.exp(jnp.sum(lambda_q1 * lambda_k1)) - jnp.exp(jnp.sum(lambda_q2 * lambda_k2))
           + lam_init).reshape(1, 1).astype(f32)
    hd = _attention(qh, kh, vh, lam, da_norm_g.reshape(1, -1), batch, seq, lam_init)

    x1, h2a, h2b, idx, gate, rank, counts = _route(hm, hd, x2, mod, w_out.astype(bf16),
                                                   w_router.T.astype(bf16), router_bias.reshape(-1, 1), seq)

    bm = MOE_BLOCK
    n_blocks = (t * TOP_K) // bm + N_EXPERTS
    counts_i = counts.reshape(-1).astype(jnp.int32)
    padded = (counts_i + bm - 1) // bm * bm
    pad_end = jnp.cumsum(padded)
    pad_start = pad_end - padded
    digits = jnp.stack([(pad_start // SLOT_RADIX ** j) % SLOT_RADIX for j in range(SLOT_DIGITS)])
    digits = jnp.pad(digits, ((0, SUBLANES - SLOT_DIGITS), (0, 0))).astype(bf16)
    dest = _slots(idx, rank, digits)

    xs_a = _sc_scatter_table(h2a, dest, n_blocks * bm)
    xs_b = _sc_scatter_table(h2b, dest, n_blocks * bm)
    total_blocks = (pad_end[-1:] // bm).astype(jnp.int32)
    ys_a, ys_b = _experts(pad_start.astype(jnp.int32), counts_i, total_blocks, xs_a, xs_b, w1, w3, w2)
    ga, gb = _sc_gather_rows(ys_a, ys_b, dest.reshape(-1))
    out = _combine(gate.T, h2a, h2b, x1, mod, ws1.astype(bf16), ws3.astype(bf16), ws2.astype(bf16),
                   ga.reshape(TOP_K, t, QUARTER), gb.reshape(TOP_K, t, QUARTER), seq)
    return out.reshape(batch, seq, d)


def kernel(x, c, w_ada, b_ada, w_in, conv_w, conv_b, gate_b, ml_norm_g, da_q_norm_g, da_k_norm_g,
           lambda_q1, lambda_k1, lambda_q2, lambda_k2, da_norm_g, w_out, w_router, router_bias,
           w1, w3, w2, ws1, ws3, ws2):
    depth = w_ada.shape[0]
    for l in range(depth):
        x = _layer(x, c, w_ada[l], b_ada[l], w_in[l], conv_w[l], conv_b[l], gate_b[l], ml_norm_g[l],
                   da_q_norm_g[l], da_k_norm_g[l], lambda_q1[l], lambda_k1[l], lambda_q2[l], lambda_k2[l],
                   da_norm_g[l], w_out[l], w_router[l], router_bias[l], w1[l], w3[l], w2[l],
                   ws1[l], ws3[l], ws2[l], l)
    return x
```

```python
import functools
import math

import jax
import jax.numpy as jnp
from jax import lax
from jax.experimental import pallas as pl
from jax.experimental.pallas import tpu as pltpu
from jax.experimental.pallas import tpu_sc as plsc

D_MODEL = 1024
ML_HEADS = 4
ML_DH = 128
ML_W = ML_HEADS * ML_DH
ML_CHUNK = 128
CONV_K = 4
DA_HEADS = 4
DA_DH = 64
DA_W = DA_HEADS * 2 * DA_DH
N_EXPERTS = 256
TOP_K = 8
N_GROUPS = 8
GROUP_SIZE = N_EXPERTS // N_GROUPS
TOPK_GROUPS = 4
D_EXPERT = 256
ROUTED_SCALE = 2.5
EPS = 1e-6
N_GATES = 2 * ML_HEADS

LANES = 128
SUBLANES = 8
VMEM_LIMIT_BYTES = 48 * 1024 * 1024

TM_INPROJ = 512
ML_STEP_CHUNKS = 4
ATT_BLOCK = 1024
ATT_QBLOCKS = 2
TM_ROUTE = 256
TM_SLOTS = 512
TM_MOE = 512
MOE_BLOCK = 256
HALF = D_MODEL // 2
QUARTER = HALF // 2
SC_WINDOW = 128

NEG_BIG = -1e30
LOG2E = 1.4426950408889634

f32 = jnp.float32
bf16 = jnp.bfloat16
HIGHEST = lax.Precision.HIGHEST


def _cparams(n_axes):
    return pltpu.CompilerParams(dimension_semantics=("arbitrary",) * n_axes,
                                vmem_limit_bytes=VMEM_LIMIT_BYTES)


def _dot(a, b):
    return jnp.dot(a, b, preferred_element_type=f32)


def _dot_nt(a, b):
    return lax.dot_general(a, b, (((1,), (1,)), ((), ())), preferred_element_type=f32)


def _sigmoid(x):
    return 1.0 / (1.0 + jnp.exp(-x))


def _silu(x):
    return x * _sigmoid(x)


def _log_sigmoid(x):
    return jnp.minimum(x, 0.0) - jnp.log(1.0 + jnp.exp(-jnp.abs(x)))


def _pack_bf16_pair(lo, hi):
    lo_bits = lax.bitcast_convert_type(lo.astype(bf16).astype(f32), jnp.uint32)
    hi_bits = lax.bitcast_convert_type(hi.astype(bf16).astype(f32), jnp.uint32)
    return (hi_bits & jnp.uint32(0xFFFF0000)) | (lo_bits >> 16)


def _unpack_bf16_pair(w):
    lo = lax.bitcast_convert_type(w << 16, f32)
    hi = lax.bitcast_convert_type(w & jnp.uint32(0xFFFF0000), f32)
    return lo, hi


def _adaln_kernel(c_ref, w_ref, b_ref, o_ref):
    c = c_ref[...]
    o_ref[...] = jnp.dot(_silu(c), w_ref[...], precision=HIGHEST, preferred_element_type=f32) + b_ref[...]


def _adaln(c_pad, w_ada, b_ada):
    rows, d = c_pad.shape
    n = w_ada.shape[1]
    tn = 1536
    return pl.pallas_call(
        _adaln_kernel,
        out_shape=jax.ShapeDtypeStruct((rows, n), f32),
        grid=(n // tn,),
        in_specs=[pl.BlockSpec((rows, d), lambda j: (0, 0)),
                  pl.BlockSpec((d, tn), lambda j: (0, j)),
                  pl.BlockSpec((1, tn), lambda j: (0, j))],
        out_specs=pl.BlockSpec((rows, tn), lambda j: (0, j)),
        compiler_params=_cparams(1),
        name="adaln",
    )(c_pad, w_ada, b_ada)


def _alibi_slope_log2(head):
    return 2.0 ** (-8.0 * (head + 1) / DA_HEADS) * LOG2E


def _inproj_kernel(x_ref, mod_ref, w_ref, wg_ref, wgt_ref, gbr_ref, gbc_ref, qg_ref, kg_ref, cw_ref, cb_ref,
                   pm_ref, q_ref, k_ref, v_ref, gcol_ref, grow_ref, ext_ref, *, seq):
    tm = TM_INPROJ
    x = x_ref[...]
    h = x * lax.rsqrt(jnp.mean(x * x, axis=-1, keepdims=True) + EPS)
    h = h * (1.0 + mod_ref[0, 1:2, :]) + mod_ref[0, 0:1, :]
    hb = h.astype(bf16)
    cw = ML_W

    @pl.when((pl.program_id(0) * tm) % seq == 0)
    def _():
        ext_ref[0:SUBLANES, :] = jnp.zeros((SUBLANES, 2 * cw), f32)

    cur = _dot(hb, w_ref[:, 0:2 * cw])
    ext_ref[SUBLANES:SUBLANES + tm, :] = cur
    acc = cb_ref[...] + cw_ref[CONV_K - 1:CONV_K, :] * cur
    for j in range(CONV_K - 1):
        off = SUBLANES - (CONV_K - 1) + j
        acc = acc + cw_ref[j:j + 1, :] * ext_ref[off:off + tm, :]
    ext_ref[0:SUBLANES, :] = cur[tm - SUBLANES:tm, :]
    qk = _silu(acc)
    pm_ref[:, 0:cw] = qk[:, 0:cw].astype(bf16)
    pm_ref[:, cw:2 * cw] = (qk[:, cw:2 * cw] * (ML_DH ** -0.5)).astype(bf16)
    for n in range(2, 4):
        pm_ref[:, n * cw:(n + 1) * cw] = _dot(hb, w_ref[:, n * cw:(n + 1) * cw]).astype(bf16)

    lane = lax.broadcasted_iota(jnp.int32, (tm, LANES), 1)
    q_aug = jnp.where(jnp.logical_and(lane >= DA_DH, lane < DA_DH + 3), 1.0, 0.0)
    pos = (pl.program_id(0) * tm) % seq + lax.broadcasted_iota(jnp.int32, (tm, 1), 0)
    pos = pos.astype(f32)
    q0 = 4 * cw
    k0 = q0 + 2 * DA_W
    yq = _dot(hb, w_ref[:, q0:q0 + 2 * DA_W])
    yk = _dot(hb, w_ref[:, k0:k0 + 2 * DA_W])
    for grp in range(2 * DA_HEADS):
        sl = slice(grp * LANES, (grp + 1) * LANES)
        y = yq[:, sl]
        ss = jnp.sum(y * y, axis=-1, keepdims=True)
        q_ref[:, sl] = (y * lax.rsqrt(ss * (1.0 / DA_DH) + EPS) * qg_ref[:, sl] + q_aug).astype(bf16)

        y = yk[:, sl]
        ss = jnp.sum(y * y, axis=-1, keepdims=True)
        r = pos * _alibi_slope_log2(grp // 2)
        r_hi = r.astype(bf16).astype(f32)
        r_mid = (r - r_hi).astype(bf16).astype(f32)
        r_lo = r - r_hi - r_mid
        k_aug = jnp.where(lane == DA_DH, r_hi,
                          jnp.where(lane == DA_DH + 1, r_mid, jnp.where(lane == DA_DH + 2, r_lo, 0.0)))
        k_ref[:, sl] = (y * lax.rsqrt(ss * (1.0 / DA_DH) + EPS) * kg_ref[:, sl] + k_aug).astype(bf16)

    v0 = k0 + 2 * DA_W
    v_ref[...] = _dot(hb, w_ref[:, v0:v0 + DA_W]).astype(bf16)
    gcol_ref[...] = _dot(hb, wg_ref[...]) + gbr_ref[...]
    grow_ref[...] = _dot_nt(wgt_ref[...], hb) + gbc_ref[...]


def _inproj(x2, mod, w_main, wg, wgt, gb_row, gb_col, qgain, kgain, conv_w, conv_b, seq):
    t, d = x2.shape
    tm = TM_INPROJ
    tiles_per_seq = seq // tm
    n_main = w_main.shape[1]
    const = lambda shape: pl.BlockSpec(shape, lambda i: (0,) * len(shape))
    return pl.pallas_call(
        functools.partial(_inproj_kernel, seq=seq),
        out_shape=(jax.ShapeDtypeStruct((t, 4 * ML_W), bf16),
                   jax.ShapeDtypeStruct((t, 2 * DA_W), bf16),
                   jax.ShapeDtypeStruct((t, 2 * DA_W), bf16),
                   jax.ShapeDtypeStruct((t, DA_W), bf16),
                   jax.ShapeDtypeStruct((t, LANES), f32),
                   jax.ShapeDtypeStruct((N_GATES, t), f32)),
        grid=(t // tm,),
        in_specs=[pl.BlockSpec((tm, d), lambda i: (i, 0)),
                  pl.BlockSpec((1, 6, d), lambda i: (i // tiles_per_seq, 0, 0)),
                  const((d, n_main)), const((d, LANES)), const((N_GATES, d)),
                  const((1, LANES)), const((N_GATES, 1)),
                  const((1, 2 * DA_W)), const((1, 2 * DA_W)),
                  const((CONV_K, 2 * ML_W)), const((1, 2 * ML_W))],
        out_specs=(pl.BlockSpec((tm, 4 * ML_W), lambda i: (i, 0)),
                   pl.BlockSpec((tm, 2 * DA_W), lambda i: (i, 0)),
                   pl.BlockSpec((tm, 2 * DA_W), lambda i: (i, 0)),
                   pl.BlockSpec((tm, DA_W), lambda i: (i, 0)),
                   pl.BlockSpec((tm, LANES), lambda i: (i, 0)),
                   pl.BlockSpec((N_GATES, tm), lambda i: (0, i))),
        scratch_shapes=[pltpu.VMEM((SUBLANES + tm, 2 * ML_W), f32)],
        compiler_params=_cparams(1),
        name="inproj",
    )(x2, mod, w_main, wg, wgt, gb_row, gb_col, qgain, kgain, conv_w, conv_b)


def _mlstm_kernel(q_ref, k_ref, v_ref, o_ref, gcol_ref, grow_ref, ng_ref, out_ref, state_ref, m_ref):
    L = ML_CHUNK
    dh = ML_DH

    @pl.when(pl.program_id(1) == 0)
    def _():
        state_ref[...] = jnp.zeros_like(state_ref)
        m_ref[...] = jnp.zeros_like(m_ref)

    row_i = lax.broadcasted_iota(jnp.int32, (L, L), 0)
    col_i = lax.broadcasted_iota(jnp.int32, (L, L), 1)
    causal = row_i >= col_i
    tril = causal.astype(f32)
    triu = (row_i <= col_i).astype(f32)
    lane = lax.broadcasted_iota(jnp.int32, (L, dh), 1)
    ones_col = (lane == 0).astype(f32)
    g = ng_ref[...]

    for cc in range(ML_STEP_CHUNKS):
        rows = slice(cc * L, (cc + 1) * L)
        gcol = gcol_ref[rows, :]
        grow = grow_ref[:, rows]
        bcol_all = jnp.dot(tril, _log_sigmoid(gcol), precision=HIGHEST, preferred_element_type=f32)
        brow_all = jnp.dot(_log_sigmoid(grow), triu, precision=HIGHEST, preferred_element_type=f32)

        for hd in range(ML_HEADS):
            cols = slice(hd * dh, (hd + 1) * dh)
            qb = q_ref[rows, cols]
            kb = k_ref[rows, cols]
            v = v_ref[rows, cols].astype(f32)
            v_aug = jnp.concatenate([v, ones_col], axis=1)

            bcol = bcol_all[:, ML_HEADS + hd:ML_HEADS + hd + 1]
            icol = gcol[:, hd:hd + 1]
            brow = brow_all[ML_HEADS + hd:ML_HEADS + hd + 1, :]
            irow = grow[hd:hd + 1, :]
            m_prev = m_ref[hd:hd + 1, 0:1]

            d_intra = jnp.where(causal, bcol - brow + irow, NEG_BIG)
            d_inter = bcol + m_prev
            m_t = jnp.maximum(d_inter, jnp.max(d_intra, axis=1, keepdims=True))
            w_intra = jnp.exp(d_intra - m_t)
            w_inter = jnp.exp(d_inter - m_t)

            s = _dot_nt(qb, kb) * w_intra
            state = state_ref[hd]
            q_state = _dot(qb, state.astype(bf16))
            s_v = _dot(s.astype(bf16), v_aug.astype(bf16))
            num = w_inter * q_state[:, 0:dh] + s_v[:, 0:dh]
            den = w_inter * q_state[:, dh:dh + 1] + s_v[:, dh:dh + 1]
            hval = num / jnp.maximum(jnp.abs(den), jnp.exp(-m_t))

            b_last = bcol[L - 1:L, :]
            d_state = b_last - bcol + icol
            m_new = jnp.maximum(b_last + m_prev, jnp.max(d_state, axis=0, keepdims=True))
            carry_scale = jnp.exp(b_last + m_prev - m_new)
            wk = jnp.exp(d_state - m_new)
            k_t = kb.astype(f32).T.astype(bf16)
            upd = _dot(k_t, (v_aug * wk).astype(bf16))
            state_ref[hd] = carry_scale * state + upd
            m_ref[hd:hd + 1, :] = jnp.broadcast_to(m_new, (1, LANES))

            hn = hval * lax.rsqrt(jnp.mean(hval * hval, axis=-1, keepdims=True) + EPS) * g
            og = _sigmoid(o_ref[rows, cols].astype(f32))
            out_ref[rows, cols] = (hn * og).astype(bf16)


def _mlstm(pm, gcol, grow, ml_norm_g, batch, seq):
    t = pm.shape[0]
    rows = ML_STEP_CHUNKS * ML_CHUNK
    ns = seq // rows
    row = lambda b, c: b * ns + c
    return pl.pallas_call(
        _mlstm_kernel,
        out_shape=jax.ShapeDtypeStruct((t, ML_W), bf16),
        grid=(batch, ns),
        in_specs=[pl.BlockSpec((rows, ML_W), lambda b, c: (row(b, c), 0)),
                  pl.BlockSpec((rows, ML_W), lambda b, c: (row(b, c), 1)),
                  pl.BlockSpec((rows, ML_W), lambda b, c: (row(b, c), 2)),
                  pl.BlockSpec((rows, ML_W), lambda b, c: (row(b, c), 3)),
                  pl.BlockSpec((rows, LANES), lambda b, c: (row(b, c), 0)),
                  pl.BlockSpec((N_GATES, rows), lambda b, c: (0, row(b, c))),
                  pl.BlockSpec((1, ML_DH), lambda b, c: (0, 0))],
        out_specs=pl.BlockSpec((rows, ML_W), lambda b, c: (row(b, c), 0)),
        scratch_shapes=[pltpu.VMEM((ML_HEADS, ML_DH, 2 * ML_DH), f32),
                        pltpu.VMEM((SUBLANES, LANES), f32)],
        compiler_params=_cparams(2),
        name="mlstm",
    )(pm, pm, pm, pm, gcol, grow, ml_norm_g)


def _attn_block(q_ref, k_ref, v_ref, m_ref, l_ref, acc_ref, row0, nrows, nkeys, key0=None):
    rows = slice(row0, row0 + nrows)
    nch = nkeys // LANES
    v = v_ref[0:nkeys, :]
    if key0 is not None:
        keep = (key0 + lax.broadcasted_iota(jnp.int32, (nrows, nkeys), 1)
                <= row0 + lax.broadcasted_iota(jnp.int32, (nrows, nkeys), 0))
    for c in range(2):
        sl = slice(c * LANES, (c + 1) * LANES)
        s = _dot_nt(q_ref[rows, sl], k_ref[0:nkeys, sl])
        if key0 is not None:
            s = jnp.where(keep, s, NEG_BIG)
        chunks = [s[:, j * LANES:(j + 1) * LANES] for j in range(nch)]
        mc = chunks[0]
        for ch in chunks[1:]:
            mc = jnp.maximum(mc, ch)
        m_old = m_ref[c, rows, :]
        m_new = jnp.maximum(m_old, jnp.max(mc, axis=1, keepdims=True))
        alpha = jnp.exp2(m_old - m_new)
        ps = [jnp.exp2(ch - m_new) for ch in chunks]
        lsum = ps[0]
        for pj in ps[1:]:
            lsum = lsum + pj
        p = jnp.concatenate([pj.astype(bf16) for pj in ps], axis=1)
        l_ref[c, rows, :] = alpha * l_ref[c, rows, :] + lsum
        acc_ref[c, rows, :] = alpha * acc_ref[c, rows, :] + _dot(p, v)
        m_ref[c, rows, :] = m_new


def _attn_kernel(qt_ref, kt_ref, q_ref, k_ref, v_ref, lam_ref, g_ref, o_ref,
                 m_ref, l_ref, acc_ref, *, lam_init):
    step = pl.program_id(2)
    qi = qt_ref[step]
    ki = kt_ref[step]

    @pl.when(ki == 0)
    def _():
        m_ref[...] = jnp.full_like(m_ref, NEG_BIG)
        l_ref[...] = jnp.zeros_like(l_ref)
        acc_ref[...] = jnp.zeros_like(acc_ref)

    blk = ATT_BLOCK
    half = blk // 2
    qrows = ATT_QBLOCKS * blk
    first_diag = ATT_QBLOCKS * qi
    args = (q_ref, k_ref, v_ref, m_ref, l_ref, acc_ref)

    @pl.when(ki < first_diag)
    def _():
        _attn_block(*args, 0, qrows, blk)

    for j in range(ATT_QBLOCKS):
        @pl.when(ki == first_diag + j)
        def _(j=j):
            r0 = j * blk
            _attn_block(*args, r0, half, half, key0=r0)
            _attn_block(*args, r0 + half, half, blk, key0=r0)
            if r0 + blk < qrows:
                _attn_block(*args, r0 + blk, qrows - r0 - blk, blk)

    @pl.when(ki == first_diag + ATT_QBLOCKS - 1)
    def _():
        lam = lam_ref[...]
        l0 = jnp.sum(l_ref[0], axis=1, keepdims=True)
        l1 = jnp.sum(l_ref[1], axis=1, keepdims=True)
        o = acc_ref[0] / l0 - lam * (acc_ref[1] / l1)
        o = o * lax.rsqrt(jnp.mean(o * o, axis=-1, keepdims=True) + EPS)
        o_ref[...] = (o * g_ref[...] * (1.0 - lam_init)).astype(bf16)


def _attention(qh, kh, vh, lam, da_norm_g, batch, seq, lam_init):
    t = qh.shape[0]
    blk = ATT_BLOCK
    qrows = ATT_QBLOCKS * blk
    nk = seq // blk
    nq = seq // qrows
    pairs = [(i, j) for i in range(nq) for j in range(ATT_QBLOCKS * (i + 1))]
    qt = jnp.asarray([p[0] for p in pairs], jnp.int32)
    kt = jnp.asarray([p[1] for p in pairs], jnp.int32)
    w = 2 * DA_DH
    grid_spec = pltpu.PrefetchScalarGridSpec(
        num_scalar_prefetch=2,
        grid=(batch, DA_HEADS, len(pairs)),
        in_specs=[pl.BlockSpec((qrows, 2 * LANES), lambda b, h, s, qt, kt: (b * nq + qt[s], h)),
                  pl.BlockSpec((blk, 2 * LANES), lambda b, h, s, qt, kt: (b * nk + kt[s], h)),
                  pl.BlockSpec((blk, w), lambda b, h, s, qt, kt: (b * nk + kt[s], h)),
                  pl.BlockSpec((1, 1), lambda b, h, s, qt, kt: (0, 0)),
                  pl.BlockSpec((1, w), lambda b, h, s, qt, kt: (0, 0))],
        out_specs=pl.BlockSpec((qrows, w), lambda b, h, s, qt, kt: (b * nq + qt[s], h)),
        scratch_shapes=[pltpu.VMEM((2, qrows, LANES), f32),
                        pltpu.VMEM((2, qrows, LANES), f32),
                        pltpu.VMEM((2, qrows, w), f32)],
    )
    return pl.pallas_call(
        functools.partial(_attn_kernel, lam_init=lam_init),
        out_shape=jax.ShapeDtypeStruct((t, DA_W), bf16),
        grid_spec=grid_spec,
        compiler_params=_cparams(3),
        name="attn",
    )(qt, kt, qh, kh, vh, lam, da_norm_g)


def _first_index_of_max(x, iota_f, size):
    m = jnp.max(x, axis=0, keepdims=True)
    idx = jnp.min(jnp.where(x == m, iota_f, float(size)), axis=0, keepdims=True)
    return m, idx


def _route_kernel(hm_ref, hd_ref, x_ref, mod_ref, wo_ref, wr_ref, rb_ref,
                  x1_ref, ha_ref, hb_ref, idx_ref, gate_ref, rank_ref, cnt_ref, base_ref):
    tm = TM_ROUTE
    i = pl.program_id(0)

    @pl.when(i == 0)
    def _():
        base_ref[...] = jnp.zeros_like(base_ref)

    mix = _dot(hm_ref[...], wo_ref[0:ML_W, :]) + _dot(hd_ref[...], wo_ref[ML_W:ML_W + DA_W, :])
    x1 = x_ref[...] + mod_ref[0, 2:3, :] * mix
    x1_ref[...] = x1
    h2 = x1 * lax.rsqrt(jnp.mean(x1 * x1, axis=-1, keepdims=True) + EPS)
    h2 = h2 * (1.0 + mod_ref[0, 4:5, :]) + mod_ref[0, 3:4, :]
    packed = lax.bitcast_convert_type(_pack_bf16_pair(h2[:, 0:HALF], h2[:, HALF:D_MODEL]), jnp.int32)
    ha_ref[...] = packed[:, 0:QUARTER]
    hb_ref[...] = packed[:, QUARTER:HALF]

    sc = _sigmoid(_dot_nt(wr_ref[...], h2.astype(bf16)))
    sel = sc + rb_ref[...]

    gi = lax.broadcasted_iota(jnp.int32, (GROUP_SIZE, tm), 0).astype(f32)
    gscores = []
    for g in range(N_GROUPS):
        blk = sel[g * GROUP_SIZE:(g + 1) * GROUP_SIZE, :]
        m1, i1 = _first_index_of_max(blk, gi, GROUP_SIZE)
        m2 = jnp.max(jnp.where(gi == i1, NEG_BIG, blk), axis=0, keepdims=True)
        gscores.append(m1 + m2)
    gs = jnp.concatenate(gscores, axis=0)

    ngi = lax.broadcasted_iota(jnp.int32, (N_GROUPS, tm), 0).astype(f32)
    gkeep = jnp.zeros((N_GROUPS, tm), f32)
    for _ in range(TOPK_GROUPS):
        _, gidx = _first_index_of_max(gs, ngi, N_GROUPS)
        hit = ngi == gidx
        gkeep = jnp.where(hit, 1.0, gkeep)
        gs = jnp.where(hit, NEG_BIG, gs)
    masked = jnp.concatenate(
        [jnp.where(gkeep[g:g + 1, :] > 0.0, sel[g * GROUP_SIZE:(g + 1) * GROUP_SIZE, :], NEG_BIG)
         for g in range(N_GROUPS)], axis=0)

    ei = lax.broadcasted_iota(jnp.int32, (N_EXPERTS, tm), 0).astype(f32)
    idxs, gates = [], []
    chosen = jnp.zeros((N_EXPERTS, tm), f32)
    for _ in range(TOP_K):
        _, eidx = _first_index_of_max(masked, ei, N_EXPERTS)
        hit = ei == eidx
        idxs.append(eidx)
        gates.append(jnp.sum(jnp.where(hit, sc, 0.0), axis=0, keepdims=True))
        chosen = jnp.where(hit, 1.0, chosen)
        masked = jnp.where(hit, NEG_BIG, masked)
    gate = jnp.concatenate(gates, axis=0)
    gate = gate / jnp.sum(gate, axis=0, keepdims=True) * ROUTED_SCALE
    gate_ref[...] = gate
    idx_ref[...] = jnp.concatenate(idxs, axis=0).astype(jnp.int32)

    tr = lax.broadcasted_iota(jnp.int32, (tm, tm), 0)
    tc = lax.broadcasted_iota(jnp.int32, (tm, tm), 1)
    before = (tr < tc).astype(bf16)
    seen = base_ref[...] + _dot(chosen.astype(bf16), before)
    ranks = [jnp.sum(jnp.where(ei == eidx, seen, 0.0), axis=0, keepdims=True) for eidx in idxs]
    rank_ref[...] = jnp.concatenate(ranks, axis=0).astype(jnp.int32)
    total = base_ref[...] + jnp.sum(chosen, axis=1, keepdims=True)
    base_ref[...] = total
    cnt_ref[...] = total


def _route(hm, hd, x2, mod, w_out, w_router_t, rbias_col, seq):
    t, d = x2.shape
    tm = TM_ROUTE
    tiles_per_seq = seq // tm
    const = lambda shape: pl.BlockSpec(shape, lambda i: (0,) * len(shape))
    return pl.pallas_call(
        _route_kernel,
        out_shape=(jax.ShapeDtypeStruct((t, d), f32),
                   jax.ShapeDtypeStruct((t, QUARTER), jnp.int32),
                   jax.ShapeDtypeStruct((t, QUARTER), jnp.int32),
                   jax.ShapeDtypeStruct((TOP_K, t), jnp.int32),
                   jax.ShapeDtypeStruct((TOP_K, t), f32),
                   jax.ShapeDtypeStruct((TOP_K, t), jnp.int32),
                   jax.ShapeDtypeStruct((N_EXPERTS, 1), f32)),
        grid=(t // tm,),
        in_specs=[pl.BlockSpec((tm, ML_W), lambda i: (i, 0)),
                  pl.BlockSpec((tm, DA_W), lambda i: (i, 0)),
                  pl.BlockSpec((tm, d), lambda i: (i, 0)),
                  pl.BlockSpec((1, 6, d), lambda i: (i // tiles_per_seq, 0, 0)),
                  const((d, d)), const((N_EXPERTS, d)), const((N_EXPERTS, 1))],
        out_specs=(pl.BlockSpec((tm, d), lambda i: (i, 0)),
                   pl.BlockSpec((tm, QUARTER), lambda i: (i, 0)),
                   pl.BlockSpec((tm, QUARTER), lambda i: (i, 0)),
                   pl.BlockSpec((TOP_K, tm), lambda i: (0, i)),
                   pl.BlockSpec((TOP_K, tm), lambda i: (0, i)),
                   pl.BlockSpec((TOP_K, tm), lambda i: (0, i)),
                   const((N_EXPERTS, 1))),
        scratch_shapes=[pltpu.VMEM((N_EXPERTS, 1), f32)],
        compiler_params=_cparams(1),
        name="route",
    )(hm, hd, x2, mod, w_out, w_router_t, rbias_col)


SLOT_RADIX = 256
SLOT_DIGITS = 3


def _slots_kernel(idx_ref, rank_ref, dig_ref, dest_ref):
    tm = TM_SLOTS
    ei = lax.broadcasted_iota(jnp.int32, (N_EXPERTS, tm), 0)
    rows = []
    for k in range(TOP_K):
        onehot = jnp.where(ei == idx_ref[k:k + 1, :], 1.0, 0.0).astype(bf16)
        dg = _dot(dig_ref[...], onehot)
        start = dg[0:1, :]
        for j in range(1, SLOT_DIGITS):
            start = start + dg[j:j + 1, :] * float(SLOT_RADIX ** j)
        rows.append(start.astype(jnp.int32) + rank_ref[k:k + 1, :])
    dest_ref[...] = jnp.concatenate(rows, axis=0)


def _slots(idx, rank, digits):
    t = idx.shape[1]
    tm = TM_SLOTS
    return pl.pallas_call(
        _slots_kernel,
        out_shape=jax.ShapeDtypeStruct((TOP_K, t), jnp.int32),
        grid=(t // tm,),
        in_specs=[pl.BlockSpec((TOP_K, tm), lambda i: (0, i)),
                  pl.BlockSpec((TOP_K, tm), lambda i: (0, i)),
                  pl.BlockSpec((SUBLANES, N_EXPERTS), lambda i: (0, 0))],
        out_specs=pl.BlockSpec((TOP_K, tm), lambda i: (0, i)),
        compiler_params=_cparams(1),
        name="slots",
    )(idx, rank, digits)


def _sc_scatter_table(rows, dest, n_slots):
    n, width = rows.shape
    mesh = plsc.VectorSubcoreMesh(core_axis_name="core", subcore_axis_name="subcore")

    @pl.kernel(out_type=jax.ShapeDtypeStruct((n_slots, width), rows.dtype), mesh=mesh, scratch_types=[],
               name="sc_scatter")
    def scatter_kernel(rows_hbm, dest_hbm, out_hbm):
        def body(rows_vmem, dest_vmem):
            for k in range(TOP_K):
                pltpu.sync_copy(rows_vmem, out_hbm.at[dest_vmem.at[k]])

        pltpu.emit_pipeline(
            body,
            grid=(n // SC_WINDOW,),
            in_specs=[pl.BlockSpec((SC_WINDOW, width), lambda i: (i, 0)),
                      pl.BlockSpec((TOP_K, SC_WINDOW), lambda i: (0, i))],
            out_specs=[],
            core_axis_name=("core", "subcore"),
            dimension_semantics=(pltpu.PARALLEL,),
        )(rows_hbm, dest_hbm)

    return scatter_kernel(rows, dest)


EXPERT_GROUP = 4
EXPERT_RING = 16
EXPERT_AHEAD = EXPERT_RING - EXPERT_GROUP


def _expert_kernel(start_ref, cnt_ref, total_ref, xsa_ref, xsb_ref, w1_ref, w3_ref, w2_ref,
                   ysa_ref, ysb_ref, xbuf, ybuf, w1b, w3b, w2b, in_sem, out_sem):
    bm = MOE_BLOCK
    ring = EXPERT_RING
    ahead = EXPERT_AHEAD
    shift = int(math.log2(bm))
    e = pl.program_id(0)
    total = total_ref[0]
    cnt = cnt_ref[e]
    nb = lax.shift_right_logical(cnt + (bm - 1), shift)
    g0 = lax.shift_right_logical(start_ref[e], shift)

    def slot_of(g):
        return jnp.bitwise_and(g, ring - 1)

    def in_copies(g):
        s = slot_of(g)
        rows = pl.ds(pl.multiple_of(g * bm, bm), bm)
        return [pltpu.make_async_copy(x_ref.at[rows], xbuf.at[s, half], in_sem.at[s])
                for half, x_ref in enumerate((xsa_ref, xsb_ref))]

    def out_copies(g):
        s = slot_of(g)
        rows = pl.ds(pl.multiple_of(g * bm, bm), bm)
        return [pltpu.make_async_copy(ybuf.at[s, half], y_ref.at[rows], out_sem.at[s])
                for half, y_ref in enumerate((ysa_ref, ysb_ref))]

    def request(g):
        @pl.when(g < total)
        def _():
            for cp in in_copies(g):
                cp.start()

    def acquire(g):
        for cp in in_copies(g):
            cp.wait()

        @pl.when(g >= ring)
        def _():
            for cp in out_copies(g - ring):
                cp.wait()

    def compute(g):
        s = slot_of(g)
        row = (g - g0) * bm + lax.broadcasted_iota(jnp.int32, (bm, HALF), 0)
        words = jnp.concatenate([xbuf[s, 0], xbuf[s, 1]], axis=1)
        lo, hi = _unpack_bf16_pair(lax.bitcast_convert_type(jnp.where(row < cnt, words, 0), jnp.uint32))
        lo = lo.astype(bf16)
        hi = hi.astype(bf16)
        h1 = _dot(lo, w1b[0:HALF, :]) + _dot(hi, w1b[HALF:D_MODEL, :])
        h3 = _dot(lo, w3b[0:HALF, :]) + _dot(hi, w3b[HALF:D_MODEL, :])
        y = _dot((_silu(h1) * h3).astype(bf16), w2b[...])
        packed = lax.bitcast_convert_type(_pack_bf16_pair(y[:, 0:HALF], y[:, HALF:D_MODEL]), jnp.int32)
        ybuf[s, 0] = packed[:, 0:QUARTER]
        ybuf[s, 1] = packed[:, QUARTER:HALF]

    def release(g):
        for cp in out_copies(g):
            cp.start()

    @pl.when(e == 0)
    def _():
        for g in range(ahead):
            request(g)

    @pl.when(nb > 0)
    def _():
        w1b[...] = w1_ref[0].astype(bf16)
        w3b[...] = w3_ref[0].astype(bf16)
        w2b[...] = w2_ref[0].astype(bf16)

    def run(g, n):
        for d in range(n):
            request(g + d + ahead)
        for d in range(n):
            acquire(g + d)
        for d in range(n):
            compute(g + d)
        for d in range(n):
            release(g + d)

    def group(i, carry):
        run(g0 + EXPERT_GROUP * i, EXPERT_GROUP)
        return carry

    lax.fori_loop(0, lax.shift_right_logical(nb, int(math.log2(EXPERT_GROUP))), group, 0)

    n = EXPERT_GROUP // 2
    while n >= 1:
        @pl.when(jnp.bitwise_and(nb, n) != 0)
        def _(n=n):
            done = jnp.bitwise_and(nb, -2 * n)
            run(g0 + done, n)
        n //= 2

    @pl.when(e == pl.num_programs(0) - 1)
    def _():
        for back in range(1, ring + 1):
            @pl.when(total - back >= 0)
            def _():
                for cp in out_copies(total - back):
                    cp.wait()


def _experts(start, counts, total, xs_a, xs_b, w1, w3, w2):
    n_pad = xs_a.shape[0]
    bm = MOE_BLOCK
    grid_spec = pltpu.PrefetchScalarGridSpec(
        num_scalar_prefetch=3,
        grid=(N_EXPERTS,),
        in_specs=[pl.BlockSpec(memory_space=pl.ANY),
                  pl.BlockSpec(memory_space=pl.ANY),
                  pl.BlockSpec((1, D_MODEL, D_EXPERT), lambda e, st, ct, tt: (e, 0, 0)),
                  pl.BlockSpec((1, D_MODEL, D_EXPERT), lambda e, st, ct, tt: (e, 0, 0)),
                  pl.BlockSpec((1, D_EXPERT, D_MODEL), lambda e, st, ct, tt: (e, 0, 0))],
        out_specs=(pl.BlockSpec(memory_space=pl.ANY), pl.BlockSpec(memory_space=pl.ANY)),
        scratch_shapes=[pltpu.VMEM((EXPERT_RING, 2, bm, QUARTER), jnp.int32),
                        pltpu.VMEM((EXPERT_RING, 2, bm, QUARTER), jnp.int32),
                        pltpu.VMEM((D_MODEL, D_EXPERT), bf16),
                        pltpu.VMEM((D_MODEL, D_EXPERT), bf16),
                        pltpu.VMEM((D_EXPERT, D_MODEL), bf16),
                        pltpu.SemaphoreType.DMA((EXPERT_RING,)),
                        pltpu.SemaphoreType.DMA((EXPERT_RING,))],
    )
    return pl.pallas_call(
        _expert_kernel,
        out_shape=(jax.ShapeDtypeStruct((n_pad, QUARTER), jnp.int32),
                   jax.ShapeDtypeStruct((n_pad, QUARTER), jnp.int32)),
        grid_spec=grid_spec,
        compiler_params=_cparams(1),
        name="experts",
    )(start, counts, total, xs_a, xs_b, w1, w3, w2)


def _sc_gather_table(table, idx2):
    n = idx2.shape[1]
    width = table.shape[1]
    mesh = plsc.VectorSubcoreMesh(core_axis_name="core", subcore_axis_name="subcore")

    @pl.kernel(out_type=jax.ShapeDtypeStruct((n, width), table.dtype), mesh=mesh, scratch_types=[],
               name="sc_gather")
    def gather_kernel(tab_hbm, idx_hbm, out_hbm):
        def body(idx_vmem, out_vmem):
            pltpu.sync_copy(tab_hbm.at[idx_vmem.at[0]], out_vmem)

        pltpu.emit_pipeline(
            body,
            grid=(n // SC_WINDOW,),
            in_specs=[pl.BlockSpec((1, SC_WINDOW), lambda i: (0, i))],
            out_specs=[pl.BlockSpec((SC_WINDOW, width), lambda i: (i, 0))],
            core_axis_name=("core", "subcore"),
            dimension_semantics=(pltpu.PARALLEL,),
        )(idx_hbm, out_hbm)

    return gather_kernel(table, idx2)


def _sc_gather_rows(table_a, table_b, idx):
    idx2 = idx.reshape(1, -1)
    return _sc_gather_table(table_a, idx2), _sc_gather_table(table_b, idx2)


def _combine_kernel(gate_ref, ha_ref, hb_ref, x1_ref, mod_ref, ws1_ref, ws3_ref, ws2_ref, ga_ref, gb_ref,
                    out_ref):
    words = jnp.concatenate([ha_ref[...], hb_ref[...]], axis=1)
    lo, hi = _unpack_bf16_pair(lax.bitcast_convert_type(words, jnp.uint32))
    lo = lo.astype(bf16)
    hi = hi.astype(bf16)
    s1 = _dot(lo, ws1_ref[0:HALF, :]) + _dot(hi, ws1_ref[HALF:D_MODEL, :])
    s3 = _dot(lo, ws3_ref[0:HALF, :]) + _dot(hi, ws3_ref[HALF:D_MODEL, :])
    y = _dot((_silu(s1) * s3).astype(bf16), ws2_ref[...])

    gate = gate_ref[...]
    gate_f = mod_ref[0, 5:6, :]
    for part, g_ref in enumerate((ga_ref, gb_ref)):
        c_lo = slice(part * QUARTER, (part + 1) * QUARTER)
        c_hi = slice(HALF + part * QUARTER, HALF + (part + 1) * QUARTER)
        acc_lo = y[:, c_lo]
        acc_hi = y[:, c_hi]
        for k in range(TOP_K):
            rlo, rhi = _unpack_bf16_pair(lax.bitcast_convert_type(g_ref[k], jnp.uint32))
            gk = gate[:, k:k + 1]
            acc_lo = acc_lo + gk * rlo
            acc_hi = acc_hi + gk * rhi
        out_ref[:, c_lo] = x1_ref[:, c_lo] + gate_f[:, c_lo] * acc_lo
        out_ref[:, c_hi] = x1_ref[:, c_hi] + gate_f[:, c_hi] * acc_hi


def _combine(gate_col, h2a, h2b, x1, mod, ws1, ws3, ws2, ga, gb, seq):
    t, d = x1.shape
    tm = TM_MOE
    tiles_per_seq = seq // tm
    const = lambda shape: pl.BlockSpec(shape, lambda i: (0,) * len(shape))
    return pl.pallas_call(
        _combine_kernel,
        out_shape=jax.ShapeDtypeStruct((t, d), f32),
        grid=(t // tm,),
        in_specs=[pl.BlockSpec((tm, TOP_K), lambda i: (i, 0)),
                  pl.BlockSpec((tm, QUARTER), lambda i: (i, 0)),
                  pl.BlockSpec((tm, QUARTER), lambda i: (i, 0)),
                  pl.BlockSpec((tm, d), lambda i: (i, 0)),
                  pl.BlockSpec((1, 6, d), lambda i: (i // tiles_per_seq, 0, 0)),
                  const((d, D_EXPERT)), const((d, D_EXPERT)), const((D_EXPERT, d)),
                  pl.BlockSpec((TOP_K, tm, QUARTER), lambda i: (0, i, 0)),
                  pl.BlockSpec((TOP_K, tm, QUARTER), lambda i: (0, i, 0))],
        out_specs=pl.BlockSpec((tm, d), lambda i: (i, 0)),
        compiler_params=_cparams(1),
        name="combine",
    )(gate_col, h2a, h2b, x1, mod, ws1, ws3, ws2, ga, gb)


def _lambda_init(layer):
    return 0.8 - 0.6 * math.exp(-0.3 * layer)


def _layer(x, c, w_ada, b_ada, w_in, conv_w, conv_b, gate_b, ml_norm_g, da_q_norm_g, da_k_norm_g,
           lambda_q1, lambda_k1, lambda_q2, lambda_k2, da_norm_g, w_out, w_router, router_bias,
           w1, w3, w2, ws1, ws3, ws2, layer):
    batch, seq, d = x.shape
    t = batch * seq
    lam_init = _lambda_init(layer)
    x2 = x.reshape(t, d)

    c_pad = jnp.pad(c, ((0, -batch % SUBLANES), (0, 0)))
    mod = _adaln(c_pad, w_ada, b_ada.reshape(1, -1))[:batch].reshape(batch, 6, d)

    g0 = 4 * ML_W
    q0 = g0 + N_GATES
    n_grp = 2 * DA_HEADS

    def lane_groups(cols):
        r = cols.shape[0]
        return jnp.pad(cols.reshape(r, n_grp, DA_DH), ((0, 0), (0, 0), (0, LANES - DA_DH))).reshape(r, -1)

    w_main = jnp.concatenate([w_in[:, :g0],
                              lane_groups(w_in[:, q0:q0 + DA_W]),
                              lane_groups(w_in[:, q0 + DA_W:q0 + 2 * DA_W]),
                              w_in[:, q0 + 2 * DA_W:]], axis=1).astype(bf16)
    w_gate = w_in[:, g0:g0 + N_GATES]
    wg = jnp.pad(w_gate, ((0, 0), (0, LANES - N_GATES))).astype(bf16)
    wgt = w_gate.T.astype(bf16)
    gb_row = jnp.pad(gate_b, (0, LANES - N_GATES)).reshape(1, LANES)
    gb_col = gate_b.reshape(N_GATES, 1)
    qgain = lane_groups((jnp.tile(da_q_norm_g, n_grp) * (DA_DH ** -0.5 * LOG2E)).reshape(1, DA_W))
    kgain = lane_groups(jnp.tile(da_k_norm_g, n_grp).reshape(1, DA_W))

    pm, qh, kh, vh, gcol, grow = _inproj(x2, mod, w_main, wg, wgt, gb_row, gb_col, qgain, kgain,
                                         conv_w, conv_b.reshape(1, -1), seq)

    hm = _mlstm(pm, gcol, grow, ml_norm_g.reshape(1, -1), batch, seq)

    lam = (jnp.exp(jnp.sum(lambda_q1 * lambda_k1)) - jnp.exp(jnp.sum(lambda_q2 * lambda_k2))
           + lam_init).reshape(1, 1).astype(f32)
    hd = _attention(qh, kh, vh, lam, da_norm_g.reshape(1, -1), batch, seq, lam_init)

    x1, h2a, h2b, idx, gate, rank, counts = _route(hm, hd, x2, mod, w_out.astype(bf16),
                                                   w_router.T.astype(bf16), router_bias.reshape(-1, 1), seq)

    bm = MOE_BLOCK
    n_blocks = (t * TOP_K) // bm + N_EXPERTS
    counts_i = counts.reshape(-1).astype(jnp.int32)
    padded = (counts_i + bm - 1) // bm * bm
    pad_end = jnp.cumsum(padded)
    pad_start = pad_end - padded
    digits = jnp.stack([(pad_start // SLOT_RADIX ** j) % SLOT_RADIX for j in range(SLOT_DIGITS)])
    digits = jnp.pad(digits, ((0, SUBLANES - SLOT_DIGITS), (0, 0))).astype(bf16)
    dest = _slots(idx, rank, digits)

    xs_a = _sc_scatter_table(h2a, dest, n_blocks * bm)
    xs_b = _sc_scatter_table(h2b, dest, n_blocks * bm)
    total_blocks = (pad_end[-1:] // bm).astype(jnp.int32)
    ys_a, ys_b = _experts(pad_start.astype(jnp.int32), counts_i, total_blocks, xs_a, xs_b, w1, w3, w2)
    ga, gb = _sc_gather_rows(ys_a, ys_b, dest.reshape(-1))
    out = _combine(gate.T, h2a, h2b, x1, mod, ws1.astype(bf16), ws3.astype(bf16), ws2.astype(bf16),
                   ga.reshape(TOP_K, t, QUARTER), gb.reshape(TOP_K, t, QUARTER), seq)
    return out.reshape(batch, seq, d)


def kernel(x, c, w_ada, b_ada, w_in, conv_w, conv_b, gate_b, ml_norm_g, da_q_norm_g, da_k_norm_g,
           lambda_q1, lambda_k1, lambda_q2, lambda_k2, da_norm_g, w_out, w_router, router_bias,
           w1, w3, w2, ws1, ws3, ws2):
    depth = w_ada.shape[0]
    for l in range(depth):
        x = _layer(x, c, w_ada[l], b_ada[l], w_in[l], conv_w[l], conv_b[l], gate_b[l], ml_norm_g[l],
                   da_q_norm_g[l], da_k_norm_g[l], lambda_q1[l], lambda_k1[l], lambda_q2[l], lambda_k2[l],
                   da_norm_g[l], w_out[l], w_router[l], router_bias[l], w1[l], w3[l], w2[l],
                   ws1[l], ws3[l], ws2[l], l)
    return x
```

```python
import functools
import math

import jax
import jax.numpy as jnp
from jax import lax
from jax.experimental import pallas as pl
from jax.experimental.pallas import tpu as pltpu
from jax.experimental.pallas import tpu_sc as plsc

D_MODEL = 1024
ML_HEADS = 4
ML_DH = 128
ML_W = ML_HEADS * ML_DH
ML_CHUNK = 128
CONV_K = 4
DA_HEADS = 4
DA_DH = 64
DA_W = DA_HEADS * 2 * DA_DH
N_EXPERTS = 256
TOP_K = 8
N_GROUPS = 8
GROUP_SIZE = N_EXPERTS // N_GROUPS
TOPK_GROUPS = 4
D_EXPERT = 256
ROUTED_SCALE = 2.5
EPS = 1e-6
N_GATES = 2 * ML_HEADS

LANES = 128
SUBLANES = 8
VMEM_LIMIT_BYTES = 48 * 1024 * 1024

TM_INPROJ = 512
ML_STEP_CHUNKS = 4
ATT_BLOCK = 1024
ATT_QBLOCKS = 2
TM_ROUTE = 256
TM_SLOTS = 512
TM_MOE = 512
MOE_BLOCK = 256
HALF = D_MODEL // 2
QUARTER = HALF // 2
SC_WINDOW = 128

NEG_BIG = -1e30
LOG2E = 1.4426950408889634

f32 = jnp.float32
bf16 = jnp.bfloat16
HIGHEST = lax.Precision.HIGHEST


def _cparams(n_axes):
    return pltpu.CompilerParams(dimension_semantics=("arbitrary",) * n_axes,
                                vmem_limit_bytes=VMEM_LIMIT_BYTES)


def _dot(a, b):
    return jnp.dot(a, b, preferred_element_type=f32)


def _dot_nt(a, b):
    return lax.dot_general(a, b, (((1,), (1,)), ((), ())), preferred_element_type=f32)


def _sigmoid(x):
    return 1.0 / (1.0 + jnp.exp(-x))


def _silu(x):
    return x * _sigmoid(x)


def _log_sigmoid(x):
    return jnp.minimum(x, 0.0) - jnp.log(1.0 + jnp.exp(-jnp.abs(x)))


def _pack_bf16_pair(lo, hi):
    lo_bits = lax.bitcast_convert_type(lo.astype(bf16).astype(f32), jnp.uint32)
    hi_bits = lax.bitcast_convert_type(hi.astype(bf16).astype(f32), jnp.uint32)
    return (hi_bits & jnp.uint32(0xFFFF0000)) | (lo_bits >> 16)


def _unpack_bf16_pair(w):
    lo = lax.bitcast_convert_type(w << 16, f32)
    hi = lax.bitcast_convert_type(w & jnp.uint32(0xFFFF0000), f32)
    return lo, hi


def _adaln_kernel(c_ref, w_ref, b_ref, o_ref):
    c = c_ref[...]
    o_ref[...] = jnp.dot(_silu(c), w_ref[...], precision=HIGHEST, preferred_element_type=f32) + b_ref[...]


def _adaln(c_pad, w_ada, b_ada):
    rows, d = c_pad.shape
    n = w_ada.shape[1]
    tn = 1536
    return pl.pallas_call(
        _adaln_kernel,
        out_shape=jax.ShapeDtypeStruct((rows, n), f32),
        grid=(n // tn,),
        in_specs=[pl.BlockSpec((rows, d), lambda j: (0, 0)),
                  pl.BlockSpec((d, tn), lambda j: (0, j)),
                  pl.BlockSpec((1, tn), lambda j: (0, j))],
        out_specs=pl.BlockSpec((rows, tn), lambda j: (0, j)),
        compiler_params=_cparams(1),
        name="adaln",
    )(c_pad, w_ada, b_ada)


def _alibi_slope_log2(head):
    return 2.0 ** (-8.0 * (head + 1) / DA_HEADS) * LOG2E


def _inproj_kernel(x_ref, mod_ref, w_ref, wg_ref, gbr_ref, qg_ref, kg_ref, bd_ref, kaug_ref, cw_ref, cb_ref,
                   pm_ref, q_ref, k_ref, v_ref, gcol_ref, grow_ref, ext_ref, *, seq):
    tm = TM_INPROJ
    x = x_ref[...]
    h = x * lax.rsqrt(jnp.mean(x * x, axis=-1, keepdims=True) + EPS)
    h = h * (1.0 + mod_ref[0, 1:2, :]) + mod_ref[0, 0:1, :]
    hb = h.astype(bf16)
    cw = ML_W

    @pl.when((pl.program_id(0) * tm) % seq == 0)
    def _():
        ext_ref[0:SUBLANES, :] = jnp.zeros((SUBLANES, 2 * cw), f32)

    cur = _dot(hb, w_ref[:, 0:2 * cw])
    ext_ref[SUBLANES:SUBLANES + tm, :] = cur
    acc = cb_ref[...] + cw_ref[CONV_K - 1:CONV_K, :] * cur
    for j in range(CONV_K - 1):
        off = SUBLANES - (CONV_K - 1) + j
        acc = acc + cw_ref[j:j + 1, :] * ext_ref[off:off + tm, :]
    ext_ref[0:SUBLANES, :] = cur[tm - SUBLANES:tm, :]
    qk = _silu(acc)
    pm_ref[:, 0:cw] = qk[:, 0:cw].astype(bf16)
    pm_ref[:, cw:2 * cw] = (qk[:, cw:2 * cw] * (ML_DH ** -0.5)).astype(bf16)
    for n in range(2, 4):
        pm_ref[:, n * cw:(n + 1) * cw] = _dot(hb, w_ref[:, n * cw:(n + 1) * cw]).astype(bf16)

    lane = lax.broadcasted_iota(jnp.int32, (tm, LANES), 1)
    feat = lane < DA_DH
    q_aug = jnp.where(jnp.logical_and(lane >= DA_DH, lane < DA_DH + 3), 1.0, 0.0)
    q0 = 4 * cw
    k0 = q0 + DA_W

    def qk_normed(col0, gain_ref):
        y = _dot(hb, w_ref[:, col0:col0 + DA_W])
        ss = _dot((y * y).astype(bf16), bd_ref[...])
        return y * lax.rsqrt(ss * (1.0 / DA_DH) + EPS) * gain_ref[...]

    def lane_group(y2, grp, aug):
        pair = y2[:, (grp // 2) * LANES:(grp // 2 + 1) * LANES]
        if grp % 2:
            pair = pltpu.roll(pair, DA_DH, axis=1)
        return jnp.where(feat, pair, aug).astype(bf16)

    nq = qk_normed(q0, qg_ref)
    nk = qk_normed(k0, kg_ref)
    for grp in range(2 * DA_HEADS):
        sl = slice(grp * LANES, (grp + 1) * LANES)
        head = slice((grp // 2) * LANES, (grp // 2 + 1) * LANES)
        q_ref[:, sl] = lane_group(nq, grp, q_aug)
        k_ref[:, sl] = lane_group(nk, grp, kaug_ref[:, head].astype(f32))

    v0 = k0 + DA_W
    v_ref[...] = _dot(hb, w_ref[:, v0:v0 + DA_W]).astype(bf16)
    gates = _dot(hb, wg_ref[...]) + gbr_ref[...]
    gcol_ref[...] = gates
    grow_ref[...] = gates.T[0:N_GATES, :]


def _inproj(x2, mod, w_main, wg, gb_row, qgain, kgain, bd, kaug, conv_w, conv_b, seq):
    t, d = x2.shape
    tm = TM_INPROJ
    tiles_per_seq = seq // tm
    n_main = w_main.shape[1]
    const = lambda shape: pl.BlockSpec(shape, lambda i: (0,) * len(shape))
    return pl.pallas_call(
        functools.partial(_inproj_kernel, seq=seq),
        out_shape=(jax.ShapeDtypeStruct((t, 4 * ML_W), bf16),
                   jax.ShapeDtypeStruct((t, 2 * DA_W), bf16),
                   jax.ShapeDtypeStruct((t, 2 * DA_W), bf16),
                   jax.ShapeDtypeStruct((t, DA_W), bf16),
                   jax.ShapeDtypeStruct((t, LANES), f32),
                   jax.ShapeDtypeStruct((N_GATES, t), f32)),
        grid=(t // tm,),
        in_specs=[pl.BlockSpec((tm, d), lambda i: (i, 0)),
                  pl.BlockSpec((1, 6, d), lambda i: (i // tiles_per_seq, 0, 0)),
                  const((d, n_main)), const((d, LANES)), const((1, LANES)),
                  const((1, DA_W)), const((1, DA_W)), const((DA_W, DA_W)),
                  pl.BlockSpec((tm, DA_HEADS * LANES), lambda i: (i % tiles_per_seq, 0)),
                  const((CONV_K, 2 * ML_W)), const((1, 2 * ML_W))],
        out_specs=(pl.BlockSpec((tm, 4 * ML_W), lambda i: (i, 0)),
                   pl.BlockSpec((tm, 2 * DA_W), lambda i: (i, 0)),
                   pl.BlockSpec((tm, 2 * DA_W), lambda i: (i, 0)),
                   pl.BlockSpec((tm, DA_W), lambda i: (i, 0)),
                   pl.BlockSpec((tm, LANES), lambda i: (i, 0)),
                   pl.BlockSpec((N_GATES, tm), lambda i: (0, i))),
        scratch_shapes=[pltpu.VMEM((SUBLANES + tm, 2 * ML_W), f32)],
        compiler_params=_cparams(1),
        name="inproj",
    )(x2, mod, w_main, wg, gb_row, qgain, kgain, bd, kaug, conv_w, conv_b)


def _mlstm_kernel(q_ref, k_ref, v_ref, o_ref, gcol_ref, grow_ref, ng_ref, out_ref, state_ref, m_ref):
    L = ML_CHUNK
    dh = ML_DH

    @pl.when(pl.program_id(1) == 0)
    def _():
        state_ref[...] = jnp.zeros_like(state_ref)
        m_ref[...] = jnp.zeros_like(m_ref)

    row_i = lax.broadcasted_iota(jnp.int32, (L, L), 0)
    col_i = lax.broadcasted_iota(jnp.int32, (L, L), 1)
    causal = row_i >= col_i
    tril = causal.astype(f32)
    triu = (row_i <= col_i).astype(f32)
    lane = lax.broadcasted_iota(jnp.int32, (L, dh), 1)
    ones_col = (lane == 0).astype(f32)
    g = ng_ref[...]

    for cc in range(ML_STEP_CHUNKS):
        rows = slice(cc * L, (cc + 1) * L)
        gcol = gcol_ref[rows, :]
        grow = grow_ref[:, rows]
        bcol_all = jnp.dot(tril, _log_sigmoid(gcol), precision=HIGHEST, preferred_element_type=f32)
        brow_all = jnp.dot(_log_sigmoid(grow), triu, precision=HIGHEST, preferred_element_type=f32)

        for hd in range(ML_HEADS):
            cols = slice(hd * dh, (hd + 1) * dh)
            qb = q_ref[rows, cols]
            kb = k_ref[rows, cols]
            v = v_ref[rows, cols].astype(f32)
            v_aug = jnp.concatenate([v, ones_col], axis=1)

            bcol = bcol_all[:, ML_HEADS + hd:ML_HEADS + hd + 1]
            icol = gcol[:, hd:hd + 1]
            brow = brow_all[ML_HEADS + hd:ML_HEADS + hd + 1, :]
            irow = grow[hd:hd + 1, :]
            m_prev = m_ref[hd:hd + 1, 0:1]

            d_intra = jnp.where(causal, bcol - brow + irow, NEG_BIG)
            d_inter = bcol + m_prev
            m_t = jnp.maximum(d_inter, jnp.max(d_intra, axis=1, keepdims=True))
            w_intra = jnp.exp(d_intra - m_t)
            w_inter = jnp.exp(d_inter - m_t)

            s = _dot_nt(qb, kb) * w_intra
            state = state_ref[hd]
            q_state = _dot(qb, state.astype(bf16))
            s_v = _dot(s.astype(bf16), v_aug.astype(bf16))
            num = w_inter * q_state[:, 0:dh] + s_v[:, 0:dh]
            den = w_inter * q_state[:, dh:dh + 1] + s_v[:, dh:dh + 1]
            hval = num / jnp.maximum(jnp.abs(den), jnp.exp(-m_t))

            b_last = bcol[L - 1:L, :]
            d_state = b_last - bcol + icol
            m_new = jnp.maximum(b_last + m_prev, jnp.max(d_state, axis=0, keepdims=True))
            carry_scale = jnp.exp(b_last + m_prev - m_new)
            wk = jnp.exp(d_state - m_new)
            k_t = kb.astype(f32).T.astype(bf16)
            upd = _dot(k_t, (v_aug * wk).astype(bf16))
            state_ref[hd] = carry_scale * state + upd
            m_ref[hd:hd + 1, :] = jnp.broadcast_to(m_new, (1, LANES))

            hn = hval * lax.rsqrt(jnp.mean(hval * hval, axis=-1, keepdims=True) + EPS) * g
            og = _sigmoid(o_ref[rows, cols].astype(f32))
            out_ref[rows, cols] = (hn * og).astype(bf16)


def _mlstm(pm, gcol, grow, ml_norm_g, batch, seq):
    t = pm.shape[0]
    rows = ML_STEP_CHUNKS * ML_CHUNK
    ns = seq // rows
    row = lambda b, c: b * ns + c
    return pl.pallas_call(
        _mlstm_kernel,
        out_shape=jax.ShapeDtypeStruct((t, ML_W), bf16),
        grid=(batch, ns),
        in_specs=[pl.BlockSpec((rows, ML_W), lambda b, c: (row(b, c), 0)),
                  pl.BlockSpec((rows, ML_W), lambda b, c: (row(b, c), 1)),
                  pl.BlockSpec((rows, ML_W), lambda b, c: (row(b, c), 2)),
                  pl.BlockSpec((rows, ML_W), lambda b, c: (row(b, c), 3)),
                  pl.BlockSpec((rows, LANES), lambda b, c: (row(b, c), 0)),
                  pl.BlockSpec((N_GATES, rows), lambda b, c: (0, row(b, c))),
                  pl.BlockSpec((1, ML_DH), lambda b, c: (0, 0))],
        out_specs=pl.BlockSpec((rows, ML_W), lambda b, c: (row(b, c), 0)),
        scratch_shapes=[pltpu.VMEM((ML_HEADS, ML_DH, 2 * ML_DH), f32),
                        pltpu.VMEM((SUBLANES, LANES), f32)],
        compiler_params=_cparams(2),
        name="mlstm",
    )(pm, pm, pm, pm, gcol, grow, ml_norm_g)


def _attn_block(q_ref, k_ref, v_ref, m_ref, l_ref, acc_ref, row0, nrows, nkeys, key0=None):
    rows = slice(row0, row0 + nrows)
    nch = nkeys // LANES
    v = v_ref[0:nkeys, :]
    if key0 is not None:
        keep = (key0 + lax.broadcasted_iota(jnp.int32, (nrows, nkeys), 1)
                <= row0 + lax.broadcasted_iota(jnp.int32, (nrows, nkeys), 0))
    for c in range(2):
        sl = slice(c * LANES, (c + 1) * LANES)
        s = _dot_nt(q_ref[rows, sl], k_ref[0:nkeys, sl])
        if key0 is not None:
            s = jnp.where(keep, s, NEG_BIG)
        chunks = [s[:, j * LANES:(j + 1) * LANES] for j in range(nch)]
        mc = chunks[0]
        for ch in chunks[1:]:
            mc = jnp.maximum(mc, ch)
        m_old = m_ref[c, rows, :]
        m_new = jnp.maximum(m_old, jnp.max(mc, axis=1, keepdims=True))
        alpha = jnp.exp2(m_old - m_new)
        ps = [jnp.exp2(ch - m_new) for ch in chunks]
        lsum = ps[0]
        for pj in ps[1:]:
            lsum = lsum + pj
        p = jnp.concatenate([pj.astype(bf16) for pj in ps], axis=1)
        l_ref[c, rows, :] = alpha * l_ref[c, rows, :] + lsum
        acc_ref[c, rows, :] = alpha * acc_ref[c, rows, :] + _dot(p, v)
        m_ref[c, rows, :] = m_new


def _attn_kernel(qt_ref, kt_ref, q_ref, k_ref, v_ref, lam_ref, g_ref, o_ref,
                 m_ref, l_ref, acc_ref, *, lam_init):
    step = pl.program_id(2)
    qi = qt_ref[step]
    ki = kt_ref[step]

    @pl.when(ki == 0)
    def _():
        m_ref[...] = jnp.full_like(m_ref, NEG_BIG)
        l_ref[...] = jnp.zeros_like(l_ref)
        acc_ref[...] = jnp.zeros_like(acc_ref)

    blk = ATT_BLOCK
    half = blk // 2
    qrows = ATT_QBLOCKS * blk
    first_diag = ATT_QBLOCKS * qi
    args = (q_ref, k_ref, v_ref, m_ref, l_ref, acc_ref)

    @pl.when(ki < first_diag)
    def _():
        _attn_block(*args, 0, qrows, blk)

    for j in range(ATT_QBLOCKS):
        @pl.when(ki == first_diag + j)
        def _(j=j):
            r0 = j * blk
            _attn_block(*args, r0, half, half, key0=r0)
            _attn_block(*args, r0 + half, half, blk, key0=r0)
            if r0 + blk < qrows:
                _attn_block(*args, r0 + blk, qrows - r0 - blk, blk)

    @pl.when(ki == first_diag + ATT_QBLOCKS - 1)
    def _():
        lam = lam_ref[...]
        l0 = jnp.sum(l_ref[0], axis=1, keepdims=True)
        l1 = jnp.sum(l_ref[1], axis=1, keepdims=True)
        o = acc_ref[0] / l0 - lam * (acc_ref[1] / l1)
        o = o * lax.rsqrt(jnp.mean(o * o, axis=-1, keepdims=True) + EPS)
        o_ref[...] = (o * g_ref[...] * (1.0 - lam_init)).astype(bf16)


def _attention(qh, kh, vh, lam, da_norm_g, batch, seq, lam_init):
    t = qh.shape[0]
    blk = ATT_BLOCK
    qrows = ATT_QBLOCKS * blk
    nk = seq // blk
    nq = seq // qrows
    pairs = [(i, j) for i in range(nq) for j in range(ATT_QBLOCKS * (i + 1))]
    qt = jnp.asarray([p[0] for p in pairs], jnp.int32)
    kt = jnp.asarray([p[1] for p in pairs], jnp.int32)
    w = 2 * DA_DH
    grid_spec = pltpu.PrefetchScalarGridSpec(
        num_scalar_prefetch=2,
        grid=(batch, DA_HEADS, len(pairs)),
        in_specs=[pl.BlockSpec((qrows, 2 * LANES), lambda b, h, s, qt, kt: (b * nq + qt[s], h)),
                  pl.BlockSpec((blk, 2 * LANES), lambda b, h, s, qt, kt: (b * nk + kt[s], h)),
                  pl.BlockSpec((blk, w), lambda b, h, s, qt, kt: (b * nk + kt[s], h)),
                  pl.BlockSpec((1, 1), lambda b, h, s, qt, kt: (0, 0)),
                  pl.BlockSpec((1, w), lambda b, h, s, qt, kt: (0, 0))],
        out_specs=pl.BlockSpec((qrows, w), lambda b, h, s, qt, kt: (b * nq + qt[s], h)),
        scratch_shapes=[pltpu.VMEM((2, qrows, LANES), f32),
                        pltpu.VMEM((2, qrows, LANES), f32),
                        pltpu.VMEM((2, qrows, w), f32)],
    )
    return pl.pallas_call(
        functools.partial(_attn_kernel, lam_init=lam_init),
        out_shape=jax.ShapeDtypeStruct((t, DA_W), bf16),
        grid_spec=grid_spec,
        compiler_params=_cparams(3),
        name="attn",
    )(qt, kt, qh, kh, vh, lam, da_norm_g)


def _first_index_of_max(x, iota_f, size):
    m = jnp.max(x, axis=0, keepdims=True)
    idx = jnp.min(jnp.where(x == m, iota_f, float(size)), axis=0, keepdims=True)
    return m, idx


def _route_kernel(hm_ref, hd_ref, x_ref, mod_ref, wo_ref, wr_ref, rb_ref,
                  x1_ref, ha_ref, hb_ref, idx_ref, gate_ref, rank_ref, cnt_ref, base_ref):
    tm = TM_ROUTE
    i = pl.program_id(0)

    @pl.when(i == 0)
    def _():
        base_ref[...] = jnp.zeros_like(base_ref)

    mix = _dot(hm_ref[...], wo_ref[0:ML_W, :]) + _dot(hd_ref[...], wo_ref[ML_W:ML_W + DA_W, :])
    x1 = x_ref[...] + mod_ref[0, 2:3, :] * mix
    x1_ref[...] = x1
    h2 = x1 * lax.rsqrt(jnp.mean(x1 * x1, axis=-1, keepdims=True) + EPS)
    h2 = h2 * (1.0 + mod_ref[0, 4:5, :]) + mod_ref[0, 3:4, :]
    packed = lax.bitcast_convert_type(_pack_bf16_pair(h2[:, 0:HALF], h2[:, HALF:D_MODEL]), jnp.int32)
    ha_ref[...] = packed[:, 0:QUARTER]
    hb_ref[...] = packed[:, QUARTER:HALF]

    sc = _sigmoid(_dot_nt(wr_ref[...], h2.astype(bf16)))
    sel = sc + rb_ref[...]

    gi = lax.broadcasted_iota(jnp.int32, (GROUP_SIZE, tm), 0).astype(f32)
    gscores = []
    for g in range(N_GROUPS):
        blk = sel[g * GROUP_SIZE:(g + 1) * GROUP_SIZE, :]
        m1, i1 = _first_index_of_max(blk, gi, GROUP_SIZE)
        m2 = jnp.max(jnp.where(gi == i1, NEG_BIG, blk), axis=0, keepdims=True)
        gscores.append(m1 + m2)
    gs = jnp.concatenate(gscores, axis=0)

    ngi = lax.broadcasted_iota(jnp.int32, (N_GROUPS, tm), 0).astype(f32)
    gkeep = jnp.zeros((N_GROUPS, tm), f32)
    for _ in range(TOPK_GROUPS):
        _, gidx = _first_index_of_max(gs, ngi, N_GROUPS)
        hit = ngi == gidx
        gkeep = jnp.where(hit, 1.0, gkeep)
        gs = jnp.where(hit, NEG_BIG, gs)
    masked = jnp.concatenate(
        [jnp.where(gkeep[g:g + 1, :] > 0.0, sel[g * GROUP_SIZE:(g + 1) * GROUP_SIZE, :], NEG_BIG)
         for g in range(N_GROUPS)], axis=0)

    ei = lax.broadcasted_iota(jnp.int32, (N_EXPERTS, tm), 0).astype(f32)
    idxs, gates = [], []
    chosen = jnp.zeros((N_EXPERTS, tm), f32)
    for _ in range(TOP_K):
        _, eidx = _first_index_of_max(masked, ei, N_EXPERTS)
        hit = ei == eidx
        idxs.append(eidx)
        gates.append(jnp.sum(jnp.where(hit, sc, 0.0), axis=0, keepdims=True))
        chosen = jnp.where(hit, 1.0, chosen)
        masked = jnp.where(hit, NEG_BIG, masked)
    gate = jnp.concatenate(gates, axis=0)
    gate = gate / jnp.sum(gate, axis=0, keepdims=True) * ROUTED_SCALE
    gate_ref[...] = gate
    idx_ref[...] = jnp.concatenate(idxs, axis=0).astype(jnp.int32)

    tr = lax.broadcasted_iota(jnp.int32, (tm, tm), 0)
    tc = lax.broadcasted_iota(jnp.int32, (tm, tm), 1)
    before = (tr < tc).astype(bf16)
    seen = base_ref[...] + _dot(chosen.astype(bf16), before)
    ranks = [jnp.sum(jnp.where(ei == eidx, seen, 0.0), axis=0, keepdims=True) for eidx in idxs]
    rank_ref[...] = jnp.concatenate(ranks, axis=0).astype(jnp.int32)
    total = base_ref[...] + jnp.sum(chosen, axis=1, keepdims=True)
    base_ref[...] = total
    cnt_ref[...] = total


def _route(hm, hd, x2, mod, w_out, w_router_t, rbias_col, seq):
    t, d = x2.shape
    tm = TM_ROUTE
    tiles_per_seq = seq // tm
    const = lambda shape: pl.BlockSpec(shape, lambda i: (0,) * len(shape))
    return pl.pallas_call(
        _route_kernel,
        out_shape=(jax.ShapeDtypeStruct((t, d), f32),
                   jax.ShapeDtypeStruct((t, QUARTER), jnp.int32),
                   jax.ShapeDtypeStruct((t, QUARTER), jnp.int32),
                   jax.ShapeDtypeStruct((TOP_K, t), jnp.int32),
                   jax.ShapeDtypeStruct((TOP_K, t), f32),
                   jax.ShapeDtypeStruct((TOP_K, t), jnp.int32),
                   jax.ShapeDtypeStruct((N_EXPERTS, 1), f32)),
        grid=(t // tm,),
        in_specs=[pl.BlockSpec((tm, ML_W), lambda i: (i, 0)),
                  pl.BlockSpec((tm, DA_W), lambda i: (i, 0)),
                  pl.BlockSpec((tm, d), lambda i: (i, 0)),
                  pl.BlockSpec((1, 6, d), lambda i: (i // tiles_per_seq, 0, 0)),
                  const((d, d)), const((N_EXPERTS, d)), const((N_EXPERTS, 1))],
        out_specs=(pl.BlockSpec((tm, d), lambda i: (i, 0)),
                   pl.BlockSpec((tm, QUARTER), lambda i: (i, 0)),
                   pl.BlockSpec((tm, QUARTER), lambda i: (i, 0)),
                   pl.BlockSpec((TOP_K, tm), lambda i: (0, i)),
                   pl.BlockSpec((TOP_K, tm), lambda i: (0, i)),
                   pl.BlockSpec((TOP_K, tm), lambda i: (0, i)),
                   const((N_EXPERTS, 1))),
        scratch_shapes=[pltpu.VMEM((N_EXPERTS, 1), f32)],
        compiler_params=_cparams(1),
        name="route",
    )(hm, hd, x2, mod, w_out, w_router_t, rbias_col)


SLOT_RADIX = 256
SLOT_DIGITS = 3


def _slots_kernel(idx_ref, rank_ref, dig_ref, dest_ref):
    tm = TM_SLOTS
    ei = lax.broadcasted_iota(jnp.int32, (N_EXPERTS, tm), 0)
    rows = []
    for k in range(TOP_K):
        onehot = jnp.where(ei == idx_ref[k:k + 1, :], 1.0, 0.0).astype(bf16)
        dg = _dot(dig_ref[...], onehot)
        start = dg[0:1, :]
        for j in range(1, SLOT_DIGITS):
            start = start + dg[j:j + 1, :] * float(SLOT_RADIX ** j)
        rows.append(start.astype(jnp.int32) + rank_ref[k:k + 1, :])
    dest_ref[...] = jnp.concatenate(rows, axis=0)


def _slots(idx, rank, digits):
    t = idx.shape[1]
    tm = TM_SLOTS
    return pl.pallas_call(
        _slots_kernel,
        out_shape=jax.ShapeDtypeStruct((TOP_K, t), jnp.int32),
        grid=(t // tm,),
        in_specs=[pl.BlockSpec((TOP_K, tm), lambda i: (0, i)),
                  pl.BlockSpec((TOP_K, tm), lambda i: (0, i)),
                  pl.BlockSpec((SUBLANES, N_EXPERTS), lambda i: (0, 0))],
        out_specs=pl.BlockSpec((TOP_K, tm), lambda i: (0, i)),
        compiler_params=_cparams(1),
        name="slots",
    )(idx, rank, digits)


def _sc_scatter_table(rows, dest, n_slots):
    n, width = rows.shape
    mesh = plsc.VectorSubcoreMesh(core_axis_name="core", subcore_axis_name="subcore")

    @pl.kernel(out_type=jax.ShapeDtypeStruct((n_slots, width), rows.dtype), mesh=mesh, scratch_types=[],
               name="sc_scatter")
    def scatter_kernel(rows_hbm, dest_hbm, out_hbm):
        def body(rows_vmem, dest_vmem):
            for k in range(TOP_K):
                pltpu.sync_copy(rows_vmem, out_hbm.at[dest_vmem.at[k]])

        pltpu.emit_pipeline(
            body,
            grid=(n // SC_WINDOW,),
            in_specs=[pl.BlockSpec((SC_WINDOW, width), lambda i: (i, 0)),
                      pl.BlockSpec((TOP_K, SC_WINDOW), lambda i: (0, i))],
            out_specs=[],
            core_axis_name=("core", "subcore"),
            dimension_semantics=(pltpu.PARALLEL,),
        )(rows_hbm, dest_hbm)

    return scatter_kernel(rows, dest)


EXPERT_GROUP = 4
EXPERT_RING = 16
EXPERT_AHEAD = EXPERT_RING - EXPERT_GROUP


def _expert_kernel(start_ref, cnt_ref, total_ref, xsa_ref, xsb_ref, w1_ref, w3_ref, w2_ref,
                   ysa_ref, ysb_ref, xbuf, ybuf, w1b, w3b, w2b, in_sem, out_sem):
    bm = MOE_BLOCK
    ring = EXPERT_RING
    ahead = EXPERT_AHEAD
    shift = int(math.log2(bm))
    e = pl.program_id(0)
    total = total_ref[0]
    cnt = cnt_ref[e]
    nb = lax.shift_right_logical(cnt + (bm - 1), shift)
    g0 = lax.shift_right_logical(start_ref[e], shift)

    def slot_of(g):
        return jnp.bitwise_and(g, ring - 1)

    def in_copies(g):
        s = slot_of(g)
        rows = pl.ds(pl.multiple_of(g * bm, bm), bm)
        return [pltpu.make_async_copy(x_ref.at[rows], xbuf.at[s, half], in_sem.at[s])
                for half, x_ref in enumerate((xsa_ref, xsb_ref))]

    def out_copies(g):
        s = slot_of(g)
        rows = pl.ds(pl.multiple_of(g * bm, bm), bm)
        return [pltpu.make_async_copy(ybuf.at[s, half], y_ref.at[rows], out_sem.at[s])
                for half, y_ref in enumerate((ysa_ref, ysb_ref))]

    def request(g):
        @pl.when(g < total)
        def _():
            for cp in in_copies(g):
                cp.start()

    def acquire(g):
        for cp in in_copies(g):
            cp.wait()

        @pl.when(g >= ring)
        def _():
            for cp in out_copies(g - ring):
                cp.wait()

    def compute(g):
        s = slot_of(g)
        row = (g - g0) * bm + lax.broadcasted_iota(jnp.int32, (bm, HALF), 0)
        words = jnp.concatenate([xbuf[s, 0], xbuf[s, 1]], axis=1)
        lo, hi = _unpack_bf16_pair(lax.bitcast_convert_type(jnp.where(row < cnt, words, 0), jnp.uint32))
        lo = lo.astype(bf16)
        hi = hi.astype(bf16)
        h1 = _dot(lo, w1b[0:HALF, :]) + _dot(hi, w1b[HALF:D_MODEL, :])
        h3 = _dot(lo, w3b[0:HALF, :]) + _dot(hi, w3b[HALF:D_MODEL, :])
        y = _dot((_silu(h1) * h3).astype(bf16), w2b[...])
        packed = lax.bitcast_convert_type(_pack_bf16_pair(y[:, 0:HALF], y[:, HALF:D_MODEL]), jnp.int32)
        ybuf[s, 0] = packed[:, 0:QUARTER]
        ybuf[s, 1] = packed[:, QUARTER:HALF]

    def release(g):
        for cp in out_copies(g):
            cp.start()

    @pl.when(e == 0)
    def _():
        for g in range(ahead):
            request(g)

    @pl.when(nb > 0)
    def _():
        w1b[...] = w1_ref[0].astype(bf16)
        w3b[...] = w3_ref[0].astype(bf16)
        w2b[...] = w2_ref[0].astype(bf16)

    def run(g, n):
        for d in range(n):
            request(g + d + ahead)
        for d in range(n):
            acquire(g + d)
        for d in range(n):
            compute(g + d)
        for d in range(n):
            release(g + d)

    def group(i, carry):
        run(g0 + EXPERT_GROUP * i, EXPERT_GROUP)
        return carry

    lax.fori_loop(0, lax.shift_right_logical(nb, int(math.log2(EXPERT_GROUP))), group, 0)

    n = EXPERT_GROUP // 2
    while n >= 1:
        @pl.when(jnp.bitwise_and(nb, n) != 0)
        def _(n=n):
            done = jnp.bitwise_and(nb, -2 * n)
            run(g0 + done, n)
        n //= 2

    @pl.when(e == pl.num_programs(0) - 1)
    def _():
        for back in range(1, ring + 1):
            @pl.when(total - back >= 0)
            def _():
                for cp in out_copies(total - back):
                    cp.wait()


def _experts(start, counts, total, xs_a, xs_b, w1, w3, w2):
    n_pad = xs_a.shape[0]
    bm = MOE_BLOCK
    grid_spec = pltpu.PrefetchScalarGridSpec(
        num_scalar_prefetch=3,
        grid=(N_EXPERTS,),
        in_specs=[pl.BlockSpec(memory_space=pl.ANY),
                  pl.BlockSpec(memory_space=pl.ANY),
                  pl.BlockSpec((1, D_MODEL, D_EXPERT), lambda e, st, ct, tt: (e, 0, 0)),
                  pl.BlockSpec((1, D_MODEL, D_EXPERT), lambda e, st, ct, tt: (e, 0, 0)),
                  pl.BlockSpec((1, D_EXPERT, D_MODEL), lambda e, st, ct, tt: (e, 0, 0))],
        out_specs=(pl.BlockSpec(memory_space=pl.ANY), pl.BlockSpec(memory_space=pl.ANY)),
        scratch_shapes=[pltpu.VMEM((EXPERT_RING, 2, bm, QUARTER), jnp.int32),
                        pltpu.VMEM((EXPERT_RING, 2, bm, QUARTER), jnp.int32),
                        pltpu.VMEM((D_MODEL, D_EXPERT), bf16),
                        pltpu.VMEM((D_MODEL, D_EXPERT), bf16),
                        pltpu.VMEM((D_EXPERT, D_MODEL), bf16),
                        pltpu.SemaphoreType.DMA((EXPERT_RING,)),
                        pltpu.SemaphoreType.DMA((EXPERT_RING,))],
    )
    return pl.pallas_call(
        _expert_kernel,
        out_shape=(jax.ShapeDtypeStruct((n_pad, QUARTER), jnp.int32),
                   jax.ShapeDtypeStruct((n_pad, QUARTER), jnp.int32)),
        grid_spec=grid_spec,
        compiler_params=_cparams(1),
        name="experts",
    )(start, counts, total, xs_a, xs_b, w1, w3, w2)


def _sc_gather_table(table, idx2):
    n = idx2.shape[1]
    width = table.shape[1]
    mesh = plsc.VectorSubcoreMesh(core_axis_name="core", subcore_axis_name="subcore")

    @pl.kernel(out_type=jax.ShapeDtypeStruct((n, width), table.dtype), mesh=mesh, scratch_types=[],
               name="sc_gather")
    def gather_kernel(tab_hbm, idx_hbm, out_hbm):
        def body(idx_vmem, out_vmem):
            pltpu.sync_copy(tab_hbm.at[idx_vmem.at[0]], out_vmem)

        pltpu.emit_pipeline(
            body,
            grid=(n // SC_WINDOW,),
            in_specs=[pl.BlockSpec((1, SC_WINDOW), lambda i: (0, i))],
            out_specs=[pl.BlockSpec((SC_WINDOW, width), lambda i: (i, 0))],
            core_axis_name=("core", "subcore"),
            dimension_semantics=(pltpu.PARALLEL,),
        )(idx_hbm, out_hbm)

    return gather_kernel(table, idx2)


def _sc_gather_rows(table_a, table_b, idx):
    idx2 = idx.reshape(1, -1)
    return _sc_gather_table(table_a, idx2), _sc_gather_table(table_b, idx2)


def _combine_kernel(gate_ref, ha_ref, hb_ref, x1_ref, mod_ref, ws1_ref, ws3_ref, ws2_ref, ga_ref, gb_ref,
                    out_ref):
    words = jnp.concatenate([ha_ref[...], hb_ref[...]], axis=1)
    lo, hi = _unpack_bf16_pair(lax.bitcast_convert_type(words, jnp.uint32))
    lo = lo.astype(bf16)
    hi = hi.astype(bf16)
    s1 = _dot(lo, ws1_ref[0:HALF, :]) + _dot(hi, ws1_ref[HALF:D_MODEL, :])
    s3 = _dot(lo, ws3_ref[0:HALF, :]) + _dot(hi, ws3_ref[HALF:D_MODEL, :])
    y = _dot((_silu(s1) * s3).astype(bf16), ws2_ref[...])

    gate = gate_ref[...]
    gate_f = mod_ref[0, 5:6, :]
    for part, g_ref in enumerate((ga_ref, gb_ref)):
        c_lo = slice(part * QUARTER, (part + 1) * QUARTER)
        c_hi = slice(HALF + part * QUARTER, HALF + (part + 1) * QUARTER)
        acc_lo = y[:, c_lo]
        acc_hi = y[:, c_hi]
        for k in range(TOP_K):
            rlo, rhi = _unpack_bf16_pair(lax.bitcast_convert_type(g_ref[k], jnp.uint32))
            gk = gate[:, k:k + 1]
            acc_lo = acc_lo + gk * rlo
            acc_hi = acc_hi + gk * rhi
        out_ref[:, c_lo] = x1_ref[:, c_lo] + gate_f[:, c_lo] * acc_lo
        out_ref[:, c_hi] = x1_ref[:, c_hi] + gate_f[:, c_hi] * acc_hi


def _combine(gate_col, h2a, h2b, x1, mod, ws1, ws3, ws2, ga, gb, seq):
    t, d = x1.shape
    tm = TM_MOE
    tiles_per_seq = seq // tm
    const = lambda shape: pl.BlockSpec(shape, lambda i: (0,) * len(shape))
    return pl.pallas_call(
        _combine_kernel,
        out_shape=jax.ShapeDtypeStruct((t, d), f32),
        grid=(t // tm,),
        in_specs=[pl.BlockSpec((tm, TOP_K), lambda i: (i, 0)),
                  pl.BlockSpec((tm, QUARTER), lambda i: (i, 0)),
                  pl.BlockSpec((tm, QUARTER), lambda i: (i, 0)),
                  pl.BlockSpec((tm, d), lambda i: (i, 0)),
                  pl.BlockSpec((1, 6, d), lambda i: (i // tiles_per_seq, 0, 0)),
                  const((d, D_EXPERT)), const((d, D_EXPERT)), const((D_EXPERT, d)),
                  pl.BlockSpec((TOP_K, tm, QUARTER), lambda i: (0, i, 0)),
                  pl.BlockSpec((TOP_K, tm, QUARTER), lambda i: (0, i, 0))],
        out_specs=pl.BlockSpec((tm, d), lambda i: (i, 0)),
        compiler_params=_cparams(1),
        name="combine",
    )(gate_col, h2a, h2b, x1, mod, ws1, ws3, ws2, ga, gb)


def _lambda_init(layer):
    return 0.8 - 0.6 * math.exp(-0.3 * layer)


def _layer(x, c, w_ada, b_ada, w_in, conv_w, conv_b, gate_b, ml_norm_g, da_q_norm_g, da_k_norm_g,
           lambda_q1, lambda_k1, lambda_q2, lambda_k2, da_norm_g, w_out, w_router, router_bias,
           w1, w3, w2, ws1, ws3, ws2, layer):
    batch, seq, d = x.shape
    t = batch * seq
    lam_init = _lambda_init(layer)
    x2 = x.reshape(t, d)

    c_pad = jnp.pad(c, ((0, -batch % SUBLANES), (0, 0)))
    mod = _adaln(c_pad, w_ada, b_ada.reshape(1, -1))[:batch].reshape(batch, 6, d)

    g0 = 4 * ML_W
    q0 = g0 + N_GATES
    n_grp = 2 * DA_HEADS
    w_main = jnp.concatenate([w_in[:, :g0], w_in[:, q0:]], axis=1).astype(bf16)
    wg = jnp.pad(w_in[:, g0:g0 + N_GATES], ((0, 0), (0, LANES - N_GATES))).astype(bf16)
    gb_row = jnp.pad(gate_b, (0, LANES - N_GATES)).reshape(1, LANES)
    qgain = (jnp.tile(da_q_norm_g, n_grp) * (DA_DH ** -0.5 * LOG2E)).reshape(1, DA_W)
    kgain = jnp.tile(da_k_norm_g, n_grp).reshape(1, DA_W)
    seg = jnp.arange(DA_W) // DA_DH
    bd = (seg[:, None] == seg[None, :]).astype(bf16)

    pos = jnp.arange(seq, dtype=f32)[:, None]
    slopes = jnp.asarray([_alibi_slope_log2(h) for h in range(DA_HEADS)], f32)[None, :]
    r = pos * slopes

    def bf16_part(a):
        bits = lax.bitcast_convert_type(a, jnp.uint32) & jnp.uint32(0xFFFF0000)
        return lax.bitcast_convert_type(bits, f32)

    r_hi = bf16_part(r)
    r_mid = bf16_part(r - r_hi)
    r_lo = bf16_part(r - r_hi - r_mid)
    parts = jnp.stack([r_hi, r_mid, r_lo], axis=-1)
    kaug = jnp.pad(parts, ((0, 0), (0, 0), (DA_DH, LANES - DA_DH - 3))).reshape(seq, DA_HEADS * LANES)

    pm, qh, kh, vh, gcol, grow = _inproj(x2, mod, w_main, wg, gb_row, qgain, kgain, bd, kaug.astype(bf16),
                                         conv_w, conv_b.reshape(1, -1), seq)

    hm = _mlstm(pm, gcol, grow, ml_norm_g.reshape(1, -1), batch, seq)

    lam = (jnp.exp(jnp.sum(lambda_q1 * lambda_k1)) - jnp.exp(jnp.sum(lambda_q2 * lambda_k2))
           + lam_init).reshape(1, 1).astype(f32)
    hd = _attention(qh, kh, vh, lam, da_norm_g.reshape(1, -1), batch, seq, lam_init)

    x1, h2a, h2b, idx, gate, rank, counts = _route(hm, hd, x2, mod, w_out.astype(bf16),
                                                   w_router.T.astype(bf16), router_bias.reshape(-1, 1), seq)

    bm = MOE_BLOCK
    n_blocks = (t * TOP_K) // bm + N_EXPERTS
    counts_i = counts.reshape(-1).astype(jnp.int32)
    padded = (counts_i + bm - 1) // bm * bm
    pad_end = jnp.cumsum(padded)
    pad_start = pad_end - padded
    digits = jnp.stack([(pad_start // SLOT_RADIX ** j) % SLOT_RADIX for j in range(SLOT_DIGITS)])
    digits = jnp.pad(digits, ((0, SUBLANES - SLOT_DIGITS), (0, 0))).astype(bf16)
    dest = _slots(idx, rank, digits)

    xs_a = _sc_scatter_table(h2a, dest, n_blocks * bm)
    xs_b = _sc_scatter_table(h2b, dest, n_blocks * bm)
    total_blocks = (pad_end[-1:] // bm).astype(jnp.int32)
    ys_a, ys_b = _experts(pad_start.astype(jnp.int32), counts_i, total_blocks, xs_a, xs_b, w1, w3, w2)
    ga, gb = _sc_gather_rows(ys_a, ys_b, dest.reshape(-1))
    out = _combine(gate.T, h2a, h2b, x1, mod, ws1.astype(bf16), ws3.astype(bf16), ws2.astype(bf16),
                   ga.reshape(TOP_K, t, QUARTER), gb.reshape(TOP_K, t, QUARTER), seq)
    return out.reshape(batch, seq, d)


def kernel(x, c, w_ada, b_ada, w_in, conv_w, conv_b, gate_b, ml_norm_g, da_q_norm_g, da_k_norm_g,
           lambda_q1, lambda_k1, lambda_q2, lambda_k2, da_norm_g, w_out, w_router, router_bias,
           w1, w3, w2, ws1, ws3, ws2):
    depth = w_ada.shape[0]
    for l in range(depth):
        x = _layer(x, c, w_ada[l], b_ada[l], w_in[l], conv_w[l], conv_b[l], gate_b[l], ml_norm_g[l],
                   da_q_norm_g[l], da_k_norm_g[l], lambda_q1[l], lambda_k1[l], lambda_q2[l], lambda_k2[l],
                   da_norm_g[l], w_out[l], w_router[l], router_bias[l], w1[l], w3[l], w2[l],
                   ws1[l], ws3[l], ws2[l], l)
    return x
```

```python
import functools
import math

import jax
import jax.numpy as jnp
from jax import lax
from jax.experimental import pallas as pl
from jax.experimental.pallas import tpu as pltpu
from jax.experimental.pallas import tpu_sc as plsc

D_MODEL = 1024
ML_HEADS = 4
ML_DH = 128
ML_W = ML_HEADS * ML_DH
ML_CHUNK = 128
CONV_K = 4
DA_HEADS = 4
DA_DH = 64
DA_W = DA_HEADS * 2 * DA_DH
N_EXPERTS = 256
TOP_K = 8
N_GROUPS = 8
GROUP_SIZE = N_EXPERTS // N_GROUPS
TOPK_GROUPS = 4
D_EXPERT = 256
ROUTED_SCALE = 2.5
EPS = 1e-6
N_GATES = 2 * ML_HEADS

LANES = 128
SUBLANES = 8
VMEM_LIMIT_BYTES = 48 * 1024 * 1024

TM_INPROJ = 512
ML_STEP_CHUNKS = 4
ATT_BLOCK = 1024
ATT_QBLOCKS = 2
TM_ROUTE = 256
TM_SLOTS = 512
TM_MOE = 512
MOE_BLOCK = 256
HALF = D_MODEL // 2
QUARTER = HALF // 2
SC_WINDOW = 128

NEG_BIG = -1e30
LOG2E = 1.4426950408889634

f32 = jnp.float32
bf16 = jnp.bfloat16
HIGHEST = lax.Precision.HIGHEST


def _cparams(n_axes):
    return pltpu.CompilerParams(dimension_semantics=("arbitrary",) * n_axes,
                                vmem_limit_bytes=VMEM_LIMIT_BYTES)


def _dot(a, b):
    return jnp.dot(a, b, preferred_element_type=f32)


def _dot_nt(a, b):
    return lax.dot_general(a, b, (((1,), (1,)), ((), ())), preferred_element_type=f32)


def _sigmoid(x):
    return 1.0 / (1.0 + jnp.exp(-x))


def _silu(x):
    return x * _sigmoid(x)


def _log_sigmoid(x):
    return jnp.minimum(x, 0.0) - jnp.log(1.0 + jnp.exp(-jnp.abs(x)))


def _pack_bf16_pair(lo, hi):
    lo_bits = lax.bitcast_convert_type(lo.astype(bf16).astype(f32), jnp.uint32)
    hi_bits = lax.bitcast_convert_type(hi.astype(bf16).astype(f32), jnp.uint32)
    return (hi_bits & jnp.uint32(0xFFFF0000)) | (lo_bits >> 16)


def _unpack_bf16_pair(w):
    lo = lax.bitcast_convert_type(w << 16, f32)
    hi = lax.bitcast_convert_type(w & jnp.uint32(0xFFFF0000), f32)
    return lo, hi


def _adaln_kernel(c_ref, w_ref, b_ref, o_ref):
    c = c_ref[...]
    o_ref[...] = jnp.dot(_silu(c), w_ref[...], precision=HIGHEST, preferred_element_type=f32) + b_ref[...]


def _adaln(c_pad, w_ada, b_ada):
    rows, d = c_pad.shape
    n = w_ada.shape[1]
    tn = 1536
    return pl.pallas_call(
        _adaln_kernel,
        out_shape=jax.ShapeDtypeStruct((rows, n), f32),
        grid=(n // tn,),
        in_specs=[pl.BlockSpec((rows, d), lambda j: (0, 0)),
                  pl.BlockSpec((d, tn), lambda j: (0, j)),
                  pl.BlockSpec((1, tn), lambda j: (0, j))],
        out_specs=pl.BlockSpec((rows, tn), lambda j: (0, j)),
        compiler_params=_cparams(1),
        name="adaln",
    )(c_pad, w_ada, b_ada)


def _alibi_slope_log2(head):
    return 2.0 ** (-8.0 * (head + 1) / DA_HEADS) * LOG2E


def _inproj_kernel(x_ref, mod_ref, w_ref, wg_ref, gbr_ref, qg_ref, kg_ref, bd_ref, kaug_ref, cw_ref, cb_ref,
                   pm_ref, vt_ref, ot_ref, q_ref, k_ref, v_ref, gcol_ref, grow_ref, ext_ref, *, seq):
    tm = TM_INPROJ
    x = x_ref[...]
    h = x * lax.rsqrt(jnp.mean(x * x, axis=-1, keepdims=True) + EPS)
    h = h * (1.0 + mod_ref[0, 1:2, :]) + mod_ref[0, 0:1, :]
    hb = h.astype(bf16)
    cw = ML_W

    @pl.when((pl.program_id(0) * tm) % seq == 0)
    def _():
        ext_ref[0:SUBLANES, :] = jnp.zeros((SUBLANES, 2 * cw), f32)

    cur = _dot(hb, w_ref[:, 0:2 * cw])
    ext_ref[SUBLANES:SUBLANES + tm, :] = cur
    acc = cb_ref[...] + cw_ref[CONV_K - 1:CONV_K, :] * cur
    for j in range(CONV_K - 1):
        off = SUBLANES - (CONV_K - 1) + j
        acc = acc + cw_ref[j:j + 1, :] * ext_ref[off:off + tm, :]
    ext_ref[0:SUBLANES, :] = cur[tm - SUBLANES:tm, :]
    qk = _silu(acc)
    pm_ref[:, 0:cw] = qk[:, 0:cw].astype(bf16)
    pm_ref[:, cw:2 * cw] = (qk[:, cw:2 * cw] * (ML_DH ** -0.5)).astype(bf16)
    vt_ref[...] = _dot(hb, w_ref[:, 2 * cw:3 * cw]).T.astype(bf16)
    ot_ref[...] = _dot(hb, w_ref[:, 3 * cw:4 * cw]).T.astype(bf16)

    lane = lax.broadcasted_iota(jnp.int32, (tm, LANES), 1)
    feat = lane < DA_DH
    q_aug = jnp.where(jnp.logical_and(lane >= DA_DH, lane < DA_DH + 3), 1.0, 0.0)
    q0 = 4 * cw
    k0 = q0 + DA_W

    def qk_normed(col0, gain_ref):
        y = _dot(hb, w_ref[:, col0:col0 + DA_W])
        ss = _dot((y * y).astype(bf16), bd_ref[...])
        return y * lax.rsqrt(ss * (1.0 / DA_DH) + EPS) * gain_ref[...]

    def lane_group(y2, grp, aug):
        pair = y2[:, (grp // 2) * LANES:(grp // 2 + 1) * LANES]
        if grp % 2:
            pair = pltpu.roll(pair, DA_DH, axis=1)
        return jnp.where(feat, pair, aug).astype(bf16)

    nq = qk_normed(q0, qg_ref)
    nk = qk_normed(k0, kg_ref)
    for grp in range(2 * DA_HEADS):
        sl = slice(grp * LANES, (grp + 1) * LANES)
        head = slice((grp // 2) * LANES, (grp // 2 + 1) * LANES)
        q_ref[:, sl] = lane_group(nq, grp, q_aug)
        k_ref[:, sl] = lane_group(nk, grp, kaug_ref[:, head].astype(f32))

    v0 = k0 + DA_W
    v_ref[...] = _dot(hb, w_ref[:, v0:v0 + DA_W]).astype(bf16)
    gates = _dot(hb, wg_ref[...]) + gbr_ref[...]
    gcol_ref[...] = gates
    grow_ref[...] = gates.T[0:N_GATES, :]


def _inproj(x2, mod, w_main, wg, gb_row, qgain, kgain, bd, kaug, conv_w, conv_b, seq):
    t, d = x2.shape
    tm = TM_INPROJ
    tiles_per_seq = seq // tm
    n_main = w_main.shape[1]
    const = lambda shape: pl.BlockSpec(shape, lambda i: (0,) * len(shape))
    return pl.pallas_call(
        functools.partial(_inproj_kernel, seq=seq),
        out_shape=(jax.ShapeDtypeStruct((t, 2 * ML_W), bf16),
                   jax.ShapeDtypeStruct((ML_W, t), bf16),
                   jax.ShapeDtypeStruct((ML_W, t), bf16),
                   jax.ShapeDtypeStruct((t, 2 * DA_W), bf16),
                   jax.ShapeDtypeStruct((t, 2 * DA_W), bf16),
                   jax.ShapeDtypeStruct((t, DA_W), bf16),
                   jax.ShapeDtypeStruct((t, LANES), f32),
                   jax.ShapeDtypeStruct((N_GATES, t), f32)),
        grid=(t // tm,),
        in_specs=[pl.BlockSpec((tm, d), lambda i: (i, 0)),
                  pl.BlockSpec((1, 6, d), lambda i: (i // tiles_per_seq, 0, 0)),
                  const((d, n_main)), const((d, LANES)), const((1, LANES)),
                  const((1, DA_W)), const((1, DA_W)), const((DA_W, DA_W)),
                  pl.BlockSpec((tm, DA_HEADS * LANES), lambda i: (i % tiles_per_seq, 0)),
                  const((CONV_K, 2 * ML_W)), const((1, 2 * ML_W))],
        out_specs=(pl.BlockSpec((tm, 2 * ML_W), lambda i: (i, 0)),
                   pl.BlockSpec((ML_W, tm), lambda i: (0, i)),
                   pl.BlockSpec((ML_W, tm), lambda i: (0, i)),
                   pl.BlockSpec((tm, 2 * DA_W), lambda i: (i, 0)),
                   pl.BlockSpec((tm, 2 * DA_W), lambda i: (i, 0)),
                   pl.BlockSpec((tm, DA_W), lambda i: (i, 0)),
                   pl.BlockSpec((tm, LANES), lambda i: (i, 0)),
                   pl.BlockSpec((N_GATES, tm), lambda i: (0, i))),
        scratch_shapes=[pltpu.VMEM((SUBLANES + tm, 2 * ML_W), f32)],
        compiler_params=_cparams(1),
        name="inproj",
    )(x2, mod, w_main, wg, gb_row, qgain, kgain, bd, kaug, conv_w, conv_b)


def _mlstm_kernel(q_ref, k_ref, vt_ref, ot_ref, gcol_ref, grow_ref, ng_ref, out_ref, state_ref, m_ref):
    L = ML_CHUNK
    dh = ML_DH

    @pl.when(pl.program_id(1) == 0)
    def _():
        state_ref[...] = jnp.zeros_like(state_ref)
        m_ref[...] = jnp.zeros_like(m_ref)

    s_i = lax.broadcasted_iota(jnp.int32, (L, L), 0)
    t_i = lax.broadcasted_iota(jnp.int32, (L, L), 1)
    visible = s_i <= t_i
    tril = (s_i >= t_i).astype(f32)
    triu = visible.astype(f32)
    ones_rows = (lax.broadcasted_iota(jnp.int32, (dh, L), 0) == 0).astype(f32)
    g_b = jnp.broadcast_to(ng_ref[...], (dh, L))

    for cc in range(ML_STEP_CHUNKS):
        rows = slice(cc * L, (cc + 1) * L)
        gcol = gcol_ref[rows, :]
        grow = grow_ref[:, rows]
        bcol_all = jnp.dot(tril, _log_sigmoid(gcol), precision=HIGHEST, preferred_element_type=f32)
        brow_all = jnp.dot(_log_sigmoid(grow), triu, precision=HIGHEST, preferred_element_type=f32)

        for hd in range(ML_HEADS):
            cols = slice(hd * dh, (hd + 1) * dh)
            qb = q_ref[rows, cols]
            kb = k_ref[rows, cols]
            v_aug_t = jnp.concatenate([vt_ref[cols, rows].astype(f32), ones_rows], axis=0)

            brow = brow_all[ML_HEADS + hd:ML_HEADS + hd + 1, :]
            irow = grow[hd:hd + 1, :]
            key_term = gcol[:, hd:hd + 1] - bcol_all[:, ML_HEADS + hd:ML_HEADS + hd + 1]
            m_prev = m_ref[hd:hd + 1, :]

            d_intra = jnp.where(visible, brow + key_term, NEG_BIG)
            d_inter = brow + m_prev
            m_t = jnp.maximum(d_inter, jnp.max(d_intra, axis=0, keepdims=True))
            w_intra = jnp.exp(d_intra - m_t)
            w_inter = jnp.exp(d_inter - m_t)

            s_t = _dot_nt(kb, qb) * w_intra
            state = state_ref[hd]
            q_state = _dot_nt(state.astype(bf16), qb)
            s_v = _dot(v_aug_t.astype(bf16), s_t.astype(bf16))
            num = w_inter * q_state[0:dh, :] + s_v[0:dh, :]
            den = w_inter * q_state[dh:dh + 1, :] + s_v[dh:dh + 1, :]
            hval = num / jnp.maximum(jnp.abs(den), jnp.exp(-m_t))

            b_last = brow[:, L - 1:L]
            d_state = b_last - brow + irow
            m_new = jnp.maximum(b_last + m_prev, jnp.max(d_state, axis=1, keepdims=True))
            carry_scale = jnp.exp(b_last + m_prev - m_new)[:, 0:1]
            wk = jnp.exp(d_state - m_new)
            upd = _dot((v_aug_t * wk).astype(bf16), kb)
            state_ref[hd] = carry_scale * state + upd
            m_ref[hd:hd + 1, :] = m_new

            hn = hval * lax.rsqrt(jnp.mean(hval * hval, axis=0, keepdims=True) + EPS) * g_b
            og = _sigmoid(ot_ref[cols, rows].astype(f32))
            out_ref[rows, cols] = (hn * og).T.astype(bf16)


def _mlstm(pm, vt, ot, gcol, grow, ml_norm_g, batch, seq):
    t = pm.shape[0]
    rows = ML_STEP_CHUNKS * ML_CHUNK
    ns = seq // rows
    row = lambda b, c: b * ns + c
    return pl.pallas_call(
        _mlstm_kernel,
        out_shape=jax.ShapeDtypeStruct((t, ML_W), bf16),
        grid=(batch, ns),
        in_specs=[pl.BlockSpec((rows, ML_W), lambda b, c: (row(b, c), 0)),
                  pl.BlockSpec((rows, ML_W), lambda b, c: (row(b, c), 1)),
                  pl.BlockSpec((ML_W, rows), lambda b, c: (0, row(b, c))),
                  pl.BlockSpec((ML_W, rows), lambda b, c: (0, row(b, c))),
                  pl.BlockSpec((rows, LANES), lambda b, c: (row(b, c), 0)),
                  pl.BlockSpec((N_GATES, rows), lambda b, c: (0, row(b, c))),
                  pl.BlockSpec((ML_DH, 1), lambda b, c: (0, 0))],
        out_specs=pl.BlockSpec((rows, ML_W), lambda b, c: (row(b, c), 0)),
        scratch_shapes=[pltpu.VMEM((ML_HEADS, 2 * ML_DH, ML_DH), f32),
                        pltpu.VMEM((SUBLANES, LANES), f32)],
        compiler_params=_cparams(2),
        name="mlstm",
    )(pm, pm, vt, ot, gcol, grow, ml_norm_g)


def _attn_block(q_ref, k_ref, v_ref, m_ref, l_ref, acc_ref, row0, nrows, nkeys, key0=None):
    rows = slice(row0, row0 + nrows)
    nch = nkeys // LANES
    v = v_ref[0:nkeys, :]
    if key0 is not None:
        keep = (key0 + lax.broadcasted_iota(jnp.int32, (nrows, nkeys), 1)
                <= row0 + lax.broadcasted_iota(jnp.int32, (nrows, nkeys), 0))
    for c in range(2):
        sl = slice(c * LANES, (c + 1) * LANES)
        s = _dot_nt(q_ref[rows, sl], k_ref[0:nkeys, sl])
        if key0 is not None:
            s = jnp.where(keep, s, NEG_BIG)
        chunks = [s[:, j * LANES:(j + 1) * LANES] for j in range(nch)]
        mc = chunks[0]
        for ch in chunks[1:]:
            mc = jnp.maximum(mc, ch)
        m_old = m_ref[c, rows, :]
        m_new = jnp.maximum(m_old, jnp.max(mc, axis=1, keepdims=True))
        alpha = jnp.exp2(m_old - m_new)
        ps = [jnp.exp2(ch - m_new) for ch in chunks]
        lsum = ps[0]
        for pj in ps[1:]:
            lsum = lsum + pj
        p = jnp.concatenate([pj.astype(bf16) for pj in ps], axis=1)
        l_ref[c, rows, :] = alpha * l_ref[c, rows, :] + lsum
        acc_ref[c, rows, :] = alpha * acc_ref[c, rows, :] + _dot(p, v)
        m_ref[c, rows, :] = m_new


def _attn_kernel(qt_ref, kt_ref, q_ref, k_ref, v_ref, lam_ref, g_ref, o_ref,
                 m_ref, l_ref, acc_ref, *, lam_init):
    step = pl.program_id(2)
    qi = qt_ref[step]
    ki = kt_ref[step]

    @pl.when(ki == 0)
    def _():
        m_ref[...] = jnp.full_like(m_ref, NEG_BIG)
        l_ref[...] = jnp.zeros_like(l_ref)
        acc_ref[...] = jnp.zeros_like(acc_ref)

    blk = ATT_BLOCK
    half = blk // 2
    qrows = ATT_QBLOCKS * blk
    first_diag = ATT_QBLOCKS * qi
    args = (q_ref, k_ref, v_ref, m_ref, l_ref, acc_ref)

    @pl.when(ki < first_diag)
    def _():
        _attn_block(*args, 0, qrows, blk)

    for j in range(ATT_QBLOCKS):
        @pl.when(ki == first_diag + j)
        def _(j=j):
            r0 = j * blk
            _attn_block(*args, r0, half, half, key0=r0)
            _attn_block(*args, r0 + half, half, blk, key0=r0)
            if r0 + blk < qrows:
                _attn_block(*args, r0 + blk, qrows - r0 - blk, blk)

    @pl.when(ki == first_diag + ATT_QBLOCKS - 1)
    def _():
        lam = lam_ref[...]
        l0 = jnp.sum(l_ref[0], axis=1, keepdims=True)
        l1 = jnp.sum(l_ref[1], axis=1, keepdims=True)
        o = acc_ref[0] / l0 - lam * (acc_ref[1] / l1)
        o = o * lax.rsqrt(jnp.mean(o * o, axis=-1, keepdims=True) + EPS)
        o_ref[...] = (o * g_ref[...] * (1.0 - lam_init)).astype(bf16)


def _attention(qh, kh, vh, lam, da_norm_g, batch, seq, lam_init):
    t = qh.shape[0]
    blk = ATT_BLOCK
    qrows = ATT_QBLOCKS * blk
    nk = seq // blk
    nq = seq // qrows
    pairs = [(i, j) for i in range(nq) for j in range(ATT_QBLOCKS * (i + 1))]
    qt = jnp.asarray([p[0] for p in pairs], jnp.int32)
    kt = jnp.asarray([p[1] for p in pairs], jnp.int32)
    w = 2 * DA_DH
    grid_spec = pltpu.PrefetchScalarGridSpec(
        num_scalar_prefetch=2,
        grid=(batch, DA_HEADS, len(pairs)),
        in_specs=[pl.BlockSpec((qrows, 2 * LANES), lambda b, h, s, qt, kt: (b * nq + qt[s], h)),
                  pl.BlockSpec((blk, 2 * LANES), lambda b, h, s, qt, kt: (b * nk + kt[s], h)),
                  pl.BlockSpec((blk, w), lambda b, h, s, qt, kt: (b * nk + kt[s], h)),
                  pl.BlockSpec((1, 1), lambda b, h, s, qt, kt: (0, 0)),
                  pl.BlockSpec((1, w), lambda b, h, s, qt, kt: (0, 0))],
        out_specs=pl.BlockSpec((qrows, w), lambda b, h, s, qt, kt: (b * nq + qt[s], h)),
        scratch_shapes=[pltpu.VMEM((2, qrows, LANES), f32),
                        pltpu.VMEM((2, qrows, LANES), f32),
                        pltpu.VMEM((2, qrows, w), f32)],
    )
    return pl.pallas_call(
        functools.partial(_attn_kernel, lam_init=lam_init),
        out_shape=jax.ShapeDtypeStruct((t, DA_W), bf16),
        grid_spec=grid_spec,
        compiler_params=_cparams(3),
        name="attn",
    )(qt, kt, qh, kh, vh, lam, da_norm_g)


def _first_index_of_max(x, iota_f, size):
    m = jnp.max(x, axis=0, keepdims=True)
    idx = jnp.min(jnp.where(x == m, iota_f, float(size)), axis=0, keepdims=True)
    return m, idx


def _route_kernel(hm_ref, hd_ref, x_ref, mod_ref, wo_ref, wr_ref, rb_ref,
                  x1_ref, ha_ref, hb_ref, idx_ref, gate_ref, rank_ref, cnt_ref, base_ref):
    tm = TM_ROUTE
    i = pl.program_id(0)

    @pl.when(i == 0)
    def _():
        base_ref[...] = jnp.zeros_like(base_ref)

    mix = _dot(hm_ref[...], wo_ref[0:ML_W, :]) + _dot(hd_ref[...], wo_ref[ML_W:ML_W + DA_W, :])
    x1 = x_ref[...] + mod_ref[0, 2:3, :] * mix
    x1_ref[...] = x1
    h2 = x1 * lax.rsqrt(jnp.mean(x1 * x1, axis=-1, keepdims=True) + EPS)
    h2 = h2 * (1.0 + mod_ref[0, 4:5, :]) + mod_ref[0, 3:4, :]
    packed = lax.bitcast_convert_type(_pack_bf16_pair(h2[:, 0:HALF], h2[:, HALF:D_MODEL]), jnp.int32)
    ha_ref[...] = packed[:, 0:QUARTER]
    hb_ref[...] = packed[:, QUARTER:HALF]

    sc = _sigmoid(_dot_nt(wr_ref[...], h2.astype(bf16)))
    sel = sc + rb_ref[...]

    gi = lax.broadcasted_iota(jnp.int32, (GROUP_SIZE, tm), 0).astype(f32)
    gscores = []
    for g in range(N_GROUPS):
        blk = sel[g * GROUP_SIZE:(g + 1) * GROUP_SIZE, :]
        m1, i1 = _first_index_of_max(blk, gi, GROUP_SIZE)
        m2 = jnp.max(jnp.where(gi == i1, NEG_BIG, blk), axis=0, keepdims=True)
        gscores.append(m1 + m2)
    gs = jnp.concatenate(gscores, axis=0)

    ngi = lax.broadcasted_iota(jnp.int32, (N_GROUPS, tm), 0).astype(f32)
    gkeep = jnp.zeros((N_GROUPS, tm), f32)
    for _ in range(TOPK_GROUPS):
        _, gidx = _first_index_of_max(gs, ngi, N_GROUPS)
        hit = ngi == gidx
        gkeep = jnp.where(hit, 1.0, gkeep)
        gs = jnp.where(hit, NEG_BIG, gs)
    masked = jnp.concatenate(
        [jnp.where(gkeep[g:g + 1, :] > 0.0, sel[g * GROUP_SIZE:(g + 1) * GROUP_SIZE, :], NEG_BIG)
         for g in range(N_GROUPS)], axis=0)

    ei = lax.broadcasted_iota(jnp.int32, (N_EXPERTS, tm), 0).astype(f32)
    idxs, gates = [], []
    chosen = jnp.zeros((N_EXPERTS, tm), f32)
    for _ in range(TOP_K):
        _, eidx = _first_index_of_max(masked, ei, N_EXPERTS)
        hit = ei == eidx
        idxs.append(eidx)
        gates.append(jnp.sum(jnp.where(hit, sc, 0.0), axis=0, keepdims=True))
        chosen = jnp.where(hit, 1.0, chosen)
        masked = jnp.where(hit, NEG_BIG, masked)
    gate = jnp.concatenate(gates, axis=0)
    gate = gate / jnp.sum(gate, axis=0, keepdims=True) * ROUTED_SCALE
    gate_ref[...] = gate
    idx_ref[...] = jnp.concatenate(idxs, axis=0).astype(jnp.int32)

    tr = lax.broadcasted_iota(jnp.int32, (tm, tm), 0)
    tc = lax.broadcasted_iota(jnp.int32, (tm, tm), 1)
    before = (tr < tc).astype(bf16)
    seen = base_ref[...] + _dot(chosen.astype(bf16), before)
    ranks = [jnp.sum(jnp.where(ei == eidx, seen, 0.0), axis=0, keepdims=True) for eidx in idxs]
    rank_ref[...] = jnp.concatenate(ranks, axis=0).astype(jnp.int32)
    total = base_ref[...] + jnp.sum(chosen, axis=1, keepdims=True)
    base_ref[...] = total
    cnt_ref[...] = total


def _route(hm, hd, x2, mod, w_out, w_router_t, rbias_col, seq):
    t, d = x2.shape
    tm = TM_ROUTE
    tiles_per_seq = seq // tm
    const = lambda shape: pl.BlockSpec(shape, lambda i: (0,) * len(shape))
    return pl.pallas_call(
        _route_kernel,
        out_shape=(jax.ShapeDtypeStruct((t, d), f32),
                   jax.ShapeDtypeStruct((t, QUARTER), jnp.int32),
                   jax.ShapeDtypeStruct((t, QUARTER), jnp.int32),
                   jax.ShapeDtypeStruct((TOP_K, t), jnp.int32),
                   jax.ShapeDtypeStruct((TOP_K, t), f32),
                   jax.ShapeDtypeStruct((TOP_K, t), jnp.int32),
                   jax.ShapeDtypeStruct((N_EXPERTS, 1), f32)),
        grid=(t // tm,),
        in_specs=[pl.BlockSpec((tm, ML_W), lambda i: (i, 0)),
                  pl.BlockSpec((tm, DA_W), lambda i: (i, 0)),
                  pl.BlockSpec((tm, d), lambda i: (i, 0)),
                  pl.BlockSpec((1, 6, d), lambda i: (i // tiles_per_seq, 0, 0)),
                  const((d, d)), const((N_EXPERTS, d)), const((N_EXPERTS, 1))],
        out_specs=(pl.BlockSpec((tm, d), lambda i: (i, 0)),
                   pl.BlockSpec((tm, QUARTER), lambda i: (i, 0)),
                   pl.BlockSpec((tm, QUARTER), lambda i: (i, 0)),
                   pl.BlockSpec((TOP_K, tm), lambda i: (0, i)),
                   pl.BlockSpec((TOP_K, tm), lambda i: (0, i)),
                   pl.BlockSpec((TOP_K, tm), lambda i: (0, i)),
                   const((N_EXPERTS, 1))),
        scratch_shapes=[pltpu.VMEM((N_EXPERTS, 1), f32)],
        compiler_params=_cparams(1),
        name="route",
    )(hm, hd, x2, mod, w_out, w_router_t, rbias_col)


SLOT_RADIX = 256
SLOT_DIGITS = 3


def _slots_kernel(idx_ref, rank_ref, dig_ref, dest_ref):
    tm = TM_SLOTS
    ei = lax.broadcasted_iota(jnp.int32, (N_EXPERTS, tm), 0)
    rows = []
    for k in range(TOP_K):
        onehot = jnp.where(ei == idx_ref[k:k + 1, :], 1.0, 0.0).astype(bf16)
        dg = _dot(dig_ref[...], onehot)
        start = dg[0:1, :]
        for j in range(1, SLOT_DIGITS):
            start = start + dg[j:j + 1, :] * float(SLOT_RADIX ** j)
        rows.append(start.astype(jnp.int32) + rank_ref[k:k + 1, :])
    dest_ref[...] = jnp.concatenate(rows, axis=0)


def _slots(idx, rank, digits):
    t = idx.shape[1]
    tm = TM_SLOTS
    return pl.pallas_call(
        _slots_kernel,
        out_shape=jax.ShapeDtypeStruct((TOP_K, t), jnp.int32),
        grid=(t // tm,),
        in_specs=[pl.BlockSpec((TOP_K, tm), lambda i: (0, i)),
                  pl.BlockSpec((TOP_K, tm), lambda i: (0, i)),
                  pl.BlockSpec((SUBLANES, N_EXPERTS), lambda i: (0, 0))],
        out_specs=pl.BlockSpec((TOP_K, tm), lambda i: (0, i)),
        compiler_params=_cparams(1),
        name="slots",
    )(idx, rank, digits)


def _sc_scatter_table(rows, dest, n_slots):
    n, width = rows.shape
    mesh = plsc.VectorSubcoreMesh(core_axis_name="core", subcore_axis_name="subcore")

    @pl.kernel(out_type=jax.ShapeDtypeStruct((n_slots, width), rows.dtype), mesh=mesh, scratch_types=[],
               name="sc_scatter")
    def scatter_kernel(rows_hbm, dest_hbm, out_hbm):
        def body(rows_vmem, dest_vmem):
            for k in range(TOP_K):
                pltpu.sync_copy(rows_vmem, out_hbm.at[dest_vmem.at[k]])

        pltpu.emit_pipeline(
            body,
            grid=(n // SC_WINDOW,),
            in_specs=[pl.BlockSpec((SC_WINDOW, width), lambda i: (i, 0)),
                      pl.BlockSpec((TOP_K, SC_WINDOW), lambda i: (0, i))],
            out_specs=[],
            core_axis_name=("core", "subcore"),
            dimension_semantics=(pltpu.PARALLEL,),
        )(rows_hbm, dest_hbm)

    return scatter_kernel(rows, dest)


EXPERT_GROUP = 4
EXPERT_RING = 16
EXPERT_AHEAD = EXPERT_RING - EXPERT_GROUP


def _expert_kernel(start_ref, cnt_ref, total_ref, xsa_ref, xsb_ref, w1_ref, w3_ref, w2_ref,
                   ysa_ref, ysb_ref, xbuf, ybuf, w1b, w3b, w2b, in_sem, out_sem):
    bm = MOE_BLOCK
    ring = EXPERT_RING
    ahead = EXPERT_AHEAD
    shift = int(math.log2(bm))
    e = pl.program_id(0)
    total = total_ref[0]
    cnt = cnt_ref[e]
    nb = lax.shift_right_logical(cnt + (bm - 1), shift)
    g0 = lax.shift_right_logical(start_ref[e], shift)

    def slot_of(g):
        return jnp.bitwise_and(g, ring - 1)

    def in_copies(g):
        s = slot_of(g)
        rows = pl.ds(pl.multiple_of(g * bm, bm), bm)
        return [pltpu.make_async_copy(x_ref.at[rows], xbuf.at[s, half], in_sem.at[s])
                for half, x_ref in enumerate((xsa_ref, xsb_ref))]

    def out_copies(g):
        s = slot_of(g)
        rows = pl.ds(pl.multiple_of(g * bm, bm), bm)
        return [pltpu.make_async_copy(ybuf.at[s, half], y_ref.at[rows], out_sem.at[s])
                for half, y_ref in enumerate((ysa_ref, ysb_ref))]

    def request(g):
        @pl.when(g < total)
        def _():
            for cp in in_copies(g):
                cp.start()

    def acquire(g):
        for cp in in_copies(g):
            cp.wait()

        @pl.when(g >= ring)
        def _():
            for cp in out_copies(g - ring):
                cp.wait()

    def compute(g):
        s = slot_of(g)
        row = (g - g0) * bm + lax.broadcasted_iota(jnp.int32, (bm, HALF), 0)
        words = jnp.concatenate([xbuf[s, 0], xbuf[s, 1]], axis=1)
        lo, hi = _unpack_bf16_pair(lax.bitcast_convert_type(jnp.where(row < cnt, words, 0), jnp.uint32))
        lo = lo.astype(bf16)
        hi = hi.astype(bf16)
        h1 = _dot(lo, w1b[0:HALF, :]) + _dot(hi, w1b[HALF:D_MODEL, :])
        h3 = _dot(lo, w3b[0:HALF, :]) + _dot(hi, w3b[HALF:D_MODEL, :])
        y = _dot((_silu(h1) * h3).astype(bf16), w2b[...])
        packed = lax.bitcast_convert_type(_pack_bf16_pair(y[:, 0:HALF], y[:, HALF:D_MODEL]), jnp.int32)
        ybuf[s, 0] = packed[:, 0:QUARTER]
        ybuf[s, 1] = packed[:, QUARTER:HALF]

    def release(g):
        for cp in out_copies(g):
            cp.start()

    @pl.when(e == 0)
    def _():
        for g in range(ahead):
            request(g)

    @pl.when(nb > 0)
    def _():
        w1b[...] = w1_ref[0].astype(bf16)
        w3b[...] = w3_ref[0].astype(bf16)
        w2b[...] = w2_ref[0].astype(bf16)

    def run(g, n):
        for d in range(n):
            request(g + d + ahead)
        for d in range(n):
            acquire(g + d)
        for d in range(n):
            compute(g + d)
        for d in range(n):
            release(g + d)

    def group(i, carry):
        run(g0 + EXPERT_GROUP * i, EXPERT_GROUP)
        return carry

    lax.fori_loop(0, lax.shift_right_logical(nb, int(math.log2(EXPERT_GROUP))), group, 0)

    n = EXPERT_GROUP // 2
    while n >= 1:
        @pl.when(jnp.bitwise_and(nb, n) != 0)
        def _(n=n):
            done = jnp.bitwise_and(nb, -2 * n)
            run(g0 + done, n)
        n //= 2

    @pl.when(e == pl.num_programs(0) - 1)
    def _():
        for back in range(1, ring + 1):
            @pl.when(total - back >= 0)
            def _():
                for cp in out_copies(total - back):
                    cp.wait()


def _experts(start, counts, total, xs_a, xs_b, w1, w3, w2):
    n_pad = xs_a.shape[0]
    bm = MOE_BLOCK
    grid_spec = pltpu.PrefetchScalarGridSpec(
        num_scalar_prefetch=3,
        grid=(N_EXPERTS,),
        in_specs=[pl.BlockSpec(memory_space=pl.ANY),
                  pl.BlockSpec(memory_space=pl.ANY),
                  pl.BlockSpec((1, D_MODEL, D_EXPERT), lambda e, st, ct, tt: (e, 0, 0)),
                  pl.BlockSpec((1, D_MODEL, D_EXPERT), lambda e, st, ct, tt: (e, 0, 0)),
                  pl.BlockSpec((1, D_EXPERT, D_MODEL), lambda e, st, ct, tt: (e, 0, 0))],
        out_specs=(pl.BlockSpec(memory_space=pl.ANY), pl.BlockSpec(memory_space=pl.ANY)),
        scratch_shapes=[pltpu.VMEM((EXPERT_RING, 2, bm, QUARTER), jnp.int32),
                        pltpu.VMEM((EXPERT_RING, 2, bm, QUARTER), jnp.int32),
                        pltpu.VMEM((D_MODEL, D_EXPERT), bf16),
                        pltpu.VMEM((D_MODEL, D_EXPERT), bf16),
                        pltpu.VMEM((D_EXPERT, D_MODEL), bf16),
                        pltpu.SemaphoreType.DMA((EXPERT_RING,)),
                        pltpu.SemaphoreType.DMA((EXPERT_RING,))],
    )
    return pl.pallas_call(
        _expert_kernel,
        out_shape=(jax.ShapeDtypeStruct((n_pad, QUARTER), jnp.int32),
                   jax.ShapeDtypeStruct((n_pad, QUARTER), jnp.int32)),
        grid_spec=grid_spec,
        compiler_params=_cparams(1),
        name="experts",
    )(start, counts, total, xs_a, xs_b, w1, w3, w2)


def _sc_gather_table(table, idx2):
    n = idx2.shape[1]
    width = table.shape[1]
    mesh = plsc.VectorSubcoreMesh(core_axis_name="core", subcore_axis_name="subcore")

    @pl.kernel(out_type=jax.ShapeDtypeStruct((n, width), table.dtype), mesh=mesh, scratch_types=[],
               name="sc_gather")
    def gather_kernel(tab_hbm, idx_hbm, out_hbm):
        def body(idx_vmem, out_vmem):
            pltpu.sync_copy(tab_hbm.at[idx_vmem.at[0]], out_vmem)

        pltpu.emit_pipeline(
            body,
            grid=(n // SC_WINDOW,),
            in_specs=[pl.BlockSpec((1, SC_WINDOW), lambda i: (0, i))],
            out_specs=[pl.BlockSpec((SC_WINDOW, width), lambda i: (i, 0))],
            core_axis_name=("core", "subcore"),
            dimension_semantics=(pltpu.PARALLEL,),
        )(idx_hbm, out_hbm)

    return gather_kernel(table, idx2)


def _sc_gather_rows(table_a, table_b, idx):
    idx2 = idx.reshape(1, -1)
    return _sc_gather_table(table_a, idx2), _sc_gather_table(table_b, idx2)


def _combine_kernel(gate_ref, ha_ref, hb_ref, x1_ref, mod_ref, ws1_ref, ws3_ref, ws2_ref, ga_ref, gb_ref,
                    out_ref):
    words = jnp.concatenate([ha_ref[...], hb_ref[...]], axis=1)
    lo, hi = _unpack_bf16_pair(lax.bitcast_convert_type(words, jnp.uint32))
    lo = lo.astype(bf16)
    hi = hi.astype(bf16)
    s1 = _dot(lo, ws1_ref[0:HALF, :]) + _dot(hi, ws1_ref[HALF:D_MODEL, :])
    s3 = _dot(lo, ws3_ref[0:HALF, :]) + _dot(hi, ws3_ref[HALF:D_MODEL, :])
    y = _dot((_silu(s1) * s3).astype(bf16), ws2_ref[...])

    gate = gate_ref[...]
    gate_f = mod_ref[0, 5:6, :]
    for part, g_ref in enumerate((ga_ref, gb_ref)):
        c_lo = slice(part * QUARTER, (part + 1) * QUARTER)
        c_hi = slice(HALF + part * QUARTER, HALF + (part + 1) * QUARTER)
        acc_lo = y[:, c_lo]
        acc_hi = y[:, c_hi]
        for k in range(TOP_K):
            rlo, rhi = _unpack_bf16_pair(lax.bitcast_convert_type(g_ref[k], jnp.uint32))
            gk = gate[:, k:k + 1]
            acc_lo = acc_lo + gk * rlo
            acc_hi = acc_hi + gk * rhi
        out_ref[:, c_lo] = x1_ref[:, c_lo] + gate_f[:, c_lo] * acc_lo
        out_ref[:, c_hi] = x1_ref[:, c_hi] + gate_f[:, c_hi] * acc_hi


def _combine(gate_col, h2a, h2b, x1, mod, ws1, ws3, ws2, ga, gb, seq):
    t, d = x1.shape
    tm = TM_MOE
    tiles_per_seq = seq // tm
    const = lambda shape: pl.BlockSpec(shape, lambda i: (0,) * len(shape))
    return pl.pallas_call(
        _combine_kernel,
        out_shape=jax.ShapeDtypeStruct((t, d), f32),
        grid=(t // tm,),
        in_specs=[pl.BlockSpec((tm, TOP_K), lambda i: (i, 0)),
                  pl.BlockSpec((tm, QUARTER), lambda i: (i, 0)),
                  pl.BlockSpec((tm, QUARTER), lambda i: (i, 0)),
                  pl.BlockSpec((tm, d), lambda i: (i, 0)),
                  pl.BlockSpec((1, 6, d), lambda i: (i // tiles_per_seq, 0, 0)),
                  const((d, D_EXPERT)), const((d, D_EXPERT)), const((D_EXPERT, d)),
                  pl.BlockSpec((TOP_K, tm, QUARTER), lambda i: (0, i, 0)),
                  pl.BlockSpec((TOP_K, tm, QUARTER), lambda i: (0, i, 0))],
        out_specs=pl.BlockSpec((tm, d), lambda i: (i, 0)),
        compiler_params=_cparams(1),
        name="combine",
    )(gate_col, h2a, h2b, x1, mod, ws1, ws3, ws2, ga, gb)


def _lambda_init(layer):
    return 0.8 - 0.6 * math.exp(-0.3 * layer)


def _layer(x, c, w_ada, b_ada, w_in, conv_w, conv_b, gate_b, ml_norm_g, da_q_norm_g, da_k_norm_g,
           lambda_q1, lambda_k1, lambda_q2, lambda_k2, da_norm_g, w_out, w_router, router_bias,
           w1, w3, w2, ws1, ws3, ws2, layer):
    batch, seq, d = x.shape
    t = batch * seq
    lam_init = _lambda_init(layer)
    x2 = x.reshape(t, d)

    c_pad = jnp.pad(c, ((0, -batch % SUBLANES), (0, 0)))
    mod = _adaln(c_pad, w_ada, b_ada.reshape(1, -1))[:batch].reshape(batch, 6, d)

    g0 = 4 * ML_W
    q0 = g0 + N_GATES
    n_grp = 2 * DA_HEADS
    w_main = jnp.concatenate([w_in[:, :g0], w_in[:, q0:]], axis=1).astype(bf16)
    wg = jnp.pad(w_in[:, g0:g0 + N_GATES], ((0, 0), (0, LANES - N_GATES))).astype(bf16)
    gb_row = jnp.pad(gate_b, (0, LANES - N_GATES)).reshape(1, LANES)
    qgain = (jnp.tile(da_q_norm_g, n_grp) * (DA_DH ** -0.5 * LOG2E)).reshape(1, DA_W)
    kgain = jnp.tile(da_k_norm_g, n_grp).reshape(1, DA_W)
    seg = jnp.arange(DA_W) // DA_DH
    bd = (seg[:, None] == seg[None, :]).astype(bf16)

    pos = jnp.arange(seq, dtype=f32)[:, None]
    slopes = jnp.asarray([_alibi_slope_log2(h) for h in range(DA_HEADS)], f32)[None, :]
    r = pos * slopes

    def bf16_part(a):
        bits = lax.bitcast_convert_type(a, jnp.uint32) & jnp.uint32(0xFFFF0000)
        return lax.bitcast_convert_type(bits, f32)

    r_hi = bf16_part(r)
    r_mid = bf16_part(r - r_hi)
    r_lo = bf16_part(r - r_hi - r_mid)
    parts = jnp.stack([r_hi, r_mid, r_lo], axis=-1)
    kaug = jnp.pad(parts, ((0, 0), (0, 0), (DA_DH, LANES - DA_DH - 3))).reshape(seq, DA_HEADS * LANES)

    pm, vt, ot, qh, kh, vh, gcol, grow = _inproj(x2, mod, w_main, wg, gb_row, qgain, kgain, bd, kaug.astype(bf16),
                                         conv_w, conv_b.reshape(1, -1), seq)

    hm = _mlstm(pm, vt, ot, gcol, grow, ml_norm_g.reshape(-1, 1), batch, seq)

    lam = (jnp.exp(jnp.sum(lambda_q1 * lambda_k1)) - jnp.exp(jnp.sum(lambda_q2 * lambda_k2))
           + lam_init).reshape(1, 1).astype(f32)
    hd = _attention(qh, kh, vh, lam, da_norm_g.reshape(1, -1), batch, seq, lam_init)

    x1, h2a, h2b, idx, gate, rank, counts = _route(hm, hd, x2, mod, w_out.astype(bf16),
                                                   w_router.T.astype(bf16), router_bias.reshape(-1, 1), seq)

    bm = MOE_BLOCK
    n_blocks = (t * TOP_K) // bm + N_EXPERTS
    counts_i = counts.reshape(-1).astype(jnp.int32)
    padded = (counts_i + bm - 1) // bm * bm
    pad_end = jnp.cumsum(padded)
    pad_start = pad_end - padded
    digits = jnp.stack([(pad_start // SLOT_RADIX ** j) % SLOT_RADIX for j in range(SLOT_DIGITS)])
    digits = jnp.pad(digits, ((0, SUBLANES - SLOT_DIGITS), (0, 0))).astype(bf16)
    dest = _slots(idx, rank, digits)

    xs_a = _sc_scatter_table(h2a, dest, n_blocks * bm)
    xs_b = _sc_scatter_table(h2b, dest, n_blocks * bm)
    total_blocks = (pad_end[-1:] // bm).astype(jnp.int32)
    ys_a, ys_b = _experts(pad_start.astype(jnp.int32), counts_i, total_blocks, xs_a, xs_b, w1, w3, w2)
    ga, gb = _sc_gather_rows(ys_a, ys_b, dest.reshape(-1))
    out = _combine(gate.T, h2a, h2b, x1, mod, ws1.astype(bf16), ws3.astype(bf16), ws2.astype(bf16),
                   ga.reshape(TOP_K, t, QUARTER), gb.reshape(TOP_K, t, QUARTER), seq)
    return out.reshape(batch, seq, d)


def kernel(x, c, w_ada, b_ada, w_in, conv_w, conv_b, gate_b, ml_norm_g, da_q_norm_g, da_k_norm_g,
           lambda_q1, lambda_k1, lambda_q2, lambda_k2, da_norm_g, w_out, w_router, router_bias,
           w1, w3, w2, ws1, ws3, ws2):
    depth = w_ada.shape[0]
    for l in range(depth):
        x = _layer(x, c, w_ada[l], b_ada[l], w_in[l], conv_w[l], conv_b[l], gate_b[l], ml_norm_g[l],
                   da_q_norm_g[l], da_k_norm_g[l], lambda_q1[l], lambda_k1[l], lambda_q2[l], lambda_k2[l],
                   da_norm_g[l], w_out[l], w_router[l], router_bias[l], w1[l], w3[l], w2[l],
                   ws1[l], ws3[l], ws2[l], l)
    return x
```

```python
import functools
import math

import jax
import jax.numpy as jnp
from jax import lax
from jax.experimental import pallas as pl
from jax.experimental.pallas import tpu as pltpu
from jax.experimental.pallas import tpu_sc as plsc

D_MODEL = 1024
ML_HEADS = 4
ML_DH = 128
ML_W = ML_HEADS * ML_DH
ML_CHUNK = 128
CONV_K = 4
DA_HEADS = 4
DA_DH = 64
DA_W = DA_HEADS * 2 * DA_DH
N_EXPERTS = 256
TOP_K = 8
N_GROUPS = 8
GROUP_SIZE = N_EXPERTS // N_GROUPS
TOPK_GROUPS = 4
D_EXPERT = 256
ROUTED_SCALE = 2.5
EPS = 1e-6
N_GATES = 2 * ML_HEADS

LANES = 128
SUBLANES = 8
VMEM_LIMIT_BYTES = 48 * 1024 * 1024

TM_INPROJ = 512
ML_STEP_CHUNKS = 4
ATT_BLOCK = 1024
ATT_QBLOCKS = 2
TM_ROUTE = 512
TM_SLOTS = 512
TM_MOE = 512
MOE_BLOCK = 256
HALF = D_MODEL // 2
QUARTER = HALF // 2
SC_WINDOW = 128

NEG_BIG = -1e30
LOG2E = 1.4426950408889634

f32 = jnp.float32
bf16 = jnp.bfloat16
HIGHEST = lax.Precision.HIGHEST


def _cparams(n_axes):
    return pltpu.CompilerParams(dimension_semantics=("arbitrary",) * n_axes,
                                vmem_limit_bytes=VMEM_LIMIT_BYTES)


def _dot(a, b):
    return jnp.dot(a, b, preferred_element_type=f32)


def _dot_nt(a, b):
    return lax.dot_general(a, b, (((1,), (1,)), ((), ())), preferred_element_type=f32)


def _sigmoid(x):
    return 1.0 / (1.0 + jnp.exp(-x))


def _silu(x):
    return x * _sigmoid(x)


def _log_sigmoid(x):
    return jnp.minimum(x, 0.0) - jnp.log(1.0 + jnp.exp(-jnp.abs(x)))


def _pack_bf16_pair(lo, hi):
    lo_bits = lax.bitcast_convert_type(lo.astype(bf16).astype(f32), jnp.uint32)
    hi_bits = lax.bitcast_convert_type(hi.astype(bf16).astype(f32), jnp.uint32)
    return (hi_bits & jnp.uint32(0xFFFF0000)) | (lo_bits >> 16)


def _unpack_bf16_pair(w):
    lo = lax.bitcast_convert_type(w << 16, f32)
    hi = lax.bitcast_convert_type(w & jnp.uint32(0xFFFF0000), f32)
    return lo, hi


def _adaln_kernel(c_ref, w_ref, b_ref, o_ref):
    c = c_ref[...]
    o_ref[...] = jnp.dot(_silu(c), w_ref[...], precision=HIGHEST, preferred_element_type=f32) + b_ref[...]


def _adaln(c_pad, w_ada, b_ada):
    rows, d = c_pad.shape
    n = w_ada.shape[1]
    tn = 1536
    return pl.pallas_call(
        _adaln_kernel,
        out_shape=jax.ShapeDtypeStruct((rows, n), f32),
        grid=(n // tn,),
        in_specs=[pl.BlockSpec((rows, d), lambda j: (0, 0)),
                  pl.BlockSpec((d, tn), lambda j: (0, j)),
                  pl.BlockSpec((1, tn), lambda j: (0, j))],
        out_specs=pl.BlockSpec((rows, tn), lambda j: (0, j)),
        compiler_params=_cparams(1),
        name="adaln",
    )(c_pad, w_ada, b_ada)


def _alibi_slope_log2(head):
    return 2.0 ** (-8.0 * (head + 1) / DA_HEADS) * LOG2E


def _inproj_kernel(x_ref, mod_ref, w_ref, wg_ref, gbr_ref, qg_ref, kg_ref, bd_ref, kaug_ref, cw_ref, cb_ref,
                   pm_ref, vt_ref, ot_ref, q_ref, k_ref, v_ref, gcol_ref, grow_ref, ext_ref, *, seq):
    tm = TM_INPROJ
    x = x_ref[...]
    h = x * lax.rsqrt(jnp.mean(x * x, axis=-1, keepdims=True) + EPS)
    h = h * (1.0 + mod_ref[0, 1:2, :]) + mod_ref[0, 0:1, :]
    hb = h.astype(bf16)
    cw = ML_W

    @pl.when((pl.program_id(0) * tm) % seq == 0)
    def _():
        ext_ref[0:SUBLANES, :] = jnp.zeros((SUBLANES, 2 * cw), f32)

    cur = _dot(hb, w_ref[:, 0:2 * cw])
    ext_ref[SUBLANES:SUBLANES + tm, :] = cur
    acc = cb_ref[...] + cw_ref[CONV_K - 1:CONV_K, :] * cur
    for j in range(CONV_K - 1):
        off = SUBLANES - (CONV_K - 1) + j
        acc = acc + cw_ref[j:j + 1, :] * ext_ref[off:off + tm, :]
    ext_ref[0:SUBLANES, :] = cur[tm - SUBLANES:tm, :]
    qk = _silu(acc)
    pm_ref[:, 0:cw] = qk[:, 0:cw].astype(bf16)
    pm_ref[:, cw:2 * cw] = (qk[:, cw:2 * cw] * (ML_DH ** -0.5)).astype(bf16)
    vt_ref[...] = _dot(hb, w_ref[:, 2 * cw:3 * cw]).T.astype(bf16)
    ot_ref[...] = _dot(hb, w_ref[:, 3 * cw:4 * cw]).T.astype(bf16)

    lane = lax.broadcasted_iota(jnp.int32, (tm, LANES), 1)
    feat = lane < DA_DH
    q_aug = jnp.where(jnp.logical_and(lane >= DA_DH, lane < DA_DH + 3), 1.0, 0.0)
    q0 = 4 * cw
    k0 = q0 + DA_W

    def qk_normed(col0, gain_ref):
        y = _dot(hb, w_ref[:, col0:col0 + DA_W])
        ss = _dot((y * y).astype(bf16), bd_ref[...])
        return y * lax.rsqrt(ss * (1.0 / DA_DH) + EPS) * gain_ref[...]

    def lane_group(y2, grp, aug):
        pair = y2[:, (grp // 2) * LANES:(grp // 2 + 1) * LANES]
        if grp % 2:
            pair = pltpu.roll(pair, DA_DH, axis=1)
        return jnp.where(feat, pair, aug).astype(bf16)

    nq = qk_normed(q0, qg_ref)
    nk = qk_normed(k0, kg_ref)
    for grp in range(2 * DA_HEADS):
        sl = slice(grp * LANES, (grp + 1) * LANES)
        head = slice((grp // 2) * LANES, (grp // 2 + 1) * LANES)
        q_ref[:, sl] = lane_group(nq, grp, q_aug)
        k_ref[:, sl] = lane_group(nk, grp, kaug_ref[:, head].astype(f32))

    v0 = k0 + DA_W
    v_ref[...] = _dot(hb, w_ref[:, v0:v0 + DA_W]).astype(bf16)
    gates = _dot(hb, wg_ref[...]) + gbr_ref[...]
    gcol_ref[...] = gates
    grow_ref[...] = gates.T[0:N_GATES, :]


def _inproj(x2, mod, w_main, wg, gb_row, qgain, kgain, bd, kaug, conv_w, conv_b, seq):
    t, d = x2.shape
    tm = TM_INPROJ
    tiles_per_seq = seq // tm
    n_main = w_main.shape[1]
    const = lambda shape: pl.BlockSpec(shape, lambda i: (0,) * len(shape))
    return pl.pallas_call(
        functools.partial(_inproj_kernel, seq=seq),
        out_shape=(jax.ShapeDtypeStruct((t, 2 * ML_W), bf16),
                   jax.ShapeDtypeStruct((ML_W, t), bf16),
                   jax.ShapeDtypeStruct((ML_W, t), bf16),
                   jax.ShapeDtypeStruct((t, 2 * DA_W), bf16),
                   jax.ShapeDtypeStruct((t, 2 * DA_W), bf16),
                   jax.ShapeDtypeStruct((t, DA_W), bf16),
                   jax.ShapeDtypeStruct((t, LANES), f32),
                   jax.ShapeDtypeStruct((N_GATES, t), f32)),
        grid=(t // tm,),
        in_specs=[pl.BlockSpec((tm, d), lambda i: (i, 0)),
                  pl.BlockSpec((1, 6, d), lambda i: (i // tiles_per_seq, 0, 0)),
                  const((d, n_main)), const((d, LANES)), const((1, LANES)),
                  const((1, DA_W)), const((1, DA_W)), const((DA_W, DA_W)),
                  pl.BlockSpec((tm, DA_HEADS * LANES), lambda i: (i % tiles_per_seq, 0)),
                  const((CONV_K, 2 * ML_W)), const((1, 2 * ML_W))],
        out_specs=(pl.BlockSpec((tm, 2 * ML_W), lambda i: (i, 0)),
                   pl.BlockSpec((ML_W, tm), lambda i: (0, i)),
                   pl.BlockSpec((ML_W, tm), lambda i: (0, i)),
                   pl.BlockSpec((tm, 2 * DA_W), lambda i: (i, 0)),
                   pl.BlockSpec((tm, 2 * DA_W), lambda i: (i, 0)),
                   pl.BlockSpec((tm, DA_W), lambda i: (i, 0)),
                   pl.BlockSpec((tm, LANES), lambda i: (i, 0)),
                   pl.BlockSpec((N_GATES, tm), lambda i: (0, i))),
        scratch_shapes=[pltpu.VMEM((SUBLANES + tm, 2 * ML_W), f32)],
        compiler_params=_cparams(1),
        name="inproj",
    )(x2, mod, w_main, wg, gb_row, qgain, kgain, bd, kaug, conv_w, conv_b)


def _mlstm_kernel(q_ref, k_ref, vt_ref, ot_ref, gcol_ref, grow_ref, ng_ref, out_ref, state_ref, m_ref):
    L = ML_CHUNK
    dh = ML_DH

    @pl.when(pl.program_id(1) == 0)
    def _():
        state_ref[...] = jnp.zeros_like(state_ref)
        m_ref[...] = jnp.zeros_like(m_ref)

    s_i = lax.broadcasted_iota(jnp.int32, (L, L), 0)
    t_i = lax.broadcasted_iota(jnp.int32, (L, L), 1)
    visible = s_i <= t_i
    tril = (s_i >= t_i).astype(f32)
    triu = visible.astype(f32)
    ones_rows = (lax.broadcasted_iota(jnp.int32, (dh, L), 0) == 0).astype(f32)
    g_b = jnp.broadcast_to(ng_ref[...], (dh, L))

    for cc in range(ML_STEP_CHUNKS):
        rows = slice(cc * L, (cc + 1) * L)
        gcol = gcol_ref[rows, :]
        grow = grow_ref[:, rows]
        bcol_all = jnp.dot(tril, _log_sigmoid(gcol), precision=HIGHEST, preferred_element_type=f32)
        brow_all = jnp.dot(_log_sigmoid(grow), triu, precision=HIGHEST, preferred_element_type=f32)

        for hd in range(ML_HEADS):
            cols = slice(hd * dh, (hd + 1) * dh)
            qb = q_ref[rows, cols]
            kb = k_ref[rows, cols]
            v_aug_t = jnp.concatenate([vt_ref[cols, rows].astype(f32), ones_rows], axis=0)

            brow = brow_all[ML_HEADS + hd:ML_HEADS + hd + 1, :]
            irow = grow[hd:hd + 1, :]
            key_term = gcol[:, hd:hd + 1] - bcol_all[:, ML_HEADS + hd:ML_HEADS + hd + 1]
            m_prev = m_ref[hd:hd + 1, :]

            d_intra = jnp.where(visible, brow + key_term, NEG_BIG)
            d_inter = brow + m_prev
            m_t = jnp.maximum(d_inter, jnp.max(d_intra, axis=0, keepdims=True))
            w_intra = jnp.exp(d_intra - m_t)
            w_inter = jnp.exp(d_inter - m_t)

            s_t = _dot_nt(kb, qb) * w_intra
            state = state_ref[hd]
            q_state = _dot_nt(state.astype(bf16), qb)
            s_v = _dot(v_aug_t.astype(bf16), s_t.astype(bf16))
            num = w_inter * q_state[0:dh, :] + s_v[0:dh, :]
            den = w_inter * q_state[dh:dh + 1, :] + s_v[dh:dh + 1, :]
            hval = num / jnp.maximum(jnp.abs(den), jnp.exp(-m_t))

            b_last = brow[:, L - 1:L]
            d_state = b_last - brow + irow
            m_new = jnp.maximum(b_last + m_prev, jnp.max(d_state, axis=1, keepdims=True))
            carry_scale = jnp.exp(b_last + m_prev - m_new)[:, 0:1]
            wk = jnp.exp(d_state - m_new)
            upd = _dot((v_aug_t * wk).astype(bf16), kb)
            state_ref[hd] = carry_scale * state + upd
            m_ref[hd:hd + 1, :] = m_new

            hn = hval * lax.rsqrt(jnp.mean(hval * hval, axis=0, keepdims=True) + EPS) * g_b
            og = _sigmoid(ot_ref[cols, rows].astype(f32))
            out_ref[rows, cols] = (hn * og).T.astype(bf16)


def _mlstm(pm, vt, ot, gcol, grow, ml_norm_g, batch, seq):
    t = pm.shape[0]
    rows = ML_STEP_CHUNKS * ML_CHUNK
    ns = seq // rows
    row = lambda b, c: b * ns + c
    return pl.pallas_call(
        _mlstm_kernel,
        out_shape=jax.ShapeDtypeStruct((t, ML_W), bf16),
        grid=(batch, ns),
        in_specs=[pl.BlockSpec((rows, ML_W), lambda b, c: (row(b, c), 0)),
                  pl.BlockSpec((rows, ML_W), lambda b, c: (row(b, c), 1)),
                  pl.BlockSpec((ML_W, rows), lambda b, c: (0, row(b, c))),
                  pl.BlockSpec((ML_W, rows), lambda b, c: (0, row(b, c))),
                  pl.BlockSpec((rows, LANES), lambda b, c: (row(b, c), 0)),
                  pl.BlockSpec((N_GATES, rows), lambda b, c: (0, row(b, c))),
                  pl.BlockSpec((ML_DH, 1), lambda b, c: (0, 0))],
        out_specs=pl.BlockSpec((rows, ML_W), lambda b, c: (row(b, c), 0)),
        scratch_shapes=[pltpu.VMEM((ML_HEADS, 2 * ML_DH, ML_DH), f32),
                        pltpu.VMEM((SUBLANES, LANES), f32)],
        compiler_params=_cparams(2),
        name="mlstm",
    )(pm, pm, vt, ot, gcol, grow, ml_norm_g)


def _attn_block(q_ref, k_ref, v_ref, m_ref, l_ref, acc_ref, row0, nrows, nkeys, key0=None):
    rows = slice(row0, row0 + nrows)
    nch = nkeys // LANES
    v = v_ref[0:nkeys, :]
    if key0 is not None:
        keep = (key0 + lax.broadcasted_iota(jnp.int32, (nrows, nkeys), 1)
                <= row0 + lax.broadcasted_iota(jnp.int32, (nrows, nkeys), 0))
    for c in range(2):
        sl = slice(c * LANES, (c + 1) * LANES)
        s = _dot_nt(q_ref[rows, sl], k_ref[0:nkeys, sl])
        if key0 is not None:
            s = jnp.where(keep, s, NEG_BIG)
        chunks = [s[:, j * LANES:(j + 1) * LANES] for j in range(nch)]
        mc = chunks[0]
        for ch in chunks[1:]:
            mc = jnp.maximum(mc, ch)
        m_old = m_ref[c, rows, :]
        m_new = jnp.maximum(m_old, jnp.max(mc, axis=1, keepdims=True))
        alpha = jnp.exp2(m_old - m_new)
        ps = [jnp.exp2(ch - m_new) for ch in chunks]
        lsum = ps[0]
        for pj in ps[1:]:
            lsum = lsum + pj
        p = jnp.concatenate([pj.astype(bf16) for pj in ps], axis=1)
        l_ref[c, rows, :] = alpha * l_ref[c, rows, :] + lsum
        acc_ref[c, rows, :] = alpha * acc_ref[c, rows, :] + _dot(p, v)
        m_ref[c, rows, :] = m_new


def _attn_kernel(qt_ref, kt_ref, q_ref, k_ref, v_ref, lam_ref, g_ref, o_ref,
                 m_ref, l_ref, acc_ref, *, lam_init):
    step = pl.program_id(2)
    qi = qt_ref[step]
    ki = kt_ref[step]

    @pl.when(ki == 0)
    def _():
        m_ref[...] = jnp.full_like(m_ref, NEG_BIG)
        l_ref[...] = jnp.zeros_like(l_ref)
        acc_ref[...] = jnp.zeros_like(acc_ref)

    blk = ATT_BLOCK
    half = blk // 2
    qrows = ATT_QBLOCKS * blk
    first_diag = ATT_QBLOCKS * qi
    args = (q_ref, k_ref, v_ref, m_ref, l_ref, acc_ref)

    @pl.when(ki < first_diag)
    def _():
        _attn_block(*args, 0, qrows, blk)

    for j in range(ATT_QBLOCKS):
        @pl.when(ki == first_diag + j)
        def _(j=j):
            r0 = j * blk
            _attn_block(*args, r0, half, half, key0=r0)
            _attn_block(*args, r0 + half, half, blk, key0=r0)
            if r0 + blk < qrows:
                _attn_block(*args, r0 + blk, qrows - r0 - blk, blk)

    @pl.when(ki == first_diag + ATT_QBLOCKS - 1)
    def _():
        lam = lam_ref[...]
        l0 = jnp.sum(l_ref[0], axis=1, keepdims=True)
        l1 = jnp.sum(l_ref[1], axis=1, keepdims=True)
        o = acc_ref[0] / l0 - lam * (acc_ref[1] / l1)
        o = o * lax.rsqrt(jnp.mean(o * o, axis=-1, keepdims=True) + EPS)
        o_ref[...] = (o * g_ref[...] * (1.0 - lam_init)).astype(bf16)


def _attention(qh, kh, vh, lam, da_norm_g, batch, seq, lam_init):
    t = qh.shape[0]
    blk = ATT_BLOCK
    qrows = ATT_QBLOCKS * blk
    assert seq % qrows == 0, (seq, qrows)
    nk = seq // blk
    nq = seq // qrows
    pairs = [(i, j) for i in range(nq) for j in range(ATT_QBLOCKS * (i + 1))]
    qt = jnp.asarray([p[0] for p in pairs], jnp.int32)
    kt = jnp.asarray([p[1] for p in pairs], jnp.int32)
    w = 2 * DA_DH
    grid_spec = pltpu.PrefetchScalarGridSpec(
        num_scalar_prefetch=2,
        grid=(batch, DA_HEADS, len(pairs)),
        in_specs=[pl.BlockSpec((qrows, 2 * LANES), lambda b, h, s, qt, kt: (b * nq + qt[s], h)),
                  pl.BlockSpec((blk, 2 * LANES), lambda b, h, s, qt, kt: (b * nk + kt[s], h)),
                  pl.BlockSpec((blk, w), lambda b, h, s, qt, kt: (b * nk + kt[s], h)),
                  pl.BlockSpec((1, 1), lambda b, h, s, qt, kt: (0, 0)),
                  pl.BlockSpec((1, w), lambda b, h, s, qt, kt: (0, 0))],
        out_specs=pl.BlockSpec((qrows, w), lambda b, h, s, qt, kt: (b * nq + qt[s], h)),
        scratch_shapes=[pltpu.VMEM((2, qrows, LANES), f32),
                        pltpu.VMEM((2, qrows, LANES), f32),
                        pltpu.VMEM((2, qrows, w), f32)],
    )
    return pl.pallas_call(
        functools.partial(_attn_kernel, lam_init=lam_init),
        out_shape=jax.ShapeDtypeStruct((t, DA_W), bf16),
        grid_spec=grid_spec,
        compiler_params=_cparams(3),
        name="attn",
    )(qt, kt, qh, kh, vh, lam, da_norm_g)


def _first_index_of_max(x, iota_f, size):
    m = jnp.max(x, axis=0, keepdims=True)
    idx = jnp.min(jnp.where(x == m, iota_f, float(size)), axis=0, keepdims=True)
    return m, idx


def _route_kernel(hm_ref, hd_ref, x_ref, mod_ref, wo_ref, wr_ref, rb_ref,
                  x1_ref, ha_ref, hb_ref, idx_ref, gate_ref, rank_ref, cnt_ref, base_ref):
    tm = TM_ROUTE
    i = pl.program_id(0)

    @pl.when(i == 0)
    def _():
        base_ref[...] = jnp.zeros_like(base_ref)

    mix = _dot(hm_ref[...], wo_ref[0:ML_W, :]) + _dot(hd_ref[...], wo_ref[ML_W:ML_W + DA_W, :])
    x1 = x_ref[...] + mod_ref[0, 2:3, :] * mix
    x1_ref[...] = x1
    h2 = x1 * lax.rsqrt(jnp.mean(x1 * x1, axis=-1, keepdims=True) + EPS)
    h2 = h2 * (1.0 + mod_ref[0, 4:5, :]) + mod_ref[0, 3:4, :]
    packed = lax.bitcast_convert_type(_pack_bf16_pair(h2[:, 0:HALF], h2[:, HALF:D_MODEL]), jnp.int32)
    ha_ref[...] = packed[:, 0:QUARTER]
    hb_ref[...] = packed[:, QUARTER:HALF]

    sc = _sigmoid(_dot_nt(wr_ref[...], h2.astype(bf16)))
    sel = sc + rb_ref[...]

    gi = lax.broadcasted_iota(jnp.int32, (GROUP_SIZE, tm), 0).astype(f32)
    gscores = []
    for g in range(N_GROUPS):
        blk = sel[g * GROUP_SIZE:(g + 1) * GROUP_SIZE, :]
        m1, i1 = _first_index_of_max(blk, gi, GROUP_SIZE)
        m2 = jnp.max(jnp.where(gi == i1, NEG_BIG, blk), axis=0, keepdims=True)
        gscores.append(m1 + m2)
    gs = jnp.concatenate(gscores, axis=0)

    ngi = lax.broadcasted_iota(jnp.int32, (N_GROUPS, tm), 0).astype(f32)
    gkeep = jnp.zeros((N_GROUPS, tm), f32)
    for _ in range(TOPK_GROUPS):
        _, gidx = _first_index_of_max(gs, ngi, N_GROUPS)
        hit = ngi == gidx
        gkeep = jnp.where(hit, 1.0, gkeep)
        gs = jnp.where(hit, NEG_BIG, gs)
    masked = jnp.concatenate(
        [jnp.where(gkeep[g:g + 1, :] > 0.0, sel[g * GROUP_SIZE:(g + 1) * GROUP_SIZE, :], NEG_BIG)
         for g in range(N_GROUPS)], axis=0)

    ei = lax.broadcasted_iota(jnp.int32, (N_EXPERTS, tm), 0).astype(f32)
    idxs, gates = [], []
    chosen = jnp.zeros((N_EXPERTS, tm), f32)
    for _ in range(TOP_K):
        _, eidx = _first_index_of_max(masked, ei, N_EXPERTS)
        hit = ei == eidx
        idxs.append(eidx)
        gates.append(jnp.sum(jnp.where(hit, sc, 0.0), axis=0, keepdims=True))
        chosen = jnp.where(hit, 1.0, chosen)
        masked = jnp.where(hit, NEG_BIG, masked)
    gate = jnp.concatenate(gates, axis=0)
    gate = gate / jnp.sum(gate, axis=0, keepdims=True) * ROUTED_SCALE
    gate_ref[...] = gate
    idx_ref[...] = jnp.concatenate(idxs, axis=0).astype(jnp.int32)

    tr = lax.broadcasted_iota(jnp.int32, (tm, tm), 0)
    tc = lax.broadcasted_iota(jnp.int32, (tm, tm), 1)
    before = (tr < tc).astype(bf16)
    seen = base_ref[...] + _dot(chosen.astype(bf16), before)
    ranks = [jnp.sum(jnp.where(ei == eidx, seen, 0.0), axis=0, keepdims=True) for eidx in idxs]
    rank_ref[...] = jnp.concatenate(ranks, axis=0).astype(jnp.int32)
    total = base_ref[...] + jnp.sum(chosen, axis=1, keepdims=True)
    base_ref[...] = total
    cnt_ref[...] = total


def _route(hm, hd, x2, mod, w_out, w_router_t, rbias_col, seq):
    t, d = x2.shape
    tm = TM_ROUTE
    tiles_per_seq = seq // tm
    const = lambda shape: pl.BlockSpec(shape, lambda i: (0,) * len(shape))
    return pl.pallas_call(
        _route_kernel,
        out_shape=(jax.ShapeDtypeStruct((t, d), f32),
                   jax.ShapeDtypeStruct((t, QUARTER), jnp.int32),
                   jax.ShapeDtypeStruct((t, QUARTER), jnp.int32),
                   jax.ShapeDtypeStruct((TOP_K, t), jnp.int32),
                   jax.ShapeDtypeStruct((TOP_K, t), f32),
                   jax.ShapeDtypeStruct((TOP_K, t), jnp.int32),
                   jax.ShapeDtypeStruct((N_EXPERTS, 1), f32)),
        grid=(t // tm,),
        in_specs=[pl.BlockSpec((tm, ML_W), lambda i: (i, 0)),
                  pl.BlockSpec((tm, DA_W), lambda i: (i, 0)),
                  pl.BlockSpec((tm, d), lambda i: (i, 0)),
                  pl.BlockSpec((1, 6, d), lambda i: (i // tiles_per_seq, 0, 0)),
                  const((d, d)), const((N_EXPERTS, d)), const((N_EXPERTS, 1))],
        out_specs=(pl.BlockSpec((tm, d), lambda i: (i, 0)),
                   pl.BlockSpec((tm, QUARTER), lambda i: (i, 0)),
                   pl.BlockSpec((tm, QUARTER), lambda i: (i, 0)),
                   pl.BlockSpec((TOP_K, tm), lambda i: (0, i)),
                   pl.BlockSpec((TOP_K, tm), lambda i: (0, i)),
                   pl.BlockSpec((TOP_K, tm), lambda i: (0, i)),
                   const((N_EXPERTS, 1))),
        scratch_shapes=[pltpu.VMEM((N_EXPERTS, 1), f32)],
        compiler_params=_cparams(1),
        name="route",
    )(hm, hd, x2, mod, w_out, w_router_t, rbias_col)


SLOT_RADIX = 256
SLOT_DIGITS = 3


def _slots_kernel(idx_ref, rank_ref, dig_ref, dest_ref):
    tm = TM_SLOTS
    ei = lax.broadcasted_iota(jnp.int32, (N_EXPERTS, tm), 0)
    rows = []
    for k in range(TOP_K):
        onehot = jnp.where(ei == idx_ref[k:k + 1, :], 1.0, 0.0).astype(bf16)
        dg = _dot(dig_ref[...], onehot)
        start = dg[0:1, :]
        for j in range(1, SLOT_DIGITS):
            start = start + dg[j:j + 1, :] * float(SLOT_RADIX ** j)
        rows.append(start.astype(jnp.int32) + rank_ref[k:k + 1, :])
    dest_ref[...] = jnp.concatenate(rows, axis=0)


def _slots(idx, rank, digits):
    t = idx.shape[1]
    tm = TM_SLOTS
    return pl.pallas_call(
        _slots_kernel,
        out_shape=jax.ShapeDtypeStruct((TOP_K, t), jnp.int32),
        grid=(t // tm,),
        in_specs=[pl.BlockSpec((TOP_K, tm), lambda i: (0, i)),
                  pl.BlockSpec((TOP_K, tm), lambda i: (0, i)),
                  pl.BlockSpec((SUBLANES, N_EXPERTS), lambda i: (0, 0))],
        out_specs=pl.BlockSpec((TOP_K, tm), lambda i: (0, i)),
        compiler_params=_cparams(1),
        name="slots",
    )(idx, rank, digits)


def _sc_scatter_table(rows, dest, n_slots):
    n, width = rows.shape
    mesh = plsc.VectorSubcoreMesh(core_axis_name="core", subcore_axis_name="subcore")

    @pl.kernel(out_type=jax.ShapeDtypeStruct((n_slots, width), rows.dtype), mesh=mesh, scratch_types=[],
               name="sc_scatter")
    def scatter_kernel(rows_hbm, dest_hbm, out_hbm):
        def body(rows_vmem, dest_vmem):
            for k in range(TOP_K):
                pltpu.sync_copy(rows_vmem, out_hbm.at[dest_vmem.at[k]])

        pltpu.emit_pipeline(
            body,
            grid=(n // SC_WINDOW,),
            in_specs=[pl.BlockSpec((SC_WINDOW, width), lambda i: (i, 0)),
                      pl.BlockSpec((TOP_K, SC_WINDOW), lambda i: (0, i))],
            out_specs=[],
            core_axis_name=("core", "subcore"),
            dimension_semantics=(pltpu.PARALLEL,),
        )(rows_hbm, dest_hbm)

    return scatter_kernel(rows, dest)


EXPERT_GROUP = 4
EXPERT_RING = 16
EXPERT_AHEAD = EXPERT_RING - EXPERT_GROUP


def _expert_kernel(start_ref, cnt_ref, total_ref, xsa_ref, xsb_ref, w1_ref, w3_ref, w2_ref,
                   ysa_ref, ysb_ref, xbuf, ybuf, w1b, w3b, w2b, in_sem, out_sem):
    bm = MOE_BLOCK
    ring = EXPERT_RING
    ahead = EXPERT_AHEAD
    shift = int(math.log2(bm))
    e = pl.program_id(0)
    total = total_ref[0]
    cnt = cnt_ref[e]
    nb = lax.shift_right_logical(cnt + (bm - 1), shift)
    g0 = lax.shift_right_logical(start_ref[e], shift)

    def slot_of(g):
        return jnp.bitwise_and(g, ring - 1)

    def in_copies(g):
        s = slot_of(g)
        rows = pl.ds(pl.multiple_of(g * bm, bm), bm)
        return [pltpu.make_async_copy(x_ref.at[rows], xbuf.at[s, half], in_sem.at[s])
                for half, x_ref in enumerate((xsa_ref, xsb_ref))]

    def out_copies(g):
        s = slot_of(g)
        rows = pl.ds(pl.multiple_of(g * bm, bm), bm)
        return [pltpu.make_async_copy(ybuf.at[s, half], y_ref.at[rows], out_sem.at[s])
                for half, y_ref in enumerate((ysa_ref, ysb_ref))]

    def request(g):
        @pl.when(g < total)
        def _():
            for cp in in_copies(g):
                cp.start()

    def acquire(g):
        for cp in in_copies(g):
            cp.wait()

        @pl.when(g >= ring)
        def _():
            for cp in out_copies(g - ring):
                cp.wait()

    def compute(g):
        s = slot_of(g)
        row = (g - g0) * bm + lax.broadcasted_iota(jnp.int32, (bm, HALF), 0)
        words = jnp.concatenate([xbuf[s, 0], xbuf[s, 1]], axis=1)
        lo, hi = _unpack_bf16_pair(lax.bitcast_convert_type(jnp.where(row < cnt, words, 0), jnp.uint32))
        lo = lo.astype(bf16)
        hi = hi.astype(bf16)
        h1 = _dot(lo, w1b[0:HALF, :]) + _dot(hi, w1b[HALF:D_MODEL, :])
        h3 = _dot(lo, w3b[0:HALF, :]) + _dot(hi, w3b[HALF:D_MODEL, :])
        y = _dot((_silu(h1) * h3).astype(bf16), w2b[...])
        packed = lax.bitcast_convert_type(_pack_bf16_pair(y[:, 0:HALF], y[:, HALF:D_MODEL]), jnp.int32)
        ybuf[s, 0] = packed[:, 0:QUARTER]
        ybuf[s, 1] = packed[:, QUARTER:HALF]

    def release(g):
        for cp in out_copies(g):
            cp.start()

    @pl.when(e == 0)
    def _():
        for g in range(ahead):
            request(g)

    @pl.when(nb > 0)
    def _():
        w1b[...] = w1_ref[0].astype(bf16)
        w3b[...] = w3_ref[0].astype(bf16)
        w2b[...] = w2_ref[0].astype(bf16)

    def run(g, n):
        for d in range(n):
            request(g + d + ahead)
        for d in range(n):
            acquire(g + d)
        for d in range(n):
            compute(g + d)
        for d in range(n):
            release(g + d)

    def group(i, carry):
        run(g0 + EXPERT_GROUP * i, EXPERT_GROUP)
        return carry

    lax.fori_loop(0, lax.shift_right_logical(nb, int(math.log2(EXPERT_GROUP))), group, 0)

    n = EXPERT_GROUP // 2
    while n >= 1:
        @pl.when(jnp.bitwise_and(nb, n) != 0)
        def _(n=n):
            done = jnp.bitwise_and(nb, -2 * n)
            run(g0 + done, n)
        n //= 2

    @pl.when(e == pl.num_programs(0) - 1)
    def _():
        for back in range(1, ring + 1):
            @pl.when(total - back >= 0)
            def _():
                for cp in out_copies(total - back):
                    cp.wait()


def _experts(start, counts, total, xs_a, xs_b, w1, w3, w2):
    n_pad = xs_a.shape[0]
    bm = MOE_BLOCK
    grid_spec = pltpu.PrefetchScalarGridSpec(
        num_scalar_prefetch=3,
        grid=(N_EXPERTS,),
        in_specs=[pl.BlockSpec(memory_space=pl.ANY),
                  pl.BlockSpec(memory_space=pl.ANY),
                  pl.BlockSpec((1, D_MODEL, D_EXPERT), lambda e, st, ct, tt: (e, 0, 0)),
                  pl.BlockSpec((1, D_MODEL, D_EXPERT), lambda e, st, ct, tt: (e, 0, 0)),
                  pl.BlockSpec((1, D_EXPERT, D_MODEL), lambda e, st, ct, tt: (e, 0, 0))],
        out_specs=(pl.BlockSpec(memory_space=pl.ANY), pl.BlockSpec(memory_space=pl.ANY)),
        scratch_shapes=[pltpu.VMEM((EXPERT_RING, 2, bm, QUARTER), jnp.int32),
                        pltpu.VMEM((EXPERT_RING, 2, bm, QUARTER), jnp.int32),
                        pltpu.VMEM((D_MODEL, D_EXPERT), bf16),
                        pltpu.VMEM((D_MODEL, D_EXPERT), bf16),
                        pltpu.VMEM((D_EXPERT, D_MODEL), bf16),
                        pltpu.SemaphoreType.DMA((EXPERT_RING,)),
                        pltpu.SemaphoreType.DMA((EXPERT_RING,))],
    )
    return pl.pallas_call(
        _expert_kernel,
        out_shape=(jax.ShapeDtypeStruct((n_pad, QUARTER), jnp.int32),
                   jax.ShapeDtypeStruct((n_pad, QUARTER), jnp.int32)),
        grid_spec=grid_spec,
        compiler_params=_cparams(1),
        name="experts",
    )(start, counts, total, xs_a, xs_b, w1, w3, w2)


def _sc_gather_table(table, idx2):
    n = idx2.shape[1]
    width = table.shape[1]
    mesh = plsc.VectorSubcoreMesh(core_axis_name="core", subcore_axis_name="subcore")

    @pl.kernel(out_type=jax.ShapeDtypeStruct((n, width), table.dtype), mesh=mesh, scratch_types=[],
               name="sc_gather")
    def gather_kernel(tab_hbm, idx_hbm, out_hbm):
        def body(idx_vmem, out_vmem):
            pltpu.sync_copy(tab_hbm.at[idx_vmem.at[0]], out_vmem)

        pltpu.emit_pipeline(
            body,
            grid=(n // SC_WINDOW,),
            in_specs=[pl.BlockSpec((1, SC_WINDOW), lambda i: (0, i))],
            out_specs=[pl.BlockSpec((SC_WINDOW, width), lambda i: (i, 0))],
            core_axis_name=("core", "subcore"),
            dimension_semantics=(pltpu.PARALLEL,),
        )(idx_hbm, out_hbm)

    return gather_kernel(table, idx2)


def _sc_gather_rows(table_a, table_b, idx):
    idx2 = idx.reshape(1, -1)
    return _sc_gather_table(table_a, idx2), _sc_gather_table(table_b, idx2)


def _combine_kernel(gate_ref, ha_ref, hb_ref, x1_ref, mod_ref, ws1_ref, ws3_ref, ws2_ref, ga_ref, gb_ref,
                    out_ref):
    words = jnp.concatenate([ha_ref[...], hb_ref[...]], axis=1)
    lo, hi = _unpack_bf16_pair(lax.bitcast_convert_type(words, jnp.uint32))
    lo = lo.astype(bf16)
    hi = hi.astype(bf16)
    s1 = _dot(lo, ws1_ref[0:HALF, :]) + _dot(hi, ws1_ref[HALF:D_MODEL, :])
    s3 = _dot(lo, ws3_ref[0:HALF, :]) + _dot(hi, ws3_ref[HALF:D_MODEL, :])
    y = _dot((_silu(s1) * s3).astype(bf16), ws2_ref[...])

    gate = gate_ref[...]
    gate_f = mod_ref[0, 5:6, :]
    for part, g_ref in enumerate((ga_ref, gb_ref)):
        c_lo = slice(part * QUARTER, (part + 1) * QUARTER)
        c_hi = slice(HALF + part * QUARTER, HALF + (part + 1) * QUARTER)
        acc_lo = y[:, c_lo]
        acc_hi = y[:, c_hi]
        for k in range(TOP_K):
            rlo, rhi = _unpack_bf16_pair(lax.bitcast_convert_type(g_ref[k], jnp.uint32))
            gk = gate[:, k:k + 1]
            acc_lo = acc_lo + gk * rlo
            acc_hi = acc_hi + gk * rhi
        out_ref[:, c_lo] = x1_ref[:, c_lo] + gate_f[:, c_lo] * acc_lo
        out_ref[:, c_hi] = x1_ref[:, c_hi] + gate_f[:, c_hi] * acc_hi


def _combine(gate_col, h2a, h2b, x1, mod, ws1, ws3, ws2, ga, gb, seq):
    t, d = x1.shape
    tm = TM_MOE
    tiles_per_seq = seq // tm
    const = lambda shape: pl.BlockSpec(shape, lambda i: (0,) * len(shape))
    return pl.pallas_call(
        _combine_kernel,
        out_shape=jax.ShapeDtypeStruct((t, d), f32),
        grid=(t // tm,),
        in_specs=[pl.BlockSpec((tm, TOP_K), lambda i: (i, 0)),
                  pl.BlockSpec((tm, QUARTER), lambda i: (i, 0)),
                  pl.BlockSpec((tm, QUARTER), lambda i: (i, 0)),
                  pl.BlockSpec((tm, d), lambda i: (i, 0)),
                  pl.BlockSpec((1, 6, d), lambda i: (i // tiles_per_seq, 0, 0)),
                  const((d, D_EXPERT)), const((d, D_EXPERT)), const((D_EXPERT, d)),
                  pl.BlockSpec((TOP_K, tm, QUARTER), lambda i: (0, i, 0)),
                  pl.BlockSpec((TOP_K, tm, QUARTER), lambda i: (0, i, 0))],
        out_specs=pl.BlockSpec((tm, d), lambda i: (i, 0)),
        compiler_params=_cparams(1),
        name="combine",
    )(gate_col, h2a, h2b, x1, mod, ws1, ws3, ws2, ga, gb)


def _lambda_init(layer):
    return 0.8 - 0.6 * math.exp(-0.3 * layer)


def _layer(x, c, w_ada, b_ada, w_in, conv_w, conv_b, gate_b, ml_norm_g, da_q_norm_g, da_k_norm_g,
           lambda_q1, lambda_k1, lambda_q2, lambda_k2, da_norm_g, w_out, w_router, router_bias,
           w1, w3, w2, ws1, ws3, ws2, layer):
    batch, seq, d = x.shape
    t = batch * seq
    lam_init = _lambda_init(layer)
    x2 = x.reshape(t, d)

    c_pad = jnp.pad(c, ((0, -batch % SUBLANES), (0, 0)))
    mod = _adaln(c_pad, w_ada, b_ada.reshape(1, -1))[:batch].reshape(batch, 6, d)

    g0 = 4 * ML_W
    q0 = g0 + N_GATES
    n_grp = 2 * DA_HEADS
    w_main = jnp.concatenate([w_in[:, :g0], w_in[:, q0:]], axis=1).astype(bf16)
    wg = jnp.pad(w_in[:, g0:g0 + N_GATES], ((0, 0), (0, LANES - N_GATES))).astype(bf16)
    gb_row = jnp.pad(gate_b, (0, LANES - N_GATES)).reshape(1, LANES)
    qgain = (jnp.tile(da_q_norm_g, n_grp) * (DA_DH ** -0.5 * LOG2E)).reshape(1, DA_W)
    kgain = jnp.tile(da_k_norm_g, n_grp).reshape(1, DA_W)
    seg = jnp.arange(DA_W) // DA_DH
    bd = (seg[:, None] == seg[None, :]).astype(bf16)

    pos = jnp.arange(seq, dtype=f32)[:, None]
    slopes = jnp.asarray([_alibi_slope_log2(h) for h in range(DA_HEADS)], f32)[None, :]
    r = pos * slopes

    def bf16_part(a):
        bits = lax.bitcast_convert_type(a, jnp.uint32) & jnp.uint32(0xFFFF0000)
        return lax.bitcast_convert_type(bits, f32)

    r_hi = bf16_part(r)
    r_mid = bf16_part(r - r_hi)
    r_lo = bf16_part(r - r_hi - r_mid)
    parts = jnp.stack([r_hi, r_mid, r_lo], axis=-1)
    kaug = jnp.pad(parts, ((0, 0), (0, 0), (DA_DH, LANES - DA_DH - 3))).reshape(seq, DA_HEADS * LANES)

    pm, vt, ot, qh, kh, vh, gcol, grow = _inproj(x2, mod, w_main, wg, gb_row, qgain, kgain, bd, kaug.astype(bf16),
                                         conv_w, conv_b.reshape(1, -1), seq)

    hm = _mlstm(pm, vt, ot, gcol, grow, ml_norm_g.reshape(-1, 1), batch, seq)

    lam = (jnp.exp(jnp.sum(lambda_q1 * lambda_k1)) - jnp.exp(jnp.sum(lambda_q2 * lambda_k2))
           + lam_init).reshape(1, 1).astype(f32)
    hd = _attention(qh, kh, vh, lam, da_norm_g.reshape(1, -1), batch, seq, lam_init)

    x1, h2a, h2b, idx, gate, rank, counts = _route(hm, hd, x2, mod, w_out.astype(bf16),
                                                   w_router.T.astype(bf16), router_bias.reshape(-1, 1), seq)

    bm = MOE_BLOCK
    n_blocks = (t * TOP_K) // bm + N_EXPERTS
    counts_i = counts.reshape(-1).astype(jnp.int32)
    padded = (counts_i + bm - 1) // bm * bm
    pad_end = jnp.cumsum(padded)
    pad_start = pad_end - padded
    digits = jnp.stack([(pad_start // SLOT_RADIX ** j) % SLOT_RADIX for j in range(SLOT_DIGITS)])
    digits = jnp.pad(digits, ((0, SUBLANES - SLOT_DIGITS), (0, 0))).astype(bf16)
    dest = _slots(idx, rank, digits)

    xs_a = _sc_scatter_table(h2a, dest, n_blocks * bm)
    xs_b = _sc_scatter_table(h2b, dest, n_blocks * bm)
    total_blocks = (pad_end[-1:] // bm).astype(jnp.int32)
    ys_a, ys_b = _experts(pad_start.astype(jnp.int32), counts_i, total_blocks, xs_a, xs_b, w1, w3, w2)
    ga, gb = _sc_gather_rows(ys_a, ys_b, dest.reshape(-1))
    out = _combine(gate.T, h2a, h2b, x1, mod, ws1.astype(bf16), ws3.astype(bf16), ws2.astype(bf16),
                   ga.reshape(TOP_K, t, QUARTER), gb.reshape(TOP_K, t, QUARTER), seq)
    return out.reshape(batch, seq, d)


def kernel(x, c, w_ada, b_ada, w_in, conv_w, conv_b, gate_b, ml_norm_g, da_q_norm_g, da_k_norm_g,
           lambda_q1, lambda_k1, lambda_q2, lambda_k2, da_norm_g, w_out, w_router, router_bias,
           w1, w3, w2, ws1, ws3, ws2):
    depth = w_ada.shape[0]
    for l in range(depth):
        x = _layer(x, c, w_ada[l], b_ada[l], w_in[l], conv_w[l], conv_b[l], gate_b[l], ml_norm_g[l],
                   da_q_norm_g[l], da_k_norm_g[l], lambda_q1[l], lambda_k1[l], lambda_q2[l], lambda_k2[l],
                   da_norm_g[l], w_out[l], w_router[l], router_bias[l], w1[l], w3[l], w2[l],
                   ws1[l], ws3[l], ws2[l], l)
    return x
```

```python
import functools
import math

import jax
import jax.numpy as jnp
from jax import lax
from jax.experimental import pallas as pl
from jax.experimental.pallas import tpu as pltpu
from jax.experimental.pallas import tpu_sc as plsc

D_MODEL = 1024
ML_HEADS = 4
ML_DH = 128
ML_W = ML_HEADS * ML_DH
ML_CHUNK = 128
CONV_K = 4
DA_HEADS = 4
DA_DH = 64
DA_W = DA_HEADS * 2 * DA_DH
N_EXPERTS = 256
TOP_K = 8
N_GROUPS = 8
GROUP_SIZE = N_EXPERTS // N_GROUPS
TOPK_GROUPS = 4
D_EXPERT = 256
ROUTED_SCALE = 2.5
EPS = 1e-6
N_GATES = 2 * ML_HEADS

LANES = 128
SUBLANES = 8
VMEM_LIMIT_BYTES = 48 * 1024 * 1024

TM_INPROJ = 512
ML_STEP_CHUNKS = 4
ATT_BLOCK = 1024
ATT_QBLOCKS = 2
TM_ROUTE = 512
TM_SLOTS = 512
TM_MOE = 512
COMBINE_PARTS = 2
MOE_BLOCK = 256
HALF = D_MODEL // 2
QUARTER = HALF // 2
SC_WINDOW = 128

NEG_BIG = -1e30
LOG2E = 1.4426950408889634

f32 = jnp.float32
bf16 = jnp.bfloat16
HIGHEST = lax.Precision.HIGHEST


def _cparams(n_axes):
    return pltpu.CompilerParams(dimension_semantics=("arbitrary",) * n_axes,
                                vmem_limit_bytes=VMEM_LIMIT_BYTES)


def _dot(a, b):
    return jnp.dot(a, b, preferred_element_type=f32)


def _dot_nt(a, b):
    return lax.dot_general(a, b, (((1,), (1,)), ((), ())), preferred_element_type=f32)


def _sigmoid(x):
    return 1.0 / (1.0 + jnp.exp(-x))


def _silu(x):
    return x * _sigmoid(x)


def _log_sigmoid(x):
    return jnp.minimum(x, 0.0) - jnp.log(1.0 + jnp.exp(-jnp.abs(x)))


def _pack_bf16_pair(lo, hi):
    lo_bits = lax.bitcast_convert_type(lo.astype(bf16).astype(f32), jnp.uint32)
    hi_bits = lax.bitcast_convert_type(hi.astype(bf16).astype(f32), jnp.uint32)
    return (hi_bits & jnp.uint32(0xFFFF0000)) | (lo_bits >> 16)


def _unpack_bf16_pair(w):
    lo = lax.bitcast_convert_type(w << 16, f32)
    hi = lax.bitcast_convert_type(w & jnp.uint32(0xFFFF0000), f32)
    return lo, hi


def _adaln_kernel(c_ref, w_ref, b_ref, o_ref):
    c = c_ref[...]
    o_ref[...] = jnp.dot(_silu(c), w_ref[...], precision=HIGHEST, preferred_element_type=f32) + b_ref[...]


def _adaln(c_pad, w_ada, b_ada):
    rows, d = c_pad.shape
    n = w_ada.shape[1]
    tn = 1536
    return pl.pallas_call(
        _adaln_kernel,
        out_shape=jax.ShapeDtypeStruct((rows, n), f32),
        grid=(n // tn,),
        in_specs=[pl.BlockSpec((rows, d), lambda j: (0, 0)),
                  pl.BlockSpec((d, tn), lambda j: (0, j)),
                  pl.BlockSpec((1, tn), lambda j: (0, j))],
        out_specs=pl.BlockSpec((rows, tn), lambda j: (0, j)),
        compiler_params=_cparams(1),
        name="adaln",
    )(c_pad, w_ada, b_ada)


def _alibi_slope_log2(head):
    return 2.0 ** (-8.0 * (head + 1) / DA_HEADS) * LOG2E


def _inproj_kernel(x_ref, mod_ref, w_ref, wg_ref, gbr_ref, qg_ref, kg_ref, bd_ref, kaug_ref, cw_ref, cb_ref,
                   pm_ref, vt_ref, ot_ref, q_ref, k_ref, v_ref, gcol_ref, grow_ref, ext_ref, *, seq):
    tm = TM_INPROJ
    x = x_ref[...]
    h = x * lax.rsqrt(jnp.mean(x * x, axis=-1, keepdims=True) + EPS)
    h = h * (1.0 + mod_ref[0, 1:2, :]) + mod_ref[0, 0:1, :]
    hb = h.astype(bf16)
    cw = ML_W

    @pl.when((pl.program_id(0) * tm) % seq == 0)
    def _():
        ext_ref[0:SUBLANES, :] = jnp.zeros((SUBLANES, 2 * cw), f32)

    cur = _dot(hb, w_ref[:, 0:2 * cw])
    ext_ref[SUBLANES:SUBLANES + tm, :] = cur
    acc = cb_ref[...] + cw_ref[CONV_K - 1:CONV_K, :] * cur
    for j in range(CONV_K - 1):
        off = SUBLANES - (CONV_K - 1) + j
        acc = acc + cw_ref[j:j + 1, :] * ext_ref[off:off + tm, :]
    ext_ref[0:SUBLANES, :] = cur[tm - SUBLANES:tm, :]
    qk = _silu(acc)
    pm_ref[:, 0:cw] = qk[:, 0:cw].astype(bf16)
    pm_ref[:, cw:2 * cw] = (qk[:, cw:2 * cw] * (ML_DH ** -0.5)).astype(bf16)
    vt_ref[...] = _dot(hb, w_ref[:, 2 * cw:3 * cw]).T.astype(bf16)
    ot_ref[...] = _dot(hb, w_ref[:, 3 * cw:4 * cw]).T.astype(bf16)

    lane = lax.broadcasted_iota(jnp.int32, (tm, LANES), 1)
    feat = lane < DA_DH
    q_aug = jnp.where(jnp.logical_and(lane >= DA_DH, lane < DA_DH + 3), 1.0, 0.0)
    q0 = 4 * cw
    k0 = q0 + DA_W

    def qk_normed(col0, gain_ref):
        y = _dot(hb, w_ref[:, col0:col0 + DA_W])
        ss = _dot((y * y).astype(bf16), bd_ref[...])
        return y * lax.rsqrt(ss * (1.0 / DA_DH) + EPS) * gain_ref[...]

    def lane_group(y2, grp, aug):
        pair = y2[:, (grp // 2) * LANES:(grp // 2 + 1) * LANES]
        if grp % 2:
            pair = pltpu.roll(pair, DA_DH, axis=1)
        return jnp.where(feat, pair, aug).astype(bf16)

    nq = qk_normed(q0, qg_ref)
    nk = qk_normed(k0, kg_ref)
    for grp in range(2 * DA_HEADS):
        sl = slice(grp * LANES, (grp + 1) * LANES)
        head = slice((grp // 2) * LANES, (grp // 2 + 1) * LANES)
        q_ref[:, sl] = lane_group(nq, grp, q_aug)
        k_ref[:, sl] = lane_group(nk, grp, kaug_ref[:, head].astype(f32))

    v0 = k0 + DA_W
    v_ref[...] = _dot(hb, w_ref[:, v0:v0 + DA_W]).astype(bf16)
    gates = _dot(hb, wg_ref[...]) + gbr_ref[...]
    gcol_ref[...] = gates
    grow_ref[...] = gates.T[0:N_GATES, :]


def _inproj(x2, mod, w_main, wg, gb_row, qgain, kgain, bd, kaug, conv_w, conv_b, seq):
    t, d = x2.shape
    tm = TM_INPROJ
    tiles_per_seq = seq // tm
    n_main = w_main.shape[1]
    const = lambda shape: pl.BlockSpec(shape, lambda i: (0,) * len(shape))
    return pl.pallas_call(
        functools.partial(_inproj_kernel, seq=seq),
        out_shape=(jax.ShapeDtypeStruct((t, 2 * ML_W), bf16),
                   jax.ShapeDtypeStruct((ML_W, t), bf16),
                   jax.ShapeDtypeStruct((ML_W, t), bf16),
                   jax.ShapeDtypeStruct((t, 2 * DA_W), bf16),
                   jax.ShapeDtypeStruct((t, 2 * DA_W), bf16),
                   jax.ShapeDtypeStruct((t, DA_W), bf16),
                   jax.ShapeDtypeStruct((t, LANES), f32),
                   jax.ShapeDtypeStruct((N_GATES, t), f32)),
        grid=(t // tm,),
        in_specs=[pl.BlockSpec((tm, d), lambda i: (i, 0)),
                  pl.BlockSpec((1, 6, d), lambda i: (i // tiles_per_seq, 0, 0)),
                  const((d, n_main)), const((d, LANES)), const((1, LANES)),
                  const((1, DA_W)), const((1, DA_W)), const((DA_W, DA_W)),
                  pl.BlockSpec((tm, DA_HEADS * LANES), lambda i: (i % tiles_per_seq, 0)),
                  const((CONV_K, 2 * ML_W)), const((1, 2 * ML_W))],
        out_specs=(pl.BlockSpec((tm, 2 * ML_W), lambda i: (i, 0)),
                   pl.BlockSpec((ML_W, tm), lambda i: (0, i)),
                   pl.BlockSpec((ML_W, tm), lambda i: (0, i)),
                   pl.BlockSpec((tm, 2 * DA_W), lambda i: (i, 0)),
                   pl.BlockSpec((tm, 2 * DA_W), lambda i: (i, 0)),
                   pl.BlockSpec((tm, DA_W), lambda i: (i, 0)),
                   pl.BlockSpec((tm, LANES), lambda i: (i, 0)),
                   pl.BlockSpec((N_GATES, tm), lambda i: (0, i))),
        scratch_shapes=[pltpu.VMEM((SUBLANES + tm, 2 * ML_W), f32)],
        compiler_params=_cparams(1),
        name="inproj",
    )(x2, mod, w_main, wg, gb_row, qgain, kgain, bd, kaug, conv_w, conv_b)


def _mlstm_kernel(q_ref, k_ref, vt_ref, ot_ref, gcol_ref, grow_ref, ng_ref, out_ref, state_ref, m_ref):
    L = ML_CHUNK
    dh = ML_DH

    @pl.when(pl.program_id(1) == 0)
    def _():
        state_ref[...] = jnp.zeros_like(state_ref)
        m_ref[...] = jnp.zeros_like(m_ref)

    s_i = lax.broadcasted_iota(jnp.int32, (L, L), 0)
    t_i = lax.broadcasted_iota(jnp.int32, (L, L), 1)
    visible = s_i <= t_i
    tril = (s_i >= t_i).astype(f32)
    triu = visible.astype(f32)
    ones_rows = (lax.broadcasted_iota(jnp.int32, (dh, L), 0) == 0).astype(f32)
    g_b = jnp.broadcast_to(ng_ref[...], (dh, L))

    for cc in range(ML_STEP_CHUNKS):
        rows = slice(cc * L, (cc + 1) * L)
        gcol = gcol_ref[rows, :]
        grow = grow_ref[:, rows]
        bcol_all = jnp.dot(tril, _log_sigmoid(gcol), precision=HIGHEST, preferred_element_type=f32)
        brow_all = jnp.dot(_log_sigmoid(grow), triu, precision=HIGHEST, preferred_element_type=f32)

        for hd in range(ML_HEADS):
            cols = slice(hd * dh, (hd + 1) * dh)
            qb = q_ref[rows, cols]
            kb = k_ref[rows, cols]
            v_aug_t = jnp.concatenate([vt_ref[cols, rows].astype(f32), ones_rows], axis=0)

            brow = brow_all[ML_HEADS + hd:ML_HEADS + hd + 1, :]
            irow = grow[hd:hd + 1, :]
            key_term = gcol[:, hd:hd + 1] - bcol_all[:, ML_HEADS + hd:ML_HEADS + hd + 1]
            m_prev = m_ref[hd:hd + 1, :]

            d_intra = jnp.where(visible, brow + key_term, NEG_BIG)
            d_inter = brow + m_prev
            m_t = jnp.maximum(d_inter, jnp.max(d_intra, axis=0, keepdims=True))
            w_intra = jnp.exp(d_intra - m_t)
            w_inter = jnp.exp(d_inter - m_t)

            s_t = _dot_nt(kb, qb) * w_intra
            state = state_ref[hd]
            q_state = _dot_nt(state.astype(bf16), qb)
            s_v = _dot(v_aug_t.astype(bf16), s_t.astype(bf16))
            num = w_inter * q_state[0:dh, :] + s_v[0:dh, :]
            den = w_inter * q_state[dh:dh + 1, :] + s_v[dh:dh + 1, :]
            hval = num / jnp.maximum(jnp.abs(den), jnp.exp(-m_t))

            b_last = brow[:, L - 1:L]
            d_state = b_last - brow + irow
            m_new = jnp.maximum(b_last + m_prev, jnp.max(d_state, axis=1, keepdims=True))
            carry_scale = jnp.exp(b_last + m_prev - m_new)[:, 0:1]
            wk = jnp.exp(d_state - m_new)
            upd = _dot((v_aug_t * wk).astype(bf16), kb)
            state_ref[hd] = carry_scale * state + upd
            m_ref[hd:hd + 1, :] = m_new

            hn = hval * lax.rsqrt(jnp.mean(hval * hval, axis=0, keepdims=True) + EPS) * g_b
            og = _sigmoid(ot_ref[cols, rows].astype(f32))
            out_ref[rows, cols] = (hn * og).T.astype(bf16)


def _mlstm(pm, vt, ot, gcol, grow, ml_norm_g, batch, seq):
    t = pm.shape[0]
    rows = ML_STEP_CHUNKS * ML_CHUNK
    ns = seq // rows
    row = lambda b, c: b * ns + c
    return pl.pallas_call(
        _mlstm_kernel,
        out_shape=jax.ShapeDtypeStruct((t, ML_W), bf16),
        grid=(batch, ns),
        in_specs=[pl.BlockSpec((rows, ML_W), lambda b, c: (row(b, c), 0)),
                  pl.BlockSpec((rows, ML_W), lambda b, c: (row(b, c), 1)),
                  pl.BlockSpec((ML_W, rows), lambda b, c: (0, row(b, c))),
                  pl.BlockSpec((ML_W, rows), lambda b, c: (0, row(b, c))),
                  pl.BlockSpec((rows, LANES), lambda b, c: (row(b, c), 0)),
                  pl.BlockSpec((N_GATES, rows), lambda b, c: (0, row(b, c))),
                  pl.BlockSpec((ML_DH, 1), lambda b, c: (0, 0))],
        out_specs=pl.BlockSpec((rows, ML_W), lambda b, c: (row(b, c), 0)),
        scratch_shapes=[pltpu.VMEM((ML_HEADS, 2 * ML_DH, ML_DH), f32),
                        pltpu.VMEM((SUBLANES, LANES), f32)],
        compiler_params=_cparams(2),
        name="mlstm",
    )(pm, pm, vt, ot, gcol, grow, ml_norm_g)


def _attn_block(q_ref, k_ref, v_ref, m_ref, l_ref, acc_ref, row0, nrows, nkeys, key0=None):
    rows = slice(row0, row0 + nrows)
    nch = nkeys // LANES
    v = v_ref[0:nkeys, :]
    if key0 is not None:
        keep = (key0 + lax.broadcasted_iota(jnp.int32, (nrows, nkeys), 1)
                <= row0 + lax.broadcasted_iota(jnp.int32, (nrows, nkeys), 0))
    for c in range(2):
        sl = slice(c * LANES, (c + 1) * LANES)
        s = _dot_nt(q_ref[rows, sl], k_ref[0:nkeys, sl])
        if key0 is not None:
            s = jnp.where(keep, s, NEG_BIG)
        chunks = [s[:, j * LANES:(j + 1) * LANES] for j in range(nch)]
        mc = chunks[0]
        for ch in chunks[1:]:
            mc = jnp.maximum(mc, ch)
        m_old = m_ref[c, rows, :]
        m_new = jnp.maximum(m_old, jnp.max(mc, axis=1, keepdims=True))
        alpha = jnp.exp2(m_old - m_new)
        ps = [jnp.exp2(ch - m_new) for ch in chunks]
        lsum = ps[0]
        for pj in ps[1:]:
            lsum = lsum + pj
        p = jnp.concatenate([pj.astype(bf16) for pj in ps], axis=1)
        l_ref[c, rows, :] = alpha * l_ref[c, rows, :] + lsum
        acc_ref[c, rows, :] = alpha * acc_ref[c, rows, :] + _dot(p, v)
        m_ref[c, rows, :] = m_new


def _attn_kernel(qt_ref, kt_ref, q_ref, k_ref, v_ref, lam_ref, g_ref, o_ref,
                 m_ref, l_ref, acc_ref, *, lam_init):
    step = pl.program_id(2)
    qi = qt_ref[step]
    ki = kt_ref[step]

    @pl.when(ki == 0)
    def _():
        m_ref[...] = jnp.full_like(m_ref, NEG_BIG)
        l_ref[...] = jnp.zeros_like(l_ref)
        acc_ref[...] = jnp.zeros_like(acc_ref)

    blk = ATT_BLOCK
    half = blk // 2
    qrows = ATT_QBLOCKS * blk
    first_diag = ATT_QBLOCKS * qi
    args = (q_ref, k_ref, v_ref, m_ref, l_ref, acc_ref)

    @pl.when(ki < first_diag)
    def _():
        _attn_block(*args, 0, qrows, blk)

    for j in range(ATT_QBLOCKS):
        @pl.when(ki == first_diag + j)
        def _(j=j):
            r0 = j * blk
            _attn_block(*args, r0, half, half, key0=r0)
            _attn_block(*args, r0 + half, half, blk, key0=r0)
            if r0 + blk < qrows:
                _attn_block(*args, r0 + blk, qrows - r0 - blk, blk)

    @pl.when(ki == first_diag + ATT_QBLOCKS - 1)
    def _():
        lam = lam_ref[...]
        l0 = jnp.sum(l_ref[0], axis=1, keepdims=True)
        l1 = jnp.sum(l_ref[1], axis=1, keepdims=True)
        o = acc_ref[0] / l0 - lam * (acc_ref[1] / l1)
        o = o * lax.rsqrt(jnp.mean(o * o, axis=-1, keepdims=True) + EPS)
        o_ref[...] = (o * g_ref[...] * (1.0 - lam_init)).astype(bf16)


def _attention(qh, kh, vh, lam, da_norm_g, batch, seq, lam_init):
    t = qh.shape[0]
    blk = ATT_BLOCK
    qrows = ATT_QBLOCKS * blk
    assert seq % qrows == 0, (seq, qrows)
    nk = seq // blk
    nq = seq // qrows
    pairs = [(i, j) for i in range(nq) for j in range(ATT_QBLOCKS * (i + 1))]
    qt = jnp.asarray([p[0] for p in pairs], jnp.int32)
    kt = jnp.asarray([p[1] for p in pairs], jnp.int32)
    w = 2 * DA_DH
    grid_spec = pltpu.PrefetchScalarGridSpec(
        num_scalar_prefetch=2,
        grid=(batch, DA_HEADS, len(pairs)),
        in_specs=[pl.BlockSpec((qrows, 2 * LANES), lambda b, h, s, qt, kt: (b * nq + qt[s], h)),
                  pl.BlockSpec((blk, 2 * LANES), lambda b, h, s, qt, kt: (b * nk + kt[s], h)),
                  pl.BlockSpec((blk, w), lambda b, h, s, qt, kt: (b * nk + kt[s], h)),
                  pl.BlockSpec((1, 1), lambda b, h, s, qt, kt: (0, 0)),
                  pl.BlockSpec((1, w), lambda b, h, s, qt, kt: (0, 0))],
        out_specs=pl.BlockSpec((qrows, w), lambda b, h, s, qt, kt: (b * nq + qt[s], h)),
        scratch_shapes=[pltpu.VMEM((2, qrows, LANES), f32),
                        pltpu.VMEM((2, qrows, LANES), f32),
                        pltpu.VMEM((2, qrows, w), f32)],
    )
    return pl.pallas_call(
        functools.partial(_attn_kernel, lam_init=lam_init),
        out_shape=jax.ShapeDtypeStruct((t, DA_W), bf16),
        grid_spec=grid_spec,
        compiler_params=_cparams(3),
        name="attn",
    )(qt, kt, qh, kh, vh, lam, da_norm_g)


def _first_index_of_max(x, iota_f, size):
    m = jnp.max(x, axis=0, keepdims=True)
    idx = jnp.min(jnp.where(x == m, iota_f, float(size)), axis=0, keepdims=True)
    return m, idx


def _route_kernel(hm_ref, hd_ref, x_ref, mod_ref, wo_ref, wr_ref, rb_ref,
                  x1_ref, ha_ref, hb_ref, idx_ref, gate_ref, rank_ref, cnt_ref, base_ref):
    tm = TM_ROUTE
    i = pl.program_id(0)

    @pl.when(i == 0)
    def _():
        base_ref[...] = jnp.zeros_like(base_ref)

    mix = _dot(hm_ref[...], wo_ref[0:ML_W, :]) + _dot(hd_ref[...], wo_ref[ML_W:ML_W + DA_W, :])
    x1 = x_ref[...] + mod_ref[0, 2:3, :] * mix
    x1_ref[...] = x1
    h2 = x1 * lax.rsqrt(jnp.mean(x1 * x1, axis=-1, keepdims=True) + EPS)
    h2 = h2 * (1.0 + mod_ref[0, 4:5, :]) + mod_ref[0, 3:4, :]
    packed = lax.bitcast_convert_type(_pack_bf16_pair(h2[:, 0:HALF], h2[:, HALF:D_MODEL]), jnp.int32)
    ha_ref[...] = packed[:, 0:QUARTER]
    hb_ref[...] = packed[:, QUARTER:HALF]

    sc = _sigmoid(_dot_nt(wr_ref[...], h2.astype(bf16)))
    sel = sc + rb_ref[...]

    gi = lax.broadcasted_iota(jnp.int32, (GROUP_SIZE, tm), 0).astype(f32)
    gscores = []
    for g in range(N_GROUPS):
        blk = sel[g * GROUP_SIZE:(g + 1) * GROUP_SIZE, :]
        m1, i1 = _first_index_of_max(blk, gi, GROUP_SIZE)
        m2 = jnp.max(jnp.where(gi == i1, NEG_BIG, blk), axis=0, keepdims=True)
        gscores.append(m1 + m2)
    gs = jnp.concatenate(gscores, axis=0)

    ngi = lax.broadcasted_iota(jnp.int32, (N_GROUPS, tm), 0).astype(f32)
    gkeep = jnp.zeros((N_GROUPS, tm), f32)
    for _ in range(TOPK_GROUPS):
        _, gidx = _first_index_of_max(gs, ngi, N_GROUPS)
        hit = ngi == gidx
        gkeep = jnp.where(hit, 1.0, gkeep)
        gs = jnp.where(hit, NEG_BIG, gs)
    masked = jnp.concatenate(
        [jnp.where(gkeep[g:g + 1, :] > 0.0, sel[g * GROUP_SIZE:(g + 1) * GROUP_SIZE, :], NEG_BIG)
         for g in range(N_GROUPS)], axis=0)

    ei = lax.broadcasted_iota(jnp.int32, (N_EXPERTS, tm), 0).astype(f32)
    idxs, gates = [], []
    chosen = jnp.zeros((N_EXPERTS, tm), f32)
    for _ in range(TOP_K):
        _, eidx = _first_index_of_max(masked, ei, N_EXPERTS)
        hit = ei == eidx
        idxs.append(eidx)
        gates.append(jnp.sum(jnp.where(hit, sc, 0.0), axis=0, keepdims=True))
        chosen = jnp.where(hit, 1.0, chosen)
        masked = jnp.where(hit, NEG_BIG, masked)
    gate = jnp.concatenate(gates, axis=0)
    gate = gate / jnp.sum(gate, axis=0, keepdims=True) * ROUTED_SCALE
    gate_ref[...] = gate
    idx_ref[...] = jnp.concatenate(idxs, axis=0).astype(jnp.int32)

    tr = lax.broadcasted_iota(jnp.int32, (tm, tm), 0)
    tc = lax.broadcasted_iota(jnp.int32, (tm, tm), 1)
    before = (tr < tc).astype(bf16)
    seen = base_ref[...] + _dot(chosen.astype(bf16), before)
    ranks = [jnp.sum(jnp.where(ei == eidx, seen, 0.0), axis=0, keepdims=True) for eidx in idxs]
    rank_ref[...] = jnp.concatenate(ranks, axis=0).astype(jnp.int32)
    total = base_ref[...] + jnp.sum(chosen, axis=1, keepdims=True)
    base_ref[...] = total
    cnt_ref[...] = total


def _route(hm, hd, x2, mod, w_out, w_router_t, rbias_col, seq):
    t, d = x2.shape
    tm = TM_ROUTE
    tiles_per_seq = seq // tm
    const = lambda shape: pl.BlockSpec(shape, lambda i: (0,) * len(shape))
    return pl.pallas_call(
        _route_kernel,
        out_shape=(jax.ShapeDtypeStruct((t, d), f32),
                   jax.ShapeDtypeStruct((t, QUARTER), jnp.int32),
                   jax.ShapeDtypeStruct((t, QUARTER), jnp.int32),
                   jax.ShapeDtypeStruct((TOP_K, t), jnp.int32),
                   jax.ShapeDtypeStruct((TOP_K, t), f32),
                   jax.ShapeDtypeStruct((TOP_K, t), jnp.int32),
                   jax.ShapeDtypeStruct((N_EXPERTS, 1), f32)),
        grid=(t // tm,),
        in_specs=[pl.BlockSpec((tm, ML_W), lambda i: (i, 0)),
                  pl.BlockSpec((tm, DA_W), lambda i: (i, 0)),
                  pl.BlockSpec((tm, d), lambda i: (i, 0)),
                  pl.BlockSpec((1, 6, d), lambda i: (i // tiles_per_seq, 0, 0)),
                  const((d, d)), const((N_EXPERTS, d)), const((N_EXPERTS, 1))],
        out_specs=(pl.BlockSpec((tm, d), lambda i: (i, 0)),
                   pl.BlockSpec((tm, QUARTER), lambda i: (i, 0)),
                   pl.BlockSpec((tm, QUARTER), lambda i: (i, 0)),
                   pl.BlockSpec((TOP_K, tm), lambda i: (0, i)),
                   pl.BlockSpec((TOP_K, tm), lambda i: (0, i)),
                   pl.BlockSpec((TOP_K, tm), lambda i: (0, i)),
                   const((N_EXPERTS, 1))),
        scratch_shapes=[pltpu.VMEM((N_EXPERTS, 1), f32)],
        compiler_params=_cparams(1),
        name="route",
    )(hm, hd, x2, mod, w_out, w_router_t, rbias_col)


SLOT_RADIX = 256
SLOT_DIGITS = 3


def _slots_kernel(idx_ref, rank_ref, dig_ref, dest_ref):
    tm = TM_SLOTS
    ei = lax.broadcasted_iota(jnp.int32, (N_EXPERTS, tm), 0)
    rows = []
    for k in range(TOP_K):
        onehot = jnp.where(ei == idx_ref[k:k + 1, :], 1.0, 0.0).astype(bf16)
        dg = _dot(dig_ref[...], onehot)
        start = dg[0:1, :]
        for j in range(1, SLOT_DIGITS):
            start = start + dg[j:j + 1, :] * float(SLOT_RADIX ** j)
        rows.append(start.astype(jnp.int32) + rank_ref[k:k + 1, :])
    dest_ref[...] = jnp.concatenate(rows, axis=0)


def _slots(idx, rank, digits):
    t = idx.shape[1]
    tm = TM_SLOTS
    return pl.pallas_call(
        _slots_kernel,
        out_shape=jax.ShapeDtypeStruct((TOP_K, t), jnp.int32),
        grid=(t // tm,),
        in_specs=[pl.BlockSpec((TOP_K, tm), lambda i: (0, i)),
                  pl.BlockSpec((TOP_K, tm), lambda i: (0, i)),
                  pl.BlockSpec((SUBLANES, N_EXPERTS), lambda i: (0, 0))],
        out_specs=pl.BlockSpec((TOP_K, tm), lambda i: (0, i)),
        compiler_params=_cparams(1),
        name="slots",
    )(idx, rank, digits)


def _sc_scatter_table(rows, dest, n_slots):
    n, width = rows.shape
    mesh = plsc.VectorSubcoreMesh(core_axis_name="core", subcore_axis_name="subcore")

    @pl.kernel(out_type=jax.ShapeDtypeStruct((n_slots, width), rows.dtype), mesh=mesh, scratch_types=[],
               name="sc_scatter")
    def scatter_kernel(rows_hbm, dest_hbm, out_hbm):
        def body(rows_vmem, dest_vmem):
            for k in range(TOP_K):
                pltpu.sync_copy(rows_vmem, out_hbm.at[dest_vmem.at[k]])

        pltpu.emit_pipeline(
            body,
            grid=(n // SC_WINDOW,),
            in_specs=[pl.BlockSpec((SC_WINDOW, width), lambda i: (i, 0)),
                      pl.BlockSpec((TOP_K, SC_WINDOW), lambda i: (0, i))],
            out_specs=[],
            core_axis_name=("core", "subcore"),
            dimension_semantics=(pltpu.PARALLEL,),
        )(rows_hbm, dest_hbm)

    return scatter_kernel(rows, dest)


EXPERT_GROUP = 4
EXPERT_RING = 16
EXPERT_AHEAD = EXPERT_RING - EXPERT_GROUP


def _expert_kernel(start_ref, cnt_ref, total_ref, xsa_ref, xsb_ref, w1_ref, w3_ref, w2_ref,
                   ysa_ref, ysb_ref, xbuf, ybuf, w1b, w3b, w2b, in_sem, out_sem):
    bm = MOE_BLOCK
    ring = EXPERT_RING
    ahead = EXPERT_AHEAD
    shift = int(math.log2(bm))
    e = pl.program_id(0)
    total = total_ref[0]
    cnt = cnt_ref[e]
    nb = lax.shift_right_logical(cnt + (bm - 1), shift)
    g0 = lax.shift_right_logical(start_ref[e], shift)

    def slot_of(g):
        return jnp.bitwise_and(g, ring - 1)

    def in_copies(g):
        s = slot_of(g)
        rows = pl.ds(pl.multiple_of(g * bm, bm), bm)
        return [pltpu.make_async_copy(x_ref.at[rows], xbuf.at[s, half], in_sem.at[s])
                for half, x_ref in enumerate((xsa_ref, xsb_ref))]

    def out_copies(g):
        s = slot_of(g)
        rows = pl.ds(pl.multiple_of(g * bm, bm), bm)
        return [pltpu.make_async_copy(ybuf.at[s, half], y_ref.at[rows], out_sem.at[s])
                for half, y_ref in enumerate((ysa_ref, ysb_ref))]

    def request(g):
        @pl.when(g < total)
        def _():
            for cp in in_copies(g):
                cp.start()

    def acquire(g):
        for cp in in_copies(g):
            cp.wait()

        @pl.when(g >= ring)
        def _():
            for cp in out_copies(g - ring):
                cp.wait()

    def compute(g):
        s = slot_of(g)
        row = (g - g0) * bm + lax.broadcasted_iota(jnp.int32, (bm, HALF), 0)
        words = jnp.concatenate([xbuf[s, 0], xbuf[s, 1]], axis=1)
        lo, hi = _unpack_bf16_pair(lax.bitcast_convert_type(jnp.where(row < cnt, words, 0), jnp.uint32))
        lo = lo.astype(bf16)
        hi = hi.astype(bf16)
        h1 = _dot(lo, w1b[0:HALF, :]) + _dot(hi, w1b[HALF:D_MODEL, :])
        h3 = _dot(lo, w3b[0:HALF, :]) + _dot(hi, w3b[HALF:D_MODEL, :])
        y = _dot((_silu(h1) * h3).astype(bf16), w2b[...])
        packed = lax.bitcast_convert_type(_pack_bf16_pair(y[:, 0:HALF], y[:, HALF:D_MODEL]), jnp.int32)
        ybuf[s, 0] = packed[:, 0:QUARTER]
        ybuf[s, 1] = packed[:, QUARTER:HALF]

    def release(g):
        for cp in out_copies(g):
            cp.start()

    @pl.when(e == 0)
    def _():
        for g in range(ahead):
            request(g)

    @pl.when(nb > 0)
    def _():
        w1b[...] = w1_ref[0].astype(bf16)
        w3b[...] = w3_ref[0].astype(bf16)
        w2b[...] = w2_ref[0].astype(bf16)

    def run(g, n):
        for d in range(n):
            request(g + d + ahead)
        for d in range(n):
            acquire(g + d)
        for d in range(n):
            compute(g + d)
        for d in range(n):
            release(g + d)

    def group(i, carry):
        run(g0 + EXPERT_GROUP * i, EXPERT_GROUP)
        return carry

    lax.fori_loop(0, lax.shift_right_logical(nb, int(math.log2(EXPERT_GROUP))), group, 0)

    n = EXPERT_GROUP // 2
    while n >= 1:
        @pl.when(jnp.bitwise_and(nb, n) != 0)
        def _(n=n):
            done = jnp.bitwise_and(nb, -2 * n)
            run(g0 + done, n)
        n //= 2

    @pl.when(e == pl.num_programs(0) - 1)
    def _():
        for back in range(1, ring + 1):
            @pl.when(total - back >= 0)
            def _():
                for cp in out_copies(total - back):
                    cp.wait()


def _experts(start, counts, total, xs_a, xs_b, w1, w3, w2):
    n_pad = xs_a.shape[0]
    bm = MOE_BLOCK
    grid_spec = pltpu.PrefetchScalarGridSpec(
        num_scalar_prefetch=3,
        grid=(N_EXPERTS,),
        in_specs=[pl.BlockSpec(memory_space=pl.ANY),
                  pl.BlockSpec(memory_space=pl.ANY),
                  pl.BlockSpec((1, D_MODEL, D_EXPERT), lambda e, st, ct, tt: (e, 0, 0)),
                  pl.BlockSpec((1, D_MODEL, D_EXPERT), lambda e, st, ct, tt: (e, 0, 0)),
                  pl.BlockSpec((1, D_EXPERT, D_MODEL), lambda e, st, ct, tt: (e, 0, 0))],
        out_specs=(pl.BlockSpec(memory_space=pl.ANY), pl.BlockSpec(memory_space=pl.ANY)),
        scratch_shapes=[pltpu.VMEM((EXPERT_RING, 2, bm, QUARTER), jnp.int32),
                        pltpu.VMEM((EXPERT_RING, 2, bm, QUARTER), jnp.int32),
                        pltpu.VMEM((D_MODEL, D_EXPERT), bf16),
                        pltpu.VMEM((D_MODEL, D_EXPERT), bf16),
                        pltpu.VMEM((D_EXPERT, D_MODEL), bf16),
                        pltpu.SemaphoreType.DMA((EXPERT_RING,)),
                        pltpu.SemaphoreType.DMA((EXPERT_RING,))],
    )
    return pl.pallas_call(
        _expert_kernel,
        out_shape=(jax.ShapeDtypeStruct((n_pad, QUARTER), jnp.int32),
                   jax.ShapeDtypeStruct((n_pad, QUARTER), jnp.int32)),
        grid_spec=grid_spec,
        compiler_params=_cparams(1),
        name="experts",
    )(start, counts, total, xs_a, xs_b, w1, w3, w2)


def _sc_gather_table(table, idx2):
    n = idx2.shape[1]
    width = table.shape[1]
    mesh = plsc.VectorSubcoreMesh(core_axis_name="core", subcore_axis_name="subcore")

    @pl.kernel(out_type=jax.ShapeDtypeStruct((n, width), table.dtype), mesh=mesh, scratch_types=[],
               name="sc_gather")
    def gather_kernel(tab_hbm, idx_hbm, out_hbm):
        def body(idx_vmem, out_vmem):
            pltpu.sync_copy(tab_hbm.at[idx_vmem.at[0]], out_vmem)

        pltpu.emit_pipeline(
            body,
            grid=(n // SC_WINDOW,),
            in_specs=[pl.BlockSpec((1, SC_WINDOW), lambda i: (0, i))],
            out_specs=[pl.BlockSpec((SC_WINDOW, width), lambda i: (i, 0))],
            core_axis_name=("core", "subcore"),
            dimension_semantics=(pltpu.PARALLEL,),
        )(idx_hbm, out_hbm)

    return gather_kernel(table, idx2)


def _sc_gather_rows(table_a, table_b, idx):
    idx2 = idx.reshape(1, -1)
    return _sc_gather_table(table_a, idx2), _sc_gather_table(table_b, idx2)


def _combine_kernel(gate_ref, ha_ref, hb_ref, x1_ref, mod_ref, ws1_ref, ws3_ref, ws2_ref, ga_ref, gb_ref,
                    *rest):
    out_ref = rest[-1]
    words = jnp.concatenate([ha_ref[...], hb_ref[...]], axis=1)
    lo, hi = _unpack_bf16_pair(lax.bitcast_convert_type(words, jnp.uint32))
    lo = lo.astype(bf16)
    hi = hi.astype(bf16)
    s1 = _dot(lo, ws1_ref[0:HALF, :]) + _dot(hi, ws1_ref[HALF:D_MODEL, :])
    s3 = _dot(lo, ws3_ref[0:HALF, :]) + _dot(hi, ws3_ref[HALF:D_MODEL, :])
    y = _dot((_silu(s1) * s3).astype(bf16), ws2_ref[...])

    gate = gate_ref[...]
    gate_f = mod_ref[0, 5:6, :]
    for part, g_ref in enumerate((ga_ref, gb_ref)):
        c_lo = slice(part * QUARTER, (part + 1) * QUARTER)
        c_hi = slice(HALF + part * QUARTER, HALF + (part + 1) * QUARTER)
        acc_lo = y[:, c_lo]
        acc_hi = y[:, c_hi]
        for k in range(TOP_K):
            rlo, rhi = _unpack_bf16_pair(lax.bitcast_convert_type(g_ref[k], jnp.uint32))
            gk = gate[:, k:k + 1]
            acc_lo = acc_lo + gk * rlo
            acc_hi = acc_hi + gk * rhi
        out_ref[:, c_lo] = x1_ref[:, c_lo] + gate_f[:, c_lo] * acc_lo
        out_ref[:, c_hi] = x1_ref[:, c_hi] + gate_f[:, c_hi] * acc_hi


def _combine(gate_col, h2a, h2b, x1, mod, ws1, ws3, ws2, ga, gb, seq, part, prev_out=None):
    t, d = x1.shape
    tm = TM_MOE
    tiles_per_seq = seq // tm
    n_tiles = ga.shape[1] // tm
    off = part * n_tiles
    const = lambda shape: pl.BlockSpec(shape, lambda i: (0,) * len(shape))
    in_specs = [pl.BlockSpec((tm, TOP_K), lambda i: (i + off, 0)),
                pl.BlockSpec((tm, QUARTER), lambda i: (i + off, 0)),
                pl.BlockSpec((tm, QUARTER), lambda i: (i + off, 0)),
                pl.BlockSpec((tm, d), lambda i: (i + off, 0)),
                pl.BlockSpec((1, 6, d), lambda i: ((i + off) // tiles_per_seq, 0, 0)),
                const((d, D_EXPERT)), const((d, D_EXPERT)), const((D_EXPERT, d)),
                pl.BlockSpec((TOP_K, tm, QUARTER), lambda i: (0, i, 0)),
                pl.BlockSpec((TOP_K, tm, QUARTER), lambda i: (0, i, 0))]
    args = [gate_col, h2a, h2b, x1, mod, ws1, ws3, ws2, ga, gb]
    aliases = {}
    if prev_out is not None:
        in_specs.append(pl.BlockSpec(memory_space=pl.ANY))
        args.append(prev_out)
        aliases = {len(args) - 1: 0}
    return pl.pallas_call(
        _combine_kernel,
        out_shape=jax.ShapeDtypeStruct((t, d), f32),
        grid=(n_tiles,),
        in_specs=in_specs,
        out_specs=pl.BlockSpec((tm, d), lambda i: (i + off, 0)),
        input_output_aliases=aliases,
        compiler_params=_cparams(1),
        name="combine",
    )(*args)


def _lambda_init(layer):
    return 0.8 - 0.6 * math.exp(-0.3 * layer)


def _layer(x, c, w_ada, b_ada, w_in, conv_w, conv_b, gate_b, ml_norm_g, da_q_norm_g, da_k_norm_g,
           lambda_q1, lambda_k1, lambda_q2, lambda_k2, da_norm_g, w_out, w_router, router_bias,
           w1, w3, w2, ws1, ws3, ws2, layer):
    batch, seq, d = x.shape
    t = batch * seq
    lam_init = _lambda_init(layer)
    x2 = x.reshape(t, d)

    c_pad = jnp.pad(c, ((0, -batch % SUBLANES), (0, 0)))
    mod = _adaln(c_pad, w_ada, b_ada.reshape(1, -1))[:batch].reshape(batch, 6, d)

    g0 = 4 * ML_W
    q0 = g0 + N_GATES
    n_grp = 2 * DA_HEADS
    w_main = jnp.concatenate([w_in[:, :g0], w_in[:, q0:]], axis=1).astype(bf16)
    wg = jnp.pad(w_in[:, g0:g0 + N_GATES], ((0, 0), (0, LANES - N_GATES))).astype(bf16)
    gb_row = jnp.pad(gate_b, (0, LANES - N_GATES)).reshape(1, LANES)
    qgain = (jnp.tile(da_q_norm_g, n_grp) * (DA_DH ** -0.5 * LOG2E)).reshape(1, DA_W)
    kgain = jnp.tile(da_k_norm_g, n_grp).reshape(1, DA_W)
    seg = jnp.arange(DA_W) // DA_DH
    bd = (seg[:, None] == seg[None, :]).astype(bf16)

    pos = jnp.arange(seq, dtype=f32)[:, None]
    slopes = jnp.asarray([_alibi_slope_log2(h) for h in range(DA_HEADS)], f32)[None, :]
    r = pos * slopes

    def bf16_part(a):
        bits = lax.bitcast_convert_type(a, jnp.uint32) & jnp.uint32(0xFFFF0000)
        return lax.bitcast_convert_type(bits, f32)

    r_hi = bf16_part(r)
    r_mid = bf16_part(r - r_hi)
    r_lo = bf16_part(r - r_hi - r_mid)
    parts = jnp.stack([r_hi, r_mid, r_lo], axis=-1)
    kaug = jnp.pad(parts, ((0, 0), (0, 0), (DA_DH, LANES - DA_DH - 3))).reshape(seq, DA_HEADS * LANES)

    pm, vt, ot, qh, kh, vh, gcol, grow = _inproj(x2, mod, w_main, wg, gb_row, qgain, kgain, bd, kaug.astype(bf16),
                                         conv_w, conv_b.reshape(1, -1), seq)

    hm = _mlstm(pm, vt, ot, gcol, grow, ml_norm_g.reshape(-1, 1), batch, seq)

    lam = (jnp.exp(jnp.sum(lambda_q1 * lambda_k1)) - jnp.exp(jnp.sum(lambda_q2 * lambda_k2))
           + lam_init).reshape(1, 1).astype(f32)
    hd = _attention(qh, kh, vh, lam, da_norm_g.reshape(1, -1), batch, seq, lam_init)

    x1, h2a, h2b, idx, gate, rank, counts = _route(hm, hd, x2, mod, w_out.astype(bf16),
                                                   w_router.T.astype(bf16), router_bias.reshape(-1, 1), seq)

    bm = MOE_BLOCK
    n_blocks = (t * TOP_K) // bm + N_EXPERTS
    counts_i = counts.reshape(-1).astype(jnp.int32)
    padded = (counts_i + bm - 1) // bm * bm
    pad_end = jnp.cumsum(padded)
    pad_start = pad_end - padded
    digits = jnp.stack([(pad_start // SLOT_RADIX ** j) % SLOT_RADIX for j in range(SLOT_DIGITS)])
    digits = jnp.pad(digits, ((0, SUBLANES - SLOT_DIGITS), (0, 0))).astype(bf16)
    dest = _slots(idx, rank, digits)

    xs_a = _sc_scatter_table(h2a, dest, n_blocks * bm)
    xs_b = _sc_scatter_table(h2b, dest, n_blocks * bm)
    total_blocks = (pad_end[-1:] // bm).astype(jnp.int32)
    ys_a, ys_b = _experts(pad_start.astype(jnp.int32), counts_i, total_blocks, xs_a, xs_b, w1, w3, w2)
    gate_col = gate.T
    shared_w = (ws1.astype(bf16), ws3.astype(bf16), ws2.astype(bf16))
    n = t // COMBINE_PARTS
    out = None
    for part in range(COMBINE_PARTS):
        idx_part = dest[:, part * n:(part + 1) * n].reshape(-1)
        ga, gb = _sc_gather_rows(ys_a, ys_b, idx_part)
        out = _combine(gate_col, h2a, h2b, x1, mod, *shared_w,
                       ga.reshape(TOP_K, n, QUARTER), gb.reshape(TOP_K, n, QUARTER), seq, part, out)
    return out.reshape(batch, seq, d)


def kernel(x, c, w_ada, b_ada, w_in, conv_w, conv_b, gate_b, ml_norm_g, da_q_norm_g, da_k_norm_g,
           lambda_q1, lambda_k1, lambda_q2, lambda_k2, da_norm_g, w_out, w_router, router_bias,
           w1, w3, w2, ws1, ws3, ws2):
    depth = w_ada.shape[0]
    for l in range(depth):
        x = _layer(x, c, w_ada[l], b_ada[l], w_in[l], conv_w[l], conv_b[l], gate_b[l], ml_norm_g[l],
                   da_q_norm_g[l], da_k_norm_g[l], lambda_q1[l], lambda_k1[l], lambda_q2[l], lambda_k2[l],
                   da_norm_g[l], w_out[l], w_router[l], router_bias[l], w1[l], w3[l], w2[l],
                   ws1[l], ws3[l], ws2[l], l)
    return x
```

```python
import functools
import math

import jax
import jax.numpy as jnp
import numpy as np
from jax import lax
from jax.experimental import pallas as pl
from jax.experimental.pallas import tpu as pltpu
from jax.experimental.pallas import tpu_sc as plsc

D_MODEL = 1024
ML_HEADS = 4
ML_DH = 128
ML_W = ML_HEADS * ML_DH
ML_CHUNK = 128
CONV_K = 4
DA_HEADS = 4
DA_DH = 64
DA_W = DA_HEADS * 2 * DA_DH
N_EXPERTS = 256
TOP_K = 8
N_GROUPS = 8
GROUP_SIZE = N_EXPERTS // N_GROUPS
TOPK_GROUPS = 4
D_EXPERT = 256
ROUTED_SCALE = 2.5
EPS = 1e-6
N_GATES = 2 * ML_HEADS

LANES = 128
SUBLANES = 8
VMEM_LIMIT_BYTES = 48 * 1024 * 1024

TM_INPROJ = 512
ML_STEP_CHUNKS = 4
ATT_BLOCK = 1024
ATT_QBLOCKS = 2
TM_ROUTE = 512
TM_SLOTS = 512
TM_MOE = 512
MOE_BLOCK = 256
HALF = D_MODEL // 2
QUARTER = HALF // 2
SC_WINDOW = 128

NEG_BIG = -1e30
LOG2E = 1.4426950408889634

f32 = jnp.float32
bf16 = jnp.bfloat16
HIGHEST = lax.Precision.HIGHEST


def _cparams(n_axes):
    return pltpu.CompilerParams(dimension_semantics=("arbitrary",) * n_axes,
                                vmem_limit_bytes=VMEM_LIMIT_BYTES)


def _dot(a, b):
    return jnp.dot(a, b, preferred_element_type=f32)


def _dot_nt(a, b):
    return lax.dot_general(a, b, (((1,), (1,)), ((), ())), preferred_element_type=f32)


def _sigmoid(x):
    return 1.0 / (1.0 + jnp.exp(-x))


def _silu(x):
    return x * _sigmoid(x)


def _log_sigmoid(x):
    return jnp.minimum(x, 0.0) - jnp.log(1.0 + jnp.exp(-jnp.abs(x)))


def _pack_bf16_pair(lo, hi):
    lo_bits = lax.bitcast_convert_type(lo.astype(bf16).astype(f32), jnp.uint32)
    hi_bits = lax.bitcast_convert_type(hi.astype(bf16).astype(f32), jnp.uint32)
    return (hi_bits & jnp.uint32(0xFFFF0000)) | (lo_bits >> 16)


def _unpack_bf16_pair(w):
    lo = lax.bitcast_convert_type(w << 16, f32)
    hi = lax.bitcast_convert_type(w & jnp.uint32(0xFFFF0000), f32)
    return lo, hi


def _adaln_kernel(c_ref, w_ref, b_ref, o_ref):
    c = c_ref[...]
    o_ref[...] = jnp.dot(_silu(c), w_ref[...], precision=HIGHEST, preferred_element_type=f32) + b_ref[...]


def _adaln(c_pad, w_ada, b_ada):
    rows, d = c_pad.shape
    n = w_ada.shape[1]
    tn = 1536
    return pl.pallas_call(
        _adaln_kernel,
        out_shape=jax.ShapeDtypeStruct((rows, n), f32),
        grid=(n // tn,),
        in_specs=[pl.BlockSpec((rows, d), lambda j: (0, 0)),
                  pl.BlockSpec((d, tn), lambda j: (0, j)),
                  pl.BlockSpec((1, tn), lambda j: (0, j))],
        out_specs=pl.BlockSpec((rows, tn), lambda j: (0, j)),
        compiler_params=_cparams(1),
        name="adaln",
    )(c_pad, w_ada, b_ada)


def _alibi_slope_log2(head):
    return 2.0 ** (-8.0 * (head + 1) / DA_HEADS) * LOG2E


def _alibi_table(seq):
    pos = np.arange(seq, dtype=np.float32)[:, None]
    slopes = np.asarray([_alibi_slope_log2(h) for h in range(DA_HEADS)], np.float32)[None, :]

    def bf16_part(a):
        return (np.ascontiguousarray(a).view(np.uint32) & np.uint32(0xFFFF0000)).view(np.float32)

    r = pos * slopes
    r_hi = bf16_part(r)
    r_mid = bf16_part(r - r_hi)
    r_lo = bf16_part(r - r_hi - r_mid)
    table = np.zeros((seq, DA_HEADS, LANES), np.float32)
    for lane_off, part in enumerate((r_hi, r_mid, r_lo)):
        table[:, :, DA_DH + lane_off] = part
    return jnp.asarray(table.reshape(seq, DA_HEADS * LANES), dtype=bf16)


def _inproj_kernel(x_ref, mod_ref, w_ref, wg_ref, gbr_ref, qg_ref, kg_ref, bd_ref, kaug_ref, cw_ref, cb_ref,
                   pm_ref, vt_ref, ot_ref, q_ref, k_ref, v_ref, gcol_ref, grow_ref, ext_ref, *, seq):
    tm = TM_INPROJ
    x = x_ref[...]
    h = x * lax.rsqrt(jnp.mean(x * x, axis=-1, keepdims=True) + EPS)
    h = h * (1.0 + mod_ref[0, 1:2, :]) + mod_ref[0, 0:1, :]
    hb = h.astype(bf16)
    cw = ML_W

    @pl.when((pl.program_id(0) * tm) % seq == 0)
    def _():
        ext_ref[0:SUBLANES, :] = jnp.zeros((SUBLANES, 2 * cw), f32)

    cur = _dot(hb, w_ref[:, 0:2 * cw])
    ext_ref[SUBLANES:SUBLANES + tm, :] = cur
    acc = cb_ref[...] + cw_ref[CONV_K - 1:CONV_K, :] * cur
    for j in range(CONV_K - 1):
        off = SUBLANES - (CONV_K - 1) + j
        acc = acc + cw_ref[j:j + 1, :] * ext_ref[off:off + tm, :]
    ext_ref[0:SUBLANES, :] = cur[tm - SUBLANES:tm, :]
    qk = _silu(acc)
    pm_ref[:, 0:cw] = qk[:, 0:cw].astype(bf16)
    pm_ref[:, cw:2 * cw] = (qk[:, cw:2 * cw] * (ML_DH ** -0.5)).astype(bf16)
    vt_ref[...] = _dot(hb, w_ref[:, 2 * cw:3 * cw]).T.astype(bf16)
    ot_ref[...] = _dot(hb, w_ref[:, 3 * cw:4 * cw]).T.astype(bf16)

    lane = lax.broadcasted_iota(jnp.int32, (tm, LANES), 1)
    feat = lane < DA_DH
    q_aug = jnp.where(jnp.logical_and(lane >= DA_DH, lane < DA_DH + 3), 1.0, 0.0)
    q0 = 4 * cw
    k0 = q0 + DA_W

    def qk_normed(col0, gain_ref):
        y = _dot(hb, w_ref[:, col0:col0 + DA_W])
        ss = _dot((y * y).astype(bf16), bd_ref[...])
        return y * lax.rsqrt(ss * (1.0 / DA_DH) + EPS) * gain_ref[...]

    def lane_group(y2, grp, aug):
        pair = y2[:, (grp // 2) * LANES:(grp // 2 + 1) * LANES]
        if grp % 2:
            pair = pltpu.roll(pair, DA_DH, axis=1)
        return jnp.where(feat, pair, aug).astype(bf16)

    nq = qk_normed(q0, qg_ref)
    nk = qk_normed(k0, kg_ref)
    for grp in range(2 * DA_HEADS):
        sl = slice(grp * LANES, (grp + 1) * LANES)
        head = slice((grp // 2) * LANES, (grp // 2 + 1) * LANES)
        q_ref[:, sl] = lane_group(nq, grp, q_aug)
        k_ref[:, sl] = lane_group(nk, grp, kaug_ref[:, head].astype(f32))

    v0 = k0 + DA_W
    v_ref[...] = _dot(hb, w_ref[:, v0:v0 + DA_W]).astype(bf16)
    gates = _dot(hb, wg_ref[...]) + gbr_ref[...]
    gcol_ref[...] = gates
    grow_ref[...] = gates.T[0:N_GATES, :]


def _inproj(x2, mod, w_main, wg, gb_row, qgain, kgain, bd, kaug, conv_w, conv_b, seq):
    t, d = x2.shape
    tm = TM_INPROJ
    tiles_per_seq = seq // tm
    n_main = w_main.shape[1]
    const = lambda shape: pl.BlockSpec(shape, lambda i: (0,) * len(shape))
    return pl.pallas_call(
        functools.partial(_inproj_kernel, seq=seq),
        out_shape=(jax.ShapeDtypeStruct((t, 2 * ML_W), bf16),
                   jax.ShapeDtypeStruct((ML_W, t), bf16),
                   jax.ShapeDtypeStruct((ML_W, t), bf16),
                   jax.ShapeDtypeStruct((t, 2 * DA_W), bf16),
                   jax.ShapeDtypeStruct((t, 2 * DA_W), bf16),
                   jax.ShapeDtypeStruct((t, DA_W), bf16),
                   jax.ShapeDtypeStruct((t, LANES), f32),
                   jax.ShapeDtypeStruct((N_GATES, t), f32)),
        grid=(t // tm,),
        in_specs=[pl.BlockSpec((tm, d), lambda i: (i, 0)),
                  pl.BlockSpec((1, 6, d), lambda i: (i // tiles_per_seq, 0, 0)),
                  const((d, n_main)), const((d, LANES)), const((1, LANES)),
                  const((1, DA_W)), const((1, DA_W)), const((DA_W, DA_W)),
                  pl.BlockSpec((tm, DA_HEADS * LANES), lambda i: (i % tiles_per_seq, 0)),
                  const((CONV_K, 2 * ML_W)), const((1, 2 * ML_W))],
        out_specs=(pl.BlockSpec((tm, 2 * ML_W), lambda i: (i, 0)),
                   pl.BlockSpec((ML_W, tm), lambda i: (0, i)),
                   pl.BlockSpec((ML_W, tm), lambda i: (0, i)),
                   pl.BlockSpec((tm, 2 * DA_W), lambda i: (i, 0)),
                   pl.BlockSpec((tm, 2 * DA_W), lambda i: (i, 0)),
                   pl.BlockSpec((tm, DA_W), lambda i: (i, 0)),
                   pl.BlockSpec((tm, LANES), lambda i: (i, 0)),
                   pl.BlockSpec((N_GATES, tm), lambda i: (0, i))),
        scratch_shapes=[pltpu.VMEM((SUBLANES + tm, 2 * ML_W), f32)],
        compiler_params=_cparams(1),
        name="inproj",
    )(x2, mod, w_main, wg, gb_row, qgain, kgain, bd, kaug, conv_w, conv_b)


def _mlstm_kernel(q_ref, k_ref, vt_ref, ot_ref, gcol_ref, grow_ref, ng_ref, out_ref, state_ref, m_ref):
    L = ML_CHUNK
    dh = ML_DH

    @pl.when(pl.program_id(1) == 0)
    def _():
        state_ref[...] = jnp.zeros_like(state_ref)
        m_ref[...] = jnp.zeros_like(m_ref)

    s_i = lax.broadcasted_iota(jnp.int32, (L, L), 0)
    t_i = lax.broadcasted_iota(jnp.int32, (L, L), 1)
    visible = s_i <= t_i
    tril = (s_i >= t_i).astype(f32)
    triu = visible.astype(f32)
    ones_rows = (lax.broadcasted_iota(jnp.int32, (dh, L), 0) == 0).astype(f32)
    g_b = jnp.broadcast_to(ng_ref[...], (dh, L))

    for cc in range(ML_STEP_CHUNKS):
        rows = slice(cc * L, (cc + 1) * L)
        gcol = gcol_ref[rows, :]
        grow = grow_ref[:, rows]
        bcol_all = jnp.dot(tril, _log_sigmoid(gcol), precision=HIGHEST, preferred_element_type=f32)
        brow_all = jnp.dot(_log_sigmoid(grow), triu, precision=HIGHEST, preferred_element_type=f32)

        for hd in range(ML_HEADS):
            cols = slice(hd * dh, (hd + 1) * dh)
            qb = q_ref[rows, cols]
            kb = k_ref[rows, cols]
            v_aug_t = jnp.concatenate([vt_ref[cols, rows].astype(f32), ones_rows], axis=0)

            brow = brow_all[ML_HEADS + hd:ML_HEADS + hd + 1, :]
            irow = grow[hd:hd + 1, :]
            key_term = gcol[:, hd:hd + 1] - bcol_all[:, ML_HEADS + hd:ML_HEADS + hd + 1]
            m_prev = m_ref[hd:hd + 1, :]

            d_intra = jnp.where(visible, brow + key_term, NEG_BIG)
            d_inter = brow + m_prev
            m_t = jnp.maximum(d_inter, jnp.max(d_intra, axis=0, keepdims=True))
            w_intra = jnp.exp(d_intra - m_t)
            w_inter = jnp.exp(d_inter - m_t)

            s_t = _dot_nt(kb, qb) * w_intra
            state = state_ref[hd]
            q_state = _dot_nt(state.astype(bf16), qb)
            s_v = _dot(v_aug_t.astype(bf16), s_t.astype(bf16))
            num = w_inter * q_state[0:dh, :] + s_v[0:dh, :]
            den = w_inter * q_state[dh:dh + 1, :] + s_v[dh:dh + 1, :]
            hval = num / jnp.maximum(jnp.abs(den), jnp.exp(-m_t))

            b_last = brow[:, L - 1:L]
            d_state = b_last - brow + irow
            m_new = jnp.maximum(b_last + m_prev, jnp.max(d_state, axis=1, keepdims=True))
            carry_scale = jnp.exp(b_last + m_prev - m_new)[:, 0:1]
            wk = jnp.exp(d_state - m_new)
            upd = _dot((v_aug_t * wk).astype(bf16), kb)
            state_ref[hd] = carry_scale * state + upd
            m_ref[hd:hd + 1, :] = m_new

            hn = hval * lax.rsqrt(jnp.mean(hval * hval, axis=0, keepdims=True) + EPS) * g_b
            og = _sigmoid(ot_ref[cols, rows].astype(f32))
            out_ref[rows, cols] = (hn * og).T.astype(bf16)


def _mlstm(pm, vt, ot, gcol, grow, ml_norm_g, batch, seq):
    t = pm.shape[0]
    rows = ML_STEP_CHUNKS * ML_CHUNK
    ns = seq // rows
    row = lambda b, c: b * ns + c
    return pl.pallas_call(
        _mlstm_kernel,
        out_shape=jax.ShapeDtypeStruct((t, ML_W), bf16),
        grid=(batch, ns),
        in_specs=[pl.BlockSpec((rows, ML_W), lambda b, c: (row(b, c), 0)),
                  pl.BlockSpec((rows, ML_W), lambda b, c: (row(b, c), 1)),
                  pl.BlockSpec((ML_W, rows), lambda b, c: (0, row(b, c))),
                  pl.BlockSpec((ML_W, rows), lambda b, c: (0, row(b, c))),
                  pl.BlockSpec((rows, LANES), lambda b, c: (row(b, c), 0)),
                  pl.BlockSpec((N_GATES, rows), lambda b, c: (0, row(b, c))),
                  pl.BlockSpec((ML_DH, 1), lambda b, c: (0, 0))],
        out_specs=pl.BlockSpec((rows, ML_W), lambda b, c: (row(b, c), 0)),
        scratch_shapes=[pltpu.VMEM((ML_HEADS, 2 * ML_DH, ML_DH), f32),
                        pltpu.VMEM((SUBLANES, LANES), f32)],
        compiler_params=_cparams(2),
        name="mlstm",
    )(pm, pm, vt, ot, gcol, grow, ml_norm_g)


def _attn_block(q_ref, k_ref, v_ref, m_ref, l_ref, acc_ref, row0, nrows, nkeys, key0=None):
    rows = slice(row0, row0 + nrows)
    nch = nkeys // LANES
    v = v_ref[0:nkeys, :]
    if key0 is not None:
        keep = (key0 + lax.broadcasted_iota(jnp.int32, (nrows, nkeys), 1)
                <= row0 + lax.broadcasted_iota(jnp.int32, (nrows, nkeys), 0))
    for c in range(2):
        sl = slice(c * LANES, (c + 1) * LANES)
        s = _dot_nt(q_ref[rows, sl], k_ref[0:nkeys, sl])
        if key0 is not None:
            s = jnp.where(keep, s, NEG_BIG)
        chunks = [s[:, j * LANES:(j + 1) * LANES] for j in range(nch)]
        mc = chunks[0]
        for ch in chunks[1:]:
            mc = jnp.maximum(mc, ch)
        m_old = m_ref[c, rows, :]
        m_new = jnp.maximum(m_old, jnp.max(mc, axis=1, keepdims=True))
        alpha = jnp.exp2(m_old - m_new)
        ps = [jnp.exp2(ch - m_new) for ch in chunks]
        lsum = ps[0]
        for pj in ps[1:]:
            lsum = lsum + pj
        p = jnp.concatenate([pj.astype(bf16) for pj in ps], axis=1)
        l_ref[c, rows, :] = alpha * l_ref[c, rows, :] + lsum
        acc_ref[c, rows, :] = alpha * acc_ref[c, rows, :] + _dot(p, v)
        m_ref[c, rows, :] = m_new


def _attn_kernel(qt_ref, kt_ref, q_ref, k_ref, v_ref, lam_ref, g_ref, o_ref,
                 m_ref, l_ref, acc_ref, *, lam_init):
    step = pl.program_id(2)
    qi = qt_ref[step]
    ki = kt_ref[step]

    @pl.when(ki == 0)
    def _():
        m_ref[...] = jnp.full_like(m_ref, NEG_BIG)
        l_ref[...] = jnp.zeros_like(l_ref)
        acc_ref[...] = jnp.zeros_like(acc_ref)

    blk = ATT_BLOCK
    half = blk // 2
    qrows = ATT_QBLOCKS * blk
    first_diag = ATT_QBLOCKS * qi
    args = (q_ref, k_ref, v_ref, m_ref, l_ref, acc_ref)

    @pl.when(ki < first_diag)
    def _():
        _attn_block(*args, 0, qrows, blk)

    for j in range(ATT_QBLOCKS):
        @pl.when(ki == first_diag + j)
        def _(j=j):
            r0 = j * blk
            _attn_block(*args, r0, half, half, key0=r0)
            _attn_block(*args, r0 + half, half, blk, key0=r0)
            if r0 + blk < qrows:
                _attn_block(*args, r0 + blk, qrows - r0 - blk, blk)

    @pl.when(ki == first_diag + ATT_QBLOCKS - 1)
    def _():
        lam = lam_ref[...]
        l0 = jnp.sum(l_ref[0], axis=1, keepdims=True)
        l1 = jnp.sum(l_ref[1], axis=1, keepdims=True)
        o = acc_ref[0] / l0 - lam * (acc_ref[1] / l1)
        o = o * lax.rsqrt(jnp.mean(o * o, axis=-1, keepdims=True) + EPS)
        o_ref[...] = (o * g_ref[...] * (1.0 - lam_init)).astype(bf16)


def _attention(qh, kh, vh, lam, da_norm_g, batch, seq, lam_init):
    t = qh.shape[0]
    blk = ATT_BLOCK
    qrows = ATT_QBLOCKS * blk
    assert seq % qrows == 0, (seq, qrows)
    nk = seq // blk
    nq = seq // qrows
    pairs = [(i, j) for i in range(nq) for j in range(ATT_QBLOCKS * (i + 1))]
    qt = jnp.asarray([p[0] for p in pairs], jnp.int32)
    kt = jnp.asarray([p[1] for p in pairs], jnp.int32)
    w = 2 * DA_DH
    grid_spec = pltpu.PrefetchScalarGridSpec(
        num_scalar_prefetch=2,
        grid=(batch, DA_HEADS, len(pairs)),
        in_specs=[pl.BlockSpec((qrows, 2 * LANES), lambda b, h, s, qt, kt: (b * nq + qt[s], h)),
                  pl.BlockSpec((blk, 2 * LANES), lambda b, h, s, qt, kt: (b * nk + kt[s], h)),
                  pl.BlockSpec((blk, w), lambda b, h, s, qt, kt: (b * nk + kt[s], h)),
                  pl.BlockSpec((1, 1), lambda b, h, s, qt, kt: (0, 0)),
                  pl.BlockSpec((1, w), lambda b, h, s, qt, kt: (0, 0))],
        out_specs=pl.BlockSpec((qrows, w), lambda b, h, s, qt, kt: (b * nq + qt[s], h)),
        scratch_shapes=[pltpu.VMEM((2, qrows, LANES), f32),
                        pltpu.VMEM((2, qrows, LANES), f32),
                        pltpu.VMEM((2, qrows, w), f32)],
    )
    return pl.pallas_call(
        functools.partial(_attn_kernel, lam_init=lam_init),
        out_shape=jax.ShapeDtypeStruct((t, DA_W), bf16),
        grid_spec=grid_spec,
        compiler_params=_cparams(3),
        name="attn",
    )(qt, kt, qh, kh, vh, lam, da_norm_g)


def _first_index_of_max(x, iota_f, size):
    m = jnp.max(x, axis=0, keepdims=True)
    idx = jnp.min(jnp.where(x == m, iota_f, float(size)), axis=0, keepdims=True)
    return m, idx


def _route_kernel(hm_ref, hd_ref, x_ref, mod_ref, wo_ref, wr_ref, rb_ref,
                  x1_ref, ha_ref, hb_ref, idx_ref, gate_ref, rank_ref, cnt_ref, base_ref):
    tm = TM_ROUTE
    i = pl.program_id(0)

    @pl.when(i == 0)
    def _():
        base_ref[...] = jnp.zeros_like(base_ref)

    mix = _dot(hm_ref[...], wo_ref[0:ML_W, :]) + _dot(hd_ref[...], wo_ref[ML_W:ML_W + DA_W, :])
    x1 = x_ref[...] + mod_ref[0, 2:3, :] * mix
    x1_ref[...] = x1
    h2 = x1 * lax.rsqrt(jnp.mean(x1 * x1, axis=-1, keepdims=True) + EPS)
    h2 = h2 * (1.0 + mod_ref[0, 4:5, :]) + mod_ref[0, 3:4, :]
    packed = lax.bitcast_convert_type(_pack_bf16_pair(h2[:, 0:HALF], h2[:, HALF:D_MODEL]), jnp.int32)
    ha_ref[...] = packed[:, 0:QUARTER]
    hb_ref[...] = packed[:, QUARTER:HALF]

    sc = _sigmoid(_dot_nt(wr_ref[...], h2.astype(bf16)))
    sel = sc + rb_ref[...]

    gi = lax.broadcasted_iota(jnp.int32, (GROUP_SIZE, tm), 0).astype(f32)
    gscores = []
    for g in range(N_GROUPS):
        blk = sel[g * GROUP_SIZE:(g + 1) * GROUP_SIZE, :]
        m1, i1 = _first_index_of_max(blk, gi, GROUP_SIZE)
        m2 = jnp.max(jnp.where(gi == i1, NEG_BIG, blk), axis=0, keepdims=True)
        gscores.append(m1 + m2)
    gs = jnp.concatenate(gscores, axis=0)

    ngi = lax.broadcasted_iota(jnp.int32, (N_GROUPS, tm), 0).astype(f32)
    gkeep = jnp.zeros((N_GROUPS, tm), f32)
    for _ in range(TOPK_GROUPS):
        _, gidx = _first_index_of_max(gs, ngi, N_GROUPS)
        hit = ngi == gidx
        gkeep = jnp.where(hit, 1.0, gkeep)
        gs = jnp.where(hit, NEG_BIG, gs)
    masked = jnp.concatenate(
        [jnp.where(gkeep[g:g + 1, :] > 0.0, sel[g * GROUP_SIZE:(g + 1) * GROUP_SIZE, :], NEG_BIG)
         for g in range(N_GROUPS)], axis=0)

    ei = lax.broadcasted_iota(jnp.int32, (N_EXPERTS, tm), 0).astype(f32)
    idxs, gates = [], []
    chosen = jnp.zeros((N_EXPERTS, tm), f32)
    for _ in range(TOP_K):
        _, eidx = _first_index_of_max(masked, ei, N_EXPERTS)
        hit = ei == eidx
        idxs.append(eidx)
        gates.append(jnp.sum(jnp.where(hit, sc, 0.0), axis=0, keepdims=True))
        chosen = jnp.where(hit, 1.0, chosen)
        masked = jnp.where(hit, NEG_BIG, masked)
    gate = jnp.concatenate(gates, axis=0)
    gate = gate / jnp.sum(gate, axis=0, keepdims=True) * ROUTED_SCALE
    gate_ref[...] = gate
    idx_ref[...] = jnp.concatenate(idxs, axis=0).astype(jnp.int32)

    tr = lax.broadcasted_iota(jnp.int32, (tm, tm), 0)
    tc = lax.broadcasted_iota(jnp.int32, (tm, tm), 1)
    before = (tr < tc).astype(bf16)
    seen = base_ref[...] + _dot(chosen.astype(bf16), before)
    ranks = [jnp.sum(jnp.where(ei == eidx, seen, 0.0), axis=0, keepdims=True) for eidx in idxs]
    rank_ref[...] = jnp.concatenate(ranks, axis=0).astype(jnp.int32)
    total = base_ref[...] + jnp.sum(chosen, axis=1, keepdims=True)
    base_ref[...] = total
    cnt_ref[...] = total


def _route(hm, hd, x2, mod, w_out, w_router_t, rbias_col, seq):
    t, d = x2.shape
    tm = TM_ROUTE
    tiles_per_seq = seq // tm
    const = lambda shape: pl.BlockSpec(shape, lambda i: (0,) * len(shape))
    return pl.pallas_call(
        _route_kernel,
        out_shape=(jax.ShapeDtypeStruct((t, d), f32),
                   jax.ShapeDtypeStruct((t, QUARTER), jnp.int32),
                   jax.ShapeDtypeStruct((t, QUARTER), jnp.int32),
                   jax.ShapeDtypeStruct((TOP_K, t), jnp.int32),
                   jax.ShapeDtypeStruct((TOP_K, t), f32),
                   jax.ShapeDtypeStruct((TOP_K, t), jnp.int32),
                   jax.ShapeDtypeStruct((N_EXPERTS, 1), f32)),
        grid=(t // tm,),
        in_specs=[pl.BlockSpec((tm, ML_W), lambda i: (i, 0)),
                  pl.BlockSpec((tm, DA_W), lambda i: (i, 0)),
                  pl.BlockSpec((tm, d), lambda i: (i, 0)),
                  pl.BlockSpec((1, 6, d), lambda i: (i // tiles_per_seq, 0, 0)),
                  const((d, d)), const((N_EXPERTS, d)), const((N_EXPERTS, 1))],
        out_specs=(pl.BlockSpec((tm, d), lambda i: (i, 0)),
                   pl.BlockSpec((tm, QUARTER), lambda i: (i, 0)),
                   pl.BlockSpec((tm, QUARTER), lambda i: (i, 0)),
                   pl.BlockSpec((TOP_K, tm), lambda i: (0, i)),
                   pl.BlockSpec((TOP_K, tm), lambda i: (0, i)),
                   pl.BlockSpec((TOP_K, tm), lambda i: (0, i)),
                   const((N_EXPERTS, 1))),
        scratch_shapes=[pltpu.VMEM((N_EXPERTS, 1), f32)],
        compiler_params=_cparams(1),
        name="route",
    )(hm, hd, x2, mod, w_out, w_router_t, rbias_col)


SLOT_RADIX = 256
SLOT_DIGITS = 3


def _slots_kernel(idx_ref, rank_ref, dig_ref, dest_ref):
    tm = TM_SLOTS
    ei = lax.broadcasted_iota(jnp.int32, (N_EXPERTS, tm), 0)
    rows = []
    for k in range(TOP_K):
        onehot = jnp.where(ei == idx_ref[k:k + 1, :], 1.0, 0.0).astype(bf16)
        dg = _dot(dig_ref[...], onehot)
        start = dg[0:1, :]
        for j in range(1, SLOT_DIGITS):
            start = start + dg[j:j + 1, :] * float(SLOT_RADIX ** j)
        rows.append(start.astype(jnp.int32) + rank_ref[k:k + 1, :])
    dest_ref[...] = jnp.concatenate(rows, axis=0)


def _slots(idx, rank, digits):
    t = idx.shape[1]
    tm = TM_SLOTS
    return pl.pallas_call(
        _slots_kernel,
        out_shape=jax.ShapeDtypeStruct((TOP_K, t), jnp.int32),
        grid=(t // tm,),
        in_specs=[pl.BlockSpec((TOP_K, tm), lambda i: (0, i)),
                  pl.BlockSpec((TOP_K, tm), lambda i: (0, i)),
                  pl.BlockSpec((SUBLANES, N_EXPERTS), lambda i: (0, 0))],
        out_specs=pl.BlockSpec((TOP_K, tm), lambda i: (0, i)),
        compiler_params=_cparams(1),
        name="slots",
    )(idx, rank, digits)


def _sc_scatter_table(rows, dest, n_slots):
    n, width = rows.shape
    mesh = plsc.VectorSubcoreMesh(core_axis_name="core", subcore_axis_name="subcore")

    @pl.kernel(out_type=jax.ShapeDtypeStruct((n_slots, width), rows.dtype), mesh=mesh, scratch_types=[],
               name="sc_scatter")
    def scatter_kernel(rows_hbm, dest_hbm, out_hbm):
        def body(rows_vmem, dest_vmem):
            for k in range(TOP_K):
                pltpu.sync_copy(rows_vmem, out_hbm.at[dest_vmem.at[k]])

        pltpu.emit_pipeline(
            body,
            grid=(n // SC_WINDOW,),
            in_specs=[pl.BlockSpec((SC_WINDOW, width), lambda i: (i, 0)),
                      pl.BlockSpec((TOP_K, SC_WINDOW), lambda i: (0, i))],
            out_specs=[],
            core_axis_name=("core", "subcore"),
            dimension_semantics=(pltpu.PARALLEL,),
        )(rows_hbm, dest_hbm)

    return scatter_kernel(rows, dest)


EXPERT_GROUP = 4
EXPERT_RING = 16
EXPERT_AHEAD = EXPERT_RING - EXPERT_GROUP


def _expert_kernel(start_ref, cnt_ref, total_ref, xsa_ref, xsb_ref, w1_ref, w3_ref, w2_ref,
                   ysa_ref, ysb_ref, xbuf, ybuf, w1b, w3b, w2b, in_sem, out_sem):
    bm = MOE_BLOCK
    ring = EXPERT_RING
    ahead = EXPERT_AHEAD
    shift = int(math.log2(bm))
    e = pl.program_id(0)
    total = total_ref[0]
    cnt = cnt_ref[e]
    nb = lax.shift_right_logical(cnt + (bm - 1), shift)
    g0 = lax.shift_right_logical(start_ref[e], shift)

    def slot_of(g):
        return jnp.bitwise_and(g, ring - 1)

    def in_copies(g):
        s = slot_of(g)
        rows = pl.ds(pl.multiple_of(g * bm, bm), bm)
        return [pltpu.make_async_copy(x_ref.at[rows], xbuf.at[s, half], in_sem.at[s])
                for half, x_ref in enumerate((xsa_ref, xsb_ref))]

    def out_copies(g):
        s = slot_of(g)
        rows = pl.ds(pl.multiple_of(g * bm, bm), bm)
        return [pltpu.make_async_copy(ybuf.at[s, half], y_ref.at[rows], out_sem.at[s])
                for half, y_ref in enumerate((ysa_ref, ysb_ref))]

    def request(g):
        @pl.when(g < total)
        def _():
            for cp in in_copies(g):
                cp.start()

    def acquire(g):
        for cp in in_copies(g):
            cp.wait()

        @pl.when(g >= ring)
        def _():
            for cp in out_copies(g - ring):
                cp.wait()

    def compute(g):
        s = slot_of(g)
        row = (g - g0) * bm + lax.broadcasted_iota(jnp.int32, (bm, HALF), 0)
        words = jnp.concatenate([xbuf[s, 0], xbuf[s, 1]], axis=1)
        lo, hi = _unpack_bf16_pair(lax.bitcast_convert_type(jnp.where(row < cnt, words, 0), jnp.uint32))
        lo = lo.astype(bf16)
        hi = hi.astype(bf16)
        h1 = _dot(lo, w1b[0:HALF, :]) + _dot(hi, w1b[HALF:D_MODEL, :])
        h3 = _dot(lo, w3b[0:HALF, :]) + _dot(hi, w3b[HALF:D_MODEL, :])
        y = _dot((_silu(h1) * h3).astype(bf16), w2b[...])
        packed = lax.bitcast_convert_type(_pack_bf16_pair(y[:, 0:HALF], y[:, HALF:D_MODEL]), jnp.int32)
        ybuf[s, 0] = packed[:, 0:QUARTER]
        ybuf[s, 1] = packed[:, QUARTER:HALF]

    def release(g):
        for cp in out_copies(g):
            cp.start()

    @pl.when(e == 0)
    def _():
        for g in range(ahead):
            request(g)

    @pl.when(nb > 0)
    def _():
        w1b[...] = w1_ref[0].astype(bf16)
        w3b[...] = w3_ref[0].astype(bf16)
        w2b[...] = w2_ref[0].astype(bf16)

    def run(g, n):
        for d in range(n):
            request(g + d + ahead)
        for d in range(n):
            acquire(g + d)
        for d in range(n):
            compute(g + d)
        for d in range(n):
            release(g + d)

    def group(i, carry):
        run(g0 + EXPERT_GROUP * i, EXPERT_GROUP)
        return carry

    lax.fori_loop(0, lax.shift_right_logical(nb, int(math.log2(EXPERT_GROUP))), group, 0)

    n = EXPERT_GROUP // 2
    while n >= 1:
        @pl.when(jnp.bitwise_and(nb, n) != 0)
        def _(n=n):
            done = jnp.bitwise_and(nb, -2 * n)
            run(g0 + done, n)
        n //= 2

    @pl.when(e == pl.num_programs(0) - 1)
    def _():
        for back in range(1, ring + 1):
            @pl.when(total - back >= 0)
            def _():
                for cp in out_copies(total - back):
                    cp.wait()


def _experts(start, counts, total, xs_a, xs_b, w1, w3, w2):
    n_pad = xs_a.shape[0]
    bm = MOE_BLOCK
    grid_spec = pltpu.PrefetchScalarGridSpec(
        num_scalar_prefetch=3,
        grid=(N_EXPERTS,),
        in_specs=[pl.BlockSpec(memory_space=pl.ANY),
                  pl.BlockSpec(memory_space=pl.ANY),
                  pl.BlockSpec((1, D_MODEL, D_EXPERT), lambda e, st, ct, tt: (e, 0, 0)),
                  pl.BlockSpec((1, D_MODEL, D_EXPERT), lambda e, st, ct, tt: (e, 0, 0)),
                  pl.BlockSpec((1, D_EXPERT, D_MODEL), lambda e, st, ct, tt: (e, 0, 0))],
        out_specs=(pl.BlockSpec(memory_space=pl.ANY), pl.BlockSpec(memory_space=pl.ANY)),
        scratch_shapes=[pltpu.VMEM((EXPERT_RING, 2, bm, QUARTER), jnp.int32),
                        pltpu.VMEM((EXPERT_RING, 2, bm, QUARTER), jnp.int32),
                        pltpu.VMEM((D_MODEL, D_EXPERT), bf16),
                        pltpu.VMEM((D_MODEL, D_EXPERT), bf16),
                        pltpu.VMEM((D_EXPERT, D_MODEL), bf16),
                        pltpu.SemaphoreType.DMA((EXPERT_RING,)),
                        pltpu.SemaphoreType.DMA((EXPERT_RING,))],
    )
    return pl.pallas_call(
        _expert_kernel,
        out_shape=(jax.ShapeDtypeStruct((n_pad, QUARTER), jnp.int32),
                   jax.ShapeDtypeStruct((n_pad, QUARTER), jnp.int32)),
        grid_spec=grid_spec,
        compiler_params=_cparams(1),
        name="experts",
    )(start, counts, total, xs_a, xs_b, w1, w3, w2)


def _sc_gather_table(table, idx2):
    n = idx2.shape[1]
    width = table.shape[1]
    mesh = plsc.VectorSubcoreMesh(core_axis_name="core", subcore_axis_name="subcore")

    @pl.kernel(out_type=jax.ShapeDtypeStruct((n, width), table.dtype), mesh=mesh, scratch_types=[],
               name="sc_gather")
    def gather_kernel(tab_hbm, idx_hbm, out_hbm):
        def body(idx_vmem, out_vmem):
            pltpu.sync_copy(tab_hbm.at[idx_vmem.at[0]], out_vmem)

        pltpu.emit_pipeline(
            body,
            grid=(n // SC_WINDOW,),
            in_specs=[pl.BlockSpec((1, SC_WINDOW), lambda i: (0, i))],
            out_specs=[pl.BlockSpec((SC_WINDOW, width), lambda i: (i, 0))],
            core_axis_name=("core", "subcore"),
            dimension_semantics=(pltpu.PARALLEL,),
        )(idx_hbm, out_hbm)

    return gather_kernel(table, idx2)


def _sc_gather_rows(table_a, table_b, idx):
    idx2 = idx.reshape(1, -1)
    return _sc_gather_table(table_a, idx2), _sc_gather_table(table_b, idx2)


def _combine_kernel(gate_ref, ha_ref, hb_ref, x1_ref, mod_ref, ws1_ref, ws3_ref, ws2_ref, ga_ref, gb_ref,
                    out_ref):
    words = jnp.concatenate([ha_ref[...], hb_ref[...]], axis=1)
    lo, hi = _unpack_bf16_pair(lax.bitcast_convert_type(words, jnp.uint32))
    lo = lo.astype(bf16)
    hi = hi.astype(bf16)
    s1 = _dot(lo, ws1_ref[0:HALF, :]) + _dot(hi, ws1_ref[HALF:D_MODEL, :])
    s3 = _dot(lo, ws3_ref[0:HALF, :]) + _dot(hi, ws3_ref[HALF:D_MODEL, :])
    y = _dot((_silu(s1) * s3).astype(bf16), ws2_ref[...])

    gate = gate_ref[...]
    gate_f = mod_ref[0, 5:6, :]
    for part, g_ref in enumerate((ga_ref, gb_ref)):
        c_lo = slice(part * QUARTER, (part + 1) * QUARTER)
        c_hi = slice(HALF + part * QUARTER, HALF + (part + 1) * QUARTER)
        acc_lo = y[:, c_lo]
        acc_hi = y[:, c_hi]
        for k in range(TOP_K):
            rlo, rhi = _unpack_bf16_pair(lax.bitcast_convert_type(g_ref[k], jnp.uint32))
            gk = gate[:, k:k + 1]
            acc_lo = acc_lo + gk * rlo
            acc_hi = acc_hi + gk * rhi
        out_ref[:, c_lo] = x1_ref[:, c_lo] + gate_f[:, c_lo] * acc_lo
        out_ref[:, c_hi] = x1_ref[:, c_hi] + gate_f[:, c_hi] * acc_hi


def _combine(gate_col, h2a, h2b, x1, mod, ws1, ws3, ws2, ga, gb, seq):
    t, d = x1.shape
    tm = TM_MOE
    tiles_per_seq = seq // tm
    const = lambda shape: pl.BlockSpec(shape, lambda i: (0,) * len(shape))
    return pl.pallas_call(
        _combine_kernel,
        out_shape=jax.ShapeDtypeStruct((t, d), f32),
        grid=(t // tm,),
        in_specs=[pl.BlockSpec((tm, TOP_K), lambda i: (i, 0)),
                  pl.BlockSpec((tm, QUARTER), lambda i: (i, 0)),
                  pl.BlockSpec((tm, QUARTER), lambda i: (i, 0)),
                  pl.BlockSpec((tm, d), lambda i: (i, 0)),
                  pl.BlockSpec((1, 6, d), lambda i: (i // tiles_per_seq, 0, 0)),
                  const((d, D_EXPERT)), const((d, D_EXPERT)), const((D_EXPERT, d)),
                  pl.BlockSpec((TOP_K, tm, QUARTER), lambda i: (0, i, 0)),
                  pl.BlockSpec((TOP_K, tm, QUARTER), lambda i: (0, i, 0))],
        out_specs=pl.BlockSpec((tm, d), lambda i: (i, 0)),
        compiler_params=_cparams(1),
        name="combine",
    )(gate_col, h2a, h2b, x1, mod, ws1, ws3, ws2, ga, gb)


def _lambda_init(layer):
    return 0.8 - 0.6 * math.exp(-0.3 * layer)


def _layer(x, c, w_ada, b_ada, w_in, conv_w, conv_b, gate_b, ml_norm_g, da_q_norm_g, da_k_norm_g,
           lambda_q1, lambda_k1, lambda_q2, lambda_k2, da_norm_g, w_out, w_router, router_bias,
           w1, w3, w2, ws1, ws3, ws2, layer):
    batch, seq, d = x.shape
    t = batch * seq
    lam_init = _lambda_init(layer)
    x2 = x.reshape(t, d)

    c_pad = jnp.pad(c, ((0, -batch % SUBLANES), (0, 0)))
    mod = _adaln(c_pad, w_ada, b_ada.reshape(1, -1))[:batch].reshape(batch, 6, d)

    g0 = 4 * ML_W
    q0 = g0 + N_GATES
    n_grp = 2 * DA_HEADS
    w_main = jnp.concatenate([w_in[:, :g0], w_in[:, q0:]], axis=1).astype(bf16)
    wg = jnp.pad(w_in[:, g0:g0 + N_GATES], ((0, 0), (0, LANES - N_GATES))).astype(bf16)
    gb_row = jnp.pad(gate_b, (0, LANES - N_GATES)).reshape(1, LANES)
    qgain = (jnp.tile(da_q_norm_g, n_grp) * (DA_DH ** -0.5 * LOG2E)).reshape(1, DA_W)
    kgain = jnp.tile(da_k_norm_g, n_grp).reshape(1, DA_W)
    seg = jnp.arange(DA_W) // DA_DH
    bd = (seg[:, None] == seg[None, :]).astype(bf16)

    pm, vt, ot, qh, kh, vh, gcol, grow = _inproj(x2, mod, w_main, wg, gb_row, qgain, kgain, bd, _alibi_table(seq),
                                         conv_w, conv_b.reshape(1, -1), seq)

    hm = _mlstm(pm, vt, ot, gcol, grow, ml_norm_g.reshape(-1, 1), batch, seq)

    lam = (jnp.exp(jnp.sum(lambda_q1 * lambda_k1)) - jnp.exp(jnp.sum(lambda_q2 * lambda_k2))
           + lam_init).reshape(1, 1).astype(f32)
    hd = _attention(qh, kh, vh, lam, da_norm_g.reshape(1, -1), batch, seq, lam_init)

    x1, h2a, h2b, idx, gate, rank, counts = _route(hm, hd, x2, mod, w_out.astype(bf16),
                                                   w_router.T.astype(bf16), router_bias.reshape(-1, 1), seq)

    bm = MOE_BLOCK
    n_blocks = (t * TOP_K) // bm + N_EXPERTS
    counts_i = counts.reshape(-1).astype(jnp.int32)
    padded = (counts_i + bm - 1) // bm * bm
    pad_end = jnp.cumsum(padded)
    pad_start = pad_end - padded
    digits = jnp.stack([(pad_start // SLOT_RADIX ** j) % SLOT_RADIX for j in range(SLOT_DIGITS)])
    digits = jnp.pad(digits, ((0, SUBLANES - SLOT_DIGITS), (0, 0))).astype(bf16)
    dest = _slots(idx, rank, digits)

    xs_a = _sc_scatter_table(h2a, dest, n_blocks * bm)
    xs_b = _sc_scatter_table(h2b, dest, n_blocks * bm)
    total_blocks = (pad_end[-1:] // bm).astype(jnp.int32)
    ys_a, ys_b = _experts(pad_start.astype(jnp.int32), counts_i, total_blocks, xs_a, xs_b, w1, w3, w2)
    ga, gb = _sc_gather_rows(ys_a, ys_b, dest.reshape(-1))
    out = _combine(gate.T, h2a, h2b, x1, mod, ws1.astype(bf16), ws3.astype(bf16), ws2.astype(bf16),
                   ga.reshape(TOP_K, t, QUARTER), gb.reshape(TOP_K, t, QUARTER), seq)
    return out.reshape(batch, seq, d)


def kernel(x, c, w_ada, b_ada, w_in, conv_w, conv_b, gate_b, ml_norm_g, da_q_norm_g, da_k_norm_g,
           lambda_q1, lambda_k1, lambda_q2, lambda_k2, da_norm_g, w_out, w_router, router_bias,
           w1, w3, w2, ws1, ws3, ws2):
    depth = w_ada.shape[0]
    for l in range(depth):
        x = _layer(x, c, w_ada[l], b_ada[l], w_in[l], conv_w[l], conv_b[l], gate_b[l], ml_norm_g[l],
                   da_q_norm_g[l], da_k_norm_g[l], lambda_q1[l], lambda_k1[l], lambda_q2[l], lambda_k2[l],
                   da_norm_g[l], w_out[l], w_router[l], router_bias[l], w1[l], w3[l], w2[l],
                   ws1[l], ws3[l], ws2[l], l)
    return x
```

```python
import functools
import math

import jax
import jax.numpy as jnp
import numpy as np
from jax import lax
from jax.experimental import pallas as pl
from jax.experimental.pallas import tpu as pltpu
from jax.experimental.pallas import tpu_sc as plsc

D_MODEL = 1024
ML_HEADS = 4
ML_DH = 128
ML_W = ML_HEADS * ML_DH
ML_CHUNK = 128
CONV_K = 4
DA_HEADS = 4
DA_DH = 64
DA_W = DA_HEADS * 2 * DA_DH
N_EXPERTS = 256
TOP_K = 8
N_GROUPS = 8
GROUP_SIZE = N_EXPERTS // N_GROUPS
TOPK_GROUPS = 4
D_EXPERT = 256
ROUTED_SCALE = 2.5
EPS = 1e-6
N_GATES = 2 * ML_HEADS

LANES = 128
SUBLANES = 8
VMEM_LIMIT_BYTES = 48 * 1024 * 1024

TM_INPROJ = 512
ML_STEP_CHUNKS = 4
ATT_BLOCK = 1024
ATT_QBLOCKS = 2
TM_ROUTE = 512
TM_SLOTS = 512
TM_MOE = 512
MOE_BLOCK = 256
HALF = D_MODEL // 2
QUARTER = HALF // 2
SC_WINDOW = 128

NEG_BIG = -1e30
LOG2E = 1.4426950408889634

f32 = jnp.float32
bf16 = jnp.bfloat16
HIGHEST = lax.Precision.HIGHEST


def _cparams(n_axes):
    return pltpu.CompilerParams(dimension_semantics=("arbitrary",) * n_axes,
                                vmem_limit_bytes=VMEM_LIMIT_BYTES)


def _dot(a, b):
    return jnp.dot(a, b, preferred_element_type=f32)


def _dot_nt(a, b):
    return lax.dot_general(a, b, (((1,), (1,)), ((), ())), preferred_element_type=f32)


def _sigmoid(x):
    return 1.0 / (1.0 + jnp.exp(-x))


def _silu(x):
    return x * _sigmoid(x)


def _log_sigmoid(x):
    return jnp.minimum(x, 0.0) - jnp.log(1.0 + jnp.exp(-jnp.abs(x)))


def _pack_bf16_pair(lo, hi):
    lo_bits = lax.bitcast_convert_type(lo.astype(bf16).astype(f32), jnp.uint32)
    hi_bits = lax.bitcast_convert_type(hi.astype(bf16).astype(f32), jnp.uint32)
    return (hi_bits & jnp.uint32(0xFFFF0000)) | (lo_bits >> 16)


def _unpack_bf16_pair(w):
    lo = lax.bitcast_convert_type(w << 16, f32)
    hi = lax.bitcast_convert_type(w & jnp.uint32(0xFFFF0000), f32)
    return lo, hi


def _adaln_kernel(c_ref, w_ref, b_ref, o_ref):
    c = c_ref[...]
    o_ref[...] = jnp.dot(_silu(c), w_ref[...], precision=HIGHEST, preferred_element_type=f32) + b_ref[...]


def _adaln(c_pad, w_ada, b_ada):
    rows, d = c_pad.shape
    n = w_ada.shape[1]
    tn = 1536
    return pl.pallas_call(
        _adaln_kernel,
        out_shape=jax.ShapeDtypeStruct((rows, n), f32),
        grid=(n // tn,),
        in_specs=[pl.BlockSpec((rows, d), lambda j: (0, 0)),
                  pl.BlockSpec((d, tn), lambda j: (0, j)),
                  pl.BlockSpec((1, tn), lambda j: (0, j))],
        out_specs=pl.BlockSpec((rows, tn), lambda j: (0, j)),
        compiler_params=_cparams(1),
        name="adaln",
    )(c_pad, w_ada, b_ada)


def _alibi_slope_log2(head):
    return 2.0 ** (-8.0 * (head + 1) / DA_HEADS) * LOG2E


def _alibi_table(seq):
    pos = np.arange(seq, dtype=np.float32)[:, None]
    slopes = np.asarray([_alibi_slope_log2(h) for h in range(DA_HEADS)], np.float32)[None, :]

    def bf16_part(a):
        return (np.ascontiguousarray(a).view(np.uint32) & np.uint32(0xFFFF0000)).view(np.float32)

    r = pos * slopes
    r_hi = bf16_part(r)
    r_mid = bf16_part(r - r_hi)
    r_lo = bf16_part(r - r_hi - r_mid)
    table = np.zeros((seq, DA_HEADS, LANES), np.float32)
    for lane_off, part in enumerate((r_hi, r_mid, r_lo)):
        table[:, :, DA_DH + lane_off] = part
    return jnp.asarray(table.reshape(seq, DA_HEADS * LANES), dtype=bf16)


def _inproj_kernel(x_ref, mod_ref, w_ref, wg_ref, gbr_ref, qg_ref, kg_ref, bd_ref, kaug_ref, cw_ref, cb_ref,
                   pm_ref, vt_ref, ot_ref, q_ref, k_ref, v_ref, gcol_ref, grow_ref, ext_ref, *, seq):
    tm = TM_INPROJ
    x = x_ref[...]
    h = x * lax.rsqrt(jnp.mean(x * x, axis=-1, keepdims=True) + EPS)
    h = h * (1.0 + mod_ref[0, 1:2, :]) + mod_ref[0, 0:1, :]
    hb = h.astype(bf16)
    cw = ML_W

    @pl.when((pl.program_id(0) * tm) % seq == 0)
    def _():
        ext_ref[0:SUBLANES, :] = jnp.zeros((SUBLANES, 2 * cw), f32)

    cur = _dot(hb, w_ref[:, 0:2 * cw])
    ext_ref[SUBLANES:SUBLANES + tm, :] = cur
    acc = cb_ref[...] + cw_ref[CONV_K - 1:CONV_K, :] * cur
    for j in range(CONV_K - 1):
        off = SUBLANES - (CONV_K - 1) + j
        acc = acc + cw_ref[j:j + 1, :] * ext_ref[off:off + tm, :]
    ext_ref[0:SUBLANES, :] = cur[tm - SUBLANES:tm, :]
    qk = _silu(acc)
    pm_ref[:, 0:cw] = qk[:, 0:cw].astype(bf16)
    pm_ref[:, cw:2 * cw] = (qk[:, cw:2 * cw] * (ML_DH ** -0.5)).astype(bf16)
    vt_ref[...] = _dot(hb, w_ref[:, 2 * cw:3 * cw]).T.astype(bf16)
    ot_ref[...] = _dot(hb, w_ref[:, 3 * cw:4 * cw]).T.astype(bf16)

    lane = lax.broadcasted_iota(jnp.int32, (tm, LANES), 1)
    feat = lane < DA_DH
    q_aug = jnp.where(jnp.logical_and(lane >= DA_DH, lane < DA_DH + 3), 1.0, 0.0)
    q0 = 4 * cw
    k0 = q0 + DA_W

    def qk_normed(col0, gain_ref):
        y = _dot(hb, w_ref[:, col0:col0 + DA_W])
        ss = _dot((y * y).astype(bf16), bd_ref[...])
        return y * lax.rsqrt(ss * (1.0 / DA_DH) + EPS) * gain_ref[...]

    def lane_group(y2, grp, aug):
        pair = y2[:, (grp // 2) * LANES:(grp // 2 + 1) * LANES]
        if grp % 2:
            pair = pltpu.roll(pair, DA_DH, axis=1)
        return jnp.where(feat, pair, aug).astype(bf16)

    nq = qk_normed(q0, qg_ref)
    nk = qk_normed(k0, kg_ref)
    for grp in range(2 * DA_HEADS):
        sl = slice(grp * LANES, (grp + 1) * LANES)
        head = slice((grp // 2) * LANES, (grp // 2 + 1) * LANES)
        q_ref[:, sl] = lane_group(nq, grp, q_aug)
        k_ref[:, sl] = lane_group(nk, grp, kaug_ref[:, head].astype(f32))

    v0 = k0 + DA_W
    v_ref[...] = _dot(hb, w_ref[:, v0:v0 + DA_W]).astype(bf16)
    gates = _dot(hb, wg_ref[...]) + gbr_ref[...]
    gcol_ref[...] = gates
    grow_ref[...] = gates.T[0:N_GATES, :]


def _inproj(x2, mod, w_main, wg, gb_row, qgain, kgain, bd, kaug, conv_w, conv_b, seq):
    t, d = x2.shape
    tm = TM_INPROJ
    tiles_per_seq = seq // tm
    n_main = w_main.shape[1]
    const = lambda shape: pl.BlockSpec(shape, lambda i: (0,) * len(shape))
    return pl.pallas_call(
        functools.partial(_inproj_kernel, seq=seq),
        out_shape=(jax.ShapeDtypeStruct((t, 2 * ML_W), bf16),
                   jax.ShapeDtypeStruct((ML_W, t), bf16),
                   jax.ShapeDtypeStruct((ML_W, t), bf16),
                   jax.ShapeDtypeStruct((t, 2 * DA_W), bf16),
                   jax.ShapeDtypeStruct((t, 2 * DA_W), bf16),
                   jax.ShapeDtypeStruct((t, DA_W), bf16),
                   jax.ShapeDtypeStruct((t, LANES), f32),
                   jax.ShapeDtypeStruct((N_GATES, t), f32)),
        grid=(t // tm,),
        in_specs=[pl.BlockSpec((tm, d), lambda i: (i, 0)),
                  pl.BlockSpec((1, 6, d), lambda i: (i // tiles_per_seq, 0, 0)),
                  const((d, n_main)), const((d, LANES)), const((1, LANES)),
                  const((1, DA_W)), const((1, DA_W)), const((DA_W, DA_W)),
                  pl.BlockSpec((tm, DA_HEADS * LANES), lambda i: (i % tiles_per_seq, 0)),
                  const((CONV_K, 2 * ML_W)), const((1, 2 * ML_W))],
        out_specs=(pl.BlockSpec((tm, 2 * ML_W), lambda i: (i, 0)),
                   pl.BlockSpec((ML_W, tm), lambda i: (0, i)),
                   pl.BlockSpec((ML_W, tm), lambda i: (0, i)),
                   pl.BlockSpec((tm, 2 * DA_W), lambda i: (i, 0)),
                   pl.BlockSpec((tm, 2 * DA_W), lambda i: (i, 0)),
                   pl.BlockSpec((tm, DA_W), lambda i: (i, 0)),
                   pl.BlockSpec((tm, LANES), lambda i: (i, 0)),
                   pl.BlockSpec((N_GATES, tm), lambda i: (0, i))),
        scratch_shapes=[pltpu.VMEM((SUBLANES + tm, 2 * ML_W), f32)],
        compiler_params=_cparams(1),
        name="inproj",
    )(x2, mod, w_main, wg, gb_row, qgain, kgain, bd, kaug, conv_w, conv_b)


def _mlstm_kernel(q_ref, k_ref, vt_ref, ot_ref, gcol_ref, grow_ref, ng_ref, out_ref, state_ref, m_ref):
    L = ML_CHUNK
    dh = ML_DH

    @pl.when(pl.program_id(1) == 0)
    def _():
        state_ref[...] = jnp.zeros_like(state_ref)
        m_ref[...] = jnp.zeros_like(m_ref)

    s_i = lax.broadcasted_iota(jnp.int32, (L, L), 0)
    t_i = lax.broadcasted_iota(jnp.int32, (L, L), 1)
    visible = s_i <= t_i
    tril = (s_i >= t_i).astype(f32)
    triu = visible.astype(f32)
    ones_rows = (lax.broadcasted_iota(jnp.int32, (dh, L), 0) == 0).astype(f32)
    g_b = jnp.broadcast_to(ng_ref[...], (dh, L))

    for cc in range(ML_STEP_CHUNKS):
        rows = slice(cc * L, (cc + 1) * L)
        gcol = gcol_ref[rows, :]
        grow = grow_ref[:, rows]
        bcol_all = jnp.dot(tril, _log_sigmoid(gcol), precision=HIGHEST, preferred_element_type=f32)
        brow_all = jnp.dot(_log_sigmoid(grow), triu, precision=HIGHEST, preferred_element_type=f32)

        for hd in range(ML_HEADS):
            cols = slice(hd * dh, (hd + 1) * dh)
            qb = q_ref[rows, cols]
            kb = k_ref[rows, cols]
            v_aug_t = jnp.concatenate([vt_ref[cols, rows].astype(f32), ones_rows], axis=0)

            brow = brow_all[ML_HEADS + hd:ML_HEADS + hd + 1, :]
            irow = grow[hd:hd + 1, :]
            key_term = gcol[:, hd:hd + 1] - bcol_all[:, ML_HEADS + hd:ML_HEADS + hd + 1]
            m_prev = m_ref[hd:hd + 1, :]

            d_intra = jnp.where(visible, brow + key_term, NEG_BIG)
            d_inter = brow + m_prev
            m_t = jnp.maximum(d_inter, jnp.max(d_intra, axis=0, keepdims=True))
            w_intra = jnp.exp(d_intra - m_t)
            w_inter = jnp.exp(d_inter - m_t)

            s_t = _dot_nt(kb, qb) * w_intra
            state = state_ref[hd]
            q_state = _dot_nt(state.astype(bf16), qb)
            s_v = _dot(v_aug_t.astype(bf16), s_t.astype(bf16))
            num = w_inter * q_state[0:dh, :] + s_v[0:dh, :]
            den = w_inter * q_state[dh:dh + 1, :] + s_v[dh:dh + 1, :]
            hval = num / jnp.maximum(jnp.abs(den), jnp.exp(-m_t))

            b_last = brow[:, L - 1:L]
            d_state = b_last - brow + irow
            m_new = jnp.maximum(b_last + m_prev, jnp.max(d_state, axis=1, keepdims=True))
            carry_scale = jnp.exp(b_last + m_prev - m_new)[:, 0:1]
            wk = jnp.exp(d_state - m_new)
            upd = _dot((v_aug_t * wk).astype(bf16), kb)
            state_ref[hd] = carry_scale * state + upd
            m_ref[hd:hd + 1, :] = m_new

            hn = hval * lax.rsqrt(jnp.mean(hval * hval, axis=0, keepdims=True) + EPS) * g_b
            og = _sigmoid(ot_ref[cols, rows].astype(f32))
            out_ref[rows, cols] = (hn * og).T.astype(bf16)


def _mlstm(pm, vt, ot, gcol, grow, ml_norm_g, batch, seq):
    t = pm.shape[0]
    rows = ML_STEP_CHUNKS * ML_CHUNK
    ns = seq // rows
    row = lambda b, c: b * ns + c
    return pl.pallas_call(
        _mlstm_kernel,
        out_shape=jax.ShapeDtypeStruct((t, ML_W), bf16),
        grid=(batch, ns),
        in_specs=[pl.BlockSpec((rows, ML_W), lambda b, c: (row(b, c), 0)),
                  pl.BlockSpec((rows, ML_W), lambda b, c: (row(b, c), 1)),
                  pl.BlockSpec((ML_W, rows), lambda b, c: (0, row(b, c))),
                  pl.BlockSpec((ML_W, rows), lambda b, c: (0, row(b, c))),
                  pl.BlockSpec((rows, LANES), lambda b, c: (row(b, c), 0)),
                  pl.BlockSpec((N_GATES, rows), lambda b, c: (0, row(b, c))),
                  pl.BlockSpec((ML_DH, 1), lambda b, c: (0, 0))],
        out_specs=pl.BlockSpec((rows, ML_W), lambda b, c: (row(b, c), 0)),
        scratch_shapes=[pltpu.VMEM((ML_HEADS, 2 * ML_DH, ML_DH), f32),
                        pltpu.VMEM((SUBLANES, LANES), f32)],
        compiler_params=_cparams(2),
        name="mlstm",
    )(pm, pm, vt, ot, gcol, grow, ml_norm_g)


def _attn_block(q_ref, k_ref, v_ref, m_ref, l_ref, acc_ref, row0, nrows, nkeys, key0=None):
    rows = slice(row0, row0 + nrows)
    nch = nkeys // LANES
    v = v_ref[0:nkeys, :]
    if key0 is not None:
        keep = (key0 + lax.broadcasted_iota(jnp.int32, (nrows, nkeys), 1)
                <= row0 + lax.broadcasted_iota(jnp.int32, (nrows, nkeys), 0))
    for c in range(2):
        sl = slice(c * LANES, (c + 1) * LANES)
        s = _dot_nt(q_ref[rows, sl], k_ref[0:nkeys, sl])
        if key0 is not None:
            s = jnp.where(keep, s, NEG_BIG)
        chunks = [s[:, j * LANES:(j + 1) * LANES] for j in range(nch)]
        mc = chunks[0]
        for ch in chunks[1:]:
            mc = jnp.maximum(mc, ch)
        m_old = m_ref[c, rows, :]
        m_new = jnp.maximum(m_old, jnp.max(mc, axis=1, keepdims=True))
        alpha = jnp.exp2(m_old - m_new)
        ps = [jnp.exp2(ch - m_new) for ch in chunks]
        lsum = ps[0]
        for pj in ps[1:]:
            lsum = lsum + pj
        p = jnp.concatenate([pj.astype(bf16) for pj in ps], axis=1)
        l_ref[c, rows, :] = alpha * l_ref[c, rows, :] + lsum
        acc_ref[c, rows, :] = alpha * acc_ref[c, rows, :] + _dot(p, v)
        m_ref[c, rows, :] = m_new


def _attn_kernel(qt_ref, kt_ref, q_ref, k_ref, v_ref, lam_ref, g_ref, o_ref,
                 m_ref, l_ref, acc_ref, *, lam_init):
    step = pl.program_id(2)
    qi = qt_ref[step]
    ki = kt_ref[step]

    @pl.when(ki == 0)
    def _():
        m_ref[...] = jnp.full_like(m_ref, NEG_BIG)
        l_ref[...] = jnp.zeros_like(l_ref)
        acc_ref[...] = jnp.zeros_like(acc_ref)

    blk = ATT_BLOCK
    half = blk // 2
    qrows = ATT_QBLOCKS * blk
    first_diag = ATT_QBLOCKS * qi
    args = (q_ref, k_ref, v_ref, m_ref, l_ref, acc_ref)

    @pl.when(ki < first_diag)
    def _():
        _attn_block(*args, 0, qrows, blk)

    for j in range(ATT_QBLOCKS):
        @pl.when(ki == first_diag + j)
        def _(j=j):
            r0 = j * blk
            _attn_block(*args, r0, half, half, key0=r0)
            _attn_block(*args, r0 + half, half, blk, key0=r0)
            if r0 + blk < qrows:
                _attn_block(*args, r0 + blk, qrows - r0 - blk, blk)

    @pl.when(ki == first_diag + ATT_QBLOCKS - 1)
    def _():
        lam = lam_ref[...]
        l0 = jnp.sum(l_ref[0], axis=1, keepdims=True)
        l1 = jnp.sum(l_ref[1], axis=1, keepdims=True)
        o = acc_ref[0] / l0 - lam * (acc_ref[1] / l1)
        o = o * lax.rsqrt(jnp.mean(o * o, axis=-1, keepdims=True) + EPS)
        o_ref[...] = (o * g_ref[...] * (1.0 - lam_init)).astype(bf16)


def _attention(qh, kh, vh, lam, da_norm_g, batch, seq, lam_init):
    t = qh.shape[0]
    blk = ATT_BLOCK
    qrows = ATT_QBLOCKS * blk
    assert seq % qrows == 0, (seq, qrows)
    nk = seq // blk
    nq = seq // qrows
    pairs = [(i, j) for i in range(nq) for j in range(ATT_QBLOCKS * (i + 1))]
    qt = jnp.asarray([p[0] for p in pairs], jnp.int32)
    kt = jnp.asarray([p[1] for p in pairs], jnp.int32)
    w = 2 * DA_DH
    grid_spec = pltpu.PrefetchScalarGridSpec(
        num_scalar_prefetch=2,
        grid=(batch, DA_HEADS, len(pairs)),
        in_specs=[pl.BlockSpec((qrows, 2 * LANES), lambda b, h, s, qt, kt: (b * nq + qt[s], h)),
                  pl.BlockSpec((blk, 2 * LANES), lambda b, h, s, qt, kt: (b * nk + kt[s], h)),
                  pl.BlockSpec((blk, w), lambda b, h, s, qt, kt: (b * nk + kt[s], h)),
                  pl.BlockSpec((1, 1), lambda b, h, s, qt, kt: (0, 0)),
                  pl.BlockSpec((1, w), lambda b, h, s, qt, kt: (0, 0))],
        out_specs=pl.BlockSpec((qrows, w), lambda b, h, s, qt, kt: (b * nq + qt[s], h)),
        scratch_shapes=[pltpu.VMEM((2, qrows, LANES), f32),
                        pltpu.VMEM((2, qrows, LANES), f32),
                        pltpu.VMEM((2, qrows, w), f32)],
    )
    return pl.pallas_call(
        functools.partial(_attn_kernel, lam_init=lam_init),
        out_shape=jax.ShapeDtypeStruct((t, DA_W), bf16),
        grid_spec=grid_spec,
        compiler_params=_cparams(3),
        name="attn",
    )(qt, kt, qh, kh, vh, lam, da_norm_g)


def _first_index_of_max(x, iota_f, size):
    m = jnp.max(x, axis=0, keepdims=True)
    idx = jnp.min(jnp.where(x == m, iota_f, float(size)), axis=0, keepdims=True)
    return m, idx


def _route_kernel(hm_ref, hd_ref, x_ref, mod_ref, wo_ref, wr_ref, rb_ref,
                  x1_ref, ha_ref, hb_ref, idx_ref, gate_ref, rank_ref, cnt_ref, base_ref):
    tm = TM_ROUTE
    i = pl.program_id(0)

    @pl.when(i == 0)
    def _():
        base_ref[...] = jnp.zeros_like(base_ref)

    mix = _dot(hm_ref[...], wo_ref[0:ML_W, :]) + _dot(hd_ref[...], wo_ref[ML_W:ML_W + DA_W, :])
    x1 = x_ref[...] + mod_ref[0, 2:3, :] * mix
    x1_ref[...] = x1
    h2 = x1 * lax.rsqrt(jnp.mean(x1 * x1, axis=-1, keepdims=True) + EPS)
    h2 = h2 * (1.0 + mod_ref[0, 4:5, :]) + mod_ref[0, 3:4, :]
    packed = lax.bitcast_convert_type(_pack_bf16_pair(h2[:, 0:HALF], h2[:, HALF:D_MODEL]), jnp.int32)
    ha_ref[...] = packed[:, 0:QUARTER]
    hb_ref[...] = packed[:, QUARTER:HALF]

    sc = _sigmoid(_dot_nt(wr_ref[...], h2.astype(bf16)))
    sel = sc + rb_ref[...]

    gi = lax.broadcasted_iota(jnp.int32, (GROUP_SIZE, tm), 0).astype(f32)
    gscores = []
    for g in range(N_GROUPS):
        blk = sel[g * GROUP_SIZE:(g + 1) * GROUP_SIZE, :]
        m1, i1 = _first_index_of_max(blk, gi, GROUP_SIZE)
        m2 = jnp.max(jnp.where(gi == i1, NEG_BIG, blk), axis=0, keepdims=True)
        gscores.append(m1 + m2)
    gs = jnp.concatenate(gscores, axis=0)

    ngi = lax.broadcasted_iota(jnp.int32, (N_GROUPS, tm), 0).astype(f32)
    gkeep = jnp.zeros((N_GROUPS, tm), f32)
    for _ in range(TOPK_GROUPS):
        _, gidx = _first_index_of_max(gs, ngi, N_GROUPS)
        hit = ngi == gidx
        gkeep = jnp.where(hit, 1.0, gkeep)
        gs = jnp.where(hit, NEG_BIG, gs)
    masked = jnp.concatenate(
        [jnp.where(gkeep[g:g + 1, :] > 0.0, sel[g * GROUP_SIZE:(g + 1) * GROUP_SIZE, :], NEG_BIG)
         for g in range(N_GROUPS)], axis=0)

    ei = lax.broadcasted_iota(jnp.int32, (N_EXPERTS, tm), 0).astype(f32)
    idxs, gates = [], []
    chosen = jnp.zeros((N_EXPERTS, tm), f32)
    for _ in range(TOP_K):
        _, eidx = _first_index_of_max(masked, ei, N_EXPERTS)
        hit = ei == eidx
        idxs.append(eidx)
        gates.append(jnp.sum(jnp.where(hit, sc, 0.0), axis=0, keepdims=True))
        chosen = jnp.where(hit, 1.0, chosen)
        masked = jnp.where(hit, NEG_BIG, masked)
    gate = jnp.concatenate(gates, axis=0)
    gate = gate / jnp.sum(gate, axis=0, keepdims=True) * ROUTED_SCALE
    gate_ref[...] = gate
    idx_ref[...] = jnp.concatenate(idxs, axis=0).astype(jnp.int32)

    tr = lax.broadcasted_iota(jnp.int32, (tm, tm), 0)
    tc = lax.broadcasted_iota(jnp.int32, (tm, tm), 1)
    before = (tr < tc).astype(bf16)
    seen = base_ref[...] + _dot(chosen.astype(bf16), before)
    ranks = [jnp.sum(jnp.where(ei == eidx, seen, 0.0), axis=0, keepdims=True) for eidx in idxs]
    rank_ref[...] = jnp.concatenate(ranks, axis=0).astype(jnp.int32)
    total = base_ref[...] + jnp.sum(chosen, axis=1, keepdims=True)
    base_ref[...] = total
    cnt_ref[...] = total


def _route(hm, hd, x2, mod, w_out, w_router_t, rbias_col, seq):
    t, d = x2.shape
    tm = TM_ROUTE
    tiles_per_seq = seq // tm
    const = lambda shape: pl.BlockSpec(shape, lambda i: (0,) * len(shape))
    return pl.pallas_call(
        _route_kernel,
        out_shape=(jax.ShapeDtypeStruct((t, d), f32),
                   jax.ShapeDtypeStruct((t, QUARTER), jnp.int32),
                   jax.ShapeDtypeStruct((t, QUARTER), jnp.int32),
                   jax.ShapeDtypeStruct((TOP_K, t), jnp.int32),
                   jax.ShapeDtypeStruct((TOP_K, t), f32),
                   jax.ShapeDtypeStruct((TOP_K, t), jnp.int32),
                   jax.ShapeDtypeStruct((N_EXPERTS, 1), f32)),
        grid=(t // tm,),
        in_specs=[pl.BlockSpec((tm, ML_W), lambda i: (i, 0)),
                  pl.BlockSpec((tm, DA_W), lambda i: (i, 0)),
                  pl.BlockSpec((tm, d), lambda i: (i, 0)),
                  pl.BlockSpec((1, 6, d), lambda i: (i // tiles_per_seq, 0, 0)),
                  const((d, d)), const((N_EXPERTS, d)), const((N_EXPERTS, 1))],
        out_specs=(pl.BlockSpec((tm, d), lambda i: (i, 0)),
                   pl.BlockSpec((tm, QUARTER), lambda i: (i, 0)),
                   pl.BlockSpec((tm, QUARTER), lambda i: (i, 0)),
                   pl.BlockSpec((TOP_K, tm), lambda i: (0, i)),
                   pl.BlockSpec((TOP_K, tm), lambda i: (0, i)),
                   pl.BlockSpec((TOP_K, tm), lambda i: (0, i)),
                   const((N_EXPERTS, 1))),
        scratch_shapes=[pltpu.VMEM((N_EXPERTS, 1), f32)],
        compiler_params=_cparams(1),
        name="route",
    )(hm, hd, x2, mod, w_out, w_router_t, rbias_col)


SLOT_RADIX = 256
SLOT_DIGITS = 3


def _slots_kernel(idx_ref, rank_ref, dig_ref, dest_ref):
    tm = TM_SLOTS
    ei = lax.broadcasted_iota(jnp.int32, (N_EXPERTS, tm), 0)
    rows = []
    for k in range(TOP_K):
        onehot = jnp.where(ei == idx_ref[k:k + 1, :], 1.0, 0.0).astype(bf16)
        dg = _dot(dig_ref[...], onehot)
        start = dg[0:1, :]
        for j in range(1, SLOT_DIGITS):
            start = start + dg[j:j + 1, :] * float(SLOT_RADIX ** j)
        rows.append(start.astype(jnp.int32) + rank_ref[k:k + 1, :])
    dest_ref[...] = jnp.concatenate(rows, axis=0)


def _slots(idx, rank, digits):
    t = idx.shape[1]
    tm = TM_SLOTS
    return pl.pallas_call(
        _slots_kernel,
        out_shape=jax.ShapeDtypeStruct((TOP_K, t), jnp.int32),
        grid=(t // tm,),
        in_specs=[pl.BlockSpec((TOP_K, tm), lambda i: (0, i)),
                  pl.BlockSpec((TOP_K, tm), lambda i: (0, i)),
                  pl.BlockSpec((SUBLANES, N_EXPERTS), lambda i: (0, 0))],
        out_specs=pl.BlockSpec((TOP_K, tm), lambda i: (0, i)),
        compiler_params=_cparams(1),
        name="slots",
    )(idx, rank, digits)


def _sc_scatter_table(rows, dest, n_slots):
    n, width = rows.shape
    mesh = plsc.VectorSubcoreMesh(core_axis_name="core", subcore_axis_name="subcore")

    @pl.kernel(out_type=jax.ShapeDtypeStruct((n_slots, width), rows.dtype), mesh=mesh, scratch_types=[],
               name="sc_scatter")
    def scatter_kernel(rows_hbm, dest_hbm, out_hbm):
        def body(rows_vmem, dest_vmem):
            for k in range(TOP_K):
                pltpu.sync_copy(rows_vmem, out_hbm.at[dest_vmem.at[k]])

        pltpu.emit_pipeline(
            body,
            grid=(n // SC_WINDOW,),
            in_specs=[pl.BlockSpec((SC_WINDOW, width), lambda i: (i, 0)),
                      pl.BlockSpec((TOP_K, SC_WINDOW), lambda i: (0, i))],
            out_specs=[],
            core_axis_name=("core", "subcore"),
            dimension_semantics=(pltpu.PARALLEL,),
        )(rows_hbm, dest_hbm)

    return scatter_kernel(rows, dest)


EXPERT_GROUP = 4
EXPERT_RING = 16
EXPERT_AHEAD = EXPERT_RING - EXPERT_GROUP


def _expert_kernel(start_ref, cnt_ref, total_ref, xsa_ref, xsb_ref, w1_ref, w3_ref, w2_ref,
                   ysa_ref, ysb_ref, xbuf, ybuf, w1b, w3b, w2b, in_sem, out_sem):
    bm = MOE_BLOCK
    ring = EXPERT_RING
    ahead = EXPERT_AHEAD
    shift = int(math.log2(bm))
    e = pl.program_id(0)
    total = total_ref[0]
    cnt = cnt_ref[e]
    nb = lax.shift_right_logical(cnt + (bm - 1), shift)
    g0 = lax.shift_right_logical(start_ref[e], shift)

    def slot_of(g):
        return jnp.bitwise_and(g, ring - 1)

    def in_copies(g):
        s = slot_of(g)
        rows = pl.ds(pl.multiple_of(g * bm, bm), bm)
        return [pltpu.make_async_copy(x_ref.at[rows], xbuf.at[s, half], in_sem.at[s])
                for half, x_ref in enumerate((xsa_ref, xsb_ref))]

    def out_copies(g):
        s = slot_of(g)
        rows = pl.ds(pl.multiple_of(g * bm, bm), bm)
        return [pltpu.make_async_copy(ybuf.at[s, half], y_ref.at[rows], out_sem.at[s])
                for half, y_ref in enumerate((ysa_ref, ysb_ref))]

    def request(g):
        @pl.when(g < total)
        def _():
            for cp in in_copies(g):
                cp.start()

    def acquire(g):
        for cp in in_copies(g):
            cp.wait()

        @pl.when(g >= ring)
        def _():
            for cp in out_copies(g - ring):
                cp.wait()

    def compute(g):
        s = slot_of(g)
        row = (g - g0) * bm + lax.broadcasted_iota(jnp.int32, (bm, HALF), 0)
        words = jnp.concatenate([xbuf[s, 0], xbuf[s, 1]], axis=1)
        lo, hi = _unpack_bf16_pair(lax.bitcast_convert_type(jnp.where(row < cnt, words, 0), jnp.uint32))
        lo = lo.astype(bf16)
        hi = hi.astype(bf16)
        h1 = _dot(lo, w1b[0:HALF, :]) + _dot(hi, w1b[HALF:D_MODEL, :])
        h3 = _dot(lo, w3b[0:HALF, :]) + _dot(hi, w3b[HALF:D_MODEL, :])
        y = _dot((_silu(h1) * h3).astype(bf16), w2b[...])
        packed = lax.bitcast_convert_type(_pack_bf16_pair(y[:, 0:HALF], y[:, HALF:D_MODEL]), jnp.int32)
        ybuf[s, 0] = packed[:, 0:QUARTER]
        ybuf[s, 1] = packed[:, QUARTER:HALF]

    def release(g):
        for cp in out_copies(g):
            cp.start()

    @pl.when(e == 0)
    def _():
        for g in range(ahead):
            request(g)

    @pl.when(nb > 0)
    def _():
        w1b[...] = w1_ref[0].astype(bf16)
        w3b[...] = w3_ref[0].astype(bf16)
        w2b[...] = w2_ref[0].astype(bf16)

    def run(g, n):
        for d in range(n):
            request(g + d + ahead)
        for d in range(n):
            acquire(g + d)
        for d in range(n):
            compute(g + d)
        for d in range(n):
            release(g + d)

    def group(i, carry):
        run(g0 + EXPERT_GROUP * i, EXPERT_GROUP)
        return carry

    lax.fori_loop(0, lax.shift_right_logical(nb, int(math.log2(EXPERT_GROUP))), group, 0)

    n = EXPERT_GROUP // 2
    while n >= 1:
        @pl.when(jnp.bitwise_and(nb, n) != 0)
        def _(n=n):
            done = jnp.bitwise_and(nb, -2 * n)
            run(g0 + done, n)
        n //= 2

    @pl.when(e == pl.num_programs(0) - 1)
    def _():
        for back in range(1, ring + 1):
            @pl.when(total - back >= 0)
            def _():
                for cp in out_copies(total - back):
                    cp.wait()


def _experts(start, counts, total, xs_a, xs_b, w1, w3, w2):
    n_pad = xs_a.shape[0]
    bm = MOE_BLOCK
    grid_spec = pltpu.PrefetchScalarGridSpec(
        num_scalar_prefetch=3,
        grid=(N_EXPERTS,),
        in_specs=[pl.BlockSpec(memory_space=pl.ANY),
                  pl.BlockSpec(memory_space=pl.ANY),
                  pl.BlockSpec((1, D_MODEL, D_EXPERT), lambda e, st, ct, tt: (e, 0, 0)),
                  pl.BlockSpec((1, D_MODEL, D_EXPERT), lambda e, st, ct, tt: (e, 0, 0)),
                  pl.BlockSpec((1, D_EXPERT, D_MODEL), lambda e, st, ct, tt: (e, 0, 0))],
        out_specs=(pl.BlockSpec(memory_space=pl.ANY), pl.BlockSpec(memory_space=pl.ANY)),
        scratch_shapes=[pltpu.VMEM((EXPERT_RING, 2, bm, QUARTER), jnp.int32),
                        pltpu.VMEM((EXPERT_RING, 2, bm, QUARTER), jnp.int32),
                        pltpu.VMEM((D_MODEL, D_EXPERT), bf16),
                        pltpu.VMEM((D_MODEL, D_EXPERT), bf16),
                        pltpu.VMEM((D_EXPERT, D_MODEL), bf16),
                        pltpu.SemaphoreType.DMA((EXPERT_RING,)),
                        pltpu.SemaphoreType.DMA((EXPERT_RING,))],
    )
    return pl.pallas_call(
        _expert_kernel,
        out_shape=(jax.ShapeDtypeStruct((n_pad, QUARTER), jnp.int32),
                   jax.ShapeDtypeStruct((n_pad, QUARTER), jnp.int32)),
        grid_spec=grid_spec,
        compiler_params=_cparams(1),
        name="experts",
    )(start, counts, total, xs_a, xs_b, w1, w3, w2)


def _sc_gather_table(table, idx2):
    n = idx2.shape[1]
    width = table.shape[1]
    mesh = plsc.VectorSubcoreMesh(core_axis_name="core", subcore_axis_name="subcore")

    @pl.kernel(out_type=jax.ShapeDtypeStruct((n, width), table.dtype), mesh=mesh, scratch_types=[],
               name="sc_gather")
    def gather_kernel(tab_hbm, idx_hbm, out_hbm):
        def body(idx_vmem, out_vmem):
            pltpu.sync_copy(tab_hbm.at[idx_vmem.at[0]], out_vmem)

        pltpu.emit_pipeline(
            body,
            grid=(n // SC_WINDOW,),
            in_specs=[pl.BlockSpec((1, SC_WINDOW), lambda i: (0, i))],
            out_specs=[pl.BlockSpec((SC_WINDOW, width), lambda i: (i, 0))],
            core_axis_name=("core", "subcore"),
            dimension_semantics=(pltpu.PARALLEL,),
        )(idx_hbm, out_hbm)

    return gather_kernel(table, idx2)


def _sc_gather_rows(table_a, table_b, idx):
    idx2 = idx.reshape(1, -1)
    return _sc_gather_table(table_a, idx2), _sc_gather_table(table_b, idx2)


def _combine_kernel(gate_ref, ha_ref, hb_ref, x1_ref, mod_ref, ws1_ref, ws3_ref, ws2_ref, ga_ref, gb_ref,
                    out_ref):
    words = jnp.concatenate([ha_ref[...], hb_ref[...]], axis=1)
    lo, hi = _unpack_bf16_pair(lax.bitcast_convert_type(words, jnp.uint32))
    lo = lo.astype(bf16)
    hi = hi.astype(bf16)
    s1 = _dot(lo, ws1_ref[0:HALF, :]) + _dot(hi, ws1_ref[HALF:D_MODEL, :])
    s3 = _dot(lo, ws3_ref[0:HALF, :]) + _dot(hi, ws3_ref[HALF:D_MODEL, :])
    y = _dot((_silu(s1) * s3).astype(bf16), ws2_ref[...])

    gate = gate_ref[...].T
    gate_f = mod_ref[0, 5:6, :]
    for part, g_ref in enumerate((ga_ref, gb_ref)):
        c_lo = slice(part * QUARTER, (part + 1) * QUARTER)
        c_hi = slice(HALF + part * QUARTER, HALF + (part + 1) * QUARTER)
        acc_lo = y[:, c_lo]
        acc_hi = y[:, c_hi]
        for k in range(TOP_K):
            rlo, rhi = _unpack_bf16_pair(lax.bitcast_convert_type(g_ref[k], jnp.uint32))
            gk = gate[:, k:k + 1]
            acc_lo = acc_lo + gk * rlo
            acc_hi = acc_hi + gk * rhi
        out_ref[:, c_lo] = x1_ref[:, c_lo] + gate_f[:, c_lo] * acc_lo
        out_ref[:, c_hi] = x1_ref[:, c_hi] + gate_f[:, c_hi] * acc_hi


def _combine(gate, h2a, h2b, x1, mod, ws1, ws3, ws2, ga, gb, seq):
    t, d = x1.shape
    tm = TM_MOE
    tiles_per_seq = seq // tm
    const = lambda shape: pl.BlockSpec(shape, lambda i: (0,) * len(shape))
    return pl.pallas_call(
        _combine_kernel,
        out_shape=jax.ShapeDtypeStruct((t, d), f32),
        grid=(t // tm,),
        in_specs=[pl.BlockSpec((TOP_K, tm), lambda i: (0, i)),
                  pl.BlockSpec((tm, QUARTER), lambda i: (i, 0)),
                  pl.BlockSpec((tm, QUARTER), lambda i: (i, 0)),
                  pl.BlockSpec((tm, d), lambda i: (i, 0)),
                  pl.BlockSpec((1, 6, d), lambda i: (i // tiles_per_seq, 0, 0)),
                  const((d, D_EXPERT)), const((d, D_EXPERT)), const((D_EXPERT, d)),
                  pl.BlockSpec((TOP_K, tm, QUARTER), lambda i: (0, i, 0)),
                  pl.BlockSpec((TOP_K, tm, QUARTER), lambda i: (0, i, 0))],
        out_specs=pl.BlockSpec((tm, d), lambda i: (i, 0)),
        compiler_params=_cparams(1),
        name="combine",
    )(gate, h2a, h2b, x1, mod, ws1, ws3, ws2, ga, gb)


def _lambda_init(layer):
    return 0.8 - 0.6 * math.exp(-0.3 * layer)


def _layer(x, c, w_ada, b_ada, w_in, conv_w, conv_b, gate_b, ml_norm_g, da_q_norm_g, da_k_norm_g,
           lambda_q1, lambda_k1, lambda_q2, lambda_k2, da_norm_g, w_out, w_router, router_bias,
           w1, w3, w2, ws1, ws3, ws2, layer):
    batch, seq, d = x.shape
    t = batch * seq
    lam_init = _lambda_init(layer)
    x2 = x.reshape(t, d)

    c_pad = jnp.pad(c, ((0, -batch % SUBLANES), (0, 0)))
    mod = _adaln(c_pad, w_ada, b_ada.reshape(1, -1))[:batch].reshape(batch, 6, d)

    g0 = 4 * ML_W
    q0 = g0 + N_GATES
    n_grp = 2 * DA_HEADS
    w_main = jnp.concatenate([w_in[:, :g0], w_in[:, q0:]], axis=1).astype(bf16)
    wg = jnp.pad(w_in[:, g0:g0 + N_GATES], ((0, 0), (0, LANES - N_GATES))).astype(bf16)
    gb_row = jnp.pad(gate_b, (0, LANES - N_GATES)).reshape(1, LANES)
    qgain = (jnp.tile(da_q_norm_g, n_grp) * (DA_DH ** -0.5 * LOG2E)).reshape(1, DA_W)
    kgain = jnp.tile(da_k_norm_g, n_grp).reshape(1, DA_W)
    seg = jnp.arange(DA_W) // DA_DH
    bd = (seg[:, None] == seg[None, :]).astype(bf16)

    pm, vt, ot, qh, kh, vh, gcol, grow = _inproj(x2, mod, w_main, wg, gb_row, qgain, kgain, bd, _alibi_table(seq),
                                         conv_w, conv_b.reshape(1, -1), seq)

    hm = _mlstm(pm, vt, ot, gcol, grow, ml_norm_g.reshape(-1, 1), batch, seq)

    lam = (jnp.exp(jnp.sum(lambda_q1 * lambda_k1)) - jnp.exp(jnp.sum(lambda_q2 * lambda_k2))
           + lam_init).reshape(1, 1).astype(f32)
    hd = _attention(qh, kh, vh, lam, da_norm_g.reshape(1, -1), batch, seq, lam_init)

    x1, h2a, h2b, idx, gate, rank, counts = _route(hm, hd, x2, mod, w_out.astype(bf16),
                                                   w_router.T.astype(bf16), router_bias.reshape(-1, 1), seq)

    bm = MOE_BLOCK
    n_blocks = (t * TOP_K) // bm + N_EXPERTS
    counts_i = counts.reshape(-1).astype(jnp.int32)
    padded = (counts_i + bm - 1) // bm * bm
    pad_end = jnp.cumsum(padded)
    pad_start = pad_end - padded
    digits = jnp.stack([(pad_start // SLOT_RADIX ** j) % SLOT_RADIX for j in range(SLOT_DIGITS)])
    digits = jnp.pad(digits, ((0, SUBLANES - SLOT_DIGITS), (0, 0))).astype(bf16)
    dest = _slots(idx, rank, digits)

    xs_a = _sc_scatter_table(h2a, dest, n_blocks * bm)
    xs_b = _sc_scatter_table(h2b, dest, n_blocks * bm)
    total_blocks = (pad_end[-1:] // bm).astype(jnp.int32)
    ys_a, ys_b = _experts(pad_start.astype(jnp.int32), counts_i, total_blocks, xs_a, xs_b, w1, w3, w2)
    ga, gb = _sc_gather_rows(ys_a, ys_b, dest.reshape(-1))
    out = _combine(gate, h2a, h2b, x1, mod, ws1.astype(bf16), ws3.astype(bf16), ws2.astype(bf16),
                   ga.reshape(TOP_K, t, QUARTER), gb.reshape(TOP_K, t, QUARTER), seq)
    return out.reshape(batch, seq, d)


def kernel(x, c, w_ada, b_ada, w_in, conv_w, conv_b, gate_b, ml_norm_g, da_q_norm_g, da_k_norm_g,
           lambda_q1, lambda_k1, lambda_q2, lambda_k2, da_norm_g, w_out, w_router, router_bias,
           w1, w3, w2, ws1, ws3, ws2):
    depth = w_ada.shape[0]
    for l in range(depth):
        x = _layer(x, c, w_ada[l], b_ada[l], w_in[l], conv_w[l], conv_b[l], gate_b[l], ml_norm_g[l],
                   da_q_norm_g[l], da_k_norm_g[l], lambda_q1[l], lambda_k1[l], lambda_q2[l], lambda_k2[l],
                   da_norm_g[l], w_out[l], w_router[l], router_bias[l], w1[l], w3[l], w2[l],
                   ws1[l], ws3[l], ws2[l], l)
    return x
```

```python
import functools
import math

import jax
import jax.numpy as jnp
import numpy as np
from jax import lax
from jax.experimental import pallas as pl
from jax.experimental.pallas import tpu as pltpu
from jax.experimental.pallas import tpu_sc as plsc

D_MODEL = 1024
ML_HEADS = 4
ML_DH = 128
ML_W = ML_HEADS * ML_DH
ML_CHUNK = 128
CONV_K = 4
DA_HEADS = 4
DA_DH = 64
DA_W = DA_HEADS * 2 * DA_DH
N_EXPERTS = 256
TOP_K = 8
N_GROUPS = 8
GROUP_SIZE = N_EXPERTS // N_GROUPS
TOPK_GROUPS = 4
D_EXPERT = 256
ROUTED_SCALE = 2.5
EPS = 1e-6
N_GATES = 2 * ML_HEADS

LANES = 128
SUBLANES = 8
VMEM_LIMIT_BYTES = 48 * 1024 * 1024

TM_INPROJ = 512
ML_STEP_CHUNKS = 4
ATT_BLOCK = 1024
ATT_QBLOCKS = 2
TM_ROUTE = 512
TM_SLOTS = 512
TM_MOE = 512
MOE_BLOCK = 256
HALF = D_MODEL // 2
QUARTER = HALF // 2
SC_WINDOW = 128

NEG_BIG = -1e30
LOG2E = 1.4426950408889634

f32 = jnp.float32
bf16 = jnp.bfloat16
HIGHEST = lax.Precision.HIGHEST


def _cparams(n_axes):
    return pltpu.CompilerParams(dimension_semantics=("arbitrary",) * n_axes,
                                vmem_limit_bytes=VMEM_LIMIT_BYTES)


def _dot(a, b):
    return jnp.dot(a, b, preferred_element_type=f32)


def _dot_nt(a, b):
    return lax.dot_general(a, b, (((1,), (1,)), ((), ())), preferred_element_type=f32)


def _sigmoid(x):
    return 1.0 / (1.0 + jnp.exp(-x))


def _silu(x):
    return x * _sigmoid(x)


def _log_sigmoid(x):
    return jnp.minimum(x, 0.0) - jnp.log(1.0 + jnp.exp(-jnp.abs(x)))


def _pack_bf16_pair(lo, hi):
    lo_bits = lax.bitcast_convert_type(lo.astype(bf16).astype(f32), jnp.uint32)
    hi_bits = lax.bitcast_convert_type(hi.astype(bf16).astype(f32), jnp.uint32)
    return (hi_bits & jnp.uint32(0xFFFF0000)) | (lo_bits >> 16)


def _unpack_bf16_pair(w):
    lo = lax.bitcast_convert_type(w << 16, f32)
    hi = lax.bitcast_convert_type(w & jnp.uint32(0xFFFF0000), f32)
    return lo, hi


def _adaln_kernel(c_ref, w_ref, b_ref, o_ref):
    c = c_ref[...]
    o_ref[...] = jnp.dot(_silu(c), w_ref[...], precision=HIGHEST, preferred_element_type=f32) + b_ref[...]


def _adaln(c_pad, w_ada, b_ada):
    rows, d = c_pad.shape
    n = w_ada.shape[1]
    tn = 1536
    return pl.pallas_call(
        _adaln_kernel,
        out_shape=jax.ShapeDtypeStruct((rows, n), f32),
        grid=(n // tn,),
        in_specs=[pl.BlockSpec((rows, d), lambda j: (0, 0)),
                  pl.BlockSpec((d, tn), lambda j: (0, j)),
                  pl.BlockSpec((1, tn), lambda j: (0, j))],
        out_specs=pl.BlockSpec((rows, tn), lambda j: (0, j)),
        compiler_params=_cparams(1),
        name="adaln",
    )(c_pad, w_ada, b_ada)


def _alibi_slope_log2(head):
    return 2.0 ** (-8.0 * (head + 1) / DA_HEADS) * LOG2E


def _alibi_table(seq):
    pos = np.arange(seq, dtype=np.float32)[:, None]
    slopes = np.asarray([_alibi_slope_log2(h) for h in range(DA_HEADS)], np.float32)[None, :]

    def bf16_part(a):
        return (np.ascontiguousarray(a).view(np.uint32) & np.uint32(0xFFFF0000)).view(np.float32)

    r = pos * slopes
    r_hi = bf16_part(r)
    r_mid = bf16_part(r - r_hi)
    r_lo = bf16_part(r - r_hi - r_mid)
    table = np.zeros((seq, DA_HEADS, LANES), np.float32)
    for lane_off, part in enumerate((r_hi, r_mid, r_lo)):
        table[:, :, DA_DH + lane_off] = part
    return jnp.asarray(table.reshape(seq, DA_HEADS * LANES), dtype=bf16)


def _inproj_kernel(x_ref, mod_ref, w_ref, wg_ref, gbr_ref, qg_ref, kg_ref, bd_ref, kaug_ref, cw_ref, cb_ref,
                   pm_ref, vt_ref, ot_ref, q_ref, k_ref, v_ref, gcol_ref, grow_ref, ext_ref, *, seq):
    tm = TM_INPROJ
    x = x_ref[...]
    h = x * lax.rsqrt(jnp.mean(x * x, axis=-1, keepdims=True) + EPS)
    h = h * (1.0 + mod_ref[0, 1:2, :]) + mod_ref[0, 0:1, :]
    hb = h.astype(bf16)
    cw = ML_W

    @pl.when((pl.program_id(0) * tm) % seq == 0)
    def _():
        ext_ref[0:SUBLANES, :] = jnp.zeros((SUBLANES, 2 * cw), f32)

    cur = _dot(hb, w_ref[:, 0:2 * cw])
    ext_ref[SUBLANES:SUBLANES + tm, :] = cur
    acc = cb_ref[...] + cw_ref[CONV_K - 1:CONV_K, :] * cur
    for j in range(CONV_K - 1):
        off = SUBLANES - (CONV_K - 1) + j
        acc = acc + cw_ref[j:j + 1, :] * ext_ref[off:off + tm, :]
    ext_ref[0:SUBLANES, :] = cur[tm - SUBLANES:tm, :]
    qk = _silu(acc)
    pm_ref[:, 0:cw] = qk[:, 0:cw].astype(bf16)
    pm_ref[:, cw:2 * cw] = (qk[:, cw:2 * cw] * (ML_DH ** -0.5)).astype(bf16)
    vt_ref[...] = _dot(hb, w_ref[:, 2 * cw:3 * cw]).T.astype(bf16)
    ot_ref[...] = _dot(hb, w_ref[:, 3 * cw:4 * cw]).T.astype(bf16)

    lane = lax.broadcasted_iota(jnp.int32, (tm, LANES), 1)
    feat = lane < DA_DH
    q_aug = jnp.where(jnp.logical_and(lane >= DA_DH, lane < DA_DH + 3), 1.0, 0.0)
    q0 = 4 * cw
    k0 = q0 + DA_W

    def qk_normed(col0, gain_ref):
        y = _dot(hb, w_ref[:, col0:col0 + DA_W])
        ss = _dot((y * y).astype(bf16), bd_ref[...])
        return y * lax.rsqrt(ss * (1.0 / DA_DH) + EPS) * gain_ref[...]

    def lane_group(y2, grp, aug):
        pair = y2[:, (grp // 2) * LANES:(grp // 2 + 1) * LANES]
        if grp % 2:
            pair = pltpu.roll(pair, DA_DH, axis=1)
        return jnp.where(feat, pair, aug).astype(bf16)

    nq = qk_normed(q0, qg_ref)
    nk = qk_normed(k0, kg_ref)
    for grp in range(2 * DA_HEADS):
        sl = slice(grp * LANES, (grp + 1) * LANES)
        head = slice((grp // 2) * LANES, (grp // 2 + 1) * LANES)
        q_ref[:, sl] = lane_group(nq, grp, q_aug)
        k_ref[:, sl] = lane_group(nk, grp, kaug_ref[:, head].astype(f32))

    v0 = k0 + DA_W
    v_ref[...] = _dot(hb, w_ref[:, v0:v0 + DA_W]).astype(bf16)
    gates = _dot(hb, wg_ref[...]) + gbr_ref[...]
    gcol_ref[...] = gates
    grow_ref[...] = gates.T[0:N_GATES, :]


def _inproj(x2, mod, w_main, wg, gb_row, qgain, kgain, bd, kaug, conv_w, conv_b, seq):
    t, d = x2.shape
    tm = TM_INPROJ
    tiles_per_seq = seq // tm
    n_main = w_main.shape[1]
    const = lambda shape: pl.BlockSpec(shape, lambda i: (0,) * len(shape))
    return pl.pallas_call(
        functools.partial(_inproj_kernel, seq=seq),
        out_shape=(jax.ShapeDtypeStruct((t, 2 * ML_W), bf16),
                   jax.ShapeDtypeStruct((ML_W, t), bf16),
                   jax.ShapeDtypeStruct((ML_W, t), bf16),
                   jax.ShapeDtypeStruct((t, 2 * DA_W), bf16),
                   jax.ShapeDtypeStruct((t, 2 * DA_W), bf16),
                   jax.ShapeDtypeStruct((t, DA_W), bf16),
                   jax.ShapeDtypeStruct((t, LANES), f32),
                   jax.ShapeDtypeStruct((N_GATES, t), f32)),
        grid=(t // tm,),
        in_specs=[pl.BlockSpec((tm, d), lambda i: (i, 0)),
                  pl.BlockSpec((1, 6, d), lambda i: (i // tiles_per_seq, 0, 0)),
                  const((d, n_main)), const((d, LANES)), const((1, LANES)),
                  const((1, DA_W)), const((1, DA_W)), const((DA_W, DA_W)),
                  pl.BlockSpec((tm, DA_HEADS * LANES), lambda i: (i % tiles_per_seq, 0)),
                  const((CONV_K, 2 * ML_W)), const((1, 2 * ML_W))],
        out_specs=(pl.BlockSpec((tm, 2 * ML_W), lambda i: (i, 0)),
                   pl.BlockSpec((ML_W, tm), lambda i: (0, i)),
                   pl.BlockSpec((ML_W, tm), lambda i: (0, i)),
                   pl.BlockSpec((tm, 2 * DA_W), lambda i: (i, 0)),
                   pl.BlockSpec((tm, 2 * DA_W), lambda i: (i, 0)),
                   pl.BlockSpec((tm, DA_W), lambda i: (i, 0)),
                   pl.BlockSpec((tm, LANES), lambda i: (i, 0)),
                   pl.BlockSpec((N_GATES, tm), lambda i: (0, i))),
        scratch_shapes=[pltpu.VMEM((SUBLANES + tm, 2 * ML_W), f32)],
        compiler_params=_cparams(1),
        name="inproj",
    )(x2, mod, w_main, wg, gb_row, qgain, kgain, bd, kaug, conv_w, conv_b)


def _mlstm_kernel(q_ref, k_ref, vt_ref, ot_ref, gcol_ref, grow_ref, ng_ref, out_ref, state_ref, m_ref):
    L = ML_CHUNK
    dh = ML_DH

    @pl.when(pl.program_id(1) == 0)
    def _():
        state_ref[...] = jnp.zeros_like(state_ref)
        m_ref[...] = jnp.zeros_like(m_ref)

    s_i = lax.broadcasted_iota(jnp.int32, (L, L), 0)
    t_i = lax.broadcasted_iota(jnp.int32, (L, L), 1)
    visible = s_i <= t_i
    tril = (s_i >= t_i).astype(f32)
    triu = visible.astype(f32)
    ones_rows = (lax.broadcasted_iota(jnp.int32, (dh, L), 0) == 0).astype(f32)
    g_b = jnp.broadcast_to(ng_ref[...], (dh, L))

    for cc in range(ML_STEP_CHUNKS):
        rows = slice(cc * L, (cc + 1) * L)
        gcol = gcol_ref[rows, :]
        grow = grow_ref[:, rows]
        bcol_all = jnp.dot(tril, _log_sigmoid(gcol), precision=HIGHEST, preferred_element_type=f32)
        brow_all = jnp.dot(_log_sigmoid(grow), triu, precision=HIGHEST, preferred_element_type=f32)

        for hd in range(ML_HEADS):
            cols = slice(hd * dh, (hd + 1) * dh)
            qb = q_ref[rows, cols]
            kb = k_ref[rows, cols]
            v_aug_t = jnp.concatenate([vt_ref[cols, rows].astype(f32), ones_rows], axis=0)

            brow = brow_all[ML_HEADS + hd:ML_HEADS + hd + 1, :]
            irow = grow[hd:hd + 1, :]
            key_term = gcol[:, hd:hd + 1] - bcol_all[:, ML_HEADS + hd:ML_HEADS + hd + 1]
            m_prev = m_ref[hd:hd + 1, :]

            d_intra = jnp.where(visible, brow + key_term, NEG_BIG)
            d_inter = brow + m_prev
            m_t = jnp.maximum(d_inter, jnp.max(d_intra, axis=0, keepdims=True))
            w_intra = jnp.exp(d_intra - m_t)
            w_inter = jnp.exp(d_inter - m_t)

            s_t = _dot_nt(kb, qb) * w_intra
            state = state_ref[hd]
            q_state = _dot_nt(state.astype(bf16), qb)
            s_v = _dot(v_aug_t.astype(bf16), s_t.astype(bf16))
            num = w_inter * q_state[0:dh, :] + s_v[0:dh, :]
            den = w_inter * q_state[dh:dh + 1, :] + s_v[dh:dh + 1, :]
            hval = num / jnp.maximum(jnp.abs(den), jnp.exp(-m_t))

            b_last = brow[:, L - 1:L]
            d_state = b_last - brow + irow
            m_new = jnp.maximum(b_last + m_prev, jnp.max(d_state, axis=1, keepdims=True))
            carry_scale = jnp.exp(b_last + m_prev - m_new)[:, 0:1]
            wk = jnp.exp(d_state - m_new)
            upd = _dot((v_aug_t * wk).astype(bf16), kb)
            state_ref[hd] = carry_scale * state + upd
            m_ref[hd:hd + 1, :] = m_new

            hn = hval * lax.rsqrt(jnp.mean(hval * hval, axis=0, keepdims=True) + EPS) * g_b
            og = _sigmoid(ot_ref[cols, rows].astype(f32))
            out_ref[rows, cols] = (hn * og).T.astype(bf16)


def _mlstm(pm, vt, ot, gcol, grow, ml_norm_g, batch, seq):
    t = pm.shape[0]
    rows = ML_STEP_CHUNKS * ML_CHUNK
    ns = seq // rows
    row = lambda b, c: b * ns + c
    return pl.pallas_call(
        _mlstm_kernel,
        out_shape=jax.ShapeDtypeStruct((t, ML_W), bf16),
        grid=(batch, ns),
        in_specs=[pl.BlockSpec((rows, ML_W), lambda b, c: (row(b, c), 0)),
                  pl.BlockSpec((rows, ML_W), lambda b, c: (row(b, c), 1)),
                  pl.BlockSpec((ML_W, rows), lambda b, c: (0, row(b, c))),
                  pl.BlockSpec((ML_W, rows), lambda b, c: (0, row(b, c))),
                  pl.BlockSpec((rows, LANES), lambda b, c: (row(b, c), 0)),
                  pl.BlockSpec((N_GATES, rows), lambda b, c: (0, row(b, c))),
                  pl.BlockSpec((ML_DH, 1), lambda b, c: (0, 0))],
        out_specs=pl.BlockSpec((rows, ML_W), lambda b, c: (row(b, c), 0)),
        scratch_shapes=[pltpu.VMEM((ML_HEADS, 2 * ML_DH, ML_DH), f32),
                        pltpu.VMEM((SUBLANES, LANES), f32)],
        compiler_params=_cparams(2),
        name="mlstm",
    )(pm, pm, vt, ot, gcol, grow, ml_norm_g)


def _attn_block(q_ref, k_ref, v_ref, m_ref, l_ref, acc_ref, row0, nrows, nkeys, key0=None):
    rows = slice(row0, row0 + nrows)
    nch = nkeys // LANES
    v = v_ref[0:nkeys, :]
    if key0 is not None:
        keep = (key0 + lax.broadcasted_iota(jnp.int32, (nrows, nkeys), 1)
                <= row0 + lax.broadcasted_iota(jnp.int32, (nrows, nkeys), 0))
    for c in range(2):
        sl = slice(c * LANES, (c + 1) * LANES)
        s = _dot_nt(q_ref[rows, sl], k_ref[0:nkeys, sl])
        if key0 is not None:
            s = jnp.where(keep, s, NEG_BIG)
        chunks = [s[:, j * LANES:(j + 1) * LANES] for j in range(nch)]
        mc = chunks[0]
        for ch in chunks[1:]:
            mc = jnp.maximum(mc, ch)
        m_old = m_ref[c, rows, :]
        m_new = jnp.maximum(m_old, jnp.max(mc, axis=1, keepdims=True))
        alpha = jnp.exp2(m_old - m_new)
        ps = [jnp.exp2(ch - m_new) for ch in chunks]
        lsum = ps[0]
        for pj in ps[1:]:
            lsum = lsum + pj
        p = jnp.concatenate([pj.astype(bf16) for pj in ps], axis=1)
        l_ref[c, rows, :] = alpha * l_ref[c, rows, :] + lsum
        acc_ref[c, rows, :] = alpha * acc_ref[c, rows, :] + _dot(p, v)
        m_ref[c, rows, :] = m_new


def _attn_kernel(qt_ref, kt_ref, q_ref, k_ref, v_ref, lam_ref, g_ref, o_ref,
                 m_ref, l_ref, acc_ref, *, lam_init):
    step = pl.program_id(2)
    qi = qt_ref[step]
    ki = kt_ref[step]

    @pl.when(ki == 0)
    def _():
        m_ref[...] = jnp.full_like(m_ref, NEG_BIG)
        l_ref[...] = jnp.zeros_like(l_ref)
        acc_ref[...] = jnp.zeros_like(acc_ref)

    blk = ATT_BLOCK
    half = blk // 2
    qrows = ATT_QBLOCKS * blk
    first_diag = ATT_QBLOCKS * qi
    args = (q_ref, k_ref, v_ref, m_ref, l_ref, acc_ref)

    @pl.when(ki < first_diag)
    def _():
        _attn_block(*args, 0, qrows, blk)

    for j in range(ATT_QBLOCKS):
        @pl.when(ki == first_diag + j)
        def _(j=j):
            r0 = j * blk
            _attn_block(*args, r0, half, half, key0=r0)
            _attn_block(*args, r0 + half, half, blk, key0=r0)
            if r0 + blk < qrows:
                _attn_block(*args, r0 + blk, qrows - r0 - blk, blk)

    @pl.when(ki == first_diag + ATT_QBLOCKS - 1)
    def _():
        lam = lam_ref[...]
        l0 = jnp.sum(l_ref[0], axis=1, keepdims=True)
        l1 = jnp.sum(l_ref[1], axis=1, keepdims=True)
        o = acc_ref[0] / l0 - lam * (acc_ref[1] / l1)
        o = o * lax.rsqrt(jnp.mean(o * o, axis=-1, keepdims=True) + EPS)
        o_ref[...] = (o * g_ref[...] * (1.0 - lam_init)).astype(bf16)


def _attention(qh, kh, vh, lam, da_norm_g, batch, seq, lam_init):
    t = qh.shape[0]
    blk = ATT_BLOCK
    qrows = ATT_QBLOCKS * blk
    assert seq % qrows == 0, (seq, qrows)
    nk = seq // blk
    nq = seq // qrows
    pairs = [(i, j) for i in range(nq) for j in range(ATT_QBLOCKS * (i + 1))]
    qt = jnp.asarray([p[0] for p in pairs], jnp.int32)
    kt = jnp.asarray([p[1] for p in pairs], jnp.int32)
    w = 2 * DA_DH
    grid_spec = pltpu.PrefetchScalarGridSpec(
        num_scalar_prefetch=2,
        grid=(batch, DA_HEADS, len(pairs)),
        in_specs=[pl.BlockSpec((qrows, 2 * LANES), lambda b, h, s, qt, kt: (b * nq + qt[s], h)),
                  pl.BlockSpec((blk, 2 * LANES), lambda b, h, s, qt, kt: (b * nk + kt[s], h)),
                  pl.BlockSpec((blk, w), lambda b, h, s, qt, kt: (b * nk + kt[s], h)),
                  pl.BlockSpec((1, 1), lambda b, h, s, qt, kt: (0, 0)),
                  pl.BlockSpec((1, w), lambda b, h, s, qt, kt: (0, 0))],
        out_specs=pl.BlockSpec((qrows, w), lambda b, h, s, qt, kt: (b * nq + qt[s], h)),
        scratch_shapes=[pltpu.VMEM((2, qrows, LANES), f32),
                        pltpu.VMEM((2, qrows, LANES), f32),
                        pltpu.VMEM((2, qrows, w), f32)],
    )
    return pl.pallas_call(
        functools.partial(_attn_kernel, lam_init=lam_init),
        out_shape=jax.ShapeDtypeStruct((t, DA_W), bf16),
        grid_spec=grid_spec,
        compiler_params=_cparams(3),
        name="attn",
    )(qt, kt, qh, kh, vh, lam, da_norm_g)


def _first_index_of_max(x, iota_f, size):
    m = jnp.max(x, axis=0, keepdims=True)
    idx = jnp.min(jnp.where(x == m, iota_f, float(size)), axis=0, keepdims=True)
    return m, idx


def _route_kernel(hm_ref, hd_ref, x_ref, mod_ref, wo_ref, wr_ref, rb_ref,
                  x1_ref, ha_ref, hb_ref, idx_ref, gate_ref, rank_ref, cnt_ref, base_ref):
    tm = TM_ROUTE
    i = pl.program_id(0)

    @pl.when(i == 0)
    def _():
        base_ref[...] = jnp.zeros_like(base_ref)

    mix = _dot(hm_ref[...], wo_ref[0:ML_W, :]) + _dot(hd_ref[...], wo_ref[ML_W:ML_W + DA_W, :])
    x1 = x_ref[...] + mod_ref[0, 2:3, :] * mix
    x1_ref[...] = x1
    h2 = x1 * lax.rsqrt(jnp.mean(x1 * x1, axis=-1, keepdims=True) + EPS)
    h2 = h2 * (1.0 + mod_ref[0, 4:5, :]) + mod_ref[0, 3:4, :]
    packed = lax.bitcast_convert_type(_pack_bf16_pair(h2[:, 0:HALF], h2[:, HALF:D_MODEL]), jnp.int32)
    ha_ref[...] = packed[:, 0:QUARTER]
    hb_ref[...] = packed[:, QUARTER:HALF]

    sc = _sigmoid(_dot_nt(wr_ref[...], h2.astype(bf16)))
    sel = sc + rb_ref[...]

    gi = lax.broadcasted_iota(jnp.int32, (GROUP_SIZE, tm), 0).astype(f32)
    gscores = []
    for g in range(N_GROUPS):
        blk = sel[g * GROUP_SIZE:(g + 1) * GROUP_SIZE, :]
        m1, i1 = _first_index_of_max(blk, gi, GROUP_SIZE)
        m2 = jnp.max(jnp.where(gi == i1, NEG_BIG, blk), axis=0, keepdims=True)
        gscores.append(m1 + m2)
    gs = jnp.concatenate(gscores, axis=0)

    ngi = lax.broadcasted_iota(jnp.int32, (N_GROUPS, tm), 0).astype(f32)
    gkeep = jnp.zeros((N_GROUPS, tm), f32)
    for _ in range(TOPK_GROUPS):
        _, gidx = _first_index_of_max(gs, ngi, N_GROUPS)
        hit = ngi == gidx
        gkeep = jnp.where(hit, 1.0, gkeep)
        gs = jnp.where(hit, NEG_BIG, gs)
    masked = jnp.concatenate(
        [jnp.where(gkeep[g:g + 1, :] > 0.0, sel[g * GROUP_SIZE:(g + 1) * GROUP_SIZE, :], NEG_BIG)
         for g in range(N_GROUPS)], axis=0)

    ei = lax.broadcasted_iota(jnp.int32, (N_EXPERTS, tm), 0).astype(f32)
    idxs, gates = [], []
    chosen = jnp.zeros((N_EXPERTS, tm), f32)
    for _ in range(TOP_K):
        _, eidx = _first_index_of_max(masked, ei, N_EXPERTS)
        hit = ei == eidx
        idxs.append(eidx)
        gates.append(jnp.sum(jnp.where(hit, sc, 0.0), axis=0, keepdims=True))
        chosen = jnp.where(hit, 1.0, chosen)
        masked = jnp.where(hit, NEG_BIG, masked)
    gate = jnp.concatenate(gates, axis=0)
    gate = gate / jnp.sum(gate, axis=0, keepdims=True) * ROUTED_SCALE
    gate_ref[...] = gate
    idx_ref[...] = jnp.concatenate(idxs, axis=0).astype(jnp.int32)

    tr = lax.broadcasted_iota(jnp.int32, (tm, tm), 0)
    tc = lax.broadcasted_iota(jnp.int32, (tm, tm), 1)
    before = (tr < tc).astype(bf16)
    seen = base_ref[...] + _dot(chosen.astype(bf16), before)
    ranks = [jnp.sum(jnp.where(ei == eidx, seen, 0.0), axis=0, keepdims=True) for eidx in idxs]
    rank_ref[...] = jnp.concatenate(ranks, axis=0).astype(jnp.int32)
    total = base_ref[...] + jnp.sum(chosen, axis=1, keepdims=True)
    base_ref[...] = total
    cnt_ref[...] = total


def _route(hm, hd, x2, mod, w_out, w_router_t, rbias_col, seq):
    t, d = x2.shape
    tm = TM_ROUTE
    tiles_per_seq = seq // tm
    const = lambda shape: pl.BlockSpec(shape, lambda i: (0,) * len(shape))
    return pl.pallas_call(
        _route_kernel,
        out_shape=(jax.ShapeDtypeStruct((t, d), f32),
                   jax.ShapeDtypeStruct((t, QUARTER), jnp.int32),
                   jax.ShapeDtypeStruct((t, QUARTER), jnp.int32),
                   jax.ShapeDtypeStruct((TOP_K, t), jnp.int32),
                   jax.ShapeDtypeStruct((TOP_K, t), f32),
                   jax.ShapeDtypeStruct((TOP_K, t), jnp.int32),
                   jax.ShapeDtypeStruct((N_EXPERTS, 1), f32)),
        grid=(t // tm,),
        in_specs=[pl.BlockSpec((tm, ML_W), lambda i: (i, 0)),
                  pl.BlockSpec((tm, DA_W), lambda i: (i, 0)),
                  pl.BlockSpec((tm, d), lambda i: (i, 0)),
                  pl.BlockSpec((1, 6, d), lambda i: (i // tiles_per_seq, 0, 0)),
                  const((d, d)), const((N_EXPERTS, d)), const((N_EXPERTS, 1))],
        out_specs=(pl.BlockSpec((tm, d), lambda i: (i, 0)),
                   pl.BlockSpec((tm, QUARTER), lambda i: (i, 0)),
                   pl.BlockSpec((tm, QUARTER), lambda i: (i, 0)),
                   pl.BlockSpec((TOP_K, tm), lambda i: (0, i)),
                   pl.BlockSpec((TOP_K, tm), lambda i: (0, i)),
                   pl.BlockSpec((TOP_K, tm), lambda i: (0, i)),
                   const((N_EXPERTS, 1))),
        scratch_shapes=[pltpu.VMEM((N_EXPERTS, 1), f32)],
        compiler_params=_cparams(1),
        name="route",
    )(hm, hd, x2, mod, w_out, w_router_t, rbias_col)


SLOT_RADIX = 256
SLOT_DIGITS = 3


def _slots_kernel(idx_ref, rank_ref, dig_ref, dest_ref):
    tm = TM_SLOTS
    ei = lax.broadcasted_iota(jnp.int32, (N_EXPERTS, tm), 0)
    rows = []
    for k in range(TOP_K):
        onehot = jnp.where(ei == idx_ref[k:k + 1, :], 1.0, 0.0).astype(bf16)
        dg = _dot(dig_ref[...], onehot)
        start = dg[0:1, :]
        for j in range(1, SLOT_DIGITS):
            start = start + dg[j:j + 1, :] * float(SLOT_RADIX ** j)
        rows.append(start.astype(jnp.int32) + rank_ref[k:k + 1, :])
    dest_ref[...] = jnp.concatenate(rows, axis=0)


def _slots(idx, rank, digits):
    t = idx.shape[1]
    tm = TM_SLOTS
    return pl.pallas_call(
        _slots_kernel,
        out_shape=jax.ShapeDtypeStruct((TOP_K, t), jnp.int32),
        grid=(t // tm,),
        in_specs=[pl.BlockSpec((TOP_K, tm), lambda i: (0, i)),
                  pl.BlockSpec((TOP_K, tm), lambda i: (0, i)),
                  pl.BlockSpec((SUBLANES, N_EXPERTS), lambda i: (0, 0))],
        out_specs=pl.BlockSpec((TOP_K, tm), lambda i: (0, i)),
        compiler_params=_cparams(1),
        name="slots",
    )(idx, rank, digits)


def _sc_scatter_table(rows, dest, n_slots):
    n, width = rows.shape
    mesh = plsc.VectorSubcoreMesh(core_axis_name="core", subcore_axis_name="subcore")

    @pl.kernel(out_type=jax.ShapeDtypeStruct((n_slots, width), rows.dtype), mesh=mesh, scratch_types=[],
               name="sc_scatter")
    def scatter_kernel(rows_hbm, dest_hbm, out_hbm):
        def body(rows_vmem, dest_vmem):
            for k in range(TOP_K):
                pltpu.sync_copy(rows_vmem, out_hbm.at[dest_vmem.at[k]])

        pltpu.emit_pipeline(
            body,
            grid=(n // SC_WINDOW,),
            in_specs=[pl.BlockSpec((SC_WINDOW, width), lambda i: (i, 0)),
                      pl.BlockSpec((TOP_K, SC_WINDOW), lambda i: (0, i))],
            out_specs=[],
            core_axis_name=("core", "subcore"),
            dimension_semantics=(pltpu.PARALLEL,),
        )(rows_hbm, dest_hbm)

    return scatter_kernel(rows, dest)


EXPERT_GROUP = 4
EXPERT_RING = 16
EXPERT_AHEAD = EXPERT_RING - EXPERT_GROUP


def _expert_kernel(start_ref, cnt_ref, total_ref, xsa_ref, xsb_ref, w1_ref, w3_ref, w2_ref,
                   ysa_ref, ysb_ref, xbuf, ybuf, w1b, w3b, w2b, in_sem, out_sem):
    bm = MOE_BLOCK
    ring = EXPERT_RING
    ahead = EXPERT_AHEAD
    shift = int(math.log2(bm))
    e = pl.program_id(0)
    total = total_ref[0]
    cnt = cnt_ref[e]
    nb = lax.shift_right_logical(cnt + (bm - 1), shift)
    g0 = lax.shift_right_logical(start_ref[e], shift)

    def slot_of(g):
        return jnp.bitwise_and(g, ring - 1)

    def in_copies(g):
        s = slot_of(g)
        rows = pl.ds(pl.multiple_of(g * bm, bm), bm)
        return [pltpu.make_async_copy(x_ref.at[rows], xbuf.at[s, half], in_sem.at[s])
                for half, x_ref in enumerate((xsa_ref, xsb_ref))]

    def out_copies(g):
        s = slot_of(g)
        rows = pl.ds(pl.multiple_of(g * bm, bm), bm)
        return [pltpu.make_async_copy(ybuf.at[s, half], y_ref.at[rows], out_sem.at[s])
                for half, y_ref in enumerate((ysa_ref, ysb_ref))]

    def request(g):
        @pl.when(g < total)
        def _():
            for cp in in_copies(g):
                cp.start()

    def acquire(g):
        for cp in in_copies(g):
            cp.wait()

        @pl.when(g >= ring)
        def _():
            for cp in out_copies(g - ring):
                cp.wait()

    def compute(g, n):
        slots = [slot_of(g + d) for d in range(n)]
        row = (g - g0) * bm + lax.broadcasted_iota(jnp.int32, (n * bm, HALF), 0)
        words = jnp.concatenate(
            [jnp.concatenate([xbuf[s, 0], xbuf[s, 1]], axis=1) for s in slots], axis=0)
        lo, hi = _unpack_bf16_pair(lax.bitcast_convert_type(jnp.where(row < cnt, words, 0), jnp.uint32))
        lo = lo.astype(bf16)
        hi = hi.astype(bf16)
        h1 = _dot(lo, w1b[0:HALF, :]) + _dot(hi, w1b[HALF:D_MODEL, :])
        h3 = _dot(lo, w3b[0:HALF, :]) + _dot(hi, w3b[HALF:D_MODEL, :])
        y = _dot((_silu(h1) * h3).astype(bf16), w2b[...])
        packed = lax.bitcast_convert_type(_pack_bf16_pair(y[:, 0:HALF], y[:, HALF:D_MODEL]), jnp.int32)
        for d, s in enumerate(slots):
            ybuf[s, 0] = packed[d * bm:(d + 1) * bm, 0:QUARTER]
            ybuf[s, 1] = packed[d * bm:(d + 1) * bm, QUARTER:HALF]

    def release(g):
        for cp in out_copies(g):
            cp.start()

    @pl.when(e == 0)
    def _():
        for g in range(ahead):
            request(g)

    @pl.when(nb > 0)
    def _():
        w1b[...] = w1_ref[0].astype(bf16)
        w3b[...] = w3_ref[0].astype(bf16)
        w2b[...] = w2_ref[0].astype(bf16)

    def run(g, n):
        for d in range(n):
            request(g + d + ahead)
        for d in range(n):
            acquire(g + d)
        compute(g, n)
        for d in range(n):
            release(g + d)

    def group(i, carry):
        run(g0 + EXPERT_GROUP * i, EXPERT_GROUP)
        return carry

    lax.fori_loop(0, lax.shift_right_logical(nb, int(math.log2(EXPERT_GROUP))), group, 0)

    n = EXPERT_GROUP // 2
    while n >= 1:
        @pl.when(jnp.bitwise_and(nb, n) != 0)
        def _(n=n):
            done = jnp.bitwise_and(nb, -2 * n)
            run(g0 + done, n)
        n //= 2

    @pl.when(e == pl.num_programs(0) - 1)
    def _():
        for back in range(1, ring + 1):
            @pl.when(total - back >= 0)
            def _():
                for cp in out_copies(total - back):
                    cp.wait()


def _experts(start, counts, total, xs_a, xs_b, w1, w3, w2):
    n_pad = xs_a.shape[0]
    bm = MOE_BLOCK
    grid_spec = pltpu.PrefetchScalarGridSpec(
        num_scalar_prefetch=3,
        grid=(N_EXPERTS,),
        in_specs=[pl.BlockSpec(memory_space=pl.ANY),
                  pl.BlockSpec(memory_space=pl.ANY),
                  pl.BlockSpec((1, D_MODEL, D_EXPERT), lambda e, st, ct, tt: (e, 0, 0)),
                  pl.BlockSpec((1, D_MODEL, D_EXPERT), lambda e, st, ct, tt: (e, 0, 0)),
                  pl.BlockSpec((1, D_EXPERT, D_MODEL), lambda e, st, ct, tt: (e, 0, 0))],
        out_specs=(pl.BlockSpec(memory_space=pl.ANY), pl.BlockSpec(memory_space=pl.ANY)),
        scratch_shapes=[pltpu.VMEM((EXPERT_RING, 2, bm, QUARTER), jnp.int32),
                        pltpu.VMEM((EXPERT_RING, 2, bm, QUARTER), jnp.int32),
                        pltpu.VMEM((D_MODEL, D_EXPERT), bf16),
                        pltpu.VMEM((D_MODEL, D_EXPERT), bf16),
                        pltpu.VMEM((D_EXPERT, D_MODEL), bf16),
                        pltpu.SemaphoreType.DMA((EXPERT_RING,)),
                        pltpu.SemaphoreType.DMA((EXPERT_RING,))],
    )
    return pl.pallas_call(
        _expert_kernel,
        out_shape=(jax.ShapeDtypeStruct((n_pad, QUARTER), jnp.int32),
                   jax.ShapeDtypeStruct((n_pad, QUARTER), jnp.int32)),
        grid_spec=grid_spec,
        compiler_params=_cparams(1),
        name="experts",
    )(start, counts, total, xs_a, xs_b, w1, w3, w2)


def _sc_gather_table(table, idx2):
    n = idx2.shape[1]
    width = table.shape[1]
    mesh = plsc.VectorSubcoreMesh(core_axis_name="core", subcore_axis_name="subcore")

    @pl.kernel(out_type=jax.ShapeDtypeStruct((n, width), table.dtype), mesh=mesh, scratch_types=[],
               name="sc_gather")
    def gather_kernel(tab_hbm, idx_hbm, out_hbm):
        def body(idx_vmem, out_vmem):
            pltpu.sync_copy(tab_hbm.at[idx_vmem.at[0]], out_vmem)

        pltpu.emit_pipeline(
            body,
            grid=(n // SC_WINDOW,),
            in_specs=[pl.BlockSpec((1, SC_WINDOW), lambda i: (0, i))],
            out_specs=[pl.BlockSpec((SC_WINDOW, width), lambda i: (i, 0))],
            core_axis_name=("core", "subcore"),
            dimension_semantics=(pltpu.PARALLEL,),
        )(idx_hbm, out_hbm)

    return gather_kernel(table, idx2)


def _sc_gather_rows(table_a, table_b, idx):
    idx2 = idx.reshape(1, -1)
    return _sc_gather_table(table_a, idx2), _sc_gather_table(table_b, idx2)


def _combine_kernel(gate_ref, ha_ref, hb_ref, x1_ref, mod_ref, ws1_ref, ws3_ref, ws2_ref, ga_ref, gb_ref,
                    out_ref):
    words = jnp.concatenate([ha_ref[...], hb_ref[...]], axis=1)
    lo, hi = _unpack_bf16_pair(lax.bitcast_convert_type(words, jnp.uint32))
    lo = lo.astype(bf16)
    hi = hi.astype(bf16)
    s1 = _dot(lo, ws1_ref[0:HALF, :]) + _dot(hi, ws1_ref[HALF:D_MODEL, :])
    s3 = _dot(lo, ws3_ref[0:HALF, :]) + _dot(hi, ws3_ref[HALF:D_MODEL, :])
    y = _dot((_silu(s1) * s3).astype(bf16), ws2_ref[...])

    gate = gate_ref[...].T
    gate_f = mod_ref[0, 5:6, :]
    for part, g_ref in enumerate((ga_ref, gb_ref)):
        c_lo = slice(part * QUARTER, (part + 1) * QUARTER)
        c_hi = slice(HALF + part * QUARTER, HALF + (part + 1) * QUARTER)
        acc_lo = y[:, c_lo]
        acc_hi = y[:, c_hi]
        for k in range(TOP_K):
            rlo, rhi = _unpack_bf16_pair(lax.bitcast_convert_type(g_ref[k], jnp.uint32))
            gk = gate[:, k:k + 1]
            acc_lo = acc_lo + gk * rlo
            acc_hi = acc_hi + gk * rhi
        out_ref[:, c_lo] = x1_ref[:, c_lo] + gate_f[:, c_lo] * acc_lo
        out_ref[:, c_hi] = x1_ref[:, c_hi] + gate_f[:, c_hi] * acc_hi


def _combine(gate, h2a, h2b, x1, mod, ws1, ws3, ws2, ga, gb, seq):
    t, d = x1.shape
    tm = TM_MOE
    tiles_per_seq = seq // tm
    const = lambda shape: pl.BlockSpec(shape, lambda i: (0,) * len(shape))
    return pl.pallas_call(
        _combine_kernel,
        out_shape=jax.ShapeDtypeStruct((t, d), f32),
        grid=(t // tm,),
        in_specs=[pl.BlockSpec((TOP_K, tm), lambda i: (0, i)),
                  pl.BlockSpec((tm, QUARTER), lambda i: (i, 0)),
                  pl.BlockSpec((tm, QUARTER), lambda i: (i, 0)),
                  pl.BlockSpec((tm, d), lambda i: (i, 0)),
                  pl.BlockSpec((1, 6, d), lambda i: (i // tiles_per_seq, 0, 0)),
                  const((d, D_EXPERT)), const((d, D_EXPERT)), const((D_EXPERT, d)),
                  pl.BlockSpec((TOP_K, tm, QUARTER), lambda i: (0, i, 0)),
                  pl.BlockSpec((TOP_K, tm, QUARTER), lambda i: (0, i, 0))],
        out_specs=pl.BlockSpec((tm, d), lambda i: (i, 0)),
        compiler_params=_cparams(1),
        name="combine",
    )(gate, h2a, h2b, x1, mod, ws1, ws3, ws2, ga, gb)


def _lambda_init(layer):
    return 0.8 - 0.6 * math.exp(-0.3 * layer)


def _layer(x, c, w_ada, b_ada, w_in, conv_w, conv_b, gate_b, ml_norm_g, da_q_norm_g, da_k_norm_g,
           lambda_q1, lambda_k1, lambda_q2, lambda_k2, da_norm_g, w_out, w_router, router_bias,
           w1, w3, w2, ws1, ws3, ws2, layer):
    batch, seq, d = x.shape
    t = batch * seq
    lam_init = _lambda_init(layer)
    x2 = x.reshape(t, d)

    c_pad = jnp.pad(c, ((0, -batch % SUBLANES), (0, 0)))
    mod = _adaln(c_pad, w_ada, b_ada.reshape(1, -1))[:batch].reshape(batch, 6, d)

    g0 = 4 * ML_W
    q0 = g0 + N_GATES
    n_grp = 2 * DA_HEADS
    w_main = jnp.concatenate([w_in[:, :g0], w_in[:, q0:]], axis=1).astype(bf16)
    wg = jnp.pad(w_in[:, g0:g0 + N_GATES], ((0, 0), (0, LANES - N_GATES))).astype(bf16)
    gb_row = jnp.pad(gate_b, (0, LANES - N_GATES)).reshape(1, LANES)
    qgain = (jnp.tile(da_q_norm_g, n_grp) * (DA_DH ** -0.5 * LOG2E)).reshape(1, DA_W)
    kgain = jnp.tile(da_k_norm_g, n_grp).reshape(1, DA_W)
    seg = jnp.arange(DA_W) // DA_DH
    bd = (seg[:, None] == seg[None, :]).astype(bf16)

    pm, vt, ot, qh, kh, vh, gcol, grow = _inproj(x2, mod, w_main, wg, gb_row, qgain, kgain, bd, _alibi_table(seq),
                                         conv_w, conv_b.reshape(1, -1), seq)

    hm = _mlstm(pm, vt, ot, gcol, grow, ml_norm_g.reshape(-1, 1), batch, seq)

    lam = (jnp.exp(jnp.sum(lambda_q1 * lambda_k1)) - jnp.exp(jnp.sum(lambda_q2 * lambda_k2))
           + lam_init).reshape(1, 1).astype(f32)
    hd = _attention(qh, kh, vh, lam, da_norm_g.reshape(1, -1), batch, seq, lam_init)

    x1, h2a, h2b, idx, gate, rank, counts = _route(hm, hd, x2, mod, w_out.astype(bf16),
                                                   w_router.T.astype(bf16), router_bias.reshape(-1, 1), seq)

    bm = MOE_BLOCK
    n_blocks = (t * TOP_K) // bm + N_EXPERTS
    counts_i = counts.reshape(-1).astype(jnp.int32)
    padded = (counts_i + bm - 1) // bm * bm
    pad_end = jnp.cumsum(padded)
    pad_start = pad_end - padded
    digits = jnp.stack([(pad_start // SLOT_RADIX ** j) % SLOT_RADIX for j in range(SLOT_DIGITS)])
    digits = jnp.pad(digits, ((0, SUBLANES - SLOT_DIGITS), (0, 0))).astype(bf16)
    dest = _slots(idx, rank, digits)

    xs_a = _sc_scatter_table(h2a, dest, n_blocks * bm)
    xs_b = _sc_scatter_table(h2b, dest, n_blocks * bm)
    total_blocks = (pad_end[-1:] // bm).astype(jnp.int32)
    ys_a, ys_b = _experts(pad_start.astype(jnp.int32), counts_i, total_blocks, xs_a, xs_b, w1, w3, w2)
    ga, gb = _sc_gather_rows(ys_a, ys_b, dest.reshape(-1))
    out = _combine(gate, h2a, h2b, x1, mod, ws1.astype(bf16), ws3.astype(bf16), ws2.astype(bf16),
                   ga.reshape(TOP_K, t, QUARTER), gb.reshape(TOP_K, t, QUARTER), seq)
    return out.reshape(batch, seq, d)


def kernel(x, c, w_ada, b_ada, w_in, conv_w, conv_b, gate_b, ml_norm_g, da_q_norm_g, da_k_norm_g,
           lambda_q1, lambda_k1, lambda_q2, lambda_k2, da_norm_g, w_out, w_router, router_bias,
           w1, w3, w2, ws1, ws3, ws2):
    depth = w_ada.shape[0]
    for l in range(depth):
        x = _layer(x, c, w_ada[l], b_ada[l], w_in[l], conv_w[l], conv_b[l], gate_b[l], ml_norm_g[l],
                   da_q_norm_g[l], da_k_norm_g[l], lambda_q1[l], lambda_k1[l], lambda_q2[l], lambda_k2[l],
                   da_norm_g[l], w_out[l], w_router[l], router_bias[l], w1[l], w3[l], w2[l],
                   ws1[l], ws3[l], ws2[l], l)
    return x
```

```python
import functools
import math

import jax
import jax.numpy as jnp
import numpy as np
from jax import lax
from jax.experimental import pallas as pl
from jax.experimental.pallas import tpu as pltpu
from jax.experimental.pallas import tpu_sc as plsc

D_MODEL = 1024
ML_HEADS = 4
ML_DH = 128
ML_W = ML_HEADS * ML_DH
ML_CHUNK = 128
CONV_K = 4
DA_HEADS = 4
DA_DH = 64
DA_W = DA_HEADS * 2 * DA_DH
N_EXPERTS = 256
TOP_K = 8
N_GROUPS = 8
GROUP_SIZE = N_EXPERTS // N_GROUPS
TOPK_GROUPS = 4
D_EXPERT = 256
ROUTED_SCALE = 2.5
EPS = 1e-6
N_GATES = 2 * ML_HEADS

LANES = 128
SUBLANES = 8
VMEM_LIMIT_BYTES = 48 * 1024 * 1024

TM_INPROJ = 512
ML_STEP_CHUNKS = 4
ATT_BLOCK = 1024
ATT_QBLOCKS = 2
TM_ROUTE = 512
TM_SLOTS = 512
TM_MOE = 512
MOE_BLOCK = 256
HALF = D_MODEL // 2
QUARTER = HALF // 2
SC_WINDOW = 128

NEG_BIG = -1e30
LOG2E = 1.4426950408889634

f32 = jnp.float32
bf16 = jnp.bfloat16
HIGHEST = lax.Precision.HIGHEST


def _cparams(n_axes):
    return pltpu.CompilerParams(dimension_semantics=("arbitrary",) * n_axes,
                                vmem_limit_bytes=VMEM_LIMIT_BYTES)


def _dot(a, b):
    return jnp.dot(a, b, preferred_element_type=f32)


def _dot_nt(a, b):
    return lax.dot_general(a, b, (((1,), (1,)), ((), ())), preferred_element_type=f32)


def _sigmoid(x):
    return 1.0 / (1.0 + jnp.exp(-x))


def _silu(x):
    return x * _sigmoid(x)


def _log_sigmoid(x):
    return jnp.minimum(x, 0.0) - jnp.log(1.0 + jnp.exp(-jnp.abs(x)))


def _pack_bf16_pair(lo, hi):
    lo_bits = lax.bitcast_convert_type(lo.astype(bf16).astype(f32), jnp.uint32)
    hi_bits = lax.bitcast_convert_type(hi.astype(bf16).astype(f32), jnp.uint32)
    return (hi_bits & jnp.uint32(0xFFFF0000)) | (lo_bits >> 16)


def _unpack_bf16_pair(w):
    lo = lax.bitcast_convert_type(w << 16, f32)
    hi = lax.bitcast_convert_type(w & jnp.uint32(0xFFFF0000), f32)
    return lo, hi


def _adaln_kernel(c_ref, w_ref, b_ref, o_ref):
    c = c_ref[...]
    o_ref[...] = jnp.dot(_silu(c), w_ref[...], precision=HIGHEST, preferred_element_type=f32) + b_ref[...]


def _adaln(c_pad, w_ada, b_ada):
    rows, d = c_pad.shape
    n = w_ada.shape[1]
    tn = 1536
    return pl.pallas_call(
        _adaln_kernel,
        out_shape=jax.ShapeDtypeStruct((rows, n), f32),
        grid=(n // tn,),
        in_specs=[pl.BlockSpec((rows, d), lambda j: (0, 0)),
                  pl.BlockSpec((d, tn), lambda j: (0, j)),
                  pl.BlockSpec((1, tn), lambda j: (0, j))],
        out_specs=pl.BlockSpec((rows, tn), lambda j: (0, j)),
        compiler_params=_cparams(1),
        name="adaln",
    )(c_pad, w_ada, b_ada)


def _alibi_slope_log2(head):
    return 2.0 ** (-8.0 * (head + 1) / DA_HEADS) * LOG2E


def _alibi_table(seq):
    pos = np.arange(seq, dtype=np.float32)[:, None]
    slopes = np.asarray([_alibi_slope_log2(h) for h in range(DA_HEADS)], np.float32)[None, :]

    def bf16_part(a):
        return (np.ascontiguousarray(a).view(np.uint32) & np.uint32(0xFFFF0000)).view(np.float32)

    r = pos * slopes
    r_hi = bf16_part(r)
    r_mid = bf16_part(r - r_hi)
    r_lo = bf16_part(r - r_hi - r_mid)
    table = np.zeros((seq, DA_HEADS, LANES), np.float32)
    for lane_off, part in enumerate((r_hi, r_mid, r_lo)):
        table[:, :, DA_DH + lane_off] = part
    return jnp.asarray(table.reshape(seq, DA_HEADS * LANES), dtype=bf16)


def _inproj_kernel(x_ref, mod_ref, w_ref, wg_ref, gbr_ref, qg_ref, kg_ref, bd_ref, kaug_ref, cw_ref, cb_ref,
                   pm_ref, vt_ref, ot_ref, q_ref, k_ref, v_ref, gcol_ref, grow_ref, ext_ref, *, seq):
    tm = TM_INPROJ
    x = x_ref[...]
    h = x * lax.rsqrt(jnp.mean(x * x, axis=-1, keepdims=True) + EPS)
    h = h * (1.0 + mod_ref[0, 1:2, :]) + mod_ref[0, 0:1, :]
    hb = h.astype(bf16)
    cw = ML_W

    @pl.when((pl.program_id(0) * tm) % seq == 0)
    def _():
        ext_ref[0:SUBLANES, :] = jnp.zeros((SUBLANES, 2 * cw), f32)

    cur = _dot(hb, w_ref[:, 0:2 * cw])
    ext_ref[SUBLANES:SUBLANES + tm, :] = cur
    acc = cb_ref[...] + cw_ref[CONV_K - 1:CONV_K, :] * cur
    for j in range(CONV_K - 1):
        off = SUBLANES - (CONV_K - 1) + j
        acc = acc + cw_ref[j:j + 1, :] * ext_ref[off:off + tm, :]
    ext_ref[0:SUBLANES, :] = cur[tm - SUBLANES:tm, :]
    qk = _silu(acc)
    pm_ref[:, 0:cw] = qk[:, 0:cw].astype(bf16)
    pm_ref[:, cw:2 * cw] = (qk[:, cw:2 * cw] * (ML_DH ** -0.5)).astype(bf16)
    vt_ref[...] = _dot(hb, w_ref[:, 2 * cw:3 * cw]).T.astype(bf16)
    ot_ref[...] = _dot(hb, w_ref[:, 3 * cw:4 * cw]).T.astype(bf16)

    lane = lax.broadcasted_iota(jnp.int32, (tm, LANES), 1)
    feat = lane < DA_DH
    q_aug = jnp.where(jnp.logical_and(lane >= DA_DH, lane < DA_DH + 3), 1.0, 0.0)
    q0 = 4 * cw
    k0 = q0 + DA_W

    def qk_normed(col0, gain_ref):
        y = _dot(hb, w_ref[:, col0:col0 + DA_W])
        ss = _dot((y * y).astype(bf16), bd_ref[...])
        return y * lax.rsqrt(ss * (1.0 / DA_DH) + EPS) * gain_ref[...]

    def lane_group(y2, grp, aug):
        pair = y2[:, (grp // 2) * LANES:(grp // 2 + 1) * LANES]
        if grp % 2:
            pair = pltpu.roll(pair, DA_DH, axis=1)
        return jnp.where(feat, pair, aug).astype(bf16)

    nq = qk_normed(q0, qg_ref)
    nk = qk_normed(k0, kg_ref)
    for grp in range(2 * DA_HEADS):
        sl = slice(grp * LANES, (grp + 1) * LANES)
        head = slice((grp // 2) * LANES, (grp // 2 + 1) * LANES)
        q_ref[:, sl] = lane_group(nq, grp, q_aug)
        k_ref[:, sl] = lane_group(nk, grp, kaug_ref[:, head].astype(f32))

    v0 = k0 + DA_W
    v_ref[...] = _dot(hb, w_ref[:, v0:v0 + DA_W]).astype(bf16)
    gates = _dot(hb, wg_ref[...]) + gbr_ref[...]
    gcol_ref[...] = gates
    grow_ref[...] = gates.T[0:N_GATES, :]


def _inproj(x2, mod, w_main, wg, gb_row, qgain, kgain, bd, kaug, conv_w, conv_b, seq):
    t, d = x2.shape
    tm = TM_INPROJ
    tiles_per_seq = seq // tm
    n_main = w_main.shape[1]
    const = lambda shape: pl.BlockSpec(shape, lambda i: (0,) * len(shape))
    return pl.pallas_call(
        functools.partial(_inproj_kernel, seq=seq),
        out_shape=(jax.ShapeDtypeStruct((t, 2 * ML_W), bf16),
                   jax.ShapeDtypeStruct((ML_W, t), bf16),
                   jax.ShapeDtypeStruct((ML_W, t), bf16),
                   jax.ShapeDtypeStruct((t, 2 * DA_W), bf16),
                   jax.ShapeDtypeStruct((t, 2 * DA_W), bf16),
                   jax.ShapeDtypeStruct((t, DA_W), bf16),
                   jax.ShapeDtypeStruct((t, LANES), f32),
                   jax.ShapeDtypeStruct((N_GATES, t), f32)),
        grid=(t // tm,),
        in_specs=[pl.BlockSpec((tm, d), lambda i: (i, 0)),
                  pl.BlockSpec((1, 6, d), lambda i: (i // tiles_per_seq, 0, 0)),
                  const((d, n_main)), const((d, LANES)), const((1, LANES)),
                  const((1, DA_W)), const((1, DA_W)), const((DA_W, DA_W)),
                  pl.BlockSpec((tm, DA_HEADS * LANES), lambda i: (i % tiles_per_seq, 0)),
                  const((CONV_K, 2 * ML_W)), const((1, 2 * ML_W))],
        out_specs=(pl.BlockSpec((tm, 2 * ML_W), lambda i: (i, 0)),
                   pl.BlockSpec((ML_W, tm), lambda i: (0, i)),
                   pl.BlockSpec((ML_W, tm), lambda i: (0, i)),
                   pl.BlockSpec((tm, 2 * DA_W), lambda i: (i, 0)),
                   pl.BlockSpec((tm, 2 * DA_W), lambda i: (i, 0)),
                   pl.BlockSpec((tm, DA_W), lambda i: (i, 0)),
                   pl.BlockSpec((tm, LANES), lambda i: (i, 0)),
                   pl.BlockSpec((N_GATES, tm), lambda i: (0, i))),
        scratch_shapes=[pltpu.VMEM((SUBLANES + tm, 2 * ML_W), f32)],
        compiler_params=_cparams(1),
        name="inproj",
    )(x2, mod, w_main, wg, gb_row, qgain, kgain, bd, kaug, conv_w, conv_b)


def _mlstm_kernel(q_ref, k_ref, vt_ref, ot_ref, gcol_ref, grow_ref, ng_ref, out_ref, state_ref, m_ref):
    L = ML_CHUNK
    dh = ML_DH

    @pl.when(pl.program_id(1) == 0)
    def _():
        state_ref[...] = jnp.zeros_like(state_ref)
        m_ref[...] = jnp.zeros_like(m_ref)

    s_i = lax.broadcasted_iota(jnp.int32, (L, L), 0)
    t_i = lax.broadcasted_iota(jnp.int32, (L, L), 1)
    visible = s_i <= t_i
    tril = (s_i >= t_i).astype(f32)
    triu = visible.astype(f32)
    ones_rows = (lax.broadcasted_iota(jnp.int32, (dh, L), 0) == 0).astype(f32)
    g_b = jnp.broadcast_to(ng_ref[...], (dh, L))

    for cc in range(ML_STEP_CHUNKS):
        rows = slice(cc * L, (cc + 1) * L)
        gcol = gcol_ref[rows, :]
        grow = grow_ref[:, rows]
        bcol_all = jnp.dot(tril, _log_sigmoid(gcol), precision=HIGHEST, preferred_element_type=f32)
        brow_all = jnp.dot(_log_sigmoid(grow), triu, precision=HIGHEST, preferred_element_type=f32)

        for hd in range(ML_HEADS):
            cols = slice(hd * dh, (hd + 1) * dh)
            qb = q_ref[rows, cols]
            kb = k_ref[rows, cols]
            v_aug_t = jnp.concatenate([vt_ref[cols, rows].astype(f32), ones_rows], axis=0)

            brow = brow_all[ML_HEADS + hd:ML_HEADS + hd + 1, :]
            irow = grow[hd:hd + 1, :]
            key_term = gcol[:, hd:hd + 1] - bcol_all[:, ML_HEADS + hd:ML_HEADS + hd + 1]
            m_prev = m_ref[hd:hd + 1, :]

            d_intra = jnp.where(visible, brow + key_term, NEG_BIG)
            d_inter = brow + m_prev
            m_t = jnp.maximum(d_inter, jnp.max(d_intra, axis=0, keepdims=True))
            w_intra = jnp.exp(d_intra - m_t)
            w_inter = jnp.exp(d_inter - m_t)

            s_t = _dot_nt(kb, qb) * w_intra
            state = state_ref[hd]
            q_state = _dot_nt(state.astype(bf16), qb)
            s_v = _dot(v_aug_t.astype(bf16), s_t.astype(bf16))
            num = w_inter * q_state[0:dh, :] + s_v[0:dh, :]
            den = w_inter * q_state[dh:dh + 1, :] + s_v[dh:dh + 1, :]
            hval = num / jnp.maximum(jnp.abs(den), jnp.exp(-m_t))

            b_last = brow[:, L - 1:L]
            d_state = b_last - brow + irow
            m_new = jnp.maximum(b_last + m_prev, jnp.max(d_state, axis=1, keepdims=True))
            carry_scale = jnp.exp(b_last + m_prev - m_new)[:, 0:1]
            wk = jnp.exp(d_state - m_new)
            upd = _dot((v_aug_t * wk).astype(bf16), kb)
            state_ref[hd] = carry_scale * state + upd
            m_ref[hd:hd + 1, :] = m_new

            hn = hval * lax.rsqrt(jnp.mean(hval * hval, axis=0, keepdims=True) + EPS) * g_b
            og = _sigmoid(ot_ref[cols, rows].astype(f32))
            out_ref[rows, cols] = (hn * og).T.astype(bf16)


def _mlstm(pm, vt, ot, gcol, grow, ml_norm_g, batch, seq):
    t = pm.shape[0]
    rows = ML_STEP_CHUNKS * ML_CHUNK
    ns = seq // rows
    row = lambda b, c: b * ns + c
    return pl.pallas_call(
        _mlstm_kernel,
        out_shape=jax.ShapeDtypeStruct((t, ML_W), bf16),
        grid=(batch, ns),
        in_specs=[pl.BlockSpec((rows, ML_W), lambda b, c: (row(b, c), 0)),
                  pl.BlockSpec((rows, ML_W), lambda b, c: (row(b, c), 1)),
                  pl.BlockSpec((ML_W, rows), lambda b, c: (0, row(b, c))),
                  pl.BlockSpec((ML_W, rows), lambda b, c: (0, row(b, c))),
                  pl.BlockSpec((rows, LANES), lambda b, c: (row(b, c), 0)),
                  pl.BlockSpec((N_GATES, rows), lambda b, c: (0, row(b, c))),
                  pl.BlockSpec((ML_DH, 1), lambda b, c: (0, 0))],
        out_specs=pl.BlockSpec((rows, ML_W), lambda b, c: (row(b, c), 0)),
        scratch_shapes=[pltpu.VMEM((ML_HEADS, 2 * ML_DH, ML_DH), f32),
                        pltpu.VMEM((SUBLANES, LANES), f32)],
        compiler_params=_cparams(2),
        name="mlstm",
    )(pm, pm, vt, ot, gcol, grow, ml_norm_g)


def _attn_block(q_ref, k_ref, v_ref, m_ref, l_ref, acc_ref, row0, nrows, nkeys, key0=None):
    rows = slice(row0, row0 + nrows)
    nch = nkeys // LANES
    v = v_ref[0:nkeys, :]
    if key0 is not None:
        keep = (key0 + lax.broadcasted_iota(jnp.int32, (nrows, nkeys), 1)
                <= row0 + lax.broadcasted_iota(jnp.int32, (nrows, nkeys), 0))
    for c in range(2):
        sl = slice(c * LANES, (c + 1) * LANES)
        s = _dot_nt(q_ref[rows, sl], k_ref[0:nkeys, sl])
        if key0 is not None:
            s = jnp.where(keep, s, NEG_BIG)
        chunks = [s[:, j * LANES:(j + 1) * LANES] for j in range(nch)]
        mc = chunks[0]
        for ch in chunks[1:]:
            mc = jnp.maximum(mc, ch)
        m_old = m_ref[c, rows, :]
        m_new = jnp.maximum(m_old, jnp.max(mc, axis=1, keepdims=True))
        alpha = jnp.exp2(m_old - m_new)
        ps = [jnp.exp2(ch - m_new) for ch in chunks]
        lsum = ps[0]
        for pj in ps[1:]:
            lsum = lsum + pj
        p = jnp.concatenate([pj.astype(bf16) for pj in ps], axis=1)
        l_ref[c, rows, :] = alpha * l_ref[c, rows, :] + lsum
        acc_ref[c, rows, :] = alpha * acc_ref[c, rows, :] + _dot(p, v)
        m_ref[c, rows, :] = m_new


def _attn_kernel(qt_ref, kt_ref, q_ref, k_ref, v_ref, lam_ref, g_ref, o_ref,
                 m_ref, l_ref, acc_ref, *, lam_init):
    step = pl.program_id(2)
    qi = qt_ref[step]
    ki = kt_ref[step]

    @pl.when(ki == 0)
    def _():
        m_ref[...] = jnp.full_like(m_ref, NEG_BIG)
        l_ref[...] = jnp.zeros_like(l_ref)
        acc_ref[...] = jnp.zeros_like(acc_ref)

    blk = ATT_BLOCK
    half = blk // 2
    qrows = ATT_QBLOCKS * blk
    first_diag = ATT_QBLOCKS * qi
    args = (q_ref, k_ref, v_ref, m_ref, l_ref, acc_ref)

    @pl.when(ki < first_diag)
    def _():
        _attn_block(*args, 0, qrows, blk)

    for j in range(ATT_QBLOCKS):
        @pl.when(ki == first_diag + j)
        def _(j=j):
            r0 = j * blk
            _attn_block(*args, r0, half, half, key0=r0)
            _attn_block(*args, r0 + half, half, blk, key0=r0)
            if r0 + blk < qrows:
                _attn_block(*args, r0 + blk, qrows - r0 - blk, blk)

    @pl.when(ki == first_diag + ATT_QBLOCKS - 1)
    def _():
        lam = lam_ref[...]
        l0 = jnp.sum(l_ref[0], axis=1, keepdims=True)
        l1 = jnp.sum(l_ref[1], axis=1, keepdims=True)
        o = acc_ref[0] / l0 - lam * (acc_ref[1] / l1)
        o = o * lax.rsqrt(jnp.mean(o * o, axis=-1, keepdims=True) + EPS)
        o_ref[...] = (o * g_ref[...] * (1.0 - lam_init)).astype(bf16)


def _attention(qh, kh, vh, lam, da_norm_g, batch, seq, lam_init):
    t = qh.shape[0]
    blk = ATT_BLOCK
    qrows = ATT_QBLOCKS * blk
    assert seq % qrows == 0, (seq, qrows)
    nk = seq // blk
    nq = seq // qrows
    pairs = [(i, j) for i in range(nq) for j in range(ATT_QBLOCKS * (i + 1))]
    qt = jnp.asarray([p[0] for p in pairs], jnp.int32)
    kt = jnp.asarray([p[1] for p in pairs], jnp.int32)
    w = 2 * DA_DH
    grid_spec = pltpu.PrefetchScalarGridSpec(
        num_scalar_prefetch=2,
        grid=(batch, DA_HEADS, len(pairs)),
        in_specs=[pl.BlockSpec((qrows, 2 * LANES), lambda b, h, s, qt, kt: (b * nq + qt[s], h)),
                  pl.BlockSpec((blk, 2 * LANES), lambda b, h, s, qt, kt: (b * nk + kt[s], h)),
                  pl.BlockSpec((blk, w), lambda b, h, s, qt, kt: (b * nk + kt[s], h)),
                  pl.BlockSpec((1, 1), lambda b, h, s, qt, kt: (0, 0)),
                  pl.BlockSpec((1, w), lambda b, h, s, qt, kt: (0, 0))],
        out_specs=pl.BlockSpec((qrows, w), lambda b, h, s, qt, kt: (b * nq + qt[s], h)),
        scratch_shapes=[pltpu.VMEM((2, qrows, LANES), f32),
                        pltpu.VMEM((2, qrows, LANES), f32),
                        pltpu.VMEM((2, qrows, w), f32)],
    )
    return pl.pallas_call(
        functools.partial(_attn_kernel, lam_init=lam_init),
        out_shape=jax.ShapeDtypeStruct((t, DA_W), bf16),
        grid_spec=grid_spec,
        compiler_params=_cparams(3),
        name="attn",
    )(qt, kt, qh, kh, vh, lam, da_norm_g)


def _first_index_of_max(x, iota_f, size):
    m = jnp.max(x, axis=0, keepdims=True)
    idx = jnp.min(jnp.where(x == m, iota_f, float(size)), axis=0, keepdims=True)
    return m, idx


def _route_kernel(hm_ref, hd_ref, x_ref, mod_ref, wo_ref, wr_ref, rb_ref,
                  x1_ref, ha_ref, hb_ref, idx_ref, gate_ref, rank_ref, cnt_ref, base_ref):
    tm = TM_ROUTE
    i = pl.program_id(0)

    @pl.when(i == 0)
    def _():
        base_ref[...] = jnp.zeros_like(base_ref)

    mix = _dot(hm_ref[...], wo_ref[0:ML_W, :]) + _dot(hd_ref[...], wo_ref[ML_W:ML_W + DA_W, :])
    x1 = x_ref[...] + mod_ref[0, 2:3, :] * mix
    x1_ref[...] = x1
    h2 = x1 * lax.rsqrt(jnp.mean(x1 * x1, axis=-1, keepdims=True) + EPS)
    h2 = h2 * (1.0 + mod_ref[0, 4:5, :]) + mod_ref[0, 3:4, :]
    packed = lax.bitcast_convert_type(_pack_bf16_pair(h2[:, 0:HALF], h2[:, HALF:D_MODEL]), jnp.int32)
    ha_ref[...] = packed[:, 0:QUARTER]
    hb_ref[...] = packed[:, QUARTER:HALF]

    sc = _sigmoid(_dot_nt(wr_ref[...], h2.astype(bf16)))
    sel = sc + rb_ref[...]

    gi = lax.broadcasted_iota(jnp.int32, (GROUP_SIZE, tm), 0).astype(f32)
    gscores = []
    for g in range(N_GROUPS):
        blk = sel[g * GROUP_SIZE:(g + 1) * GROUP_SIZE, :]
        m1, i1 = _first_index_of_max(blk, gi, GROUP_SIZE)
        m2 = jnp.max(jnp.where(gi == i1, NEG_BIG, blk), axis=0, keepdims=True)
        gscores.append(m1 + m2)
    gs = jnp.concatenate(gscores, axis=0)

    ngi = lax.broadcasted_iota(jnp.int32, (N_GROUPS, tm), 0).astype(f32)
    gkeep = jnp.zeros((N_GROUPS, tm), f32)
    for _ in range(TOPK_GROUPS):
        _, gidx = _first_index_of_max(gs, ngi, N_GROUPS)
        hit = ngi == gidx
        gkeep = jnp.where(hit, 1.0, gkeep)
        gs = jnp.where(hit, NEG_BIG, gs)
    masked = jnp.concatenate(
        [jnp.where(gkeep[g:g + 1, :] > 0.0, sel[g * GROUP_SIZE:(g + 1) * GROUP_SIZE, :], NEG_BIG)
         for g in range(N_GROUPS)], axis=0)

    ei = lax.broadcasted_iota(jnp.int32, (N_EXPERTS, tm), 0).astype(f32)
    idxs, gates = [], []
    chosen = jnp.zeros((N_EXPERTS, tm), f32)
    for _ in range(TOP_K):
        _, eidx = _first_index_of_max(masked, ei, N_EXPERTS)
        hit = ei == eidx
        idxs.append(eidx)
        gates.append(jnp.sum(jnp.where(hit, sc, 0.0), axis=0, keepdims=True))
        chosen = jnp.where(hit, 1.0, chosen)
        masked = jnp.where(hit, NEG_BIG, masked)
    gate = jnp.concatenate(gates, axis=0)
    gate = gate / jnp.sum(gate, axis=0, keepdims=True) * ROUTED_SCALE
    gate_ref[...] = gate
    idx_ref[...] = jnp.concatenate(idxs, axis=0).astype(jnp.int32)

    tr = lax.broadcasted_iota(jnp.int32, (tm, tm), 0)
    tc = lax.broadcasted_iota(jnp.int32, (tm, tm), 1)
    before = (tr < tc).astype(bf16)
    seen = base_ref[...] + _dot(chosen.astype(bf16), before)
    ranks = [jnp.sum(jnp.where(ei == eidx, seen, 0.0), axis=0, keepdims=True) for eidx in idxs]
    rank_ref[...] = jnp.concatenate(ranks, axis=0).astype(jnp.int32)
    total = base_ref[...] + jnp.sum(chosen, axis=1, keepdims=True)
    base_ref[...] = total
    cnt_ref[...] = total


def _route(hm, hd, x2, mod, w_out, w_router_t, rbias_col, seq):
    t, d = x2.shape
    tm = TM_ROUTE
    tiles_per_seq = seq // tm
    const = lambda shape: pl.BlockSpec(shape, lambda i: (0,) * len(shape))
    return pl.pallas_call(
        _route_kernel,
        out_shape=(jax.ShapeDtypeStruct((t, d), f32),
                   jax.ShapeDtypeStruct((t, QUARTER), jnp.int32),
                   jax.ShapeDtypeStruct((t, QUARTER), jnp.int32),
                   jax.ShapeDtypeStruct((TOP_K, t), jnp.int32),
                   jax.ShapeDtypeStruct((TOP_K, t), f32),
                   jax.ShapeDtypeStruct((TOP_K, t), jnp.int32),
                   jax.ShapeDtypeStruct((N_EXPERTS, 1), f32)),
        grid=(t // tm,),
        in_specs=[pl.BlockSpec((tm, ML_W), lambda i: (i, 0)),
                  pl.BlockSpec((tm, DA_W), lambda i: (i, 0)),
                  pl.BlockSpec((tm, d), lambda i: (i, 0)),
                  pl.BlockSpec((1, 6, d), lambda i: (i // tiles_per_seq, 0, 0)),
                  const((d, d)), const((N_EXPERTS, d)), const((N_EXPERTS, 1))],
        out_specs=(pl.BlockSpec((tm, d), lambda i: (i, 0)),
                   pl.BlockSpec((tm, QUARTER), lambda i: (i, 0)),
                   pl.BlockSpec((tm, QUARTER), lambda i: (i, 0)),
                   pl.BlockSpec((TOP_K, tm), lambda i: (0, i)),
                   pl.BlockSpec((TOP_K, tm), lambda i: (0, i)),
                   pl.BlockSpec((TOP_K, tm), lambda i: (0, i)),
                   const((N_EXPERTS, 1))),
        scratch_shapes=[pltpu.VMEM((N_EXPERTS, 1), f32)],
        compiler_params=_cparams(1),
        name="route",
    )(hm, hd, x2, mod, w_out, w_router_t, rbias_col)


SLOT_RADIX = 256
SLOT_DIGITS = 3


def _slots_kernel(idx_ref, rank_ref, dig_ref, dest_ref):
    tm = TM_SLOTS
    ei = lax.broadcasted_iota(jnp.int32, (N_EXPERTS, tm), 0)
    rows = []
    for k in range(TOP_K):
        onehot = jnp.where(ei == idx_ref[k:k + 1, :], 1.0, 0.0).astype(bf16)
        dg = _dot(dig_ref[...], onehot)
        start = dg[0:1, :]
        for j in range(1, SLOT_DIGITS):
            start = start + dg[j:j + 1, :] * float(SLOT_RADIX ** j)
        rows.append(start.astype(jnp.int32) + rank_ref[k:k + 1, :])
    dest_ref[...] = jnp.concatenate(rows, axis=0)


def _slots(idx, rank, digits):
    t = idx.shape[1]
    tm = TM_SLOTS
    return pl.pallas_call(
        _slots_kernel,
        out_shape=jax.ShapeDtypeStruct((TOP_K, t), jnp.int32),
        grid=(t // tm,),
        in_specs=[pl.BlockSpec((TOP_K, tm), lambda i: (0, i)),
                  pl.BlockSpec((TOP_K, tm), lambda i: (0, i)),
                  pl.BlockSpec((SUBLANES, N_EXPERTS), lambda i: (0, 0))],
        out_specs=pl.BlockSpec((TOP_K, tm), lambda i: (0, i)),
        compiler_params=_cparams(1),
        name="slots",
    )(idx, rank, digits)


def _sc_scatter_table(rows, dest, n_slots):
    n, width = rows.shape
    mesh = plsc.VectorSubcoreMesh(core_axis_name="core", subcore_axis_name="subcore")

    @pl.kernel(out_type=jax.ShapeDtypeStruct((n_slots, width), rows.dtype), mesh=mesh, scratch_types=[],
               name="sc_scatter")
    def scatter_kernel(rows_hbm, dest_hbm, out_hbm):
        def body(rows_vmem, dest_vmem):
            for k in range(TOP_K):
                pltpu.sync_copy(rows_vmem, out_hbm.at[dest_vmem.at[k]])

        pltpu.emit_pipeline(
            body,
            grid=(n // SC_WINDOW,),
            in_specs=[pl.BlockSpec((SC_WINDOW, width), lambda i: (i, 0)),
                      pl.BlockSpec((TOP_K, SC_WINDOW), lambda i: (0, i))],
            out_specs=[],
            core_axis_name=("core", "subcore"),
            dimension_semantics=(pltpu.PARALLEL,),
        )(rows_hbm, dest_hbm)

    return scatter_kernel(rows, dest)


EXPERT_GROUP = 4
EXPERT_RING = 16
EXPERT_TAIL_MAX = 2 * EXPERT_GROUP - 1
EXPERT_AHEAD = EXPERT_RING - EXPERT_TAIL_MAX


def _expert_kernel(start_ref, cnt_ref, total_ref, xsa_ref, xsb_ref, w1_ref, w3_ref, w2_ref,
                   ysa_ref, ysb_ref, xbuf, ybuf, w1b, w3b, w2b, in_sem, out_sem):
    bm = MOE_BLOCK
    ring = EXPERT_RING
    ahead = EXPERT_AHEAD
    shift = int(math.log2(bm))
    e = pl.program_id(0)
    total = total_ref[0]
    cnt = cnt_ref[e]
    nb = lax.shift_right_logical(cnt + (bm - 1), shift)
    g0 = lax.shift_right_logical(start_ref[e], shift)

    def slot_of(g):
        return jnp.bitwise_and(g, ring - 1)

    def in_copies(g):
        s = slot_of(g)
        rows = pl.ds(pl.multiple_of(g * bm, bm), bm)
        return [pltpu.make_async_copy(x_ref.at[rows], xbuf.at[s, half], in_sem.at[s])
                for half, x_ref in enumerate((xsa_ref, xsb_ref))]

    def out_copies(g):
        s = slot_of(g)
        rows = pl.ds(pl.multiple_of(g * bm, bm), bm)
        return [pltpu.make_async_copy(ybuf.at[s, half], y_ref.at[rows], out_sem.at[s])
                for half, y_ref in enumerate((ysa_ref, ysb_ref))]

    def request(g):
        @pl.when(g < total)
        def _():
            for cp in in_copies(g):
                cp.start()

    def acquire(g):
        for cp in in_copies(g):
            cp.wait()

        @pl.when(g >= ring)
        def _():
            for cp in out_copies(g - ring):
                cp.wait()

    def compute(g, n):
        slots = [slot_of(g + d) for d in range(n)]
        row = (g - g0) * bm + lax.broadcasted_iota(jnp.int32, (n * bm, HALF), 0)
        words = jnp.concatenate(
            [jnp.concatenate([xbuf[s, 0], xbuf[s, 1]], axis=1) for s in slots], axis=0)
        lo, hi = _unpack_bf16_pair(lax.bitcast_convert_type(jnp.where(row < cnt, words, 0), jnp.uint32))
        lo = lo.astype(bf16)
        hi = hi.astype(bf16)
        h1 = _dot(lo, w1b[0:HALF, :]) + _dot(hi, w1b[HALF:D_MODEL, :])
        h3 = _dot(lo, w3b[0:HALF, :]) + _dot(hi, w3b[HALF:D_MODEL, :])
        y = _dot((_silu(h1) * h3).astype(bf16), w2b[...])
        packed = lax.bitcast_convert_type(_pack_bf16_pair(y[:, 0:HALF], y[:, HALF:D_MODEL]), jnp.int32)
        for d, s in enumerate(slots):
            ybuf[s, 0] = packed[d * bm:(d + 1) * bm, 0:QUARTER]
            ybuf[s, 1] = packed[d * bm:(d + 1) * bm, QUARTER:HALF]

    def release(g):
        for cp in out_copies(g):
            cp.start()

    @pl.when(e == 0)
    def _():
        for g in range(ahead):
            request(g)

    @pl.when(nb > 0)
    def _():
        w1b[...] = w1_ref[0].astype(bf16)
        w3b[...] = w3_ref[0].astype(bf16)
        w2b[...] = w2_ref[0].astype(bf16)

    def run(g, n):
        for d in range(n):
            request(g + d + ahead)
        for d in range(n):
            acquire(g + d)
        compute(g, n)
        for d in range(n):
            release(g + d)

    def group(i, carry):
        run(g0 + EXPERT_GROUP * i, EXPERT_GROUP)
        return carry

    full = jnp.maximum(lax.shift_right_logical(nb, int(math.log2(EXPERT_GROUP))) - 1, 0)
    lax.fori_loop(0, full, group, 0)
    tail = nb - EXPERT_GROUP * full
    for n in range(1, EXPERT_TAIL_MAX + 1):
        @pl.when(tail == n)
        def _(n=n):
            run(g0 + EXPERT_GROUP * full, n)

    @pl.when(e == pl.num_programs(0) - 1)
    def _():
        for back in range(1, ring + 1):
            @pl.when(total - back >= 0)
            def _():
                for cp in out_copies(total - back):
                    cp.wait()


def _experts(start, counts, total, xs_a, xs_b, w1, w3, w2):
    n_pad = xs_a.shape[0]
    bm = MOE_BLOCK
    grid_spec = pltpu.PrefetchScalarGridSpec(
        num_scalar_prefetch=3,
        grid=(N_EXPERTS,),
        in_specs=[pl.BlockSpec(memory_space=pl.ANY),
                  pl.BlockSpec(memory_space=pl.ANY),
                  pl.BlockSpec((1, D_MODEL, D_EXPERT), lambda e, st, ct, tt: (e, 0, 0)),
                  pl.BlockSpec((1, D_MODEL, D_EXPERT), lambda e, st, ct, tt: (e, 0, 0)),
                  pl.BlockSpec((1, D_EXPERT, D_MODEL), lambda e, st, ct, tt: (e, 0, 0))],
        out_specs=(pl.BlockSpec(memory_space=pl.ANY), pl.BlockSpec(memory_space=pl.ANY)),
        scratch_shapes=[pltpu.VMEM((EXPERT_RING, 2, bm, QUARTER), jnp.int32),
                        pltpu.VMEM((EXPERT_RING, 2, bm, QUARTER), jnp.int32),
                        pltpu.VMEM((D_MODEL, D_EXPERT), bf16),
                        pltpu.VMEM((D_MODEL, D_EXPERT), bf16),
                        pltpu.VMEM((D_EXPERT, D_MODEL), bf16),
                        pltpu.SemaphoreType.DMA((EXPERT_RING,)),
                        pltpu.SemaphoreType.DMA((EXPERT_RING,))],
    )
    return pl.pallas_call(
        _expert_kernel,
        out_shape=(jax.ShapeDtypeStruct((n_pad, QUARTER), jnp.int32),
                   jax.ShapeDtypeStruct((n_pad, QUARTER), jnp.int32)),
        grid_spec=grid_spec,
        compiler_params=_cparams(1),
        name="experts",
    )(start, counts, total, xs_a, xs_b, w1, w3, w2)


def _sc_gather_table(table, idx2):
    n = idx2.shape[1]
    width = table.shape[1]
    mesh = plsc.VectorSubcoreMesh(core_axis_name="core", subcore_axis_name="subcore")

    @pl.kernel(out_type=jax.ShapeDtypeStruct((n, width), table.dtype), mesh=mesh, scratch_types=[],
               name="sc_gather")
    def gather_kernel(tab_hbm, idx_hbm, out_hbm):
        def body(idx_vmem, out_vmem):
            pltpu.sync_copy(tab_hbm.at[idx_vmem.at[0]], out_vmem)

        pltpu.emit_pipeline(
            body,
            grid=(n // SC_WINDOW,),
            in_specs=[pl.BlockSpec((1, SC_WINDOW), lambda i: (0, i))],
            out_specs=[pl.BlockSpec((SC_WINDOW, width), lambda i: (i, 0))],
            core_axis_name=("core", "subcore"),
            dimension_semantics=(pltpu.PARALLEL,),
        )(idx_hbm, out_hbm)

    return gather_kernel(table, idx2)


def _sc_gather_rows(table_a, table_b, idx):
    idx2 = idx.reshape(1, -1)
    return _sc_gather_table(table_a, idx2), _sc_gather_table(table_b, idx2)


def _combine_kernel(gate_ref, ha_ref, hb_ref, x1_ref, mod_ref, ws1_ref, ws3_ref, ws2_ref, ga_ref, gb_ref,
                    out_ref):
    words = jnp.concatenate([ha_ref[...], hb_ref[...]], axis=1)
    lo, hi = _unpack_bf16_pair(lax.bitcast_convert_type(words, jnp.uint32))
    lo = lo.astype(bf16)
    hi = hi.astype(bf16)
    s1 = _dot(lo, ws1_ref[0:HALF, :]) + _dot(hi, ws1_ref[HALF:D_MODEL, :])
    s3 = _dot(lo, ws3_ref[0:HALF, :]) + _dot(hi, ws3_ref[HALF:D_MODEL, :])
    y = _dot((_silu(s1) * s3).astype(bf16), ws2_ref[...])

    gate = gate_ref[...].T
    gate_f = mod_ref[0, 5:6, :]
    for part, g_ref in enumerate((ga_ref, gb_ref)):
        c_lo = slice(part * QUARTER, (part + 1) * QUARTER)
        c_hi = slice(HALF + part * QUARTER, HALF + (part + 1) * QUARTER)
        acc_lo = y[:, c_lo]
        acc_hi = y[:, c_hi]
        for k in range(TOP_K):
            rlo, rhi = _unpack_bf16_pair(lax.bitcast_convert_type(g_ref[k], jnp.uint32))
            gk = gate[:, k:k + 1]
            acc_lo = acc_lo + gk * rlo
            acc_hi = acc_hi + gk * rhi
        out_ref[:, c_lo] = x1_ref[:, c_lo] + gate_f[:, c_lo] * acc_lo
        out_ref[:, c_hi] = x1_ref[:, c_hi] + gate_f[:, c_hi] * acc_hi


def _combine(gate, h2a, h2b, x1, mod, ws1, ws3, ws2, ga, gb, seq):
    t, d = x1.shape
    tm = TM_MOE
    tiles_per_seq = seq // tm
    const = lambda shape: pl.BlockSpec(shape, lambda i: (0,) * len(shape))
    return pl.pallas_call(
        _combine_kernel,
        out_shape=jax.ShapeDtypeStruct((t, d), f32),
        grid=(t // tm,),
        in_specs=[pl.BlockSpec((TOP_K, tm), lambda i: (0, i)),
                  pl.BlockSpec((tm, QUARTER), lambda i: (i, 0)),
                  pl.BlockSpec((tm, QUARTER), lambda i: (i, 0)),
                  pl.BlockSpec((tm, d), lambda i: (i, 0)),
                  pl.BlockSpec((1, 6, d), lambda i: (i // tiles_per_seq, 0, 0)),
                  const((d, D_EXPERT)), const((d, D_EXPERT)), const((D_EXPERT, d)),
                  pl.BlockSpec((TOP_K, tm, QUARTER), lambda i: (0, i, 0)),
                  pl.BlockSpec((TOP_K, tm, QUARTER), lambda i: (0, i, 0))],
        out_specs=pl.BlockSpec((tm, d), lambda i: (i, 0)),
        compiler_params=_cparams(1),
        name="combine",
    )(gate, h2a, h2b, x1, mod, ws1, ws3, ws2, ga, gb)


def _lambda_init(layer):
    return 0.8 - 0.6 * math.exp(-0.3 * layer)


def _layer(x, c, w_ada, b_ada, w_in, conv_w, conv_b, gate_b, ml_norm_g, da_q_norm_g, da_k_norm_g,
           lambda_q1, lambda_k1, lambda_q2, lambda_k2, da_norm_g, w_out, w_router, router_bias,
           w1, w3, w2, ws1, ws3, ws2, layer):
    batch, seq, d = x.shape
    t = batch * seq
    lam_init = _lambda_init(layer)
    x2 = x.reshape(t, d)

    c_pad = jnp.pad(c, ((0, -batch % SUBLANES), (0, 0)))
    mod = _adaln(c_pad, w_ada, b_ada.reshape(1, -1))[:batch].reshape(batch, 6, d)

    g0 = 4 * ML_W
    q0 = g0 + N_GATES
    n_grp = 2 * DA_HEADS
    w_main = jnp.concatenate([w_in[:, :g0], w_in[:, q0:]], axis=1).astype(bf16)
    wg = jnp.pad(w_in[:, g0:g0 + N_GATES], ((0, 0), (0, LANES - N_GATES))).astype(bf16)
    gb_row = jnp.pad(gate_b, (0, LANES - N_GATES)).reshape(1, LANES)
    qgain = (jnp.tile(da_q_norm_g, n_grp) * (DA_DH ** -0.5 * LOG2E)).reshape(1, DA_W)
    kgain = jnp.tile(da_k_norm_g, n_grp).reshape(1, DA_W)
    seg = jnp.arange(DA_W) // DA_DH
    bd = (seg[:, None] == seg[None, :]).astype(bf16)

    pm, vt, ot, qh, kh, vh, gcol, grow = _inproj(x2, mod, w_main, wg, gb_row, qgain, kgain, bd, _alibi_table(seq),
                                         conv_w, conv_b.reshape(1, -1), seq)

    hm = _mlstm(pm, vt, ot, gcol, grow, ml_norm_g.reshape(-1, 1), batch, seq)

    lam = (jnp.exp(jnp.sum(lambda_q1 * lambda_k1)) - jnp.exp(jnp.sum(lambda_q2 * lambda_k2))
           + lam_init).reshape(1, 1).astype(f32)
    hd = _attention(qh, kh, vh, lam, da_norm_g.reshape(1, -1), batch, seq, lam_init)

    x1, h2a, h2b, idx, gate, rank, counts = _route(hm, hd, x2, mod, w_out.astype(bf16),
                                                   w_router.T.astype(bf16), router_bias.reshape(-1, 1), seq)

    bm = MOE_BLOCK
    n_blocks = (t * TOP_K) // bm + N_EXPERTS
    counts_i = counts.reshape(-1).astype(jnp.int32)
    padded = (counts_i + bm - 1) // bm * bm
    pad_end = jnp.cumsum(padded)
    pad_start = pad_end - padded
    digits = jnp.stack([(pad_start // SLOT_RADIX ** j) % SLOT_RADIX for j in range(SLOT_DIGITS)])
    digits = jnp.pad(digits, ((0, SUBLANES - SLOT_DIGITS), (0, 0))).astype(bf16)
    dest = _slots(idx, rank, digits)

    xs_a = _sc_scatter_table(h2a, dest, n_blocks * bm)
    xs_b = _sc_scatter_table(h2b, dest, n_blocks * bm)
    total_blocks = (pad_end[-1:] // bm).astype(jnp.int32)
    ys_a, ys_b = _experts(pad_start.astype(jnp.int32), counts_i, total_blocks, xs_a, xs_b, w1, w3, w2)
    ga, gb = _sc_gather_rows(ys_a, ys_b, dest.reshape(-1))
    out = _combine(gate, h2a, h2b, x1, mod, ws1.astype(bf16), ws3.astype(bf16), ws2.astype(bf16),
                   ga.reshape(TOP_K, t, QUARTER), gb.reshape(TOP_K, t, QUARTER), seq)
    return out.reshape(batch, seq, d)


def kernel(x, c, w_ada, b_ada, w_in, conv_w, conv_b, gate_b, ml_norm_g, da_q_norm_g, da_k_norm_g,
           lambda_q1, lambda_k1, lambda_q2, lambda_k2, da_norm_g, w_out, w_router, router_bias,
           w1, w3, w2, ws1, ws3, ws2):
    depth = w_ada.shape[0]
    for l in range(depth):
        x = _layer(x, c, w_ada[l], b_ada[l], w_in[l], conv_w[l], conv_b[l], gate_b[l], ml_norm_g[l],
                   da_q_norm_g[l], da_k_norm_g[l], lambda_q1[l], lambda_k1[l], lambda_q2[l], lambda_k2[l],
                   da_norm_g[l], w_out[l], w_router[l], router_bias[l], w1[l], w3[l], w2[l],
                   ws1[l], ws3[l], ws2[l], l)
    return x
```

```python
import functools
import math

import jax
import jax.numpy as jnp
import numpy as np
from jax import lax
from jax.experimental import pallas as pl
from jax.experimental.pallas import tpu as pltpu
from jax.experimental.pallas import tpu_sc as plsc

D_MODEL = 1024
ML_HEADS = 4
ML_DH = 128
ML_W = ML_HEADS * ML_DH
ML_CHUNK = 128
CONV_K = 4
DA_HEADS = 4
DA_DH = 64
DA_W = DA_HEADS * 2 * DA_DH
N_EXPERTS = 256
TOP_K = 8
N_GROUPS = 8
GROUP_SIZE = N_EXPERTS // N_GROUPS
TOPK_GROUPS = 4
D_EXPERT = 256
ROUTED_SCALE = 2.5
EPS = 1e-6
N_GATES = 2 * ML_HEADS

LANES = 128
SUBLANES = 8
VMEM_LIMIT_BYTES = 48 * 1024 * 1024

TM_INPROJ = 512
ML_STEP_CHUNKS = 4
ATT_BLOCK = 1024
ATT_QBLOCKS = 2
TM_ROUTE = 512
TM_SLOTS = 512
TM_MOE = 512
MOE_BLOCK = 128
HALF = D_MODEL // 2
QUARTER = HALF // 2
SC_WINDOW = 128

NEG_BIG = -1e30
LOG2E = 1.4426950408889634

f32 = jnp.float32
bf16 = jnp.bfloat16
HIGHEST = lax.Precision.HIGHEST


def _cparams(n_axes):
    return pltpu.CompilerParams(dimension_semantics=("arbitrary",) * n_axes,
                                vmem_limit_bytes=VMEM_LIMIT_BYTES)


def _dot(a, b):
    return jnp.dot(a, b, preferred_element_type=f32)


def _dot_nt(a, b):
    return lax.dot_general(a, b, (((1,), (1,)), ((), ())), preferred_element_type=f32)


def _sigmoid(x):
    return 1.0 / (1.0 + jnp.exp(-x))


def _silu(x):
    return x * _sigmoid(x)


def _log_sigmoid(x):
    return jnp.minimum(x, 0.0) - jnp.log(1.0 + jnp.exp(-jnp.abs(x)))


def _pack_bf16_pair(lo, hi):
    lo_bits = lax.bitcast_convert_type(lo.astype(bf16).astype(f32), jnp.uint32)
    hi_bits = lax.bitcast_convert_type(hi.astype(bf16).astype(f32), jnp.uint32)
    return (hi_bits & jnp.uint32(0xFFFF0000)) | (lo_bits >> 16)


def _unpack_bf16_pair(w):
    lo = lax.bitcast_convert_type(w << 16, f32)
    hi = lax.bitcast_convert_type(w & jnp.uint32(0xFFFF0000), f32)
    return lo, hi


def _adaln_kernel(c_ref, w_ref, b_ref, o_ref):
    c = c_ref[...]
    o_ref[...] = jnp.dot(_silu(c), w_ref[...], precision=HIGHEST, preferred_element_type=f32) + b_ref[...]


def _adaln(c_pad, w_ada, b_ada):
    rows, d = c_pad.shape
    n = w_ada.shape[1]
    tn = 1536
    return pl.pallas_call(
        _adaln_kernel,
        out_shape=jax.ShapeDtypeStruct((rows, n), f32),
        grid=(n // tn,),
        in_specs=[pl.BlockSpec((rows, d), lambda j: (0, 0)),
                  pl.BlockSpec((d, tn), lambda j: (0, j)),
                  pl.BlockSpec((1, tn), lambda j: (0, j))],
        out_specs=pl.BlockSpec((rows, tn), lambda j: (0, j)),
        compiler_params=_cparams(1),
        name="adaln",
    )(c_pad, w_ada, b_ada)


def _alibi_slope_log2(head):
    return 2.0 ** (-8.0 * (head + 1) / DA_HEADS) * LOG2E


def _alibi_table(seq):
    pos = np.arange(seq, dtype=np.float32)[:, None]
    slopes = np.asarray([_alibi_slope_log2(h) for h in range(DA_HEADS)], np.float32)[None, :]

    def bf16_part(a):
        return (np.ascontiguousarray(a).view(np.uint32) & np.uint32(0xFFFF0000)).view(np.float32)

    r = pos * slopes
    r_hi = bf16_part(r)
    r_mid = bf16_part(r - r_hi)
    r_lo = bf16_part(r - r_hi - r_mid)
    table = np.zeros((seq, DA_HEADS, LANES), np.float32)
    for lane_off, part in enumerate((r_hi, r_mid, r_lo)):
        table[:, :, DA_DH + lane_off] = part
    return jnp.asarray(table.reshape(seq, DA_HEADS * LANES), dtype=bf16)


def _inproj_kernel(x_ref, mod_ref, w_ref, wg_ref, gbr_ref, qg_ref, kg_ref, bd_ref, kaug_ref, cw_ref, cb_ref,
                   pm_ref, vt_ref, ot_ref, q_ref, k_ref, v_ref, gcol_ref, grow_ref, ext_ref, *, seq):
    tm = TM_INPROJ
    x = x_ref[...]
    h = x * lax.rsqrt(jnp.mean(x * x, axis=-1, keepdims=True) + EPS)
    h = h * (1.0 + mod_ref[0, 1:2, :]) + mod_ref[0, 0:1, :]
    hb = h.astype(bf16)
    cw = ML_W

    @pl.when((pl.program_id(0) * tm) % seq == 0)
    def _():
        ext_ref[0:SUBLANES, :] = jnp.zeros((SUBLANES, 2 * cw), f32)

    cur = _dot(hb, w_ref[:, 0:2 * cw])
    ext_ref[SUBLANES:SUBLANES + tm, :] = cur
    acc = cb_ref[...] + cw_ref[CONV_K - 1:CONV_K, :] * cur
    for j in range(CONV_K - 1):
        off = SUBLANES - (CONV_K - 1) + j
        acc = acc + cw_ref[j:j + 1, :] * ext_ref[off:off + tm, :]
    ext_ref[0:SUBLANES, :] = cur[tm - SUBLANES:tm, :]
    qk = _silu(acc)
    pm_ref[:, 0:cw] = qk[:, 0:cw].astype(bf16)
    pm_ref[:, cw:2 * cw] = (qk[:, cw:2 * cw] * (ML_DH ** -0.5)).astype(bf16)
    vt_ref[...] = _dot(hb, w_ref[:, 2 * cw:3 * cw]).T.astype(bf16)
    ot_ref[...] = _dot(hb, w_ref[:, 3 * cw:4 * cw]).T.astype(bf16)

    lane = lax.broadcasted_iota(jnp.int32, (tm, LANES), 1)
    feat = lane < DA_DH
    q_aug = jnp.where(jnp.logical_and(lane >= DA_DH, lane < DA_DH + 3), 1.0, 0.0)
    q0 = 4 * cw
    k0 = q0 + DA_W

    def qk_normed(col0, gain_ref):
        y = _dot(hb, w_ref[:, col0:col0 + DA_W])
        ss = _dot((y * y).astype(bf16), bd_ref[...])
        return y * lax.rsqrt(ss * (1.0 / DA_DH) + EPS) * gain_ref[...]

    def lane_group(y2, grp, aug):
        pair = y2[:, (grp // 2) * LANES:(grp // 2 + 1) * LANES]
        if grp % 2:
            pair = pltpu.roll(pair, DA_DH, axis=1)
        return jnp.where(feat, pair, aug).astype(bf16)

    nq = qk_normed(q0, qg_ref)
    nk = qk_normed(k0, kg_ref)
    for grp in range(2 * DA_HEADS):
        sl = slice(grp * LANES, (grp + 1) * LANES)
        head = slice((grp // 2) * LANES, (grp // 2 + 1) * LANES)
        q_ref[:, sl] = lane_group(nq, grp, q_aug)
        k_ref[:, sl] = lane_group(nk, grp, kaug_ref[:, head].astype(f32))

    v0 = k0 + DA_W
    v_ref[...] = _dot(hb, w_ref[:, v0:v0 + DA_W]).astype(bf16)
    gates = _dot(hb, wg_ref[...]) + gbr_ref[...]
    gcol_ref[...] = gates
    grow_ref[...] = gates.T[0:N_GATES, :]


def _inproj(x2, mod, w_main, wg, gb_row, qgain, kgain, bd, kaug, conv_w, conv_b, seq):
    t, d = x2.shape
    tm = TM_INPROJ
    tiles_per_seq = seq // tm
    n_main = w_main.shape[1]
    const = lambda shape: pl.BlockSpec(shape, lambda i: (0,) * len(shape))
    return pl.pallas_call(
        functools.partial(_inproj_kernel, seq=seq),
        out_shape=(jax.ShapeDtypeStruct((t, 2 * ML_W), bf16),
                   jax.ShapeDtypeStruct((ML_W, t), bf16),
                   jax.ShapeDtypeStruct((ML_W, t), bf16),
                   jax.ShapeDtypeStruct((t, 2 * DA_W), bf16),
                   jax.ShapeDtypeStruct((t, 2 * DA_W), bf16),
                   jax.ShapeDtypeStruct((t, DA_W), bf16),
                   jax.ShapeDtypeStruct((t, LANES), f32),
                   jax.ShapeDtypeStruct((N_GATES, t), f32)),
        grid=(t // tm,),
        in_specs=[pl.BlockSpec((tm, d), lambda i: (i, 0)),
                  pl.BlockSpec((1, 6, d), lambda i: (i // tiles_per_seq, 0, 0)),
                  const((d, n_main)), const((d, LANES)), const((1, LANES)),
                  const((1, DA_W)), const((1, DA_W)), const((DA_W, DA_W)),
                  pl.BlockSpec((tm, DA_HEADS * LANES), lambda i: (i % tiles_per_seq, 0)),
                  const((CONV_K, 2 * ML_W)), const((1, 2 * ML_W))],
        out_specs=(pl.BlockSpec((tm, 2 * ML_W), lambda i: (i, 0)),
                   pl.BlockSpec((ML_W, tm), lambda i: (0, i)),
                   pl.BlockSpec((ML_W, tm), lambda i: (0, i)),
                   pl.BlockSpec((tm, 2 * DA_W), lambda i: (i, 0)),
                   pl.BlockSpec((tm, 2 * DA_W), lambda i: (i, 0)),
                   pl.BlockSpec((tm, DA_W), lambda i: (i, 0)),
                   pl.BlockSpec((tm, LANES), lambda i: (i, 0)),
                   pl.BlockSpec((N_GATES, tm), lambda i: (0, i))),
        scratch_shapes=[pltpu.VMEM((SUBLANES + tm, 2 * ML_W), f32)],
        compiler_params=_cparams(1),
        name="inproj",
    )(x2, mod, w_main, wg, gb_row, qgain, kgain, bd, kaug, conv_w, conv_b)


def _mlstm_kernel(q_ref, k_ref, vt_ref, ot_ref, gcol_ref, grow_ref, ng_ref, out_ref, state_ref, m_ref):
    L = ML_CHUNK
    dh = ML_DH

    @pl.when(pl.program_id(1) == 0)
    def _():
        state_ref[...] = jnp.zeros_like(state_ref)
        m_ref[...] = jnp.zeros_like(m_ref)

    s_i = lax.broadcasted_iota(jnp.int32, (L, L), 0)
    t_i = lax.broadcasted_iota(jnp.int32, (L, L), 1)
    visible = s_i <= t_i
    tril = (s_i >= t_i).astype(f32)
    triu = visible.astype(f32)
    ones_rows = (lax.broadcasted_iota(jnp.int32, (dh, L), 0) == 0).astype(f32)
    g_b = jnp.broadcast_to(ng_ref[...], (dh, L))

    for cc in range(ML_STEP_CHUNKS):
        rows = slice(cc * L, (cc + 1) * L)
        gcol = gcol_ref[rows, :]
        grow = grow_ref[:, rows]
        bcol_all = jnp.dot(tril, _log_sigmoid(gcol), precision=HIGHEST, preferred_element_type=f32)
        brow_all = jnp.dot(_log_sigmoid(grow), triu, precision=HIGHEST, preferred_element_type=f32)

        for hd in range(ML_HEADS):
            cols = slice(hd * dh, (hd + 1) * dh)
            qb = q_ref[rows, cols]
            kb = k_ref[rows, cols]
            v_aug_t = jnp.concatenate([vt_ref[cols, rows].astype(f32), ones_rows], axis=0)

            brow = brow_all[ML_HEADS + hd:ML_HEADS + hd + 1, :]
            irow = grow[hd:hd + 1, :]
            key_term = gcol[:, hd:hd + 1] - bcol_all[:, ML_HEADS + hd:ML_HEADS + hd + 1]
            m_prev = m_ref[hd:hd + 1, :]

            d_intra = jnp.where(visible, brow + key_term, NEG_BIG)
            d_inter = brow + m_prev
            m_t = jnp.maximum(d_inter, jnp.max(d_intra, axis=0, keepdims=True))
            w_intra = jnp.exp(d_intra - m_t)
            w_inter = jnp.exp(d_inter - m_t)

            s_t = _dot_nt(kb, qb) * w_intra
            state = state_ref[hd]
            q_state = _dot_nt(state.astype(bf16), qb)
            s_v = _dot(v_aug_t.astype(bf16), s_t.astype(bf16))
            num = w_inter * q_state[0:dh, :] + s_v[0:dh, :]
            den = w_inter * q_state[dh:dh + 1, :] + s_v[dh:dh + 1, :]
            hval = num / jnp.maximum(jnp.abs(den), jnp.exp(-m_t))

            b_last = brow[:, L - 1:L]
            d_state = b_last - brow + irow
            m_new = jnp.maximum(b_last + m_prev, jnp.max(d_state, axis=1, keepdims=True))
            carry_scale = jnp.exp(b_last + m_prev - m_new)[:, 0:1]
            wk = jnp.exp(d_state - m_new)
            upd = _dot((v_aug_t * wk).astype(bf16), kb)
            state_ref[hd] = carry_scale * state + upd
            m_ref[hd:hd + 1, :] = m_new

            hn = hval * lax.rsqrt(jnp.mean(hval * hval, axis=0, keepdims=True) + EPS) * g_b
            og = _sigmoid(ot_ref[cols, rows].astype(f32))
            out_ref[rows, cols] = (hn * og).T.astype(bf16)


def _mlstm(pm, vt, ot, gcol, grow, ml_norm_g, batch, seq):
    t = pm.shape[0]
    rows = ML_STEP_CHUNKS * ML_CHUNK
    ns = seq // rows
    row = lambda b, c: b * ns + c
    return pl.pallas_call(
        _mlstm_kernel,
        out_shape=jax.ShapeDtypeStruct((t, ML_W), bf16),
        grid=(batch, ns),
        in_specs=[pl.BlockSpec((rows, ML_W), lambda b, c: (row(b, c), 0)),
                  pl.BlockSpec((rows, ML_W), lambda b, c: (row(b, c), 1)),
                  pl.BlockSpec((ML_W, rows), lambda b, c: (0, row(b, c))),
                  pl.BlockSpec((ML_W, rows), lambda b, c: (0, row(b, c))),
                  pl.BlockSpec((rows, LANES), lambda b, c: (row(b, c), 0)),
                  pl.BlockSpec((N_GATES, rows), lambda b, c: (0, row(b, c))),
                  pl.BlockSpec((ML_DH, 1), lambda b, c: (0, 0))],
        out_specs=pl.BlockSpec((rows, ML_W), lambda b, c: (row(b, c), 0)),
        scratch_shapes=[pltpu.VMEM((ML_HEADS, 2 * ML_DH, ML_DH), f32),
                        pltpu.VMEM((SUBLANES, LANES), f32)],
        compiler_params=_cparams(2),
        name="mlstm",
    )(pm, pm, vt, ot, gcol, grow, ml_norm_g)


def _attn_block(q_ref, k_ref, v_ref, m_ref, l_ref, acc_ref, row0, nrows, nkeys, key0=None):
    rows = slice(row0, row0 + nrows)
    nch = nkeys // LANES
    v = v_ref[0:nkeys, :]
    if key0 is not None:
        keep = (key0 + lax.broadcasted_iota(jnp.int32, (nrows, nkeys), 1)
                <= row0 + lax.broadcasted_iota(jnp.int32, (nrows, nkeys), 0))
    for c in range(2):
        sl = slice(c * LANES, (c + 1) * LANES)
        s = _dot_nt(q_ref[rows, sl], k_ref[0:nkeys, sl])
        if key0 is not None:
            s = jnp.where(keep, s, NEG_BIG)
        chunks = [s[:, j * LANES:(j + 1) * LANES] for j in range(nch)]
        mc = chunks[0]
        for ch in chunks[1:]:
            mc = jnp.maximum(mc, ch)
        m_old = m_ref[c, rows, :]
        m_new = jnp.maximum(m_old, jnp.max(mc, axis=1, keepdims=True))
        alpha = jnp.exp2(m_old - m_new)
        ps = [jnp.exp2(ch - m_new) for ch in chunks]
        lsum = ps[0]
        for pj in ps[1:]:
            lsum = lsum + pj
        p = jnp.concatenate([pj.astype(bf16) for pj in ps], axis=1)
        l_ref[c, rows, :] = alpha * l_ref[c, rows, :] + lsum
        acc_ref[c, rows, :] = alpha * acc_ref[c, rows, :] + _dot(p, v)
        m_ref[c, rows, :] = m_new


def _attn_kernel(qt_ref, kt_ref, q_ref, k_ref, v_ref, lam_ref, g_ref, o_ref,
                 m_ref, l_ref, acc_ref, *, lam_init):
    step = pl.program_id(2)
    qi = qt_ref[step]
    ki = kt_ref[step]

    @pl.when(ki == 0)
    def _():
        m_ref[...] = jnp.full_like(m_ref, NEG_BIG)
        l_ref[...] = jnp.zeros_like(l_ref)
        acc_ref[...] = jnp.zeros_like(acc_ref)

    blk = ATT_BLOCK
    half = blk // 2
    qrows = ATT_QBLOCKS * blk
    first_diag = ATT_QBLOCKS * qi
    args = (q_ref, k_ref, v_ref, m_ref, l_ref, acc_ref)

    @pl.when(ki < first_diag)
    def _():
        _attn_block(*args, 0, qrows, blk)

    for j in range(ATT_QBLOCKS):
        @pl.when(ki == first_diag + j)
        def _(j=j):
            r0 = j * blk
            _attn_block(*args, r0, half, half, key0=r0)
            _attn_block(*args, r0 + half, half, blk, key0=r0)
            if r0 + blk < qrows:
                _attn_block(*args, r0 + blk, qrows - r0 - blk, blk)

    @pl.when(ki == first_diag + ATT_QBLOCKS - 1)
    def _():
        lam = lam_ref[...]
        l0 = jnp.sum(l_ref[0], axis=1, keepdims=True)
        l1 = jnp.sum(l_ref[1], axis=1, keepdims=True)
        o = acc_ref[0] / l0 - lam * (acc_ref[1] / l1)
        o = o * lax.rsqrt(jnp.mean(o * o, axis=-1, keepdims=True) + EPS)
        o_ref[...] = (o * g_ref[...] * (1.0 - lam_init)).astype(bf16)


def _attention(qh, kh, vh, lam, da_norm_g, batch, seq, lam_init):
    t = qh.shape[0]
    blk = ATT_BLOCK
    qrows = ATT_QBLOCKS * blk
    assert seq % qrows == 0, (seq, qrows)
    nk = seq // blk
    nq = seq // qrows
    pairs = [(i, j) for i in range(nq) for j in range(ATT_QBLOCKS * (i + 1))]
    qt = jnp.asarray([p[0] for p in pairs], jnp.int32)
    kt = jnp.asarray([p[1] for p in pairs], jnp.int32)
    w = 2 * DA_DH
    grid_spec = pltpu.PrefetchScalarGridSpec(
        num_scalar_prefetch=2,
        grid=(batch, DA_HEADS, len(pairs)),
        in_specs=[pl.BlockSpec((qrows, 2 * LANES), lambda b, h, s, qt, kt: (b * nq + qt[s], h)),
                  pl.BlockSpec((blk, 2 * LANES), lambda b, h, s, qt, kt: (b * nk + kt[s], h)),
                  pl.BlockSpec((blk, w), lambda b, h, s, qt, kt: (b * nk + kt[s], h)),
                  pl.BlockSpec((1, 1), lambda b, h, s, qt, kt: (0, 0)),
                  pl.BlockSpec((1, w), lambda b, h, s, qt, kt: (0, 0))],
        out_specs=pl.BlockSpec((qrows, w), lambda b, h, s, qt, kt: (b * nq + qt[s], h)),
        scratch_shapes=[pltpu.VMEM((2, qrows, LANES), f32),
                        pltpu.VMEM((2, qrows, LANES), f32),
                        pltpu.VMEM((2, qrows, w), f32)],
    )
    return pl.pallas_call(
        functools.partial(_attn_kernel, lam_init=lam_init),
        out_shape=jax.ShapeDtypeStruct((t, DA_W), bf16),
        grid_spec=grid_spec,
        compiler_params=_cparams(3),
        name="attn",
    )(qt, kt, qh, kh, vh, lam, da_norm_g)


def _first_index_of_max(x, iota_f, size):
    m = jnp.max(x, axis=0, keepdims=True)
    idx = jnp.min(jnp.where(x == m, iota_f, float(size)), axis=0, keepdims=True)
    return m, idx


def _route_kernel(hm_ref, hd_ref, x_ref, mod_ref, wo_ref, wr_ref, rb_ref,
                  x1_ref, ha_ref, hb_ref, idx_ref, gate_ref, rank_ref, cnt_ref, base_ref):
    tm = TM_ROUTE
    i = pl.program_id(0)

    @pl.when(i == 0)
    def _():
        base_ref[...] = jnp.zeros_like(base_ref)

    mix = _dot(hm_ref[...], wo_ref[0:ML_W, :]) + _dot(hd_ref[...], wo_ref[ML_W:ML_W + DA_W, :])
    x1 = x_ref[...] + mod_ref[0, 2:3, :] * mix
    x1_ref[...] = x1
    h2 = x1 * lax.rsqrt(jnp.mean(x1 * x1, axis=-1, keepdims=True) + EPS)
    h2 = h2 * (1.0 + mod_ref[0, 4:5, :]) + mod_ref[0, 3:4, :]
    packed = lax.bitcast_convert_type(_pack_bf16_pair(h2[:, 0:HALF], h2[:, HALF:D_MODEL]), jnp.int32)
    ha_ref[...] = packed[:, 0:QUARTER]
    hb_ref[...] = packed[:, QUARTER:HALF]

    sc = _sigmoid(_dot_nt(wr_ref[...], h2.astype(bf16)))
    sel = sc + rb_ref[...]

    gi = lax.broadcasted_iota(jnp.int32, (GROUP_SIZE, tm), 0).astype(f32)
    gscores = []
    for g in range(N_GROUPS):
        blk = sel[g * GROUP_SIZE:(g + 1) * GROUP_SIZE, :]
        m1, i1 = _first_index_of_max(blk, gi, GROUP_SIZE)
        m2 = jnp.max(jnp.where(gi == i1, NEG_BIG, blk), axis=0, keepdims=True)
        gscores.append(m1 + m2)
    gs = jnp.concatenate(gscores, axis=0)

    ngi = lax.broadcasted_iota(jnp.int32, (N_GROUPS, tm), 0).astype(f32)
    gkeep = jnp.zeros((N_GROUPS, tm), f32)
    for _ in range(TOPK_GROUPS):
        _, gidx = _first_index_of_max(gs, ngi, N_GROUPS)
        hit = ngi == gidx
        gkeep = jnp.where(hit, 1.0, gkeep)
        gs = jnp.where(hit, NEG_BIG, gs)
    masked = jnp.concatenate(
        [jnp.where(gkeep[g:g + 1, :] > 0.0, sel[g * GROUP_SIZE:(g + 1) * GROUP_SIZE, :], NEG_BIG)
         for g in range(N_GROUPS)], axis=0)

    ei = lax.broadcasted_iota(jnp.int32, (N_EXPERTS, tm), 0).astype(f32)
    idxs, gates = [], []
    chosen = jnp.zeros((N_EXPERTS, tm), f32)
    for _ in range(TOP_K):
        _, eidx = _first_index_of_max(masked, ei, N_EXPERTS)
        hit = ei == eidx
        idxs.append(eidx)
        gates.append(jnp.sum(jnp.where(hit, sc, 0.0), axis=0, keepdims=True))
        chosen = jnp.where(hit, 1.0, chosen)
        masked = jnp.where(hit, NEG_BIG, masked)
    gate = jnp.concatenate(gates, axis=0)
    gate = gate / jnp.sum(gate, axis=0, keepdims=True) * ROUTED_SCALE
    gate_ref[...] = gate
    idx_ref[...] = jnp.concatenate(idxs, axis=0).astype(jnp.int32)

    tr = lax.broadcasted_iota(jnp.int32, (tm, tm), 0)
    tc = lax.broadcasted_iota(jnp.int32, (tm, tm), 1)
    before = (tr < tc).astype(bf16)
    seen = base_ref[...] + _dot(chosen.astype(bf16), before)
    ranks = [jnp.sum(jnp.where(ei == eidx, seen, 0.0), axis=0, keepdims=True) for eidx in idxs]
    rank_ref[...] = jnp.concatenate(ranks, axis=0).astype(jnp.int32)
    total = base_ref[...] + jnp.sum(chosen, axis=1, keepdims=True)
    base_ref[...] = total
    cnt_ref[...] = total


def _route(hm, hd, x2, mod, w_out, w_router_t, rbias_col, seq):
    t, d = x2.shape
    tm = TM_ROUTE
    tiles_per_seq = seq // tm
    const = lambda shape: pl.BlockSpec(shape, lambda i: (0,) * len(shape))
    return pl.pallas_call(
        _route_kernel,
        out_shape=(jax.ShapeDtypeStruct((t, d), f32),
                   jax.ShapeDtypeStruct((t, QUARTER), jnp.int32),
                   jax.ShapeDtypeStruct((t, QUARTER), jnp.int32),
                   jax.ShapeDtypeStruct((TOP_K, t), jnp.int32),
                   jax.ShapeDtypeStruct((TOP_K, t), f32),
                   jax.ShapeDtypeStruct((TOP_K, t), jnp.int32),
                   jax.ShapeDtypeStruct((N_EXPERTS, 1), f32)),
        grid=(t // tm,),
        in_specs=[pl.BlockSpec((tm, ML_W), lambda i: (i, 0)),
                  pl.BlockSpec((tm, DA_W), lambda i: (i, 0)),
                  pl.BlockSpec((tm, d), lambda i: (i, 0)),
                  pl.BlockSpec((1, 6, d), lambda i: (i // tiles_per_seq, 0, 0)),
                  const((d, d)), const((N_EXPERTS, d)), const((N_EXPERTS, 1))],
        out_specs=(pl.BlockSpec((tm, d), lambda i: (i, 0)),
                   pl.BlockSpec((tm, QUARTER), lambda i: (i, 0)),
                   pl.BlockSpec((tm, QUARTER), lambda i: (i, 0)),
                   pl.BlockSpec((TOP_K, tm), lambda i: (0, i)),
                   pl.BlockSpec((TOP_K, tm), lambda i: (0, i)),
                   pl.BlockSpec((TOP_K, tm), lambda i: (0, i)),
                   const((N_EXPERTS, 1))),
        scratch_shapes=[pltpu.VMEM((N_EXPERTS, 1), f32)],
        compiler_params=_cparams(1),
        name="route",
    )(hm, hd, x2, mod, w_out, w_router_t, rbias_col)


SLOT_RADIX = 256
SLOT_DIGITS = 3


def _slots_kernel(idx_ref, rank_ref, dig_ref, dest_ref):
    tm = TM_SLOTS
    ei = lax.broadcasted_iota(jnp.int32, (N_EXPERTS, tm), 0)
    rows = []
    for k in range(TOP_K):
        onehot = jnp.where(ei == idx_ref[k:k + 1, :], 1.0, 0.0).astype(bf16)
        dg = _dot(dig_ref[...], onehot)
        start = dg[0:1, :]
        for j in range(1, SLOT_DIGITS):
            start = start + dg[j:j + 1, :] * float(SLOT_RADIX ** j)
        rows.append(start.astype(jnp.int32) + rank_ref[k:k + 1, :])
    dest_ref[...] = jnp.concatenate(rows, axis=0)


def _slots(idx, rank, digits):
    t = idx.shape[1]
    tm = TM_SLOTS
    return pl.pallas_call(
        _slots_kernel,
        out_shape=jax.ShapeDtypeStruct((TOP_K, t), jnp.int32),
        grid=(t // tm,),
        in_specs=[pl.BlockSpec((TOP_K, tm), lambda i: (0, i)),
                  pl.BlockSpec((TOP_K, tm), lambda i: (0, i)),
                  pl.BlockSpec((SUBLANES, N_EXPERTS), lambda i: (0, 0))],
        out_specs=pl.BlockSpec((TOP_K, tm), lambda i: (0, i)),
        compiler_params=_cparams(1),
        name="slots",
    )(idx, rank, digits)


def _sc_scatter_table(rows, dest, n_slots):
    n, width = rows.shape
    mesh = plsc.VectorSubcoreMesh(core_axis_name="core", subcore_axis_name="subcore")

    @pl.kernel(out_type=jax.ShapeDtypeStruct((n_slots, width), rows.dtype), mesh=mesh, scratch_types=[],
               name="sc_scatter")
    def scatter_kernel(rows_hbm, dest_hbm, out_hbm):
        def body(rows_vmem, dest_vmem):
            for k in range(TOP_K):
                pltpu.sync_copy(rows_vmem, out_hbm.at[dest_vmem.at[k]])

        pltpu.emit_pipeline(
            body,
            grid=(n // SC_WINDOW,),
            in_specs=[pl.BlockSpec((SC_WINDOW, width), lambda i: (i, 0)),
                      pl.BlockSpec((TOP_K, SC_WINDOW), lambda i: (0, i))],
            out_specs=[],
            core_axis_name=("core", "subcore"),
            dimension_semantics=(pltpu.PARALLEL,),
        )(rows_hbm, dest_hbm)

    return scatter_kernel(rows, dest)


EXPERT_GROUP = 8
EXPERT_RING = 32
EXPERT_AHEAD = EXPERT_RING - EXPERT_GROUP


def _expert_kernel(start_ref, cnt_ref, total_ref, xsa_ref, xsb_ref, w1_ref, w3_ref, w2_ref,
                   ysa_ref, ysb_ref, xbuf, ybuf, w1b, w3b, w2b, in_sem, out_sem):
    bm = MOE_BLOCK
    ring = EXPERT_RING
    ahead = EXPERT_AHEAD
    shift = int(math.log2(bm))
    e = pl.program_id(0)
    total = total_ref[0]
    cnt = cnt_ref[e]
    nb = lax.shift_right_logical(cnt + (bm - 1), shift)
    g0 = lax.shift_right_logical(start_ref[e], shift)

    def slot_of(g):
        return jnp.bitwise_and(g, ring - 1)

    def in_copies(g):
        s = slot_of(g)
        rows = pl.ds(pl.multiple_of(g * bm, bm), bm)
        return [pltpu.make_async_copy(x_ref.at[rows], xbuf.at[s, half], in_sem.at[s])
                for half, x_ref in enumerate((xsa_ref, xsb_ref))]

    def out_copies(g):
        s = slot_of(g)
        rows = pl.ds(pl.multiple_of(g * bm, bm), bm)
        return [pltpu.make_async_copy(ybuf.at[s, half], y_ref.at[rows], out_sem.at[s])
                for half, y_ref in enumerate((ysa_ref, ysb_ref))]

    def request(g):
        @pl.when(g < total)
        def _():
            for cp in in_copies(g):
                cp.start()

    def acquire(g):
        for cp in in_copies(g):
            cp.wait()

        @pl.when(g >= ring)
        def _():
            for cp in out_copies(g - ring):
                cp.wait()

    def compute(g, n):
        slots = [slot_of(g + d) for d in range(n)]
        row = (g - g0) * bm + lax.broadcasted_iota(jnp.int32, (n * bm, HALF), 0)
        words = jnp.concatenate(
            [jnp.concatenate([xbuf[s, 0], xbuf[s, 1]], axis=1) for s in slots], axis=0)
        lo, hi = _unpack_bf16_pair(lax.bitcast_convert_type(jnp.where(row < cnt, words, 0), jnp.uint32))
        lo = lo.astype(bf16)
        hi = hi.astype(bf16)
        h1 = _dot(lo, w1b[0:HALF, :]) + _dot(hi, w1b[HALF:D_MODEL, :])
        h3 = _dot(lo, w3b[0:HALF, :]) + _dot(hi, w3b[HALF:D_MODEL, :])
        y = _dot((_silu(h1) * h3).astype(bf16), w2b[...])
        packed = lax.bitcast_convert_type(_pack_bf16_pair(y[:, 0:HALF], y[:, HALF:D_MODEL]), jnp.int32)
        for d, s in enumerate(slots):
            ybuf[s, 0] = packed[d * bm:(d + 1) * bm, 0:QUARTER]
            ybuf[s, 1] = packed[d * bm:(d + 1) * bm, QUARTER:HALF]

    def release(g):
        for cp in out_copies(g):
            cp.start()

    @pl.when(e == 0)
    def _():
        for g in range(ahead):
            request(g)

    @pl.when(nb > 0)
    def _():
        w1b[...] = w1_ref[0].astype(bf16)
        w3b[...] = w3_ref[0].astype(bf16)
        w2b[...] = w2_ref[0].astype(bf16)

    def run(g, n):
        for d in range(n):
            request(g + d + ahead)
        for d in range(n):
            acquire(g + d)
        compute(g, n)
        for d in range(n):
            release(g + d)

    def group(i, carry):
        run(g0 + EXPERT_GROUP * i, EXPERT_GROUP)
        return carry

    lax.fori_loop(0, lax.shift_right_logical(nb, int(math.log2(EXPERT_GROUP))), group, 0)

    n = EXPERT_GROUP // 2
    while n >= 1:
        @pl.when(jnp.bitwise_and(nb, n) != 0)
        def _(n=n):
            done = jnp.bitwise_and(nb, -2 * n)
            run(g0 + done, n)
        n //= 2

    @pl.when(e == pl.num_programs(0) - 1)
    def _():
        for back in range(1, ring + 1):
            @pl.when(total - back >= 0)
            def _():
                for cp in out_copies(total - back):
                    cp.wait()


def _experts(start, counts, total, xs_a, xs_b, w1, w3, w2):
    n_pad = xs_a.shape[0]
    bm = MOE_BLOCK
    grid_spec = pltpu.PrefetchScalarGridSpec(
        num_scalar_prefetch=3,
        grid=(N_EXPERTS,),
        in_specs=[pl.BlockSpec(memory_space=pl.ANY),
                  pl.BlockSpec(memory_space=pl.ANY),
                  pl.BlockSpec((1, D_MODEL, D_EXPERT), lambda e, st, ct, tt: (e, 0, 0)),
                  pl.BlockSpec((1, D_MODEL, D_EXPERT), lambda e, st, ct, tt: (e, 0, 0)),
                  pl.BlockSpec((1, D_EXPERT, D_MODEL), lambda e, st, ct, tt: (e, 0, 0))],
        out_specs=(pl.BlockSpec(memory_space=pl.ANY), pl.BlockSpec(memory_space=pl.ANY)),
        scratch_shapes=[pltpu.VMEM((EXPERT_RING, 2, bm, QUARTER), jnp.int32),
                        pltpu.VMEM((EXPERT_RING, 2, bm, QUARTER), jnp.int32),
                        pltpu.VMEM((D_MODEL, D_EXPERT), bf16),
                        pltpu.VMEM((D_MODEL, D_EXPERT), bf16),
                        pltpu.VMEM((D_EXPERT, D_MODEL), bf16),
                        pltpu.SemaphoreType.DMA((EXPERT_RING,)),
                        pltpu.SemaphoreType.DMA((EXPERT_RING,))],
    )
    return pl.pallas_call(
        _expert_kernel,
        out_shape=(jax.ShapeDtypeStruct((n_pad, QUARTER), jnp.int32),
                   jax.ShapeDtypeStruct((n_pad, QUARTER), jnp.int32)),
        grid_spec=grid_spec,
        compiler_params=_cparams(1),
        name="experts",
    )(start, counts, total, xs_a, xs_b, w1, w3, w2)


def _sc_gather_table(table, idx2):
    n = idx2.shape[1]
    width = table.shape[1]
    mesh = plsc.VectorSubcoreMesh(core_axis_name="core", subcore_axis_name="subcore")

    @pl.kernel(out_type=jax.ShapeDtypeStruct((n, width), table.dtype), mesh=mesh, scratch_types=[],
               name="sc_gather")
    def gather_kernel(tab_hbm, idx_hbm, out_hbm):
        def body(idx_vmem, out_vmem):
            pltpu.sync_copy(tab_hbm.at[idx_vmem.at[0]], out_vmem)

        pltpu.emit_pipeline(
            body,
            grid=(n // SC_WINDOW,),
            in_specs=[pl.BlockSpec((1, SC_WINDOW), lambda i: (0, i))],
            out_specs=[pl.BlockSpec((SC_WINDOW, width), lambda i: (i, 0))],
            core_axis_name=("core", "subcore"),
            dimension_semantics=(pltpu.PARALLEL,),
        )(idx_hbm, out_hbm)

    return gather_kernel(table, idx2)


def _sc_gather_rows(table_a, table_b, idx):
    idx2 = idx.reshape(1, -1)
    return _sc_gather_table(table_a, idx2), _sc_gather_table(table_b, idx2)


def _combine_kernel(gate_ref, ha_ref, hb_ref, x1_ref, mod_ref, ws1_ref, ws3_ref, ws2_ref, ga_ref, gb_ref,
                    out_ref):
    words = jnp.concatenate([ha_ref[...], hb_ref[...]], axis=1)
    lo, hi = _unpack_bf16_pair(lax.bitcast_convert_type(words, jnp.uint32))
    lo = lo.astype(bf16)
    hi = hi.astype(bf16)
    s1 = _dot(lo, ws1_ref[0:HALF, :]) + _dot(hi, ws1_ref[HALF:D_MODEL, :])
    s3 = _dot(lo, ws3_ref[0:HALF, :]) + _dot(hi, ws3_ref[HALF:D_MODEL, :])
    y = _dot((_silu(s1) * s3).astype(bf16), ws2_ref[...])

    gate = gate_ref[...].T
    gate_f = mod_ref[0, 5:6, :]
    for part, g_ref in enumerate((ga_ref, gb_ref)):
        c_lo = slice(part * QUARTER, (part + 1) * QUARTER)
        c_hi = slice(HALF + part * QUARTER, HALF + (part + 1) * QUARTER)
        acc_lo = y[:, c_lo]
        acc_hi = y[:, c_hi]
        for k in range(TOP_K):
            rlo, rhi = _unpack_bf16_pair(lax.bitcast_convert_type(g_ref[k], jnp.uint32))
            gk = gate[:, k:k + 1]
            acc_lo = acc_lo + gk * rlo
            acc_hi = acc_hi + gk * rhi
        out_ref[:, c_lo] = x1_ref[:, c_lo] + gate_f[:, c_lo] * acc_lo
        out_ref[:, c_hi] = x1_ref[:, c_hi] + gate_f[:, c_hi] * acc_hi


def _combine(gate, h2a, h2b, x1, mod, ws1, ws3, ws2, ga, gb, seq):
    t, d = x1.shape
    tm = TM_MOE
    tiles_per_seq = seq // tm
    const = lambda shape: pl.BlockSpec(shape, lambda i: (0,) * len(shape))
    return pl.pallas_call(
        _combine_kernel,
        out_shape=jax.ShapeDtypeStruct((t, d), f32),
        grid=(t // tm,),
        in_specs=[pl.BlockSpec((TOP_K, tm), lambda i: (0, i)),
                  pl.BlockSpec((tm, QUARTER), lambda i: (i, 0)),
                  pl.BlockSpec((tm, QUARTER), lambda i: (i, 0)),
                  pl.BlockSpec((tm, d), lambda i: (i, 0)),
                  pl.BlockSpec((1, 6, d), lambda i: (i // tiles_per_seq, 0, 0)),
                  const((d, D_EXPERT)), const((d, D_EXPERT)), const((D_EXPERT, d)),
                  pl.BlockSpec((TOP_K, tm, QUARTER), lambda i: (0, i, 0)),
                  pl.BlockSpec((TOP_K, tm, QUARTER), lambda i: (0, i, 0))],
        out_specs=pl.BlockSpec((tm, d), lambda i: (i, 0)),
        compiler_params=_cparams(1),
        name="combine",
    )(gate, h2a, h2b, x1, mod, ws1, ws3, ws2, ga, gb)


def _lambda_init(layer):
    return 0.8 - 0.6 * math.exp(-0.3 * layer)


def _layer(x, c, w_ada, b_ada, w_in, conv_w, conv_b, gate_b, ml_norm_g, da_q_norm_g, da_k_norm_g,
           lambda_q1, lambda_k1, lambda_q2, lambda_k2, da_norm_g, w_out, w_router, router_bias,
           w1, w3, w2, ws1, ws3, ws2, layer):
    batch, seq, d = x.shape
    t = batch * seq
    lam_init = _lambda_init(layer)
    x2 = x.reshape(t, d)

    c_pad = jnp.pad(c, ((0, -batch % SUBLANES), (0, 0)))
    mod = _adaln(c_pad, w_ada, b_ada.reshape(1, -1))[:batch].reshape(batch, 6, d)

    g0 = 4 * ML_W
    q0 = g0 + N_GATES
    n_grp = 2 * DA_HEADS
    w_main = jnp.concatenate([w_in[:, :g0], w_in[:, q0:]], axis=1).astype(bf16)
    wg = jnp.pad(w_in[:, g0:g0 + N_GATES], ((0, 0), (0, LANES - N_GATES))).astype(bf16)
    gb_row = jnp.pad(gate_b, (0, LANES - N_GATES)).reshape(1, LANES)
    qgain = (jnp.tile(da_q_norm_g, n_grp) * (DA_DH ** -0.5 * LOG2E)).reshape(1, DA_W)
    kgain = jnp.tile(da_k_norm_g, n_grp).reshape(1, DA_W)
    seg = jnp.arange(DA_W) // DA_DH
    bd = (seg[:, None] == seg[None, :]).astype(bf16)

    pm, vt, ot, qh, kh, vh, gcol, grow = _inproj(x2, mod, w_main, wg, gb_row, qgain, kgain, bd, _alibi_table(seq),
                                         conv_w, conv_b.reshape(1, -1), seq)

    hm = _mlstm(pm, vt, ot, gcol, grow, ml_norm_g.reshape(-1, 1), batch, seq)

    lam = (jnp.exp(jnp.sum(lambda_q1 * lambda_k1)) - jnp.exp(jnp.sum(lambda_q2 * lambda_k2))
           + lam_init).reshape(1, 1).astype(f32)
    hd = _attention(qh, kh, vh, lam, da_norm_g.reshape(1, -1), batch, seq, lam_init)

    x1, h2a, h2b, idx, gate, rank, counts = _route(hm, hd, x2, mod, w_out.astype(bf16),
                                                   w_router.T.astype(bf16), router_bias.reshape(-1, 1), seq)

    bm = MOE_BLOCK
    n_blocks = (t * TOP_K) // bm + N_EXPERTS
    counts_i = counts.reshape(-1).astype(jnp.int32)
    padded = (counts_i + bm - 1) // bm * bm
    pad_end = jnp.cumsum(padded)
    pad_start = pad_end - padded
    digits = jnp.stack([(pad_start // SLOT_RADIX ** j) % SLOT_RADIX for j in range(SLOT_DIGITS)])
    digits = jnp.pad(digits, ((0, SUBLANES - SLOT_DIGITS), (0, 0))).astype(bf16)
    dest = _slots(idx, rank, digits)

    xs_a = _sc_scatter_table(h2a, dest, n_blocks * bm)
    xs_b = _sc_scatter_table(h2b, dest, n_blocks * bm)
    total_blocks = (pad_end[-1:] // bm).astype(jnp.int32)
    ys_a, ys_b = _experts(pad_start.astype(jnp.int32), counts_i, total_blocks, xs_a, xs_b, w1, w3, w2)
    ga, gb = _sc_gather_rows(ys_a, ys_b, dest.reshape(-1))
    out = _combine(gate, h2a, h2b, x1, mod, ws1.astype(bf16), ws3.astype(bf16), ws2.astype(bf16),
                   ga.reshape(TOP_K, t, QUARTER), gb.reshape(TOP_K, t, QUARTER), seq)
    return out.reshape(batch, seq, d)


def kernel(x, c, w_ada, b_ada, w_in, conv_w, conv_b, gate_b, ml_norm_g, da_q_norm_g, da_k_norm_g,
           lambda_q1, lambda_k1, lambda_q2, lambda_k2, da_norm_g, w_out, w_router, router_bias,
           w1, w3, w2, ws1, ws3, ws2):
    depth = w_ada.shape[0]
    for l in range(depth):
        x = _layer(x, c, w_ada[l], b_ada[l], w_in[l], conv_w[l], conv_b[l], gate_b[l], ml_norm_g[l],
                   da_q_norm_g[l], da_k_norm_g[l], lambda_q1[l], lambda_k1[l], lambda_q2[l], lambda_k2[l],
                   da_norm_g[l], w_out[l], w_router[l], router_bias[l], w1[l], w3[l], w2[l],
                   ws1[l], ws3[l], ws2[l], l)
    return x
```
